```python
import jax
import jax.numpy as jnp
from jax import lax
import numpy as np


D_MODEL = 1024
BATCH = 1
SEQ = 16384
DEPTH = 4

N_HEADS = 16
HEAD_DIM = D_MODEL // N_HEADS
ROPE_DIM = HEAD_DIM // 4
ROPE_THETA = 500000.0
NSA_KV_HEADS = 4
NSA_GROUP = N_HEADS // NSA_KV_HEADS
CMP_BLOCK = 32
CMP_STRIDE = 16
CMP_HIDDEN = 256
SEL_BLOCK = 64
SEL_TOPK = 16
WINDOW = 512
NSA_Q_CHUNK = 128
NSA_Q_DIM = N_HEADS * HEAD_DIM
NSA_KV_DIM = NSA_KV_HEADS * HEAD_DIM
NSA_IN = NSA_Q_DIM + 6 * NSA_KV_DIM + 3 * N_HEADS
MOBA_BLOCK = 256
MOBA_TOPK = 3
MOBA_Q_CHUNK = 64
D_FF = -(-8 * D_MODEL // (3 * 256)) * 256
ALPHA = (2 * DEPTH) ** 0.25
BETA = (8 * DEPTH) ** -0.25
N_NSA = (DEPTH + 1) // 2
N_MOBA = DEPTH // 2
LN_EPS = 1e-5
LARGE = 1e30

kernel_name = 'hybrid_nsa_moba_deepnorm'


def rope_partial(x, pos):
    inv = ROPE_THETA ** (-jnp.arange(0, ROPE_DIM, 2, dtype=jnp.float32) / ROPE_DIM)
    ang = pos.astype(jnp.float32)[..., None] * inv
    cos = jnp.cos(ang)[:, :, None, :]
    sin = jnp.sin(ang)[:, :, None, :]
    xr = x[..., :ROPE_DIM].astype(jnp.float32)
    x1, x2 = xr[..., :ROPE_DIM // 2], xr[..., ROPE_DIM // 2:]
    rot = jnp.concatenate([x1 * cos - x2 * sin, x2 * cos + x1 * sin], axis=-1).astype(x.dtype)
    return jnp.concatenate([rot, x[..., ROPE_DIM:]], axis=-1)


def masked_softmax(s, mask):
    return jax.nn.softmax(jnp.where(mask, s, -LARGE), axis=-1)


def layer_norm(x, g, b):
    xf = x.astype(jnp.float32)
    mu = jnp.mean(xf, axis=-1, keepdims=True)
    var = jnp.mean(jnp.square(xf - mu), axis=-1, keepdims=True)
    y = (xf - mu) * lax.rsqrt(var + LN_EPS)
    return (y * g + b).astype(x.dtype)


def compress(k, pos_emb, w1, w2):
    B, S, G, hd = k.shape
    n_cmp = (S - CMP_BLOCK) // CMP_STRIDE + 1
    idx = jnp.arange(n_cmp)[:, None] * CMP_STRIDE + jnp.arange(CMP_BLOCK)[None, :]
    blocks = k[:, idx] + pos_emb[None, None, :, None, :]
    blocks = blocks.transpose(0, 3, 1, 2, 4).reshape(B, G, n_cmp, CMP_BLOCK * hd)
    return jax.nn.gelu(blocks @ w1) @ w2


def nsa_mixer(h, pos, w_in, w_o, ck_pos, ck_w1, ck_w2, cv_pos, cv_w1, cv_w2):
    B, S, _ = h.shape
    G, R, hd, CH = NSA_KV_HEADS, NSA_GROUP, HEAD_DIM, NSA_Q_CHUNK
    proj = h @ w_in
    q = rope_partial(proj[..., :NSA_Q_DIM].reshape(B, S, N_HEADS, hd), pos).reshape(B, S, G, R, hd)
    kv = proj[..., NSA_Q_DIM:NSA_Q_DIM + 6 * NSA_KV_DIM].reshape(B, S, 6, G, hd)
    gates = jax.nn.sigmoid(proj[..., NSA_Q_DIM + 6 * NSA_KV_DIM:].astype(jnp.float32)).reshape(B, S, G, R, 3)
    k_cmp = rope_partial(kv[:, :, 0], pos)
    v_cmp = kv[:, :, 1]
    k_sel = rope_partial(kv[:, :, 2], pos)
    v_sel = kv[:, :, 3]
    k_win = rope_partial(kv[:, :, 4], pos)
    v_win = kv[:, :, 5]

    kc = compress(k_cmp, ck_pos, ck_w1, ck_w2)
    vc = compress(v_cmp, cv_pos, cv_w1, cv_w2)
    n_cmp = kc.shape[2]
    cmp_start = jnp.arange(n_cmp) * CMP_STRIDE
    cmp_end = cmp_start + CMP_BLOCK - 1
    n_sel = S // SEL_BLOCK
    n_top = min(SEL_TOPK, n_sel)
    sel_ids = jnp.arange(n_sel)
    sel_start = sel_ids * SEL_BLOCK
    lo = jnp.maximum(cmp_start[:, None], sel_start[None, :])
    hi = jnp.minimum(cmp_end[:, None], sel_start[None, :] + SEL_BLOCK - 1)
    overlap = jnp.clip(hi - lo + 1, 0, None).astype(jnp.float32) / CMP_STRIDE

    ks_blk = k_sel.reshape(B, n_sel, SEL_BLOCK, G, hd).transpose(0, 3, 1, 2, 4)
    vs_blk = v_sel.reshape(B, n_sel, SEL_BLOCK, G, hd).transpose(0, 3, 1, 2, 4)
    kw_pad = jnp.pad(k_win, ((0, 0), (WINDOW, 0), (0, 0), (0, 0)))
    vw_pad = jnp.pad(v_win, ((0, 0), (WINDOW, 0), (0, 0), (0, 0)))
    scale = HEAD_DIM ** -0.5
    b_ix = jnp.arange(B)[:, None, None, None]
    g_ix = jnp.arange(G)[None, :, None, None]

    def step(c):
        qs = c * CH
        tq = qs + jnp.arange(CH)
        qc = lax.dynamic_slice_in_dim(q, qs, CH, axis=1)
        gc = lax.dynamic_slice_in_dim(gates, qs, CH, axis=1)
        s = jnp.einsum('bqgrd,bgnd->bgrqn', qc, kc).astype(jnp.float32) * scale
        m = cmp_end[None, :] <= tq[:, None]
        p_cmp = masked_softmax(s, m) * jnp.any(m, axis=-1)[:, None]
        o_cmp = jnp.einsum('bgrqn,bgnd->bqgrd', p_cmp.astype(vc.dtype), vc)
        imp = jnp.einsum('bgrqn,nj->bgqj', p_cmp, overlap)
        blk = tq // SEL_BLOCK
        causal = sel_start[None, :] <= tq[:, None]
        forced = (sel_ids[None, :] == 0) | (sel_ids[None, :] == blk[:, None]) | (sel_ids[None, :] == blk[:, None] - 1)
        score = jnp.where(forced, LARGE, jnp.where(causal, imp, -LARGE))
        _, sel = lax.top_k(score, n_top)
        kg = ks_blk[b_ix, g_ix, sel].reshape(B, G, CH, n_top * SEL_BLOCK, hd)
        vg = vs_blk[b_ix, g_ix, sel].reshape(B, G, CH, n_top * SEL_BLOCK, hd)
        kpos = sel[..., None] * SEL_BLOCK + jnp.arange(SEL_BLOCK)
        msel = (kpos <= tq[:, None, None]).reshape(B, G, 1, CH, n_top * SEL_BLOCK)
        s = jnp.einsum('bqgrd,bgqkd->bgrqk', qc, kg).astype(jnp.float32) * scale
        p = masked_softmax(s, msel)
        o_sel = jnp.einsum('bgrqk,bgqkd->bqgrd', p.astype(vg.dtype), vg)
        kwc = lax.dynamic_slice_in_dim(kw_pad, qs, CH + WINDOW, axis=1)
        vwc = lax.dynamic_slice_in_dim(vw_pad, qs, CH + WINDOW, axis=1)
        kpos_w = qs - WINDOW + jnp.arange(CH + WINDOW)
        dist = tq[:, None] - kpos_w[None, :]
        mw = (kpos_w[None, :] >= 0) & (dist >= 0) & (dist < WINDOW)
        s = jnp.einsum('bqgrd,bkgd->bgrqk', qc, kwc).astype(jnp.float32) * scale
        p = masked_softmax(s, mw)
        o_win = jnp.einsum('bgrqk,bkgd->bqgrd', p.astype(vwc.dtype), vwc)
        o = gc[..., 0:1] * o_cmp + gc[..., 1:2] * o_sel + gc[..., 2:3] * o_win
        return o.reshape(B, CH, NSA_Q_DIM).astype(h.dtype)

    out = lax.map(step, jnp.arange(S // CH))
    out = out.transpose(1, 0, 2, 3).reshape(B, S, NSA_Q_DIM)
    return out @ w_o


def moba_mixer(h, pos, w_in, w_o):
    B, S, _ = h.shape
    H, hd, CH, BLK = N_HEADS, HEAD_DIM, MOBA_Q_CHUNK, MOBA_BLOCK
    proj = h @ w_in
    q = rope_partial(proj[..., :D_MODEL].reshape(B, S, H, hd), pos)
    k = rope_partial(proj[..., D_MODEL:2 * D_MODEL].reshape(B, S, H, hd), pos)
    v = proj[..., 2 * D_MODEL:].reshape(B, S, H, hd)
    n_blk = -(-S // BLK)
    pad = n_blk * BLK - S
    kb = jnp.pad(k, ((0, 0), (0, pad), (0, 0), (0, 0))).reshape(B, n_blk, BLK, H, hd).transpose(0, 3, 1, 2, 4)
    vb = jnp.pad(v, ((0, 0), (0, pad), (0, 0), (0, 0))).reshape(B, n_blk, BLK, H, hd).transpose(0, 3, 1, 2, 4)
    kmean = jnp.mean(kb.astype(jnp.float32), axis=3)
    k_eff = min(MOBA_TOPK, max(n_blk - 1, 1))
    scale = HEAD_DIM ** -0.5
    b_ix = jnp.arange(B)[:, None, None, None]
    h_ix = jnp.arange(H)[None, :, None, None]
    blk_ids = jnp.arange(n_blk)

    def step(c):
        qs = c * CH
        tq = qs + jnp.arange(CH)
        cur = qs // BLK
        qc = lax.dynamic_slice_in_dim(q, qs, CH, axis=1)
        gate = jnp.einsum('bqhd,bhnd->bhqn', qc.astype(jnp.float32), kmean)
        gate = jnp.where(blk_ids < cur, gate, -LARGE)
        _, sel = lax.top_k(gate, k_eff)
        valid = sel < cur
        kg = kb[b_ix, h_ix, sel].reshape(B, H, CH, k_eff * BLK, hd)
        vg = vb[b_ix, h_ix, sel].reshape(B, H, CH, k_eff * BLK, hd)
        msel = jnp.broadcast_to(valid[..., None], valid.shape + (BLK,)).reshape(B, H, CH, k_eff * BLK)
        k_own = lax.dynamic_index_in_dim(kb, cur, axis=2, keepdims=False)
        v_own = lax.dynamic_index_in_dim(vb, cur, axis=2, keepdims=False)
        own_pos = cur * BLK + jnp.arange(BLK)
        mown = jnp.broadcast_to(own_pos[None, :] <= tq[:, None], (B, H, CH, BLK))
        s_sel = jnp.einsum('bqhd,bhqkd->bhqk', qc, kg).astype(jnp.float32) * scale
        s_own = jnp.einsum('bqhd,bhkd->bhqk', qc, k_own).astype(jnp.float32) * scale
        p = masked_softmax(jnp.concatenate([s_sel, s_own], axis=-1), jnp.concatenate([msel, mown], axis=-1))
        p = p.astype(vg.dtype)
        o = (jnp.einsum('bhqk,bhqkd->bqhd', p[..., :k_eff * BLK], vg)
             + jnp.einsum('bhqk,bhkd->bqhd', p[..., k_eff * BLK:], v_own))
        return o.reshape(B, CH, D_MODEL).astype(h.dtype)

    out = lax.map(step, jnp.arange(S // CH))
    out = out.transpose(1, 0, 2, 3).reshape(B, S, D_MODEL)
    return out @ w_o


def swiglu(h, wg, wu, wd):
    return (jax.nn.silu(h @ wg) * (h @ wu)) @ wd


def setup_inputs(seed: int = 0) -> dict:
    key = jax.random.key(seed)
    ks = jax.random.split(key, 20)

    def nrm(k, shape, s):
        return jax.random.normal(k, shape, jnp.float32) * s

    x = nrm(ks[0], (BATCH, SEQ, D_MODEL), 1.0)
    positions = jnp.broadcast_to(jnp.arange(SEQ, dtype=jnp.int32)[None, :], (BATCH, SEQ))
    nsa_col = np.ones((NSA_IN,), np.float32)
    for m in (1, 3, 5):
        s0 = NSA_Q_DIM + m * NSA_KV_DIM
        nsa_col[s0:s0 + NSA_KV_DIM] = BETA
    nsa_w_in = nrm(ks[1], (N_NSA, D_MODEL, NSA_IN), D_MODEL ** -0.5) * jnp.asarray(nsa_col)
    nsa_w_o = nrm(ks[2], (N_NSA, NSA_Q_DIM, D_MODEL), BETA * NSA_Q_DIM ** -0.5)
    nsa_ck_pos = nrm(ks[3], (N_NSA, CMP_BLOCK, HEAD_DIM), 0.02)
    nsa_ck_w1 = nrm(ks[4], (N_NSA, CMP_BLOCK * HEAD_DIM, CMP_HIDDEN), (CMP_BLOCK * HEAD_DIM) ** -0.5)
    nsa_ck_w2 = nrm(ks[5], (N_NSA, CMP_HIDDEN, HEAD_DIM), CMP_HIDDEN ** -0.5)
    nsa_cv_pos = nrm(ks[6], (N_NSA, CMP_BLOCK, HEAD_DIM), 0.02)
    nsa_cv_w1 = nrm(ks[7], (N_NSA, CMP_BLOCK * HEAD_DIM, CMP_HIDDEN), (CMP_BLOCK * HEAD_DIM) ** -0.5)
    nsa_cv_w2 = nrm(ks[8], (N_NSA, CMP_HIDDEN, HEAD_DIM), CMP_HIDDEN ** -0.5)
    moba_col = np.ones((3 * D_MODEL,), np.float32)
    moba_col[2 * D_MODEL:] = BETA
    moba_w_in = nrm(ks[9], (N_MOBA, D_MODEL, 3 * D_MODEL), D_MODEL ** -0.5) * jnp.asarray(moba_col)
    moba_w_o = nrm(ks[10], (N_MOBA, D_MODEL, D_MODEL), BETA * D_MODEL ** -0.5)
    ffn_wg = nrm(ks[11], (DEPTH, D_MODEL, D_FF), D_MODEL ** -0.5)
    ffn_wu = nrm(ks[12], (DEPTH, D_MODEL, D_FF), D_MODEL ** -0.5)
    ffn_wd = nrm(ks[13], (DEPTH, D_FF, D_MODEL), BETA * D_FF ** -0.5)
    ln1_g = 1.0 + nrm(ks[14], (DEPTH, D_MODEL), 0.02)
    ln1_b = nrm(ks[15], (DEPTH, D_MODEL), 0.02)
    ln2_g = 1.0 + nrm(ks[16], (DEPTH, D_MODEL), 0.02)
    ln2_b = nrm(ks[17], (DEPTH, D_MODEL), 0.02)
    return {'x': x, 'positions': positions,
            'nsa_w_in': nsa_w_in, 'nsa_w_o': nsa_w_o,
            'nsa_ck_pos': nsa_ck_pos, 'nsa_ck_w1': nsa_ck_w1, 'nsa_ck_w2': nsa_ck_w2,
            'nsa_cv_pos': nsa_cv_pos, 'nsa_cv_w1': nsa_cv_w1, 'nsa_cv_w2': nsa_cv_w2,
            'moba_w_in': moba_w_in, 'moba_w_o': moba_w_o,
            'ffn_wg': ffn_wg, 'ffn_wu': ffn_wu, 'ffn_wd': ffn_wd,
            'ln1_g': ln1_g, 'ln1_b': ln1_b, 'ln2_g': ln2_g, 'ln2_b': ln2_b}


def reference(x, positions, nsa_w_in, nsa_w_o, nsa_ck_pos, nsa_ck_w1, nsa_ck_w2,
              nsa_cv_pos, nsa_cv_w1, nsa_cv_w2, moba_w_in, moba_w_o,
              ffn_wg, ffn_wu, ffn_wd, ln1_g, ln1_b, ln2_g, ln2_b):
    for i in range(DEPTH):
        j = i // 2
        if i % 2 == 0:
            a = nsa_mixer(x, positions, nsa_w_in[j], nsa_w_o[j], nsa_ck_pos[j], nsa_ck_w1[j],
                          nsa_ck_w2[j], nsa_cv_pos[j], nsa_cv_w1[j], nsa_cv_w2[j])
        else:
            a = moba_mixer(x, positions, moba_w_in[j], moba_w_o[j])
        x = layer_norm(ALPHA * x + a, ln1_g[i], ln1_b[i])
        f = swiglu(x, ffn_wg[i], ffn_wu[i], ffn_wd[i])
        x = layer_norm(ALPHA * x + f, ln2_g[i], ln2_b[i])
    return x
```

```python
import functools

import numpy as np
import jax
import jax.numpy as jnp
from jax import lax
from jax.experimental import pallas as pl
from jax.experimental.pallas import tpu as pltpu

F32 = jnp.float32
BF16 = jnp.bfloat16

N_HEADS = 16
HEAD_DIM = 64
ROPE_DIM = HEAD_DIM // 4
ROPE_THETA = 500000.0
NSA_KV_HEADS = 4
NSA_GROUP = N_HEADS // NSA_KV_HEADS
CMP_BLOCK = 32
CMP_STRIDE = 16
SEL_BLOCK = 64
SEL_TOPK = 16
WINDOW = 512
MOBA_BLOCK = 256
MOBA_TOPK = 3
LN_EPS = 1e-5
LARGE = 1e30
SCALE = HEAD_DIM ** -0.5

LANES = 128
VMEM_LIMIT_BYTES = 48 * 1024 * 1024

ROW_TILE = 256
MM_TILE = 512
TQ = 256
TK = 512
BIAS_BLOCKS = 128
MASKED = -1e30
M_INIT = -5e29

assert (CMP_BLOCK, CMP_STRIDE, SEL_BLOCK) == (32, 16, 64)
assert WINDOW % TQ == 0 and TK % TQ == 0 and ROW_TILE == MOBA_BLOCK


def _cparams(*sem):
    return pltpu.CompilerParams(dimension_semantics=sem, vmem_limit_bytes=VMEM_LIMIT_BYTES)


def _dot(a, b):
    return jnp.dot(a, b, preferred_element_type=F32)


def _dot_t(a, b, precision=None):
    return lax.dot_general(a, b, (((1,), (1,)), ((), ())), preferred_element_type=F32,
                           precision=precision)


def _rope_tables_kernel(pos_ref, c_ref, cos_ref, sa_ref, sb_ref):
    ang = pos_ref[...].astype(F32) * c_ref[0:1, :]
    c = jnp.cos(ang)
    s = jnp.sin(ang)
    cos_ref[...] = jnp.where(c_ref[1:2, :] > 0, c, 1.0)
    sa_ref[...] = jnp.where(c_ref[2:3, :] > 0, s, 0.0)
    sb_ref[...] = jnp.where(c_ref[3:4, :] > 0, -s, 0.0)


def _rope_tables(positions):
    s = positions.shape[0]
    half = ROPE_DIM // 2
    inv = ROPE_THETA ** (-jnp.arange(0, ROPE_DIM, 2, dtype=F32) / ROPE_DIM)
    d = np.arange(LANES) % HEAD_DIM
    consts = jnp.zeros((8, LANES), F32)
    consts = consts.at[0].set(inv[d % half])
    consts = consts.at[1].set(jnp.asarray(d < ROPE_DIM, F32))
    consts = consts.at[2].set(jnp.asarray((d >= half) & (d < ROPE_DIM), F32))
    consts = consts.at[3].set(jnp.asarray(d < half, F32))
    tm = min(MM_TILE, s)
    tab = jax.ShapeDtypeStruct((s, LANES), F32)
    return pl.pallas_call(
        _rope_tables_kernel,
        grid=(s // tm,),
        in_specs=[pl.BlockSpec((tm, 1), lambda i: (i, 0)),
                  pl.BlockSpec((8, LANES), lambda i: (0, 0))],
        out_specs=[pl.BlockSpec((tm, LANES), lambda i: (i, 0))] * 3,
        out_shape=[tab, tab, tab],
        name="rope_tables",
        compiler_params=_cparams("parallel"),
    )(positions.reshape(s, 1), consts)


def _rope(x, cos_t, sin_a, sin_b):
    half = ROPE_DIM // 2
    outs = []
    for c in range(x.shape[1] // LANES):
        xs = x[:, c * LANES:(c + 1) * LANES]
        outs.append(xs * cos_t + pltpu.roll(xs, half, 1) * sin_a
                    + pltpu.roll(xs, LANES - half, 1) * sin_b)
    return outs[0] if len(outs) == 1 else jnp.concatenate(outs, axis=1)


def _ones_col(rows, width):
    return jnp.where(lax.broadcasted_iota(jnp.int32, (rows, width), 1) == 0, 1.0, 0.0).astype(BF16)


def _nsa_proj_kernel(x_ref, cos_ref, sa_ref, sb_ref, wq_ref, wkv_ref, wg_ref,
                     q_ref, cmp_ref, kaug_ref, vaug_ref, kwin_ref, vwin_ref, gate_ref):
    tm = x_ref.shape[0]
    g_n, hd, kvd = NSA_KV_HEADS, HEAD_DIM, NSA_KV_HEADS * HEAD_DIM
    xb = x_ref[...].astype(BF16)
    cos_t, sin_a, sin_b = cos_ref[...], sa_ref[...], sb_ref[...]
    q = _rope(_dot(xb, wq_ref[...]), cos_t, sin_a, sin_b) * SCALE
    q_ref[...] = q.astype(BF16)
    kv = _dot(xb, wkv_ref[...])
    k_cmp = _rope(kv[:, 0:kvd], cos_t, sin_a, sin_b)
    v_cmp = kv[:, kvd:2 * kvd]
    k_sel = _rope(kv[:, 2 * kvd:3 * kvd], cos_t, sin_a, sin_b)
    v_sel = kv[:, 3 * kvd:4 * kvd]
    k_win = _rope(kv[:, 4 * kvd:5 * kvd], cos_t, sin_a, sin_b)
    v_win = kv[:, 5 * kvd:6 * kvd]
    gate = jax.nn.sigmoid(_dot(xb, wg_ref[...]))
    tok = pl.program_id(0) * tm + lax.broadcasted_iota(jnp.int32, (tm, BIAS_BLOCKS), 0)
    blk = (tok // SEL_BLOCK) % BIAS_BLOCKS
    onehot = jnp.where(lax.broadcasted_iota(jnp.int32, (tm, BIAS_BLOCKS), 1) == blk,
                       1.0, 0.0).astype(BF16)
    ones_col = _ones_col(tm, LANES - hd)
    for g in range(g_n):
        sl = slice(g * hd, (g + 1) * hd)
        cmp_ref[0, g] = k_cmp[:, sl]
        cmp_ref[1, g] = v_cmp[:, sl]
        kwin_ref[g] = k_win[:, sl].astype(BF16)
        vwin_ref[g] = v_win[:, sl].astype(BF16)
        gate_ref[g] = gate[:, g * LANES:(g + 1) * LANES]
        kb = g * 2 * LANES
        kaug_ref[:, kb:kb + BIAS_BLOCKS] = onehot
        kaug_ref[:, kb + BIAS_BLOCKS:kb + BIAS_BLOCKS + hd] = k_sel[:, sl].astype(BF16)
        kaug_ref[:, kb + BIAS_BLOCKS + hd:kb + 2 * LANES] = jnp.zeros((tm, 2 * LANES - BIAS_BLOCKS - hd), BF16)
        vb = g * LANES
        vaug_ref[:, vb:vb + hd] = v_sel[:, sl].astype(BF16)
        vaug_ref[:, vb + hd:vb + LANES] = ones_col


def _nsa_proj(x, tabs, wq, wkv, wg):
    s, d = x.shape
    g_n, hd = NSA_KV_HEADS, HEAD_DIM
    tm = min(ROW_TILE, s)
    row = lambda i: (i, 0)
    whole = lambda i: (0, 0)
    out_shape = [
        jax.ShapeDtypeStruct((s, N_HEADS * hd), BF16),
        jax.ShapeDtypeStruct((2, g_n, s, hd), F32),
        jax.ShapeDtypeStruct((s, g_n * 2 * LANES), BF16),
        jax.ShapeDtypeStruct((s, g_n * LANES), BF16),
        jax.ShapeDtypeStruct((g_n, s, hd), BF16),
        jax.ShapeDtypeStruct((g_n, s, hd), BF16),
        jax.ShapeDtypeStruct((g_n, s, LANES), F32),
    ]
    out_specs = [
        pl.BlockSpec((tm, N_HEADS * hd), row),
        pl.BlockSpec((2, g_n, tm, hd), lambda i: (0, 0, i, 0)),
        pl.BlockSpec((tm, g_n * 2 * LANES), row),
        pl.BlockSpec((tm, g_n * LANES), row),
        pl.BlockSpec((g_n, tm, hd), lambda i: (0, i, 0)),
        pl.BlockSpec((g_n, tm, hd), lambda i: (0, i, 0)),
        pl.BlockSpec((g_n, tm, LANES), lambda i: (0, i, 0)),
    ]
    return pl.pallas_call(
        _nsa_proj_kernel,
        grid=(s // tm,),
        in_specs=[pl.BlockSpec((tm, d), row)] + [pl.BlockSpec((tm, LANES), row)] * 3
        + [pl.BlockSpec(wq.shape, whole), pl.BlockSpec(wkv.shape, whole),
           pl.BlockSpec(wg.shape, whole)],
        out_specs=out_specs,
        out_shape=out_shape,
        name="nsa_proj",
        compiler_params=_cparams("parallel"),
    )(x, *tabs, wq, wkv, wg)


def _compress_kernel(x_ref, pos_ref, w1_ref, w2_ref, out_ref):
    n16 = x_ref.shape[2]
    half = w1_ref.shape[1] // 2
    x = x_ref[0, 0]
    a = _dot((x + pos_ref[0, 0:1, :]).astype(BF16), w1_ref[0, 0:half, :])
    b = _dot((x + pos_ref[0, 1:2, :]).astype(BF16), w1_ref[0, half:, :])
    hid = a + pltpu.roll(b, n16 - 1, 0)
    out_ref[0, 0] = _dot(jax.nn.gelu(hid).astype(BF16), w2_ref[0])


def _compress(cmp_kv, pos, w1, w2):
    _, g_n, s, hd = cmp_kv.shape
    n16 = s // CMP_STRIDE
    x = cmp_kv.reshape(2, g_n, n16, CMP_STRIDE * hd)
    pos2 = pos.reshape(2, 2, CMP_STRIDE * hd)
    return pl.pallas_call(
        _compress_kernel,
        grid=(2, g_n),
        in_specs=[pl.BlockSpec((1, 1, n16, CMP_STRIDE * hd), lambda a, g: (a, g, 0, 0)),
                  pl.BlockSpec((1, 2, CMP_STRIDE * hd), lambda a, g: (a, 0, 0)),
                  pl.BlockSpec((1,) + w1.shape[1:], lambda a, g: (a, 0, 0)),
                  pl.BlockSpec((1,) + w2.shape[1:], lambda a, g: (a, 0, 0))],
        out_specs=pl.BlockSpec((1, 1, n16, hd), lambda a, g: (a, g, 0, 0)),
        out_shape=jax.ShapeDtypeStruct((2, g_n, n16, hd), F32),
        name="nsa_compress",
        compiler_params=_cparams("parallel", "parallel"),
    )(x, pos2, w1, w2)


def _top_k_mask(score, n_top):
    n = score.shape[1]
    lane = lax.broadcasted_iota(jnp.int32, score.shape, 1).astype(F32)
    sel = jnp.zeros(score.shape, jnp.bool_)
    for _ in range(n_top):
        m = jnp.max(score, axis=1, keepdims=True)
        idx = jnp.min(jnp.where(score == m, lane, float(n)), axis=1, keepdims=True)
        hit = lane == idx
        sel = sel | hit
        score = jnp.where(hit, -jnp.inf, score)
    return sel


def _nsa_local_kernel(n_sel, n_top, q_ref, kc_ref, vc_ref, *rest):
    n_wb = WINDOW // TQ + 1
    kw_refs = rest[:n_wb]
    vw_refs = rest[n_wb:2 * n_wb]
    gate_ref, pre_ref, qaug_ref = rest[2 * n_wb:]
    r_n, hd = NSA_GROUP, HEAD_DIM
    qi = pl.program_id(1)
    qg = q_ref[...]
    qs = jnp.concatenate([qg[:, r * hd:(r + 1) * hd] for r in range(r_n)], axis=0)
    rows = r_n * TQ
    tq = qi * TQ + lax.broadcasted_iota(jnp.int32, (rows, 1), 0) % TQ

    nc = kc_ref.shape[1]
    col = lax.broadcasted_iota(jnp.int32, (1, nc), 1)
    cmp_end = (col % n_sel) * SEL_BLOCK + (col // n_sel) * CMP_STRIDE + CMP_BLOCK - 1
    valid = (cmp_end <= tq) & (col < nc - 1)
    s = jnp.where(valid, _dot_t(qs, kc_ref[0]), -LARGE)
    m = jnp.max(s, axis=1, keepdims=True)
    p = jnp.where(valid, jnp.exp(s - m), 0.0)
    l = jnp.sum(p, axis=1, keepdims=True)
    p = p * (1.0 / jnp.where(l > 0, l, 1.0))
    o_cmp = _dot(p.astype(BF16), vc_ref[0])

    p3 = p[:, 3 * n_sel:4 * n_sel]
    blk = lax.broadcasted_iota(jnp.int32, (1, n_sel), 1)
    p3_prev = jnp.where(blk == 0, 0.0, pltpu.roll(p3, 1, 1))
    imp_h = 2.0 * (p[:, 0:n_sel] + p[:, n_sel:2 * n_sel] + p[:, 2 * n_sel:3 * n_sel]) + p3 + p3_prev
    imp = imp_h[0:TQ]
    for r in range(1, r_n):
        imp = imp + imp_h[r * TQ:(r + 1) * TQ]
    tq_q = qi * TQ + lax.broadcasted_iota(jnp.int32, (TQ, 1), 0)
    blk_q = tq_q // SEL_BLOCK
    forced = (blk == 0) | (blk == blk_q) | (blk == blk_q - 1)
    causal = blk * SEL_BLOCK <= tq_q
    score = jnp.where(forced, LARGE, jnp.where(causal, imp, -LARGE))
    bias = jnp.where(_top_k_mask(score, n_top), 0.0, MASKED).astype(BF16)
    n_var = qaug_ref.shape[0]
    if n_sel < n_var * BIAS_BLOCKS:
        bias = jnp.concatenate(
            [bias, jnp.full((TQ, n_var * BIAS_BLOCKS - n_sel), MASKED, BF16)], axis=1)
    zeros = jnp.zeros((TQ, 2 * LANES - BIAS_BLOCKS - hd), BF16)
    for v in range(n_var):
        for r in range(r_n):
            rs = slice(r * TQ, (r + 1) * TQ)
            qaug_ref[v, 0, 0, rs, 0:BIAS_BLOCKS] = bias[:, v * BIAS_BLOCKS:(v + 1) * BIAS_BLOCKS]
            qaug_ref[v, 0, 0, rs, BIAS_BLOCKS:BIAS_BLOCKS + hd] = qg[:, r * hd:(r + 1) * hd]
            qaug_ref[v, 0, 0, rs, BIAS_BLOCKS + hd:2 * LANES] = zeros

    kcat = jnp.concatenate([kw[0] for kw in kw_refs], axis=0)
    vcat = jnp.concatenate([vw[0] for vw in vw_refs], axis=0)
    kpos = (qi - (n_wb - 1)) * TQ + lax.broadcasted_iota(jnp.int32, (1, n_wb * TQ), 1)
    dist = tq - kpos
    mw = (kpos >= 0) & (dist >= 0) & (dist < WINDOW)
    s = jnp.where(mw, _dot_t(qs, kcat), -LARGE)
    p = jnp.exp(s - jnp.max(s, axis=1, keepdims=True))
    o_win = _dot(p.astype(BF16), vcat) * (1.0 / jnp.sum(p, axis=1, keepdims=True))

    gate = gate_ref[0]
    for r in range(r_n):
        rs = slice(r * TQ, (r + 1) * TQ)
        pre_ref[:, r * hd:(r + 1) * hd] = (gate[:, 3 * r:3 * r + 1] * o_cmp[rs]
                                          + gate[:, 3 * r + 2:3 * r + 3] * o_win[rs])


def _nsa_local(q, kc, vc, kwin, vwin, gates, n_sel, n_var):
    s = q.shape[0]
    g_n, r_n, hd = NSA_KV_HEADS, NSA_GROUP, HEAD_DIM
    n_q = s // TQ
    n_wb = WINDOW // TQ + 1
    n_top = min(SEL_TOPK, n_sel)
    nc = kc.shape[1]

    def win_spec(j):
        return pl.BlockSpec((1, TQ, hd), lambda g, i: (g, jnp.maximum(i - (n_wb - 1) + j, 0), 0))

    return pl.pallas_call(
        functools.partial(_nsa_local_kernel, n_sel, n_top),
        grid=(g_n, n_q),
        in_specs=[pl.BlockSpec((TQ, r_n * hd), lambda g, i: (i, g)),
                  pl.BlockSpec((1, nc, hd), lambda g, i: (g, 0, 0)),
                  pl.BlockSpec((1, nc, hd), lambda g, i: (g, 0, 0))]
        + [win_spec(j) for j in range(n_wb)] * 2
        + [pl.BlockSpec((1, TQ, LANES), lambda g, i: (g, i, 0))],
        out_specs=[pl.BlockSpec((TQ, r_n * hd), lambda g, i: (i, g)),
                   pl.BlockSpec((n_var, 1, 1, r_n * TQ, 2 * LANES), lambda g, i: (0, g, i, 0, 0))],
        out_shape=[jax.ShapeDtypeStruct((s, N_HEADS * hd), F32),
                   jax.ShapeDtypeStruct((n_var, g_n, n_q, r_n * TQ, 2 * LANES), BF16)],
        name="nsa_local",
        compiler_params=_cparams("parallel", "parallel"),
    )(q, kc, vc, *([kwin] * n_wb), *([vwin] * n_wb), gates)


def _causal_pairs(n_q):
    qs, ks = [], []
    for qi in range(n_q):
        for ki in range(((qi + 1) * TQ - 1) // TK + 1):
            qs.append(qi)
            ks.append(ki)
    return jnp.asarray(qs, jnp.int32), jnp.asarray(ks, jnp.int32)


def _flash_update(q, k, v, m_ref, acc_ref, idx, causal_mask):
    s = _dot_t(q, k)
    if causal_mask is not None:
        s = jnp.where(causal_mask, s, MASKED)
    m_prev = m_ref[idx]
    m_new = jnp.maximum(m_prev, jnp.max(s, axis=1, keepdims=True))
    p = jnp.exp(s - m_new)
    acc_ref[idx] = jnp.exp(m_prev - m_new) * acc_ref[idx] + _dot(p.astype(BF16), v)
    m_ref[idx] = m_new


def _flash_steps(qi, ki, rows_per_q, n_batch, operands, m_ref, acc_ref, finalize):
    last_k = ((qi + 1) * TQ - 1) // TK

    @pl.when(ki == 0)
    def _():
        m_ref[...] = jnp.full(m_ref.shape, M_INIT, F32)
        acc_ref[...] = jnp.zeros(acc_ref.shape, F32)

    @pl.when(ki < last_k)
    def _():
        for b in range(n_batch):
            q, k, v = operands(b)
            _flash_update(q, k, v, m_ref, acc_ref, b, None)

    @pl.when(ki == last_k)
    def _():
        rows = rows_per_q * TQ
        tq = qi * TQ + lax.broadcasted_iota(jnp.int32, (rows, 1), 0) % TQ
        kpos = ki * TK + lax.broadcasted_iota(jnp.int32, (1, TK), 1)
        mask = kpos <= tq
        for b in range(n_batch):
            q, k, v = operands(b)
            _flash_update(q, k, v, m_ref, acc_ref, b, mask)
        finalize()


def _nsa_sel_kernel(qtab, ktab, qaug_ref, kaug_ref, vaug_ref, pre_ref, gate_ref, o_ref,
                    m_ref, acc_ref):
    g_n, r_n, hd = NSA_KV_HEADS, NSA_GROUP, HEAD_DIM
    t = pl.program_id(0)
    qi, ki = qtab[t], ktab[t]

    def operands(g):
        return (qaug_ref[0, g, 0], kaug_ref[:, g * 2 * LANES:(g + 1) * 2 * LANES],
                vaug_ref[:, g * LANES:(g + 1) * LANES])

    def finalize():
        for g in range(g_n):
            acc = acc_ref[g]
            gate = gate_ref[g]
            o = acc[:, 0:hd] * (1.0 / acc[:, hd:hd + 1])
            for r in range(r_n):
                c = (g * r_n + r) * hd
                o_ref[:, c:c + hd] = (pre_ref[:, c:c + hd] + gate[:, 3 * r + 1:3 * r + 2]
                                      * o[r * TQ:(r + 1) * TQ]).astype(o_ref.dtype)

    _flash_steps(qi, ki, r_n, g_n, operands, m_ref, acc_ref, finalize)


def _nsa_sel(qaug, kaug, vaug, pre, gates):
    n_var, g_n, n_q, rows, _ = qaug.shape
    s = kaug.shape[0]
    qtab, ktab = _causal_pairs(n_q)
    tiles_per_var = BIAS_BLOCKS * SEL_BLOCK // TK
    grid_spec = pltpu.PrefetchScalarGridSpec(
        num_scalar_prefetch=2,
        grid=(qtab.shape[0],),
        in_specs=[
            pl.BlockSpec((1, g_n, 1, rows, 2 * LANES),
                         lambda t, qt, kt: (kt[t] // tiles_per_var, 0, qt[t], 0, 0)),
            pl.BlockSpec((TK, g_n * 2 * LANES), lambda t, qt, kt: (kt[t], 0)),
            pl.BlockSpec((TK, g_n * LANES), lambda t, qt, kt: (kt[t], 0)),
            pl.BlockSpec((TQ, N_HEADS * HEAD_DIM), lambda t, qt, kt: (qt[t], 0)),
            pl.BlockSpec((g_n, TQ, LANES), lambda t, qt, kt: (0, qt[t], 0)),
        ],
        out_specs=pl.BlockSpec((TQ, N_HEADS * HEAD_DIM), lambda t, qt, kt: (qt[t], 0)),
        scratch_shapes=[pltpu.VMEM((g_n, rows, 1), F32), pltpu.VMEM((g_n, rows, LANES), F32)],
    )
    return pl.pallas_call(
        _nsa_sel_kernel,
        grid_spec=grid_spec,
        out_shape=jax.ShapeDtypeStruct((s, N_HEADS * HEAD_DIM), BF16),
        name="nsa_sel",
        compiler_params=_cparams("arbitrary"),
    )(qtab, ktab, qaug, kaug, vaug, pre, gates)


def _moba_proj_kernel(n_blk, k_eff, x_ref, cos_ref, sa_ref, sb_ref, w_ref,
                      qaug_ref, kaug_ref, vaug_ref, kmean_ref):
    tm = x_ref.shape[0]
    hd, d = HEAD_DIM, N_HEADS * HEAD_DIM
    i = pl.program_id(0)

    @pl.when(i == 0)
    def _():
        kmean_ref[...] = jnp.zeros(kmean_ref.shape, F32)

    xb = x_ref[...].astype(BF16)
    cos_t, sin_a, sin_b = cos_ref[...], sa_ref[...], sb_ref[...]
    q = _rope(_dot(xb, w_ref[:, 0:d]), cos_t, sin_a, sin_b)
    k = _rope(_dot(xb, w_ref[:, d:2 * d]), cos_t, sin_a, sin_b)
    v = _dot(xb, w_ref[:, 2 * d:3 * d])
    kmean = kmean_ref[...]
    nb = kmean.shape[0]
    blk = lax.broadcasted_iota(jnp.int32, (1, nb), 1)
    blk_f = blk.astype(F32)
    own = jnp.where(blk == i, 1.0, 0.0).astype(BF16)
    own_onehot = jnp.broadcast_to(own, (tm, nb))
    ones_col = _ones_col(tm, LANES - hd)
    for h in range(N_HEADS):
        sl = slice(h * hd, (h + 1) * hd)
        gate = _dot_t(q[:, sl], kmean[:, sl], precision=lax.Precision.HIGHEST)
        gate = jnp.where(blk < i, gate, -LARGE)
        sel = jnp.zeros(gate.shape, jnp.bool_)
        for _ in range(k_eff):
            m = jnp.max(gate, axis=1, keepdims=True)
            idx = jnp.min(jnp.where(gate == m, blk_f, float(nb)), axis=1, keepdims=True)
            hit = blk_f == idx
            sel = sel | (hit & (m > -LARGE))
            gate = jnp.where(hit, -jnp.inf, gate)
        bias = jnp.where(sel | (blk == i), 0.0, MASKED).astype(BF16)
        qaug_ref[h, :, 0:hd] = (q[:, sl] * SCALE).astype(BF16)
        qaug_ref[h, :, hd:hd + nb] = bias
        kaug_ref[h, :, 0:hd] = k[:, sl].astype(BF16)
        kaug_ref[h, :, hd:hd + nb] = own_onehot
        vaug_ref[h, :, 0:hd] = v[:, sl].astype(BF16)
        vaug_ref[h, :, hd:LANES] = ones_col
    kmean_ref[pl.ds(i, 1), :] = jnp.mean(k, axis=0, keepdims=True)


def _moba_proj(x, tabs, w):
    s, d = x.shape
    n_blk = s // MOBA_BLOCK
    k_eff = min(MOBA_TOPK, max(n_blk - 1, 1))
    nb = LANES - HEAD_DIM
    assert s % MOBA_BLOCK == 0 and n_blk <= nb
    tm = MOBA_BLOCK
    row = lambda i: (i, 0)
    head = jax.ShapeDtypeStruct((N_HEADS, s, LANES), BF16)
    head_spec = pl.BlockSpec((N_HEADS, tm, LANES), lambda i: (0, i, 0))
    return pl.pallas_call(
        functools.partial(_moba_proj_kernel, n_blk, k_eff),
        grid=(s // tm,),
        in_specs=[pl.BlockSpec((tm, d), row)] + [pl.BlockSpec((tm, LANES), row)] * 3
        + [pl.BlockSpec(w.shape, lambda i: (0, 0))],
        out_specs=[head_spec] * 3,
        out_shape=[head] * 3,
        scratch_shapes=[pltpu.VMEM((nb, N_HEADS * HEAD_DIM), F32)],
        name="moba_proj",
        compiler_params=_cparams("arbitrary"),
    )(x, *tabs, w)


def _moba_attn_kernel(qtab, ktab, q_ref, k_ref, v_ref, o_ref, m_ref, acc_ref):
    hd = HEAD_DIM
    t = pl.program_id(0)
    qi, ki = qtab[t], ktab[t]

    def operands(h):
        return q_ref[h], k_ref[h], v_ref[h]

    def finalize():
        for h in range(N_HEADS):
            acc = acc_ref[h]
            o_ref[:, h * hd:(h + 1) * hd] = (acc[:, 0:hd] * (1.0 / acc[:, hd:hd + 1])).astype(o_ref.dtype)

    _flash_steps(qi, ki, 1, N_HEADS, operands, m_ref, acc_ref, finalize)


def _moba_attn(qaug, kaug, vaug):
    h_n, s, _ = qaug.shape
    qtab, ktab = _causal_pairs(s // TQ)
    grid_spec = pltpu.PrefetchScalarGridSpec(
        num_scalar_prefetch=2,
        grid=(qtab.shape[0],),
        in_specs=[pl.BlockSpec((h_n, TQ, LANES), lambda t, qt, kt: (0, qt[t], 0)),
                  pl.BlockSpec((h_n, TK, LANES), lambda t, qt, kt: (0, kt[t], 0)),
                  pl.BlockSpec((h_n, TK, LANES), lambda t, qt, kt: (0, kt[t], 0))],
        out_specs=pl.BlockSpec((TQ, h_n * HEAD_DIM), lambda t, qt, kt: (qt[t], 0)),
        scratch_shapes=[pltpu.VMEM((h_n, TQ, 1), F32), pltpu.VMEM((h_n, TQ, LANES), F32)],
    )
    return pl.pallas_call(
        _moba_attn_kernel,
        grid_spec=grid_spec,
        out_shape=jax.ShapeDtypeStruct((s, h_n * HEAD_DIM), BF16),
        name="moba_attn",
        compiler_params=_cparams("arbitrary"),
    )(qtab, ktab, qaug, kaug, vaug)


def _mm_res_ln_kernel(alpha, a_ref, w_ref, x_ref, g_ref, b_ref, o_ref):
    z = alpha * x_ref[...] + _dot(a_ref[...], w_ref[...])
    mu = jnp.mean(z, axis=-1, keepdims=True)
    zc = z - mu
    var = jnp.mean(zc * zc, axis=-1, keepdims=True)
    o_ref[...] = zc * lax.rsqrt(var + LN_EPS) * g_ref[...] + b_ref[...]


def _mm_res_ln(a, w, x, gain, bias, alpha):
    s, k = a.shape
    d = w.shape[1]
    tm = min(MM_TILE, s)
    row = lambda i: (i, 0)
    whole = lambda i: (0, 0)
    return pl.pallas_call(
        functools.partial(_mm_res_ln_kernel, alpha),
        grid=(s // tm,),
        in_specs=[pl.BlockSpec((tm, k), row), pl.BlockSpec((k, d), whole),
                  pl.BlockSpec((tm, d), row), pl.BlockSpec((1, d), whole), pl.BlockSpec((1, d), whole)],
        out_specs=pl.BlockSpec((tm, d), row),
        out_shape=jax.ShapeDtypeStruct((s, d), F32),
        name="mm_res_ln",
        compiler_params=_cparams("parallel"),
    )(a, w, x, gain.reshape(1, d), bias.reshape(1, d))


def _ffn_up_kernel(n_chunk, x_ref, wg_ref, wu_ref, o_ref):
    xb = x_ref[...].astype(BF16)
    cw = wg_ref.shape[1] // n_chunk
    for c in range(n_chunk):
        sl = slice(c * cw, (c + 1) * cw)
        o_ref[:, sl] = (jax.nn.silu(_dot(xb, wg_ref[:, sl])) * _dot(xb, wu_ref[:, sl])).astype(o_ref.dtype)


def _ffn_up(x, wg, wu):
    s, d = x.shape
    f = wg.shape[1]
    n_chunk = 2 if f % (2 * LANES) == 0 else 1
    tm = min(MM_TILE, s)
    row = lambda i: (i, 0)
    whole = lambda i: (0, 0)
    return pl.pallas_call(
        functools.partial(_ffn_up_kernel, n_chunk),
        grid=(s // tm,),
        in_specs=[pl.BlockSpec((tm, d), row), pl.BlockSpec((d, f), whole), pl.BlockSpec((d, f), whole)],
        out_specs=pl.BlockSpec((tm, f), row),
        out_shape=jax.ShapeDtypeStruct((s, f), BF16),
        name="ffn_up",
        compiler_params=_cparams("parallel"),
    )(x, wg, wu)


def _nsa_mixer(x, tabs, w_in, ck_pos, ck_w1, ck_w2, cv_pos, cv_w1, cv_w2):
    s = x.shape[0]
    g_n, r_n, hd = NSA_KV_HEADS, NSA_GROUP, HEAD_DIM
    q_dim, kv_dim = N_HEADS * hd, NSA_KV_HEADS * hd
    n_sel = s // SEL_BLOCK
    n_var = -(-n_sel // BIAS_BLOCKS)
    assert s % TK == 0 and (BIAS_BLOCKS * SEL_BLOCK) % TK == 0
    wq = w_in[:, :q_dim].astype(BF16)
    wkv = w_in[:, q_dim:q_dim + 6 * kv_dim].astype(BF16)
    wg = w_in[:, q_dim + 6 * kv_dim:].reshape(-1, g_n, 3 * r_n)
    wg = jnp.pad(wg, ((0, 0), (0, 0), (0, LANES - 3 * r_n))).reshape(-1, g_n * LANES).astype(BF16)
    q, cmp_kv, kaug, vaug, kwin, vwin, gates = _nsa_proj(x, tabs, wq, wkv, wg)
    comp = _compress(cmp_kv, jnp.stack([ck_pos, cv_pos]),
                     jnp.stack([ck_w1, cv_w1]).astype(BF16), jnp.stack([ck_w2, cv_w2]).astype(BF16))
    comp = comp.reshape(2, g_n, n_sel, 4, hd).transpose(0, 1, 3, 2, 4).reshape(2, g_n, 4 * n_sel, hd)
    comp = comp.astype(BF16)
    pre, qaug = _nsa_local(q, comp[0], comp[1], kwin, vwin, gates, n_sel, n_var)
    return _nsa_sel(qaug, kaug, vaug, pre, gates)


def _moba_mixer(x, tabs, w_in):
    qaug, kaug, vaug = _moba_proj(x, tabs, w_in.astype(BF16))
    return _moba_attn(qaug, kaug, vaug)


def kernel(x, positions, nsa_w_in, nsa_w_o, nsa_ck_pos, nsa_ck_w1, nsa_ck_w2, nsa_cv_pos, nsa_cv_w1, nsa_cv_w2, moba_w_in, moba_w_o, ffn_wg, ffn_wu, ffn_wd, ln1_g, ln1_b, ln2_g, ln2_b):
    b, s, d = x.shape
    depth = ffn_wg.shape[0]
    alpha = (2 * depth) ** 0.25
    outs = []
    for bi in range(b):
        h = x[bi]
        tabs = _rope_tables(positions[bi])
        for i in range(depth):
            j = i // 2
            if i % 2 == 0:
                a = _nsa_mixer(h, tabs, nsa_w_in[j], nsa_ck_pos[j], nsa_ck_w1[j], nsa_ck_w2[j],
                               nsa_cv_pos[j], nsa_cv_w1[j], nsa_cv_w2[j])
                w_o = nsa_w_o[j]
            else:
                a = _moba_mixer(h, tabs, moba_w_in[j])
                w_o = moba_w_o[j]
            h = _mm_res_ln(a, w_o.astype(BF16), h, ln1_g[i], ln1_b[i], alpha)
            u = _ffn_up(h, ffn_wg[i].astype(BF16), ffn_wu[i].astype(BF16))
            h = _mm_res_ln(u, ffn_wd[i].astype(BF16), h, ln2_g[i], ln2_b[i], alpha)
        outs.append(h)
    return jnp.stack(outs)
```

```python
import functools

import numpy as np
import jax
import jax.numpy as jnp
from jax import lax
from jax.experimental import pallas as pl
from jax.experimental.pallas import tpu as pltpu

F32 = jnp.float32
BF16 = jnp.bfloat16

N_HEADS = 16
HEAD_DIM = 64
ROPE_DIM = HEAD_DIM // 4
ROPE_THETA = 500000.0
NSA_KV_HEADS = 4
NSA_GROUP = N_HEADS // NSA_KV_HEADS
CMP_BLOCK = 32
CMP_STRIDE = 16
SEL_BLOCK = 64
SEL_TOPK = 16
WINDOW = 512
MOBA_BLOCK = 256
MOBA_TOPK = 3
LN_EPS = 1e-5
LARGE = 1e30
SCALE = HEAD_DIM ** -0.5

LANES = 128
VMEM_LIMIT_BYTES = 48 * 1024 * 1024

ROW_TILE = 256
MM_TILE = 512
TQ = 256
TK = 512
MOBA_TQ = 1024
MOBA_TK = 512
BIAS_BLOCKS = 128
MASKED = -1e30
M_INIT = -5e29

assert (CMP_BLOCK, CMP_STRIDE, SEL_BLOCK) == (32, 16, 64)
assert WINDOW % TQ == 0 and TK % TQ == 0 and ROW_TILE == MOBA_BLOCK


def _cparams(*sem):
    return pltpu.CompilerParams(dimension_semantics=sem, vmem_limit_bytes=VMEM_LIMIT_BYTES)


def _dot(a, b):
    return jnp.dot(a, b, preferred_element_type=F32)


def _dot_t(a, b, precision=None):
    return lax.dot_general(a, b, (((1,), (1,)), ((), ())), preferred_element_type=F32,
                           precision=precision)


def _rope_tables_kernel(pos_ref, c_ref, cos_ref, sa_ref, sb_ref):
    ang = pos_ref[...].astype(F32) * c_ref[0:1, :]
    c = jnp.cos(ang)
    s = jnp.sin(ang)
    cos_ref[...] = jnp.where(c_ref[1:2, :] > 0, c, 1.0)
    sa_ref[...] = jnp.where(c_ref[2:3, :] > 0, s, 0.0)
    sb_ref[...] = jnp.where(c_ref[3:4, :] > 0, -s, 0.0)


def _rope_tables(positions):
    s = positions.shape[0]
    half = ROPE_DIM // 2
    inv = ROPE_THETA ** (-jnp.arange(0, ROPE_DIM, 2, dtype=F32) / ROPE_DIM)
    d = np.arange(LANES) % HEAD_DIM
    consts = jnp.zeros((8, LANES), F32)
    consts = consts.at[0].set(inv[d % half])
    consts = consts.at[1].set(jnp.asarray(d < ROPE_DIM, F32))
    consts = consts.at[2].set(jnp.asarray((d >= half) & (d < ROPE_DIM), F32))
    consts = consts.at[3].set(jnp.asarray(d < half, F32))
    tm = min(MM_TILE, s)
    tab = jax.ShapeDtypeStruct((s, LANES), F32)
    return pl.pallas_call(
        _rope_tables_kernel,
        grid=(s // tm,),
        in_specs=[pl.BlockSpec((tm, 1), lambda i: (i, 0)),
                  pl.BlockSpec((8, LANES), lambda i: (0, 0))],
        out_specs=[pl.BlockSpec((tm, LANES), lambda i: (i, 0))] * 3,
        out_shape=[tab, tab, tab],
        name="rope_tables",
        compiler_params=_cparams("parallel"),
    )(positions.reshape(s, 1), consts)


def _rope(x, cos_t, sin_a, sin_b):
    half = ROPE_DIM // 2
    outs = []
    for c in range(x.shape[1] // LANES):
        xs = x[:, c * LANES:(c + 1) * LANES]
        outs.append(xs * cos_t + pltpu.roll(xs, half, 1) * sin_a
                    + pltpu.roll(xs, LANES - half, 1) * sin_b)
    return outs[0] if len(outs) == 1 else jnp.concatenate(outs, axis=1)


def _ones_col(rows, width):
    return jnp.where(lax.broadcasted_iota(jnp.int32, (rows, width), 1) == 0, 1.0, 0.0).astype(BF16)


def _nsa_proj_kernel(x_ref, cos_ref, sa_ref, sb_ref, wq_ref, wkv_ref, wg_ref,
                     q_ref, cmp_ref, kaug_ref, vaug_ref, kwin_ref, vwin_ref, gate_ref):
    tm = x_ref.shape[0]
    g_n, hd, kvd = NSA_KV_HEADS, HEAD_DIM, NSA_KV_HEADS * HEAD_DIM
    xb = x_ref[...].astype(BF16)
    cos_t, sin_a, sin_b = cos_ref[...], sa_ref[...], sb_ref[...]
    q = _rope(_dot(xb, wq_ref[...]), cos_t, sin_a, sin_b) * SCALE
    q_ref[...] = q.astype(BF16)
    kv = _dot(xb, wkv_ref[...])
    k_cmp = _rope(kv[:, 0:kvd], cos_t, sin_a, sin_b)
    v_cmp = kv[:, kvd:2 * kvd]
    k_sel = _rope(kv[:, 2 * kvd:3 * kvd], cos_t, sin_a, sin_b)
    v_sel = kv[:, 3 * kvd:4 * kvd]
    k_win = _rope(kv[:, 4 * kvd:5 * kvd], cos_t, sin_a, sin_b)
    v_win = kv[:, 5 * kvd:6 * kvd]
    gate = jax.nn.sigmoid(_dot(xb, wg_ref[...]))
    tok = pl.program_id(0) * tm + lax.broadcasted_iota(jnp.int32, (tm, BIAS_BLOCKS), 0)
    blk = (tok // SEL_BLOCK) % BIAS_BLOCKS
    onehot = jnp.where(lax.broadcasted_iota(jnp.int32, (tm, BIAS_BLOCKS), 1) == blk,
                       1.0, 0.0).astype(BF16)
    ones_col = _ones_col(tm, LANES - hd)
    for g in range(g_n):
        sl = slice(g * hd, (g + 1) * hd)
        cmp_ref[0, g] = k_cmp[:, sl]
        cmp_ref[1, g] = v_cmp[:, sl]
        kwin_ref[g] = k_win[:, sl].astype(BF16)
        vwin_ref[g] = v_win[:, sl].astype(BF16)
        gate_ref[g] = gate[:, g * LANES:(g + 1) * LANES]
        kb = g * 2 * LANES
        kaug_ref[:, kb:kb + BIAS_BLOCKS] = onehot
        kaug_ref[:, kb + BIAS_BLOCKS:kb + BIAS_BLOCKS + hd] = k_sel[:, sl].astype(BF16)
        kaug_ref[:, kb + BIAS_BLOCKS + hd:kb + 2 * LANES] = jnp.zeros((tm, 2 * LANES - BIAS_BLOCKS - hd), BF16)
        vb = g * LANES
        vaug_ref[:, vb:vb + hd] = v_sel[:, sl].astype(BF16)
        vaug_ref[:, vb + hd:vb + LANES] = ones_col


def _nsa_proj(x, tabs, wq, wkv, wg):
    s, d = x.shape
    g_n, hd = NSA_KV_HEADS, HEAD_DIM
    tm = min(ROW_TILE, s)
    row = lambda i: (i, 0)
    whole = lambda i: (0, 0)
    out_shape = [
        jax.ShapeDtypeStruct((s, N_HEADS * hd), BF16),
        jax.ShapeDtypeStruct((2, g_n, s, hd), F32),
        jax.ShapeDtypeStruct((s, g_n * 2 * LANES), BF16),
        jax.ShapeDtypeStruct((s, g_n * LANES), BF16),
        jax.ShapeDtypeStruct((g_n, s, hd), BF16),
        jax.ShapeDtypeStruct((g_n, s, hd), BF16),
        jax.ShapeDtypeStruct((g_n, s, LANES), F32),
    ]
    out_specs = [
        pl.BlockSpec((tm, N_HEADS * hd), row),
        pl.BlockSpec((2, g_n, tm, hd), lambda i: (0, 0, i, 0)),
        pl.BlockSpec((tm, g_n * 2 * LANES), row),
        pl.BlockSpec((tm, g_n * LANES), row),
        pl.BlockSpec((g_n, tm, hd), lambda i: (0, i, 0)),
        pl.BlockSpec((g_n, tm, hd), lambda i: (0, i, 0)),
        pl.BlockSpec((g_n, tm, LANES), lambda i: (0, i, 0)),
    ]
    return pl.pallas_call(
        _nsa_proj_kernel,
        grid=(s // tm,),
        in_specs=[pl.BlockSpec((tm, d), row)] + [pl.BlockSpec((tm, LANES), row)] * 3
        + [pl.BlockSpec(wq.shape, whole), pl.BlockSpec(wkv.shape, whole),
           pl.BlockSpec(wg.shape, whole)],
        out_specs=out_specs,
        out_shape=out_shape,
        name="nsa_proj",
        compiler_params=_cparams("parallel"),
    )(x, *tabs, wq, wkv, wg)


def _compress_kernel(x_ref, pos_ref, w1_ref, w2_ref, out_ref):
    n16 = x_ref.shape[2]
    half = w1_ref.shape[1] // 2
    x = x_ref[0, 0]
    a = _dot((x + pos_ref[0, 0:1, :]).astype(BF16), w1_ref[0, 0:half, :])
    b = _dot((x + pos_ref[0, 1:2, :]).astype(BF16), w1_ref[0, half:, :])
    hid = a + pltpu.roll(b, n16 - 1, 0)
    out_ref[0, 0] = _dot(jax.nn.gelu(hid).astype(BF16), w2_ref[0])


def _compress(cmp_kv, pos, w1, w2):
    _, g_n, s, hd = cmp_kv.shape
    n16 = s // CMP_STRIDE
    x = cmp_kv.reshape(2, g_n, n16, CMP_STRIDE * hd)
    pos2 = pos.reshape(2, 2, CMP_STRIDE * hd)
    return pl.pallas_call(
        _compress_kernel,
        grid=(2, g_n),
        in_specs=[pl.BlockSpec((1, 1, n16, CMP_STRIDE * hd), lambda a, g: (a, g, 0, 0)),
                  pl.BlockSpec((1, 2, CMP_STRIDE * hd), lambda a, g: (a, 0, 0)),
                  pl.BlockSpec((1,) + w1.shape[1:], lambda a, g: (a, 0, 0)),
                  pl.BlockSpec((1,) + w2.shape[1:], lambda a, g: (a, 0, 0))],
        out_specs=pl.BlockSpec((1, 1, n16, hd), lambda a, g: (a, g, 0, 0)),
        out_shape=jax.ShapeDtypeStruct((2, g_n, n16, hd), F32),
        name="nsa_compress",
        compiler_params=_cparams("parallel", "parallel"),
    )(x, pos2, w1, w2)


def _top_k_mask(score, n_top):
    n = score.shape[1]
    lane = lax.broadcasted_iota(jnp.int32, score.shape, 1).astype(F32)
    sel = jnp.zeros(score.shape, jnp.bool_)
    for _ in range(n_top):
        m = jnp.max(score, axis=1, keepdims=True)
        idx = jnp.min(jnp.where(score == m, lane, float(n)), axis=1, keepdims=True)
        hit = lane == idx
        sel = sel | hit
        score = jnp.where(hit, -jnp.inf, score)
    return sel


def _nsa_local_kernel(n_sel, n_top, q_ref, kc_ref, vc_ref, *rest):
    n_wb = WINDOW // TQ + 1
    kw_refs = rest[:n_wb]
    vw_refs = rest[n_wb:2 * n_wb]
    gate_ref, pre_ref, qaug_ref = rest[2 * n_wb:]
    r_n, hd = NSA_GROUP, HEAD_DIM
    qi = pl.program_id(1)
    qg = q_ref[...]
    qs = jnp.concatenate([qg[:, r * hd:(r + 1) * hd] for r in range(r_n)], axis=0)
    rows = r_n * TQ
    tq = qi * TQ + lax.broadcasted_iota(jnp.int32, (rows, 1), 0) % TQ

    nc = kc_ref.shape[1]
    col = lax.broadcasted_iota(jnp.int32, (1, nc), 1)
    cmp_end = (col % n_sel) * SEL_BLOCK + (col // n_sel) * CMP_STRIDE + CMP_BLOCK - 1
    valid = (cmp_end <= tq) & (col < nc - 1)
    s = jnp.where(valid, _dot_t(qs, kc_ref[0]), -LARGE)
    m = jnp.max(s, axis=1, keepdims=True)
    p = jnp.where(valid, jnp.exp(s - m), 0.0)
    l = jnp.sum(p, axis=1, keepdims=True)
    p = p * (1.0 / jnp.where(l > 0, l, 1.0))
    o_cmp = _dot(p.astype(BF16), vc_ref[0])

    p3 = p[:, 3 * n_sel:4 * n_sel]
    blk = lax.broadcasted_iota(jnp.int32, (1, n_sel), 1)
    p3_prev = jnp.where(blk == 0, 0.0, pltpu.roll(p3, 1, 1))
    imp_h = 2.0 * (p[:, 0:n_sel] + p[:, n_sel:2 * n_sel] + p[:, 2 * n_sel:3 * n_sel]) + p3 + p3_prev
    imp = imp_h[0:TQ]
    for r in range(1, r_n):
        imp = imp + imp_h[r * TQ:(r + 1) * TQ]
    tq_q = qi * TQ + lax.broadcasted_iota(jnp.int32, (TQ, 1), 0)
    blk_q = tq_q // SEL_BLOCK
    forced = (blk == 0) | (blk == blk_q) | (blk == blk_q - 1)
    causal = blk * SEL_BLOCK <= tq_q
    score = jnp.where(forced, LARGE, jnp.where(causal, imp, -LARGE))
    bias = jnp.where(_top_k_mask(score, n_top), 0.0, MASKED).astype(BF16)
    n_var = qaug_ref.shape[0]
    if n_sel < n_var * BIAS_BLOCKS:
        bias = jnp.concatenate(
            [bias, jnp.full((TQ, n_var * BIAS_BLOCKS - n_sel), MASKED, BF16)], axis=1)
    zeros = jnp.zeros((TQ, 2 * LANES - BIAS_BLOCKS - hd), BF16)
    for v in range(n_var):
        for r in range(r_n):
            rs = slice(r * TQ, (r + 1) * TQ)
            qaug_ref[v, 0, 0, rs, 0:BIAS_BLOCKS] = bias[:, v * BIAS_BLOCKS:(v + 1) * BIAS_BLOCKS]
            qaug_ref[v, 0, 0, rs, BIAS_BLOCKS:BIAS_BLOCKS + hd] = qg[:, r * hd:(r + 1) * hd]
            qaug_ref[v, 0, 0, rs, BIAS_BLOCKS + hd:2 * LANES] = zeros

    kcat = jnp.concatenate([kw[0] for kw in kw_refs], axis=0)
    vcat = jnp.concatenate([vw[0] for vw in vw_refs], axis=0)
    kpos = (qi - (n_wb - 1)) * TQ + lax.broadcasted_iota(jnp.int32, (1, n_wb * TQ), 1)
    dist = tq - kpos
    mw = (kpos >= 0) & (dist >= 0) & (dist < WINDOW)
    s = jnp.where(mw, _dot_t(qs, kcat), -LARGE)
    p = jnp.exp(s - jnp.max(s, axis=1, keepdims=True))
    o_win = _dot(p.astype(BF16), vcat) * (1.0 / jnp.sum(p, axis=1, keepdims=True))

    gate = gate_ref[0]
    for r in range(r_n):
        rs = slice(r * TQ, (r + 1) * TQ)
        pre_ref[:, r * hd:(r + 1) * hd] = (gate[:, 3 * r:3 * r + 1] * o_cmp[rs]
                                          + gate[:, 3 * r + 2:3 * r + 3] * o_win[rs])


def _nsa_local(q, kc, vc, kwin, vwin, gates, n_sel, n_var):
    s = q.shape[0]
    g_n, r_n, hd = NSA_KV_HEADS, NSA_GROUP, HEAD_DIM
    n_q = s // TQ
    n_wb = WINDOW // TQ + 1
    n_top = min(SEL_TOPK, n_sel)
    nc = kc.shape[1]

    def win_spec(j):
        return pl.BlockSpec((1, TQ, hd), lambda g, i: (g, jnp.maximum(i - (n_wb - 1) + j, 0), 0))

    return pl.pallas_call(
        functools.partial(_nsa_local_kernel, n_sel, n_top),
        grid=(g_n, n_q),
        in_specs=[pl.BlockSpec((TQ, r_n * hd), lambda g, i: (i, g)),
                  pl.BlockSpec((1, nc, hd), lambda g, i: (g, 0, 0)),
                  pl.BlockSpec((1, nc, hd), lambda g, i: (g, 0, 0))]
        + [win_spec(j) for j in range(n_wb)] * 2
        + [pl.BlockSpec((1, TQ, LANES), lambda g, i: (g, i, 0))],
        out_specs=[pl.BlockSpec((TQ, r_n * hd), lambda g, i: (i, g)),
                   pl.BlockSpec((n_var, 1, 1, r_n * TQ, 2 * LANES), lambda g, i: (0, g, i, 0, 0))],
        out_shape=[jax.ShapeDtypeStruct((s, N_HEADS * hd), F32),
                   jax.ShapeDtypeStruct((n_var, g_n, n_q, r_n * TQ, 2 * LANES), BF16)],
        name="nsa_local",
        compiler_params=_cparams("parallel", "parallel"),
    )(q, kc, vc, *([kwin] * n_wb), *([vwin] * n_wb), gates)


def _causal_pairs(n_q, tq, tk):
    qs, ks = [], []
    for qi in range(n_q):
        for ki in range(((qi + 1) * tq - 1) // tk + 1):
            qs.append(qi)
            ks.append(ki)
    return jnp.asarray(qs, jnp.int32), jnp.asarray(ks, jnp.int32)


def _flash_update(q, k, v, m_ref, acc_ref, idx, causal_mask):
    s = _dot_t(q, k)
    if causal_mask is not None:
        s = jnp.where(causal_mask, s, MASKED)
    m_prev = m_ref[idx]
    m_new = jnp.maximum(m_prev, jnp.max(s, axis=1, keepdims=True))
    p = jnp.concatenate([jnp.exp(s[:, c * LANES:(c + 1) * LANES] - m_new).astype(BF16)
                         for c in range(s.shape[1] // LANES)], axis=1)
    acc_ref[idx] = jnp.exp(m_prev - m_new) * acc_ref[idx] + _dot(p, v)
    m_ref[idx] = m_new


def _flash_steps(qi, ki, tq, tk, rows_per_q, n_batch, operands, m_ref, acc_ref, finalize):
    first_diag = (qi * tq) // tk
    last_k = ((qi + 1) * tq - 1) // tk

    @pl.when(ki == 0)
    def _():
        m_ref[...] = jnp.full(m_ref.shape, M_INIT, F32)
        acc_ref[...] = jnp.zeros(acc_ref.shape, F32)

    @pl.when(ki < first_diag)
    def _():
        for b in range(n_batch):
            q, k, v = operands(b)
            _flash_update(q, k, v, m_ref, acc_ref, b, None)

    @pl.when(ki >= first_diag)
    def _():
        shape = (rows_per_q * tq, tk)
        q_off = lax.broadcasted_iota(jnp.int32, shape, 0) % tq
        k_off = lax.broadcasted_iota(jnp.int32, shape, 1)
        mask = q_off - k_off >= ki * tk - qi * tq
        for b in range(n_batch):
            q, k, v = operands(b)
            _flash_update(q, k, v, m_ref, acc_ref, b, mask)

    @pl.when(ki == last_k)
    def _():
        finalize()


def _nsa_sel_kernel(qtab, ktab, qaug_ref, kaug_ref, vaug_ref, pre_ref, gate_ref, o_ref,
                    m_ref, acc_ref):
    g_n, r_n, hd = NSA_KV_HEADS, NSA_GROUP, HEAD_DIM
    t = pl.program_id(0)
    qi, ki = qtab[t], ktab[t]

    def operands(g):
        return (qaug_ref[0, g, 0], kaug_ref[:, g * 2 * LANES:(g + 1) * 2 * LANES],
                vaug_ref[:, g * LANES:(g + 1) * LANES])

    def finalize():
        for g in range(g_n):
            acc = acc_ref[g]
            gate = gate_ref[g]
            o = acc[:, 0:hd] * (1.0 / acc[:, hd:hd + 1])
            for r in range(r_n):
                c = (g * r_n + r) * hd
                o_ref[:, c:c + hd] = (pre_ref[:, c:c + hd] + gate[:, 3 * r + 1:3 * r + 2]
                                      * o[r * TQ:(r + 1) * TQ]).astype(o_ref.dtype)

    _flash_steps(qi, ki, TQ, TK, r_n, g_n, operands, m_ref, acc_ref, finalize)


def _nsa_sel(qaug, kaug, vaug, pre, gates):
    n_var, g_n, n_q, rows, _ = qaug.shape
    s = kaug.shape[0]
    qtab, ktab = _causal_pairs(n_q, TQ, TK)
    tiles_per_var = BIAS_BLOCKS * SEL_BLOCK // TK
    grid_spec = pltpu.PrefetchScalarGridSpec(
        num_scalar_prefetch=2,
        grid=(qtab.shape[0],),
        in_specs=[
            pl.BlockSpec((1, g_n, 1, rows, 2 * LANES),
                         lambda t, qt, kt: (kt[t] // tiles_per_var, 0, qt[t], 0, 0)),
            pl.BlockSpec((TK, g_n * 2 * LANES), lambda t, qt, kt: (kt[t], 0)),
            pl.BlockSpec((TK, g_n * LANES), lambda t, qt, kt: (kt[t], 0)),
            pl.BlockSpec((TQ, N_HEADS * HEAD_DIM), lambda t, qt, kt: (qt[t], 0)),
            pl.BlockSpec((g_n, TQ, LANES), lambda t, qt, kt: (0, qt[t], 0)),
        ],
        out_specs=pl.BlockSpec((TQ, N_HEADS * HEAD_DIM), lambda t, qt, kt: (qt[t], 0)),
        scratch_shapes=[pltpu.VMEM((g_n, rows, LANES), F32), pltpu.VMEM((g_n, rows, LANES), F32)],
    )
    return pl.pallas_call(
        _nsa_sel_kernel,
        grid_spec=grid_spec,
        out_shape=jax.ShapeDtypeStruct((s, N_HEADS * HEAD_DIM), BF16),
        name="nsa_sel",
        compiler_params=_cparams("arbitrary"),
    )(qtab, ktab, qaug, kaug, vaug, pre, gates)


def _moba_proj_kernel(n_blk, k_eff, x_ref, cos_ref, sa_ref, sb_ref, w_ref,
                      qaug_ref, kaug_ref, vaug_ref, kmean_ref):
    tm = x_ref.shape[0]
    hd, d = HEAD_DIM, N_HEADS * HEAD_DIM
    i = pl.program_id(0)

    @pl.when(i == 0)
    def _():
        kmean_ref[...] = jnp.zeros(kmean_ref.shape, F32)

    xb = x_ref[...].astype(BF16)
    cos_t, sin_a, sin_b = cos_ref[...], sa_ref[...], sb_ref[...]
    q = _rope(_dot(xb, w_ref[:, 0:d]), cos_t, sin_a, sin_b)
    k = _rope(_dot(xb, w_ref[:, d:2 * d]), cos_t, sin_a, sin_b)
    v = _dot(xb, w_ref[:, 2 * d:3 * d])
    kmean = kmean_ref[...]
    nb = kmean.shape[0]
    blk = lax.broadcasted_iota(jnp.int32, (1, nb), 1)
    blk_f = blk.astype(F32)
    own = jnp.where(blk == i, 1.0, 0.0).astype(BF16)
    own_onehot = jnp.broadcast_to(own, (tm, nb))
    ones_col = _ones_col(tm, LANES - hd)
    for h in range(N_HEADS):
        sl = slice(h * hd, (h + 1) * hd)
        gate = _dot_t(q[:, sl], kmean[:, sl], precision=lax.Precision.HIGHEST)
        gate = jnp.where(blk < i, gate, -LARGE)
        sel = jnp.zeros(gate.shape, jnp.bool_)
        for _ in range(k_eff):
            m = jnp.max(gate, axis=1, keepdims=True)
            idx = jnp.min(jnp.where(gate == m, blk_f, float(nb)), axis=1, keepdims=True)
            hit = blk_f == idx
            sel = sel | (hit & (m > -LARGE))
            gate = jnp.where(hit, -jnp.inf, gate)
        bias = jnp.where(sel | (blk == i), 0.0, MASKED).astype(BF16)
        qaug_ref[h, :, 0:hd] = (q[:, sl] * SCALE).astype(BF16)
        qaug_ref[h, :, hd:hd + nb] = bias
        kaug_ref[h, :, 0:hd] = k[:, sl].astype(BF16)
        kaug_ref[h, :, hd:hd + nb] = own_onehot
        vaug_ref[h, :, 0:hd] = v[:, sl].astype(BF16)
        vaug_ref[h, :, hd:LANES] = ones_col
    kmean_ref[pl.ds(i, 1), :] = jnp.mean(k, axis=0, keepdims=True)


def _moba_proj(x, tabs, w):
    s, d = x.shape
    n_blk = s // MOBA_BLOCK
    k_eff = min(MOBA_TOPK, max(n_blk - 1, 1))
    nb = LANES - HEAD_DIM
    assert s % MOBA_BLOCK == 0 and n_blk <= nb
    tm = MOBA_BLOCK
    row = lambda i: (i, 0)
    head = jax.ShapeDtypeStruct((N_HEADS, s, LANES), BF16)
    head_spec = pl.BlockSpec((N_HEADS, tm, LANES), lambda i: (0, i, 0))
    return pl.pallas_call(
        functools.partial(_moba_proj_kernel, n_blk, k_eff),
        grid=(s // tm,),
        in_specs=[pl.BlockSpec((tm, d), row)] + [pl.BlockSpec((tm, LANES), row)] * 3
        + [pl.BlockSpec(w.shape, lambda i: (0, 0))],
        out_specs=[head_spec] * 3,
        out_shape=[head] * 3,
        scratch_shapes=[pltpu.VMEM((nb, N_HEADS * HEAD_DIM), F32)],
        name="moba_proj",
        compiler_params=_cparams("arbitrary"),
    )(x, *tabs, w)


def _moba_attn_kernel(qtab, ktab, q_ref, k_ref, v_ref, o_ref, m_ref, acc_ref):
    hd = HEAD_DIM
    t = pl.program_id(0)
    qi, ki = qtab[t], ktab[t]

    def operands(h):
        return q_ref[h], k_ref[h], v_ref[h]

    def finalize():
        for h in range(N_HEADS):
            acc = acc_ref[h]
            o_ref[:, h * hd:(h + 1) * hd] = (acc[:, 0:hd] * (1.0 / acc[:, hd:hd + 1])).astype(o_ref.dtype)

    _flash_steps(qi, ki, q_ref.shape[1], k_ref.shape[1], 1, N_HEADS, operands, m_ref, acc_ref, finalize)


def _moba_attn(qaug, kaug, vaug):
    h_n, s, _ = qaug.shape
    tq, tk = min(MOBA_TQ, s), min(MOBA_TK, s)
    assert s % tq == 0 and s % tk == 0
    qtab, ktab = _causal_pairs(s // tq, tq, tk)
    grid_spec = pltpu.PrefetchScalarGridSpec(
        num_scalar_prefetch=2,
        grid=(qtab.shape[0],),
        in_specs=[pl.BlockSpec((h_n, tq, LANES), lambda t, qt, kt: (0, qt[t], 0)),
                  pl.BlockSpec((h_n, tk, LANES), lambda t, qt, kt: (0, kt[t], 0)),
                  pl.BlockSpec((h_n, tk, LANES), lambda t, qt, kt: (0, kt[t], 0))],
        out_specs=pl.BlockSpec((tq, h_n * HEAD_DIM), lambda t, qt, kt: (qt[t], 0)),
        scratch_shapes=[pltpu.VMEM((h_n, tq, LANES), F32), pltpu.VMEM((h_n, tq, LANES), F32)],
    )
    return pl.pallas_call(
        _moba_attn_kernel,
        grid_spec=grid_spec,
        out_shape=jax.ShapeDtypeStruct((s, h_n * HEAD_DIM), BF16),
        name="moba_attn",
        compiler_params=_cparams("arbitrary"),
    )(qtab, ktab, qaug, kaug, vaug)


def _mm_res_ln_kernel(alpha, a_ref, w_ref, x_ref, g_ref, b_ref, o_ref):
    z = alpha * x_ref[...] + _dot(a_ref[...], w_ref[...])
    mu = jnp.mean(z, axis=-1, keepdims=True)
    zc = z - mu
    var = jnp.mean(zc * zc, axis=-1, keepdims=True)
    o_ref[...] = zc * lax.rsqrt(var + LN_EPS) * g_ref[...] + b_ref[...]


def _mm_res_ln(a, w, x, gain, bias, alpha):
    s, k = a.shape
    d = w.shape[1]
    tm = min(MM_TILE, s)
    row = lambda i: (i, 0)
    whole = lambda i: (0, 0)
    return pl.pallas_call(
        functools.partial(_mm_res_ln_kernel, alpha),
        grid=(s // tm,),
        in_specs=[pl.BlockSpec((tm, k), row), pl.BlockSpec((k, d), whole),
                  pl.BlockSpec((tm, d), row), pl.BlockSpec((1, d), whole), pl.BlockSpec((1, d), whole)],
        out_specs=pl.BlockSpec((tm, d), row),
        out_shape=jax.ShapeDtypeStruct((s, d), F32),
        name="mm_res_ln",
        compiler_params=_cparams("parallel"),
    )(a, w, x, gain.reshape(1, d), bias.reshape(1, d))


def _ffn_up_kernel(n_chunk, x_ref, wg_ref, wu_ref, o_ref):
    xb = x_ref[...].astype(BF16)
    cw = wg_ref.shape[1] // n_chunk
    for c in range(n_chunk):
        sl = slice(c * cw, (c + 1) * cw)
        o_ref[:, sl] = (jax.nn.silu(_dot(xb, wg_ref[:, sl])) * _dot(xb, wu_ref[:, sl])).astype(o_ref.dtype)


def _ffn_up(x, wg, wu):
    s, d = x.shape
    f = wg.shape[1]
    n_chunk = 2 if f % (2 * LANES) == 0 else 1
    tm = min(MM_TILE, s)
    row = lambda i: (i, 0)
    whole = lambda i: (0, 0)
    return pl.pallas_call(
        functools.partial(_ffn_up_kernel, n_chunk),
        grid=(s // tm,),
        in_specs=[pl.BlockSpec((tm, d), row), pl.BlockSpec((d, f), whole), pl.BlockSpec((d, f), whole)],
        out_specs=pl.BlockSpec((tm, f), row),
        out_shape=jax.ShapeDtypeStruct((s, f), BF16),
        name="ffn_up",
        compiler_params=_cparams("parallel"),
    )(x, wg, wu)


def _nsa_mixer(x, tabs, w_in, ck_pos, ck_w1, ck_w2, cv_pos, cv_w1, cv_w2):
    s = x.shape[0]
    g_n, r_n, hd = NSA_KV_HEADS, NSA_GROUP, HEAD_DIM
    q_dim, kv_dim = N_HEADS * hd, NSA_KV_HEADS * hd
    n_sel = s // SEL_BLOCK
    n_var = -(-n_sel // BIAS_BLOCKS)
    assert s % TK == 0 and (BIAS_BLOCKS * SEL_BLOCK) % TK == 0
    wq = w_in[:, :q_dim].astype(BF16)
    wkv = w_in[:, q_dim:q_dim + 6 * kv_dim].astype(BF16)
    wg = w_in[:, q_dim + 6 * kv_dim:].reshape(-1, g_n, 3 * r_n)
    wg = jnp.pad(wg, ((0, 0), (0, 0), (0, LANES - 3 * r_n))).reshape(-1, g_n * LANES).astype(BF16)
    q, cmp_kv, kaug, vaug, kwin, vwin, gates = _nsa_proj(x, tabs, wq, wkv, wg)
    comp = _compress(cmp_kv, jnp.stack([ck_pos, cv_pos]),
                     jnp.stack([ck_w1, cv_w1]).astype(BF16), jnp.stack([ck_w2, cv_w2]).astype(BF16))
    comp = comp.reshape(2, g_n, n_sel, 4, hd).transpose(0, 1, 3, 2, 4).reshape(2, g_n, 4 * n_sel, hd)
    comp = comp.astype(BF16)
    pre, qaug = _nsa_local(q, comp[0], comp[1], kwin, vwin, gates, n_sel, n_var)
    return _nsa_sel(qaug, kaug, vaug, pre, gates)


def _moba_mixer(x, tabs, w_in):
    qaug, kaug, vaug = _moba_proj(x, tabs, w_in.astype(BF16))
    return _moba_attn(qaug, kaug, vaug)


def kernel(x, positions, nsa_w_in, nsa_w_o, nsa_ck_pos, nsa_ck_w1, nsa_ck_w2, nsa_cv_pos, nsa_cv_w1, nsa_cv_w2, moba_w_in, moba_w_o, ffn_wg, ffn_wu, ffn_wd, ln1_g, ln1_b, ln2_g, ln2_b):
    b, s, d = x.shape
    depth = ffn_wg.shape[0]
    alpha = (2 * depth) ** 0.25
    outs = []
    for bi in range(b):
        h = x[bi]
        tabs = _rope_tables(positions[bi])
        for i in range(depth):
            j = i // 2
            if i % 2 == 0:
                a = _nsa_mixer(h, tabs, nsa_w_in[j], nsa_ck_pos[j], nsa_ck_w1[j], nsa_ck_w2[j],
                               nsa_cv_pos[j], nsa_cv_w1[j], nsa_cv_w2[j])
                w_o = nsa_w_o[j]
            else:
                a = _moba_mixer(h, tabs, moba_w_in[j])
                w_o = moba_w_o[j]
            h = _mm_res_ln(a, w_o.astype(BF16), h, ln1_g[i], ln1_b[i], alpha)
            u = _ffn_up(h, ffn_wg[i].astype(BF16), ffn_wu[i].astype(BF16))
            h = _mm_res_ln(u, ffn_wd[i].astype(BF16), h, ln2_g[i], ln2_b[i], alpha)
        outs.append(h)
    return jnp.stack(outs)
```

```python
import functools

import numpy as np
import jax
import jax.numpy as jnp
from jax import lax
from jax.experimental import pallas as pl
from jax.experimental.pallas import tpu as pltpu

F32 = jnp.float32
BF16 = jnp.bfloat16

N_HEADS = 16
HEAD_DIM = 64
ROPE_DIM = HEAD_DIM // 4
ROPE_THETA = 500000.0
NSA_KV_HEADS = 4
NSA_GROUP = N_HEADS // NSA_KV_HEADS
CMP_BLOCK = 32
CMP_STRIDE = 16
SEL_BLOCK = 64
SEL_TOPK = 16
WINDOW = 512
MOBA_BLOCK = 256
MOBA_TOPK = 3
LN_EPS = 1e-5
LARGE = 1e30
SCALE = HEAD_DIM ** -0.5

LANES = 128
VMEM_LIMIT_BYTES = 48 * 1024 * 1024

ROW_TILE = 256
MM_TILE = 512
TQ = 256
TK = 1024
MOBA_TQ = 512
MOBA_TK = 1024
BIAS_BLOCKS = 128
MASKED = -1e30
M_INIT = -5e29

assert (CMP_BLOCK, CMP_STRIDE, SEL_BLOCK) == (32, 16, 64)
assert WINDOW % TQ == 0 and TK % TQ == 0 and ROW_TILE == MOBA_BLOCK


def _cparams(*sem):
    return pltpu.CompilerParams(dimension_semantics=sem, vmem_limit_bytes=VMEM_LIMIT_BYTES)


def _dot(a, b):
    return jnp.dot(a, b, preferred_element_type=F32)


def _dot_t(a, b, precision=None):
    return lax.dot_general(a, b, (((1,), (1,)), ((), ())), preferred_element_type=F32,
                           precision=precision)


def _rope_tables_kernel(pos_ref, c_ref, cos_ref, sa_ref, sb_ref):
    ang = pos_ref[...].astype(F32) * c_ref[0:1, :]
    c = jnp.cos(ang)
    s = jnp.sin(ang)
    cos_ref[...] = jnp.where(c_ref[1:2, :] > 0, c, 1.0)
    sa_ref[...] = jnp.where(c_ref[2:3, :] > 0, s, 0.0)
    sb_ref[...] = jnp.where(c_ref[3:4, :] > 0, -s, 0.0)


def _rope_tables(positions):
    s = positions.shape[0]
    half = ROPE_DIM // 2
    inv = ROPE_THETA ** (-jnp.arange(0, ROPE_DIM, 2, dtype=F32) / ROPE_DIM)
    d = np.arange(LANES) % HEAD_DIM
    consts = jnp.zeros((8, LANES), F32)
    consts = consts.at[0].set(inv[d % half])
    consts = consts.at[1].set(jnp.asarray(d < ROPE_DIM, F32))
    consts = consts.at[2].set(jnp.asarray((d >= half) & (d < ROPE_DIM), F32))
    consts = consts.at[3].set(jnp.asarray(d < half, F32))
    tm = min(MM_TILE, s)
    tab = jax.ShapeDtypeStruct((s, LANES), F32)
    return pl.pallas_call(
        _rope_tables_kernel,
        grid=(s // tm,),
        in_specs=[pl.BlockSpec((tm, 1), lambda i: (i, 0)),
                  pl.BlockSpec((8, LANES), lambda i: (0, 0))],
        out_specs=[pl.BlockSpec((tm, LANES), lambda i: (i, 0))] * 3,
        out_shape=[tab, tab, tab],
        name="rope_tables",
        compiler_params=_cparams("parallel"),
    )(positions.reshape(s, 1), consts)


def _rope(x, cos_t, sin_a, sin_b):
    half = ROPE_DIM // 2
    outs = []
    for c in range(x.shape[1] // LANES):
        xs = x[:, c * LANES:(c + 1) * LANES]
        outs.append(xs * cos_t + pltpu.roll(xs, half, 1) * sin_a
                    + pltpu.roll(xs, LANES - half, 1) * sin_b)
    return outs[0] if len(outs) == 1 else jnp.concatenate(outs, axis=1)


def _ones_col(rows, width):
    return jnp.where(lax.broadcasted_iota(jnp.int32, (rows, width), 1) == 0, 1.0, 0.0).astype(BF16)


def _nsa_proj_kernel(x_ref, cos_ref, sa_ref, sb_ref, wq_ref, wkv_ref, wg_ref,
                     q_ref, cmp_ref, kaug_ref, vaug_ref, kwin_ref, vwin_ref, gate_ref):
    tm = x_ref.shape[0]
    g_n, hd, kvd = NSA_KV_HEADS, HEAD_DIM, NSA_KV_HEADS * HEAD_DIM
    xb = x_ref[...].astype(BF16)
    cos_t, sin_a, sin_b = cos_ref[...], sa_ref[...], sb_ref[...]
    q = _rope(_dot(xb, wq_ref[...]), cos_t, sin_a, sin_b) * SCALE
    q_ref[...] = q.astype(BF16)
    kv = _dot(xb, wkv_ref[...])
    k_cmp = _rope(kv[:, 0:kvd], cos_t, sin_a, sin_b)
    v_cmp = kv[:, kvd:2 * kvd]
    k_sel = _rope(kv[:, 2 * kvd:3 * kvd], cos_t, sin_a, sin_b)
    v_sel = kv[:, 3 * kvd:4 * kvd]
    k_win = _rope(kv[:, 4 * kvd:5 * kvd], cos_t, sin_a, sin_b)
    v_win = kv[:, 5 * kvd:6 * kvd]
    gate = jax.nn.sigmoid(_dot(xb, wg_ref[...]))
    tok = pl.program_id(0) * tm + lax.broadcasted_iota(jnp.int32, (tm, BIAS_BLOCKS), 0)
    blk = (tok // SEL_BLOCK) % BIAS_BLOCKS
    onehot = jnp.where(lax.broadcasted_iota(jnp.int32, (tm, BIAS_BLOCKS), 1) == blk,
                       1.0, 0.0).astype(BF16)
    ones_col = _ones_col(tm, LANES - hd)
    for g in range(g_n):
        sl = slice(g * hd, (g + 1) * hd)
        cmp_ref[0, g] = k_cmp[:, sl]
        cmp_ref[1, g] = v_cmp[:, sl]
        kwin_ref[g] = k_win[:, sl].astype(BF16)
        vwin_ref[g] = v_win[:, sl].astype(BF16)
        gate_ref[g] = gate[:, g * LANES:(g + 1) * LANES]
        kb = g * 2 * LANES
        kaug_ref[:, kb:kb + BIAS_BLOCKS] = onehot
        kaug_ref[:, kb + BIAS_BLOCKS:kb + BIAS_BLOCKS + hd] = k_sel[:, sl].astype(BF16)
        kaug_ref[:, kb + BIAS_BLOCKS + hd:kb + 2 * LANES] = jnp.zeros((tm, 2 * LANES - BIAS_BLOCKS - hd), BF16)
        vb = g * LANES
        vaug_ref[:, vb:vb + hd] = v_sel[:, sl].astype(BF16)
        vaug_ref[:, vb + hd:vb + LANES] = ones_col


def _nsa_proj(x, tabs, wq, wkv, wg):
    s, d = x.shape
    g_n, hd = NSA_KV_HEADS, HEAD_DIM
    tm = min(ROW_TILE, s)
    row = lambda i: (i, 0)
    whole = lambda i: (0, 0)
    out_shape = [
        jax.ShapeDtypeStruct((s, N_HEADS * hd), BF16),
        jax.ShapeDtypeStruct((2, g_n, s, hd), F32),
        jax.ShapeDtypeStruct((s, g_n * 2 * LANES), BF16),
        jax.ShapeDtypeStruct((s, g_n * LANES), BF16),
        jax.ShapeDtypeStruct((g_n, s, hd), BF16),
        jax.ShapeDtypeStruct((g_n, s, hd), BF16),
        jax.ShapeDtypeStruct((g_n, s, LANES), F32),
    ]
    out_specs = [
        pl.BlockSpec((tm, N_HEADS * hd), row),
        pl.BlockSpec((2, g_n, tm, hd), lambda i: (0, 0, i, 0)),
        pl.BlockSpec((tm, g_n * 2 * LANES), row),
        pl.BlockSpec((tm, g_n * LANES), row),
        pl.BlockSpec((g_n, tm, hd), lambda i: (0, i, 0)),
        pl.BlockSpec((g_n, tm, hd), lambda i: (0, i, 0)),
        pl.BlockSpec((g_n, tm, LANES), lambda i: (0, i, 0)),
    ]
    return pl.pallas_call(
        _nsa_proj_kernel,
        grid=(s // tm,),
        in_specs=[pl.BlockSpec((tm, d), row)] + [pl.BlockSpec((tm, LANES), row)] * 3
        + [pl.BlockSpec(wq.shape, whole), pl.BlockSpec(wkv.shape, whole),
           pl.BlockSpec(wg.shape, whole)],
        out_specs=out_specs,
        out_shape=out_shape,
        name="nsa_proj",
        compiler_params=_cparams("parallel"),
    )(x, *tabs, wq, wkv, wg)


def _compress_kernel(x_ref, pos_ref, w1_ref, w2_ref, out_ref):
    n16 = x_ref.shape[2]
    half = w1_ref.shape[1] // 2
    x = x_ref[0, 0]
    a = _dot((x + pos_ref[0, 0:1, :]).astype(BF16), w1_ref[0, 0:half, :])
    b = _dot((x + pos_ref[0, 1:2, :]).astype(BF16), w1_ref[0, half:, :])
    hid = a + pltpu.roll(b, n16 - 1, 0)
    out_ref[0, 0] = _dot(jax.nn.gelu(hid).astype(BF16), w2_ref[0])


def _compress(cmp_kv, pos, w1, w2):
    _, g_n, s, hd = cmp_kv.shape
    n16 = s // CMP_STRIDE
    x = cmp_kv.reshape(2, g_n, n16, CMP_STRIDE * hd)
    pos2 = pos.reshape(2, 2, CMP_STRIDE * hd)
    return pl.pallas_call(
        _compress_kernel,
        grid=(2, g_n),
        in_specs=[pl.BlockSpec((1, 1, n16, CMP_STRIDE * hd), lambda a, g: (a, g, 0, 0)),
                  pl.BlockSpec((1, 2, CMP_STRIDE * hd), lambda a, g: (a, 0, 0)),
                  pl.BlockSpec((1,) + w1.shape[1:], lambda a, g: (a, 0, 0)),
                  pl.BlockSpec((1,) + w2.shape[1:], lambda a, g: (a, 0, 0))],
        out_specs=pl.BlockSpec((1, 1, n16, hd), lambda a, g: (a, g, 0, 0)),
        out_shape=jax.ShapeDtypeStruct((2, g_n, n16, hd), F32),
        name="nsa_compress",
        compiler_params=_cparams("parallel", "parallel"),
    )(x, pos2, w1, w2)


def _top_k_mask(score, n_top):
    n = score.shape[1]
    lane = lax.broadcasted_iota(jnp.int32, score.shape, 1).astype(F32)
    sel = jnp.zeros(score.shape, jnp.bool_)
    for _ in range(n_top):
        m = jnp.max(score, axis=1, keepdims=True)
        idx = jnp.min(jnp.where(score == m, lane, float(n)), axis=1, keepdims=True)
        hit = lane == idx
        sel = sel | hit
        score = jnp.where(hit, -jnp.inf, score)
    return sel


def _nsa_local_kernel(n_sel, n_top, q_ref, kc_ref, vc_ref, *rest):
    n_wb = WINDOW // TQ + 1
    kw_refs = rest[:n_wb]
    vw_refs = rest[n_wb:2 * n_wb]
    gate_ref, pre_ref, qaug_ref = rest[2 * n_wb:]
    r_n, hd = NSA_GROUP, HEAD_DIM
    qi = pl.program_id(1)
    qg = q_ref[...]
    qs = jnp.concatenate([qg[:, r * hd:(r + 1) * hd] for r in range(r_n)], axis=0)
    rows = r_n * TQ
    tq = qi * TQ + lax.broadcasted_iota(jnp.int32, (rows, 1), 0) % TQ

    nc = kc_ref.shape[1]
    col = lax.broadcasted_iota(jnp.int32, (1, nc), 1)
    cmp_end = (col % n_sel) * SEL_BLOCK + (col // n_sel) * CMP_STRIDE + CMP_BLOCK - 1
    valid = (cmp_end <= tq) & (col < nc - 1)
    s = jnp.where(valid, _dot_t(qs, kc_ref[0]), -LARGE)
    m = jnp.max(s, axis=1, keepdims=True)
    p = jnp.where(valid, jnp.exp(s - m), 0.0)
    l = jnp.sum(p, axis=1, keepdims=True)
    p = p * (1.0 / jnp.where(l > 0, l, 1.0))
    o_cmp = _dot(p.astype(BF16), vc_ref[0])

    p3 = p[:, 3 * n_sel:4 * n_sel]
    blk = lax.broadcasted_iota(jnp.int32, (1, n_sel), 1)
    p3_prev = jnp.where(blk == 0, 0.0, pltpu.roll(p3, 1, 1))
    imp_h = 2.0 * (p[:, 0:n_sel] + p[:, n_sel:2 * n_sel] + p[:, 2 * n_sel:3 * n_sel]) + p3 + p3_prev
    imp = imp_h[0:TQ]
    for r in range(1, r_n):
        imp = imp + imp_h[r * TQ:(r + 1) * TQ]
    tq_q = qi * TQ + lax.broadcasted_iota(jnp.int32, (TQ, 1), 0)
    blk_q = tq_q // SEL_BLOCK
    forced = (blk == 0) | (blk == blk_q) | (blk == blk_q - 1)
    causal = blk * SEL_BLOCK <= tq_q
    score = jnp.where(forced, LARGE, jnp.where(causal, imp, -LARGE))
    bias = jnp.where(_top_k_mask(score, n_top), 0.0, MASKED).astype(BF16)
    n_var = qaug_ref.shape[0]
    if n_sel < n_var * BIAS_BLOCKS:
        bias = jnp.concatenate(
            [bias, jnp.full((TQ, n_var * BIAS_BLOCKS - n_sel), MASKED, BF16)], axis=1)
    zeros = jnp.zeros((TQ, 2 * LANES - BIAS_BLOCKS - hd), BF16)
    for v in range(n_var):
        for r in range(r_n):
            rs = slice(r * TQ, (r + 1) * TQ)
            qaug_ref[v, 0, 0, rs, 0:BIAS_BLOCKS] = bias[:, v * BIAS_BLOCKS:(v + 1) * BIAS_BLOCKS]
            qaug_ref[v, 0, 0, rs, BIAS_BLOCKS:BIAS_BLOCKS + hd] = qg[:, r * hd:(r + 1) * hd]
            qaug_ref[v, 0, 0, rs, BIAS_BLOCKS + hd:2 * LANES] = zeros

    kcat = jnp.concatenate([kw[0] for kw in kw_refs], axis=0)
    vcat = jnp.concatenate([vw[0] for vw in vw_refs], axis=0)
    kpos = (qi - (n_wb - 1)) * TQ + lax.broadcasted_iota(jnp.int32, (1, n_wb * TQ), 1)
    dist = tq - kpos
    mw = (kpos >= 0) & (dist >= 0) & (dist < WINDOW)
    s = jnp.where(mw, _dot_t(qs, kcat), -LARGE)
    p = jnp.exp(s - jnp.max(s, axis=1, keepdims=True))
    o_win = _dot(p.astype(BF16), vcat) * (1.0 / jnp.sum(p, axis=1, keepdims=True))

    gate = gate_ref[0]
    for r in range(r_n):
        rs = slice(r * TQ, (r + 1) * TQ)
        pre_ref[:, r * hd:(r + 1) * hd] = (gate[:, 3 * r:3 * r + 1] * o_cmp[rs]
                                          + gate[:, 3 * r + 2:3 * r + 3] * o_win[rs])


def _nsa_local(q, kc, vc, kwin, vwin, gates, n_sel, n_var):
    s = q.shape[0]
    g_n, r_n, hd = NSA_KV_HEADS, NSA_GROUP, HEAD_DIM
    n_q = s // TQ
    n_wb = WINDOW // TQ + 1
    n_top = min(SEL_TOPK, n_sel)
    nc = kc.shape[1]

    def win_spec(j):
        return pl.BlockSpec((1, TQ, hd), lambda g, i: (g, jnp.maximum(i - (n_wb - 1) + j, 0), 0))

    return pl.pallas_call(
        functools.partial(_nsa_local_kernel, n_sel, n_top),
        grid=(g_n, n_q),
        in_specs=[pl.BlockSpec((TQ, r_n * hd), lambda g, i: (i, g)),
                  pl.BlockSpec((1, nc, hd), lambda g, i: (g, 0, 0)),
                  pl.BlockSpec((1, nc, hd), lambda g, i: (g, 0, 0))]
        + [win_spec(j) for j in range(n_wb)] * 2
        + [pl.BlockSpec((1, TQ, LANES), lambda g, i: (g, i, 0))],
        out_specs=[pl.BlockSpec((TQ, r_n * hd), lambda g, i: (i, g)),
                   pl.BlockSpec((n_var, 1, 1, r_n * TQ, 2 * LANES), lambda g, i: (0, g, i, 0, 0))],
        out_shape=[jax.ShapeDtypeStruct((s, N_HEADS * hd), F32),
                   jax.ShapeDtypeStruct((n_var, g_n, n_q, r_n * TQ, 2 * LANES), BF16)],
        name="nsa_local",
        compiler_params=_cparams("parallel", "parallel"),
    )(q, kc, vc, *([kwin] * n_wb), *([vwin] * n_wb), gates)


def _causal_pairs(n_q, tq, tk):
    qs, ks = [], []
    for qi in range(n_q):
        for ki in range(((qi + 1) * tq - 1) // tk + 1):
            qs.append(qi)
            ks.append(ki)
    return jnp.asarray(qs, jnp.int32), jnp.asarray(ks, jnp.int32)


def _flash_update(q, k, v, m_ref, acc_ref, idx, causal_mask):
    s = _dot_t(q, k)
    if causal_mask is not None:
        s = jnp.where(causal_mask, s, MASKED)
    m_prev = m_ref[idx]
    m_new = jnp.maximum(m_prev, jnp.max(s, axis=1, keepdims=True))
    p = jnp.concatenate([jnp.exp(s[:, c * LANES:(c + 1) * LANES] - m_new).astype(BF16)
                         for c in range(s.shape[1] // LANES)], axis=1)
    acc_ref[idx] = jnp.exp(m_prev - m_new) * acc_ref[idx] + _dot(p, v)
    m_ref[idx] = m_new


def _flash_steps(qi, ki, tq, tk, rows_per_q, n_batch, operands, m_ref, acc_ref, finalize):
    first_diag = (qi * tq) // tk
    last_k = ((qi + 1) * tq - 1) // tk

    @pl.when(ki == 0)
    def _():
        m_ref[...] = jnp.full(m_ref.shape, M_INIT, F32)
        acc_ref[...] = jnp.zeros(acc_ref.shape, F32)

    @pl.when(ki < first_diag)
    def _():
        for b in range(n_batch):
            q, k, v = operands(b)
            _flash_update(q, k, v, m_ref, acc_ref, b, None)

    @pl.when(ki >= first_diag)
    def _():
        shape = (rows_per_q * tq, tk)
        q_off = lax.broadcasted_iota(jnp.int32, shape, 0) % tq
        k_off = lax.broadcasted_iota(jnp.int32, shape, 1)
        mask = q_off - k_off >= ki * tk - qi * tq
        for b in range(n_batch):
            q, k, v = operands(b)
            _flash_update(q, k, v, m_ref, acc_ref, b, mask)

    @pl.when(ki == last_k)
    def _():
        finalize()


def _nsa_sel_kernel(qtab, ktab, qaug_ref, kaug_ref, vaug_ref, pre_ref, gate_ref, o_ref,
                    m_ref, acc_ref):
    g_n, r_n, hd = NSA_KV_HEADS, NSA_GROUP, HEAD_DIM
    t = pl.program_id(0)
    qi, ki = qtab[t], ktab[t]

    def operands(g):
        return (qaug_ref[0, g, 0], kaug_ref[:, g * 2 * LANES:(g + 1) * 2 * LANES],
                vaug_ref[:, g * LANES:(g + 1) * LANES])

    def finalize():
        for g in range(g_n):
            acc = acc_ref[g]
            gate = gate_ref[g]
            o = acc[:, 0:hd] * (1.0 / acc[:, hd:hd + 1])
            for r in range(r_n):
                c = (g * r_n + r) * hd
                o_ref[:, c:c + hd] = (pre_ref[:, c:c + hd] + gate[:, 3 * r + 1:3 * r + 2]
                                      * o[r * TQ:(r + 1) * TQ]).astype(o_ref.dtype)

    _flash_steps(qi, ki, TQ, TK, r_n, g_n, operands, m_ref, acc_ref, finalize)


def _nsa_sel(qaug, kaug, vaug, pre, gates):
    n_var, g_n, n_q, rows, _ = qaug.shape
    s = kaug.shape[0]
    qtab, ktab = _causal_pairs(n_q, TQ, TK)
    tiles_per_var = BIAS_BLOCKS * SEL_BLOCK // TK
    grid_spec = pltpu.PrefetchScalarGridSpec(
        num_scalar_prefetch=2,
        grid=(qtab.shape[0],),
        in_specs=[
            pl.BlockSpec((1, g_n, 1, rows, 2 * LANES),
                         lambda t, qt, kt: (kt[t] // tiles_per_var, 0, qt[t], 0, 0)),
            pl.BlockSpec((TK, g_n * 2 * LANES), lambda t, qt, kt: (kt[t], 0)),
            pl.BlockSpec((TK, g_n * LANES), lambda t, qt, kt: (kt[t], 0)),
            pl.BlockSpec((TQ, N_HEADS * HEAD_DIM), lambda t, qt, kt: (qt[t], 0)),
            pl.BlockSpec((g_n, TQ, LANES), lambda t, qt, kt: (0, qt[t], 0)),
        ],
        out_specs=pl.BlockSpec((TQ, N_HEADS * HEAD_DIM), lambda t, qt, kt: (qt[t], 0)),
        scratch_shapes=[pltpu.VMEM((g_n, rows, LANES), F32), pltpu.VMEM((g_n, rows, LANES), F32)],
    )
    return pl.pallas_call(
        _nsa_sel_kernel,
        grid_spec=grid_spec,
        out_shape=jax.ShapeDtypeStruct((s, N_HEADS * HEAD_DIM), BF16),
        name="nsa_sel",
        compiler_params=_cparams("arbitrary"),
    )(qtab, ktab, qaug, kaug, vaug, pre, gates)


def _moba_proj_kernel(n_blk, k_eff, x_ref, cos_ref, sa_ref, sb_ref, w_ref,
                      qaug_ref, kaug_ref, vaug_ref, kmean_ref):
    tm = x_ref.shape[0]
    hd, d = HEAD_DIM, N_HEADS * HEAD_DIM
    i = pl.program_id(0)

    @pl.when(i == 0)
    def _():
        kmean_ref[...] = jnp.zeros(kmean_ref.shape, F32)

    xb = x_ref[...].astype(BF16)
    cos_t, sin_a, sin_b = cos_ref[...], sa_ref[...], sb_ref[...]
    q = _rope(_dot(xb, w_ref[:, 0:d]), cos_t, sin_a, sin_b)
    k = _rope(_dot(xb, w_ref[:, d:2 * d]), cos_t, sin_a, sin_b)
    v = _dot(xb, w_ref[:, 2 * d:3 * d])
    kmean = kmean_ref[...]
    nb = kmean.shape[0]
    blk = lax.broadcasted_iota(jnp.int32, (1, nb), 1)
    blk_f = blk.astype(F32)
    own = jnp.where(blk == i, 1.0, 0.0).astype(BF16)
    own_onehot = jnp.broadcast_to(own, (tm, nb))
    ones_col = _ones_col(tm, LANES - hd)
    for h in range(N_HEADS):
        sl = slice(h * hd, (h + 1) * hd)
        gate = _dot_t(q[:, sl], kmean[:, sl], precision=lax.Precision.HIGHEST)
        gate = jnp.where(blk < i, gate, -LARGE)
        sel = jnp.zeros(gate.shape, jnp.bool_)
        for _ in range(k_eff):
            m = jnp.max(gate, axis=1, keepdims=True)
            idx = jnp.min(jnp.where(gate == m, blk_f, float(nb)), axis=1, keepdims=True)
            hit = blk_f == idx
            sel = sel | (hit & (m > -LARGE))
            gate = jnp.where(hit, -jnp.inf, gate)
        bias = jnp.where(sel | (blk == i), 0.0, MASKED).astype(BF16)
        qaug_ref[h, :, 0:hd] = (q[:, sl] * SCALE).astype(BF16)
        qaug_ref[h, :, hd:hd + nb] = bias
        kaug_ref[h, :, 0:hd] = k[:, sl].astype(BF16)
        kaug_ref[h, :, hd:hd + nb] = own_onehot
        vaug_ref[h, :, 0:hd] = v[:, sl].astype(BF16)
        vaug_ref[h, :, hd:LANES] = ones_col
    kmean_ref[pl.ds(i, 1), :] = jnp.mean(k, axis=0, keepdims=True)


def _moba_proj(x, tabs, w):
    s, d = x.shape
    n_blk = s // MOBA_BLOCK
    k_eff = min(MOBA_TOPK, max(n_blk - 1, 1))
    nb = LANES - HEAD_DIM
    assert s % MOBA_BLOCK == 0 and n_blk <= nb
    tm = MOBA_BLOCK
    row = lambda i: (i, 0)
    head = jax.ShapeDtypeStruct((N_HEADS, s, LANES), BF16)
    head_spec = pl.BlockSpec((N_HEADS, tm, LANES), lambda i: (0, i, 0))
    return pl.pallas_call(
        functools.partial(_moba_proj_kernel, n_blk, k_eff),
        grid=(s // tm,),
        in_specs=[pl.BlockSpec((tm, d), row)] + [pl.BlockSpec((tm, LANES), row)] * 3
        + [pl.BlockSpec(w.shape, lambda i: (0, 0))],
        out_specs=[head_spec] * 3,
        out_shape=[head] * 3,
        scratch_shapes=[pltpu.VMEM((nb, N_HEADS * HEAD_DIM), F32)],
        name="moba_proj",
        compiler_params=_cparams("arbitrary"),
    )(x, *tabs, w)


def _moba_attn_kernel(qtab, ktab, q_ref, k_ref, v_ref, o_ref, m_ref, acc_ref):
    hd = HEAD_DIM
    t = pl.program_id(0)
    qi, ki = qtab[t], ktab[t]

    def operands(h):
        return q_ref[h], k_ref[h], v_ref[h]

    def finalize():
        for h in range(N_HEADS):
            acc = acc_ref[h]
            o_ref[:, h * hd:(h + 1) * hd] = (acc[:, 0:hd] * (1.0 / acc[:, hd:hd + 1])).astype(o_ref.dtype)

    _flash_steps(qi, ki, q_ref.shape[1], k_ref.shape[1], 1, N_HEADS, operands, m_ref, acc_ref, finalize)


def _moba_attn(qaug, kaug, vaug):
    h_n, s, _ = qaug.shape
    tq, tk = min(MOBA_TQ, s), min(MOBA_TK, s)
    assert s % tq == 0 and s % tk == 0
    qtab, ktab = _causal_pairs(s // tq, tq, tk)
    grid_spec = pltpu.PrefetchScalarGridSpec(
        num_scalar_prefetch=2,
        grid=(qtab.shape[0],),
        in_specs=[pl.BlockSpec((h_n, tq, LANES), lambda t, qt, kt: (0, qt[t], 0)),
                  pl.BlockSpec((h_n, tk, LANES), lambda t, qt, kt: (0, kt[t], 0)),
                  pl.BlockSpec((h_n, tk, LANES), lambda t, qt, kt: (0, kt[t], 0))],
        out_specs=pl.BlockSpec((tq, h_n * HEAD_DIM), lambda t, qt, kt: (qt[t], 0)),
        scratch_shapes=[pltpu.VMEM((h_n, tq, LANES), F32), pltpu.VMEM((h_n, tq, LANES), F32)],
    )
    return pl.pallas_call(
        _moba_attn_kernel,
        grid_spec=grid_spec,
        out_shape=jax.ShapeDtypeStruct((s, h_n * HEAD_DIM), BF16),
        name="moba_attn",
        compiler_params=_cparams("arbitrary"),
    )(qtab, ktab, qaug, kaug, vaug)


def _mm_res_ln_kernel(alpha, a_ref, w_ref, x_ref, g_ref, b_ref, o_ref):
    z = alpha * x_ref[...] + _dot(a_ref[...], w_ref[...])
    mu = jnp.mean(z, axis=-1, keepdims=True)
    zc = z - mu
    var = jnp.mean(zc * zc, axis=-1, keepdims=True)
    o_ref[...] = zc * lax.rsqrt(var + LN_EPS) * g_ref[...] + b_ref[...]


def _mm_res_ln(a, w, x, gain, bias, alpha):
    s, k = a.shape
    d = w.shape[1]
    tm = min(MM_TILE, s)
    row = lambda i: (i, 0)
    whole = lambda i: (0, 0)
    return pl.pallas_call(
        functools.partial(_mm_res_ln_kernel, alpha),
        grid=(s // tm,),
        in_specs=[pl.BlockSpec((tm, k), row), pl.BlockSpec((k, d), whole),
                  pl.BlockSpec((tm, d), row), pl.BlockSpec((1, d), whole), pl.BlockSpec((1, d), whole)],
        out_specs=pl.BlockSpec((tm, d), row),
        out_shape=jax.ShapeDtypeStruct((s, d), F32),
        name="mm_res_ln",
        compiler_params=_cparams("parallel"),
    )(a, w, x, gain.reshape(1, d), bias.reshape(1, d))


def _ffn_up_kernel(n_chunk, x_ref, wg_ref, wu_ref, o_ref):
    xb = x_ref[...].astype(BF16)
    cw = wg_ref.shape[1] // n_chunk
    for c in range(n_chunk):
        sl = slice(c * cw, (c + 1) * cw)
        o_ref[:, sl] = (jax.nn.silu(_dot(xb, wg_ref[:, sl])) * _dot(xb, wu_ref[:, sl])).astype(o_ref.dtype)


def _ffn_up(x, wg, wu):
    s, d = x.shape
    f = wg.shape[1]
    n_chunk = 2 if f % (2 * LANES) == 0 else 1
    tm = min(MM_TILE, s)
    row = lambda i: (i, 0)
    whole = lambda i: (0, 0)
    return pl.pallas_call(
        functools.partial(_ffn_up_kernel, n_chunk),
        grid=(s // tm,),
        in_specs=[pl.BlockSpec((tm, d), row), pl.BlockSpec((d, f), whole), pl.BlockSpec((d, f), whole)],
        out_specs=pl.BlockSpec((tm, f), row),
        out_shape=jax.ShapeDtypeStruct((s, f), BF16),
        name="ffn_up",
        compiler_params=_cparams("parallel"),
    )(x, wg, wu)


def _nsa_mixer(x, tabs, w_in, ck_pos, ck_w1, ck_w2, cv_pos, cv_w1, cv_w2):
    s = x.shape[0]
    g_n, r_n, hd = NSA_KV_HEADS, NSA_GROUP, HEAD_DIM
    q_dim, kv_dim = N_HEADS * hd, NSA_KV_HEADS * hd
    n_sel = s // SEL_BLOCK
    n_var = -(-n_sel // BIAS_BLOCKS)
    assert s % TK == 0 and (BIAS_BLOCKS * SEL_BLOCK) % TK == 0
    wq = w_in[:, :q_dim].astype(BF16)
    wkv = w_in[:, q_dim:q_dim + 6 * kv_dim].astype(BF16)
    wg = w_in[:, q_dim + 6 * kv_dim:].reshape(-1, g_n, 3 * r_n)
    wg = jnp.pad(wg, ((0, 0), (0, 0), (0, LANES - 3 * r_n))).reshape(-1, g_n * LANES).astype(BF16)
    q, cmp_kv, kaug, vaug, kwin, vwin, gates = _nsa_proj(x, tabs, wq, wkv, wg)
    comp = _compress(cmp_kv, jnp.stack([ck_pos, cv_pos]),
                     jnp.stack([ck_w1, cv_w1]).astype(BF16), jnp.stack([ck_w2, cv_w2]).astype(BF16))
    comp = comp.reshape(2, g_n, n_sel, 4, hd).transpose(0, 1, 3, 2, 4).reshape(2, g_n, 4 * n_sel, hd)
    comp = comp.astype(BF16)
    pre, qaug = _nsa_local(q, comp[0], comp[1], kwin, vwin, gates, n_sel, n_var)
    return _nsa_sel(qaug, kaug, vaug, pre, gates)


def _moba_mixer(x, tabs, w_in):
    qaug, kaug, vaug = _moba_proj(x, tabs, w_in.astype(BF16))
    return _moba_attn(qaug, kaug, vaug)


def kernel(x, positions, nsa_w_in, nsa_w_o, nsa_ck_pos, nsa_ck_w1, nsa_ck_w2, nsa_cv_pos, nsa_cv_w1, nsa_cv_w2, moba_w_in, moba_w_o, ffn_wg, ffn_wu, ffn_wd, ln1_g, ln1_b, ln2_g, ln2_b):
    b, s, d = x.shape
    depth = ffn_wg.shape[0]
    alpha = (2 * depth) ** 0.25
    outs = []
    for bi in range(b):
        h = x[bi]
        tabs = _rope_tables(positions[bi])
        for i in range(depth):
            j = i // 2
            if i % 2 == 0:
                a = _nsa_mixer(h, tabs, nsa_w_in[j], nsa_ck_pos[j], nsa_ck_w1[j], nsa_ck_w2[j],
                               nsa_cv_pos[j], nsa_cv_w1[j], nsa_cv_w2[j])
                w_o = nsa_w_o[j]
            else:
                a = _moba_mixer(h, tabs, moba_w_in[j])
                w_o = moba_w_o[j]
            h = _mm_res_ln(a, w_o.astype(BF16), h, ln1_g[i], ln1_b[i], alpha)
            u = _ffn_up(h, ffn_wg[i].astype(BF16), ffn_wu[i].astype(BF16))
            h = _mm_res_ln(u, ffn_wd[i].astype(BF16), h, ln2_g[i], ln2_b[i], alpha)
        outs.append(h)
    return jnp.stack(outs)
```

```python
import functools

import numpy as np
import jax
import jax.numpy as jnp
from jax import lax
from jax.experimental import pallas as pl
from jax.experimental.pallas import tpu as pltpu

F32 = jnp.float32
BF16 = jnp.bfloat16

N_HEADS = 16
HEAD_DIM = 64
ROPE_DIM = HEAD_DIM // 4
ROPE_THETA = 500000.0
NSA_KV_HEADS = 4
NSA_GROUP = N_HEADS // NSA_KV_HEADS
CMP_BLOCK = 32
CMP_STRIDE = 16
SEL_BLOCK = 64
SEL_TOPK = 16
WINDOW = 512
MOBA_BLOCK = 256
MOBA_TOPK = 3
LN_EPS = 1e-5
LARGE = 1e30
SCALE = HEAD_DIM ** -0.5

LANES = 128
VMEM_LIMIT_BYTES = 48 * 1024 * 1024

ROW_TILE = 256
MM_TILE = 512
TQ = 256
TK = 1024
MOBA_TQ = 1024
MOBA_TK = 512
BIAS_BLOCKS = 128
MASKED = -1e30
M_INIT = -5e29

assert WINDOW % TQ == 0 and TK % TQ == 0 and ROW_TILE == MOBA_BLOCK


def _cparams(*sem):
    return pltpu.CompilerParams(dimension_semantics=sem, vmem_limit_bytes=VMEM_LIMIT_BYTES)


def _dot(a, b):
    return jnp.dot(a, b, preferred_element_type=F32)


def _dot_t(a, b, precision=None):
    return lax.dot_general(a, b, (((1,), (1,)), ((), ())), preferred_element_type=F32,
                           precision=precision)


def _rope_tables_kernel(pos_ref, c_ref, cos_ref, sa_ref, sb_ref):
    ang = pos_ref[...].astype(F32) * c_ref[0:1, :]
    c = jnp.cos(ang)
    s = jnp.sin(ang)
    cos_ref[...] = jnp.where(c_ref[1:2, :] > 0, c, 1.0)
    sa_ref[...] = jnp.where(c_ref[2:3, :] > 0, s, 0.0)
    sb_ref[...] = jnp.where(c_ref[3:4, :] > 0, -s, 0.0)


def _rope_tables(positions):
    s = positions.shape[0]
    half = ROPE_DIM // 2
    inv = ROPE_THETA ** (-jnp.arange(0, ROPE_DIM, 2, dtype=F32) / ROPE_DIM)
    d = np.arange(LANES) % HEAD_DIM
    consts = jnp.zeros((8, LANES), F32)
    consts = consts.at[0].set(inv[d % half])
    consts = consts.at[1].set(jnp.asarray(d < ROPE_DIM, F32))
    consts = consts.at[2].set(jnp.asarray((d >= half) & (d < ROPE_DIM), F32))
    consts = consts.at[3].set(jnp.asarray(d < half, F32))
    tm = min(MM_TILE, s)
    tab = jax.ShapeDtypeStruct((s, LANES), F32)
    return pl.pallas_call(
        _rope_tables_kernel,
        grid=(s // tm,),
        in_specs=[pl.BlockSpec((tm, 1), lambda i: (i, 0)),
                  pl.BlockSpec((8, LANES), lambda i: (0, 0))],
        out_specs=[pl.BlockSpec((tm, LANES), lambda i: (i, 0))] * 3,
        out_shape=[tab, tab, tab],
        name="rope_tables",
        compiler_params=_cparams("parallel"),
    )(positions.reshape(s, 1), consts)


def _rope(x, cos_t, sin_a, sin_b):
    half = ROPE_DIM // 2
    outs = []
    for c in range(x.shape[1] // LANES):
        xs = x[:, c * LANES:(c + 1) * LANES]
        outs.append(xs * cos_t + pltpu.roll(xs, half, 1) * sin_a
                    + pltpu.roll(xs, LANES - half, 1) * sin_b)
    return outs[0] if len(outs) == 1 else jnp.concatenate(outs, axis=1)


def _ones_col(rows, width):
    return jnp.where(lax.broadcasted_iota(jnp.int32, (rows, width), 1) == 0, 1.0, 0.0).astype(BF16)


def _nsa_proj_kernel(x_ref, cos_ref, sa_ref, sb_ref, wq_ref, wkv_ref, wg_ref,
                     q_ref, cmp_ref, kaug_ref, vaug_ref, kwin_ref, vwin_ref, gate_ref):
    tm = x_ref.shape[0]
    g_n, hd, kvd = NSA_KV_HEADS, HEAD_DIM, NSA_KV_HEADS * HEAD_DIM
    xb = x_ref[...].astype(BF16)
    cos_t, sin_a, sin_b = cos_ref[...], sa_ref[...], sb_ref[...]
    q = _rope(_dot(xb, wq_ref[...]), cos_t, sin_a, sin_b) * SCALE
    q_ref[...] = q.astype(BF16)
    kv = _dot(xb, wkv_ref[...])
    k_cmp = _rope(kv[:, 0:kvd], cos_t, sin_a, sin_b)
    v_cmp = kv[:, kvd:2 * kvd]
    k_sel = _rope(kv[:, 2 * kvd:3 * kvd], cos_t, sin_a, sin_b)
    v_sel = kv[:, 3 * kvd:4 * kvd]
    k_win = _rope(kv[:, 4 * kvd:5 * kvd], cos_t, sin_a, sin_b)
    v_win = kv[:, 5 * kvd:6 * kvd]
    gate = jax.nn.sigmoid(_dot(xb, wg_ref[...]))
    tok = pl.program_id(0) * tm + lax.broadcasted_iota(jnp.int32, (tm, BIAS_BLOCKS), 0)
    blk = (tok // SEL_BLOCK) % BIAS_BLOCKS
    onehot = jnp.where(lax.broadcasted_iota(jnp.int32, (tm, BIAS_BLOCKS), 1) == blk,
                       1.0, 0.0).astype(BF16)
    ones_col = _ones_col(tm, LANES - hd)
    for g in range(g_n):
        sl = slice(g * hd, (g + 1) * hd)
        cmp_ref[0, g] = k_cmp[:, sl]
        cmp_ref[1, g] = v_cmp[:, sl]
        kwin_ref[g] = k_win[:, sl].astype(BF16)
        vwin_ref[g, :, 0:hd] = v_win[:, sl].astype(BF16)
        vwin_ref[g, :, hd:LANES] = ones_col
        gate_ref[g] = gate[:, g * LANES:(g + 1) * LANES]
        kb = g * 2 * LANES
        kaug_ref[:, kb:kb + BIAS_BLOCKS] = onehot
        kaug_ref[:, kb + BIAS_BLOCKS:kb + BIAS_BLOCKS + hd] = k_sel[:, sl].astype(BF16)
        kaug_ref[:, kb + BIAS_BLOCKS + hd:kb + 2 * LANES] = jnp.zeros((tm, 2 * LANES - BIAS_BLOCKS - hd), BF16)
        vb = g * LANES
        vaug_ref[:, vb:vb + hd] = v_sel[:, sl].astype(BF16)
        vaug_ref[:, vb + hd:vb + LANES] = ones_col


def _nsa_proj(x, tabs, wq, wkv, wg):
    s, d = x.shape
    g_n, hd = NSA_KV_HEADS, HEAD_DIM
    tm = min(ROW_TILE, s)
    row = lambda i: (i, 0)
    whole = lambda i: (0, 0)
    out_shape = [
        jax.ShapeDtypeStruct((s, N_HEADS * hd), BF16),
        jax.ShapeDtypeStruct((2, g_n, s, hd), F32),
        jax.ShapeDtypeStruct((s, g_n * 2 * LANES), BF16),
        jax.ShapeDtypeStruct((s, g_n * LANES), BF16),
        jax.ShapeDtypeStruct((g_n, s, hd), BF16),
        jax.ShapeDtypeStruct((g_n, s, LANES), BF16),
        jax.ShapeDtypeStruct((g_n, s, LANES), F32),
    ]
    out_specs = [
        pl.BlockSpec((tm, N_HEADS * hd), row),
        pl.BlockSpec((2, g_n, tm, hd), lambda i: (0, 0, i, 0)),
        pl.BlockSpec((tm, g_n * 2 * LANES), row),
        pl.BlockSpec((tm, g_n * LANES), row),
        pl.BlockSpec((g_n, tm, hd), lambda i: (0, i, 0)),
        pl.BlockSpec((g_n, tm, LANES), lambda i: (0, i, 0)),
        pl.BlockSpec((g_n, tm, LANES), lambda i: (0, i, 0)),
    ]
    return pl.pallas_call(
        _nsa_proj_kernel,
        grid=(s // tm,),
        in_specs=[pl.BlockSpec((tm, d), row)] + [pl.BlockSpec((tm, LANES), row)] * 3
        + [pl.BlockSpec(wq.shape, whole), pl.BlockSpec(wkv.shape, whole),
           pl.BlockSpec(wg.shape, whole)],
        out_specs=out_specs,
        out_shape=out_shape,
        name="nsa_proj",
        compiler_params=_cparams("parallel"),
    )(x, *tabs, wq, wkv, wg)


def _compress_kernel(x_ref, pos_ref, w1_ref, w2_ref, out_ref):
    n16 = x_ref.shape[2]
    half = w1_ref.shape[1] // 2
    x = x_ref[0, 0]
    a = _dot((x + pos_ref[0, 0:1, :]).astype(BF16), w1_ref[0, 0:half, :])
    b = _dot((x + pos_ref[0, 1:2, :]).astype(BF16), w1_ref[0, half:, :])
    hid = a + pltpu.roll(b, n16 - 1, 0)
    out_ref[0, 0] = _dot(jax.nn.gelu(hid).astype(BF16), w2_ref[0])


def _compress(cmp_kv, pos, w1, w2):
    _, g_n, s, hd = cmp_kv.shape
    n16 = s // CMP_STRIDE
    x = cmp_kv.reshape(2, g_n, n16, CMP_STRIDE * hd)
    pos2 = pos.reshape(2, 2, CMP_STRIDE * hd)
    return pl.pallas_call(
        _compress_kernel,
        grid=(2, g_n),
        in_specs=[pl.BlockSpec((1, 1, n16, CMP_STRIDE * hd), lambda a, g: (a, g, 0, 0)),
                  pl.BlockSpec((1, 2, CMP_STRIDE * hd), lambda a, g: (a, 0, 0)),
                  pl.BlockSpec((1,) + w1.shape[1:], lambda a, g: (a, 0, 0)),
                  pl.BlockSpec((1,) + w2.shape[1:], lambda a, g: (a, 0, 0))],
        out_specs=pl.BlockSpec((1, 1, n16, hd), lambda a, g: (a, g, 0, 0)),
        out_shape=jax.ShapeDtypeStruct((2, g_n, n16, hd), F32),
        name="nsa_compress",
        compiler_params=_cparams("parallel", "parallel"),
    )(x, pos2, w1, w2)


def _top_k_axis0(score, n_rounds, sel, enable=None):
    n = score.shape[0]
    row = lax.broadcasted_iota(jnp.int32, score.shape, 0).astype(F32)
    for _ in range(n_rounds):
        m = jnp.max(score, axis=0, keepdims=True)
        idx = jnp.min(jnp.where(score == m, row, float(n)), axis=0, keepdims=True)
        ok = m > -jnp.inf if enable is None else (m > -jnp.inf) & enable
        hit = row == jnp.where(ok, idx, -1.0)
        sel = jnp.where(hit, 1.0, sel)
        score = jnp.where(hit, -jnp.inf, score)
    return sel, score


def _nsa_local_kernel(n_sel, n_top, q_ref, kc_ref, vc_ref, ov_ref, wmask_ref, *rest):
    n_wb = WINDOW // TQ + 1
    kw_refs = rest[:n_wb]
    vw_refs = rest[n_wb:2 * n_wb]
    gate_ref, pre_ref, qaug_ref, s_scr, m_scr, o_acc, imp_acc = rest[2 * n_wb:]
    r_n, hd = NSA_GROUP, HEAD_DIM
    qi = pl.program_id(1)
    qg = q_ref[...]
    qs = jnp.concatenate([qg[:, r * hd:(r + 1) * hd] for r in range(r_n)], axis=0)
    rows = r_n * TQ

    nc = kc_ref.shape[1]
    ct = min(2 * LANES, nc)
    t_max = ((qi * TQ + TQ - CMP_BLOCK) // CMP_STRIDE) // ct
    t_full = (jnp.maximum(qi * TQ - (CMP_BLOCK - 1) + CMP_STRIDE, 0) // CMP_STRIDE) // ct

    def lane_fold_max(s):
        m = s[:, 0:LANES]
        for c in range(1, s.shape[1] // LANES):
            m = jnp.maximum(m, s[:, c * LANES:(c + 1) * LANES])
        return m

    m_scr[...] = jnp.full(m_scr.shape, M_INIT, F32)
    for t in range(nc // ct):
        tile = slice(t * ct, (t + 1) * ct)

        @pl.when(t < t_full)
        def _():
            s = _dot_t(qs, kc_ref[0, tile, :])
            s_scr[:, tile] = s
            m_scr[...] = jnp.maximum(m_scr[...], lane_fold_max(s))

        @pl.when((t >= t_full) & (t <= t_max))
        def _():
            q_off = lax.broadcasted_iota(jnp.int32, (rows, ct), 0) % TQ
            tok = lax.broadcasted_iota(jnp.int32, (rows, ct), 1)
            visible = q_off - CMP_STRIDE * tok >= CMP_STRIDE * ct * t + CMP_BLOCK - 1 - qi * TQ
            s = jnp.where(visible, _dot_t(qs, kc_ref[0, tile, :]), MASKED)
            s_scr[:, tile] = s
            m_scr[...] = jnp.maximum(m_scr[...], lane_fold_max(s))

    m_scr[...] = jnp.broadcast_to(jnp.max(m_scr[...], axis=1, keepdims=True), m_scr.shape)
    o_acc[...] = jnp.zeros(o_acc.shape, F32)
    imp_acc[...] = jnp.zeros(imp_acc.shape, F32)
    for t in range(nc // ct):
        tile = slice(t * ct, (t + 1) * ct)

        @pl.when(t <= t_max)
        def _():
            m = m_scr[...]
            p = jnp.concatenate(
                [jnp.exp(s_scr[:, t * ct + c * LANES:t * ct + (c + 1) * LANES] - m).astype(BF16)
                 for c in range(ct // LANES)], axis=1)
            o_acc[...] += _dot(p, vc_ref[0, tile, :])
            imp_acc[...] += _dot_t(ov_ref[:, tile], p)

    o_aug = o_acc[...]
    l_col = o_aug[:, hd:hd + 1]
    o_cmp = o_aug[:, 0:hd] * jnp.where(l_col > 0, 1.0 / l_col, 0.0)

    imp_t = imp_acc[...]
    l_row = imp_t[n_sel:n_sel + 1, :]
    imp_t = imp_t[0:n_sel, :] * jnp.where(l_row > 0, 1.0 / l_row, 0.0)
    imp = imp_t[:, 0:TQ]
    for r in range(1, r_n):
        imp = imp + imp_t[:, r * TQ:(r + 1) * TQ]
    blk = lax.broadcasted_iota(jnp.int32, (n_sel, TQ), 0)
    blk_q = (qi * TQ + lax.broadcasted_iota(jnp.int32, (n_sel, TQ), 1)) // SEL_BLOCK
    forced = (blk == 0) | (blk == blk_q) | (blk == blk_q - 1)
    score = jnp.where((blk <= blk_q) & jnp.logical_not(forced), imp, -jnp.inf)
    sel = jnp.where(forced, 1.0, 0.0)
    n_forced_min = 3
    sel, score = _top_k_axis0(score, n_top - n_forced_min, sel)
    blk_q_row = blk_q[0:1, :]
    for extra in range(1, n_forced_min):
        sel, score = _top_k_axis0(score, 1, sel, enable=blk_q_row < n_forced_min - extra)
    bias = jnp.where(sel > 0, 0.0, MASKED).T.astype(BF16)
    n_var = qaug_ref.shape[0]
    if n_sel < n_var * BIAS_BLOCKS:
        bias = jnp.concatenate(
            [bias, jnp.full((TQ, n_var * BIAS_BLOCKS - n_sel), MASKED, BF16)], axis=1)
    zeros = jnp.zeros((TQ, 2 * LANES - BIAS_BLOCKS - hd), BF16)
    for v in range(n_var):
        for r in range(r_n):
            rs = slice(r * TQ, (r + 1) * TQ)
            qaug_ref[v, 0, 0, rs, 0:BIAS_BLOCKS] = bias[:, v * BIAS_BLOCKS:(v + 1) * BIAS_BLOCKS]
            qaug_ref[v, 0, 0, rs, BIAS_BLOCKS:BIAS_BLOCKS + hd] = qg[:, r * hd:(r + 1) * hd]
            qaug_ref[v, 0, 0, rs, BIAS_BLOCKS + hd:2 * LANES] = zeros

    kcat = jnp.concatenate([kw[0] for kw in kw_refs], axis=0)
    vcat = jnp.concatenate([vw[0] for vw in vw_refs], axis=0)
    s = _dot_t(qs, kcat)
    wmask = wmask_ref[0]
    s = jnp.concatenate([s[r * TQ:(r + 1) * TQ] + wmask for r in range(r_n)], axis=0)
    m = jnp.broadcast_to(jnp.max(s, axis=1, keepdims=True), (rows, LANES))
    p = jnp.concatenate([jnp.exp(s[:, c * LANES:(c + 1) * LANES] - m).astype(BF16)
                         for c in range(s.shape[1] // LANES)], axis=1)
    o_aug = _dot(p, vcat)
    o_win = o_aug[:, 0:hd] * (1.0 / o_aug[:, hd:hd + 1])

    gate = gate_ref[0]
    for r in range(r_n):
        rs = slice(r * TQ, (r + 1) * TQ)
        pre_ref[:, r * hd:(r + 1) * hd] = (gate[:, 3 * r:3 * r + 1] * o_cmp[rs]
                                          + gate[:, 3 * r + 2:3 * r + 3] * o_win[rs])


def _overlap_matrix(nc, n_sel):
    i = np.arange(nc)[None, :]
    j = np.arange(n_sel)[:, None]
    lo = np.maximum(i * CMP_STRIDE, j * SEL_BLOCK)
    hi = np.minimum(i * CMP_STRIDE + CMP_BLOCK - 1, j * SEL_BLOCK + SEL_BLOCK - 1)
    ov = np.zeros((n_sel + 8, nc), np.float32)
    ov[:n_sel] = np.clip(hi - lo + 1, 0, None) / CMP_STRIDE
    ov[n_sel] = 1.0
    return jnp.asarray(ov, BF16)


def _window_masks(n_wb):
    q = np.arange(TQ)[:, None]
    c = np.arange(n_wb * TQ)[None, :]
    dist = q + (n_wb - 1) * TQ - c
    masks = []
    for v in range(n_wb):
        kpos = (v - (n_wb - 1)) * TQ + c
        masks.append(np.where((kpos >= 0) & (dist >= 0) & (dist < WINDOW), 0.0, MASKED))
    return jnp.asarray(np.stack(masks), F32)


def _nsa_local(q, kc, vc, kwin, vwin, gates, n_sel, n_var):
    s = q.shape[0]
    g_n, r_n, hd = NSA_KV_HEADS, NSA_GROUP, HEAD_DIM
    n_q = s // TQ
    n_wb = WINDOW // TQ + 1
    n_top = min(SEL_TOPK, n_sel)
    nc = kc.shape[1]
    rows = r_n * TQ
    assert n_top >= 3 and nc % min(2 * LANES, nc) == 0

    def win_spec(j, width):
        return pl.BlockSpec((1, TQ, width), lambda g, i: (g, jnp.maximum(i - (n_wb - 1) + j, 0), 0))

    return pl.pallas_call(
        functools.partial(_nsa_local_kernel, n_sel, n_top),
        grid=(g_n, n_q),
        in_specs=[pl.BlockSpec((TQ, r_n * hd), lambda g, i: (i, g)),
                  pl.BlockSpec((1, nc, hd), lambda g, i: (g, 0, 0)),
                  pl.BlockSpec((1, nc, LANES), lambda g, i: (g, 0, 0)),
                  pl.BlockSpec((n_sel + 8, nc), lambda g, i: (0, 0)),
                  pl.BlockSpec((1, TQ, n_wb * TQ), lambda g, i: (jnp.minimum(i, n_wb - 1), 0, 0))]
        + [win_spec(j, hd) for j in range(n_wb)] + [win_spec(j, LANES) for j in range(n_wb)]
        + [pl.BlockSpec((1, TQ, LANES), lambda g, i: (g, i, 0))],
        out_specs=[pl.BlockSpec((TQ, r_n * hd), lambda g, i: (i, g)),
                   pl.BlockSpec((n_var, 1, 1, rows, 2 * LANES), lambda g, i: (0, g, i, 0, 0))],
        out_shape=[jax.ShapeDtypeStruct((s, N_HEADS * hd), F32),
                   jax.ShapeDtypeStruct((n_var, g_n, n_q, rows, 2 * LANES), BF16)],
        scratch_shapes=[pltpu.VMEM((rows, nc), F32), pltpu.VMEM((rows, LANES), F32),
                        pltpu.VMEM((rows, LANES), F32), pltpu.VMEM((n_sel + 8, rows), F32)],
        name="nsa_local",
        compiler_params=_cparams("parallel", "parallel"),
    )(q, kc, vc, _overlap_matrix(nc, n_sel), _window_masks(n_wb),
      *([kwin] * n_wb), *([vwin] * n_wb), gates)


def _causal_pairs(n_q, tq, tk):
    qs, ks = [], []
    for qi in range(n_q):
        for ki in range(((qi + 1) * tq - 1) // tk + 1):
            qs.append(qi)
            ks.append(ki)
    return jnp.asarray(qs, jnp.int32), jnp.asarray(ks, jnp.int32)


def _flash_update(q, k, v, m_ref, acc_ref, idx, causal_mask):
    s = _dot_t(q, k)
    if causal_mask is not None:
        s = jnp.where(causal_mask, s, MASKED)
    m_prev = m_ref[idx]
    m_new = jnp.maximum(m_prev, jnp.max(s, axis=1, keepdims=True))
    p = jnp.concatenate([jnp.exp(s[:, c * LANES:(c + 1) * LANES] - m_new).astype(BF16)
                         for c in range(s.shape[1] // LANES)], axis=1)
    acc_ref[idx] = jnp.exp(m_prev - m_new) * acc_ref[idx] + _dot(p, v)
    m_ref[idx] = m_new


def _flash_steps(qi, ki, tq, tk, rows_per_q, n_batch, operands, m_ref, acc_ref, finalize):
    first_diag = (qi * tq) // tk
    last_k = ((qi + 1) * tq - 1) // tk

    @pl.when(ki == 0)
    def _():
        m_ref[...] = jnp.full(m_ref.shape, M_INIT, F32)
        acc_ref[...] = jnp.zeros(acc_ref.shape, F32)

    @pl.when(ki < first_diag)
    def _():
        for b in range(n_batch):
            q, k, v = operands(b)
            _flash_update(q, k, v, m_ref, acc_ref, b, None)

    @pl.when(ki >= first_diag)
    def _():
        shape = (rows_per_q * tq, tk)
        q_off = lax.broadcasted_iota(jnp.int32, shape, 0) % tq
        k_off = lax.broadcasted_iota(jnp.int32, shape, 1)
        mask = q_off - k_off >= ki * tk - qi * tq
        for b in range(n_batch):
            q, k, v = operands(b)
            _flash_update(q, k, v, m_ref, acc_ref, b, mask)

    @pl.when(ki == last_k)
    def _():
        finalize()


def _nsa_sel_kernel(qtab, ktab, qaug_ref, kaug_ref, vaug_ref, pre_ref, gate_ref, o_ref,
                    m_ref, acc_ref):
    g_n, r_n, hd = NSA_KV_HEADS, NSA_GROUP, HEAD_DIM
    t = pl.program_id(0)
    qi, ki = qtab[t], ktab[t]

    def operands(g):
        return (qaug_ref[0, g, 0], kaug_ref[:, g * 2 * LANES:(g + 1) * 2 * LANES],
                vaug_ref[:, g * LANES:(g + 1) * LANES])

    def finalize():
        for g in range(g_n):
            acc = acc_ref[g]
            gate = gate_ref[g]
            o = acc[:, 0:hd] * (1.0 / acc[:, hd:hd + 1])
            for r in range(r_n):
                c = (g * r_n + r) * hd
                o_ref[:, c:c + hd] = (pre_ref[:, c:c + hd] + gate[:, 3 * r + 1:3 * r + 2]
                                      * o[r * TQ:(r + 1) * TQ]).astype(o_ref.dtype)

    _flash_steps(qi, ki, TQ, TK, r_n, g_n, operands, m_ref, acc_ref, finalize)


def _nsa_sel(qaug, kaug, vaug, pre, gates):
    n_var, g_n, n_q, rows, _ = qaug.shape
    s = kaug.shape[0]
    qtab, ktab = _causal_pairs(n_q, TQ, TK)
    tiles_per_var = BIAS_BLOCKS * SEL_BLOCK // TK
    grid_spec = pltpu.PrefetchScalarGridSpec(
        num_scalar_prefetch=2,
        grid=(qtab.shape[0],),
        in_specs=[
            pl.BlockSpec((1, g_n, 1, rows, 2 * LANES),
                         lambda t, qt, kt: (kt[t] // tiles_per_var, 0, qt[t], 0, 0)),
            pl.BlockSpec((TK, g_n * 2 * LANES), lambda t, qt, kt: (kt[t], 0)),
            pl.BlockSpec((TK, g_n * LANES), lambda t, qt, kt: (kt[t], 0)),
            pl.BlockSpec((TQ, N_HEADS * HEAD_DIM), lambda t, qt, kt: (qt[t], 0)),
            pl.BlockSpec((g_n, TQ, LANES), lambda t, qt, kt: (0, qt[t], 0)),
        ],
        out_specs=pl.BlockSpec((TQ, N_HEADS * HEAD_DIM), lambda t, qt, kt: (qt[t], 0)),
        scratch_shapes=[pltpu.VMEM((g_n, rows, LANES), F32), pltpu.VMEM((g_n, rows, LANES), F32)],
    )
    return pl.pallas_call(
        _nsa_sel_kernel,
        grid_spec=grid_spec,
        out_shape=jax.ShapeDtypeStruct((s, N_HEADS * HEAD_DIM), BF16),
        name="nsa_sel",
        compiler_params=_cparams("arbitrary"),
    )(qtab, ktab, qaug, kaug, vaug, pre, gates)


def _moba_proj_kernel(n_blk, k_eff, x_ref, cos_ref, sa_ref, sb_ref, w_ref,
                      qaug_ref, kaug_ref, vaug_ref, kmean_ref):
    tm = x_ref.shape[0]
    hd, d = HEAD_DIM, N_HEADS * HEAD_DIM
    i = pl.program_id(0)

    @pl.when(i == 0)
    def _():
        kmean_ref[...] = jnp.zeros(kmean_ref.shape, F32)

    xb = x_ref[...].astype(BF16)
    cos_t, sin_a, sin_b = cos_ref[...], sa_ref[...], sb_ref[...]
    q = _rope(_dot(xb, w_ref[:, 0:d]), cos_t, sin_a, sin_b)
    k = _rope(_dot(xb, w_ref[:, d:2 * d]), cos_t, sin_a, sin_b)
    v = _dot(xb, w_ref[:, 2 * d:3 * d])
    kmean = kmean_ref[...]
    nb = kmean.shape[0]
    own = jnp.where(lax.broadcasted_iota(jnp.int32, (1, nb), 1) == i, 1.0, 0.0).astype(BF16)
    own_onehot = jnp.broadcast_to(own, (tm, nb))
    ones_col = _ones_col(tm, LANES - hd)
    blk_t = lax.broadcasted_iota(jnp.int32, (nb, tm), 0)
    own_t = jnp.where(blk_t == i, 1.0, 0.0)
    for hp in range(N_HEADS // 2):
        bias_t = []
        for h in (2 * hp, 2 * hp + 1):
            sl = slice(h * hd, (h + 1) * hd)
            gate_t = _dot_t(kmean[:, sl], q[:, sl], precision=lax.Precision.HIGHEST)
            gate_t = jnp.where(blk_t < i, gate_t, -jnp.inf)
            bias_t.append(jnp.where(_top_k_axis0(gate_t, k_eff, own_t)[0] > 0, 0.0, MASKED))
        bias = jnp.concatenate(bias_t, axis=0).T.astype(BF16)
        for j, h in enumerate((2 * hp, 2 * hp + 1)):
            sl = slice(h * hd, (h + 1) * hd)
            qaug_ref[h, :, 0:hd] = (q[:, sl] * SCALE).astype(BF16)
            qaug_ref[h, :, hd:hd + nb] = bias[:, j * nb:(j + 1) * nb]
            kaug_ref[h, :, 0:hd] = k[:, sl].astype(BF16)
            kaug_ref[h, :, hd:hd + nb] = own_onehot
            vaug_ref[h, :, 0:hd] = v[:, sl].astype(BF16)
            vaug_ref[h, :, hd:LANES] = ones_col
    kmean_ref[pl.ds(i, 1), :] = jnp.mean(k, axis=0, keepdims=True)


def _moba_proj(x, tabs, w):
    s, d = x.shape
    n_blk = s // MOBA_BLOCK
    k_eff = min(MOBA_TOPK, max(n_blk - 1, 1))
    nb = LANES - HEAD_DIM
    assert s % MOBA_BLOCK == 0 and n_blk <= nb
    tm = MOBA_BLOCK
    row = lambda i: (i, 0)
    head = jax.ShapeDtypeStruct((N_HEADS, s, LANES), BF16)
    head_spec = pl.BlockSpec((N_HEADS, tm, LANES), lambda i: (0, i, 0))
    return pl.pallas_call(
        functools.partial(_moba_proj_kernel, n_blk, k_eff),
        grid=(s // tm,),
        in_specs=[pl.BlockSpec((tm, d), row)] + [pl.BlockSpec((tm, LANES), row)] * 3
        + [pl.BlockSpec(w.shape, lambda i: (0, 0))],
        out_specs=[head_spec] * 3,
        out_shape=[head] * 3,
        scratch_shapes=[pltpu.VMEM((nb, N_HEADS * HEAD_DIM), F32)],
        name="moba_proj",
        compiler_params=_cparams("arbitrary"),
    )(x, *tabs, w)


def _moba_attn_kernel(qtab, ktab, q_ref, k_ref, v_ref, o_ref, m_ref, acc_ref):
    hd = HEAD_DIM
    t = pl.program_id(0)
    qi, ki = qtab[t], ktab[t]

    def operands(h):
        return q_ref[h], k_ref[h], v_ref[h]

    def finalize():
        for h in range(N_HEADS):
            acc = acc_ref[h]
            o_ref[:, h * hd:(h + 1) * hd] = (acc[:, 0:hd] * (1.0 / acc[:, hd:hd + 1])).astype(o_ref.dtype)

    _flash_steps(qi, ki, q_ref.shape[1], k_ref.shape[1], 1, N_HEADS, operands, m_ref, acc_ref, finalize)


def _moba_attn(qaug, kaug, vaug):
    h_n, s, _ = qaug.shape
    tq, tk = min(MOBA_TQ, s), min(MOBA_TK, s)
    assert s % tq == 0 and s % tk == 0
    qtab, ktab = _causal_pairs(s // tq, tq, tk)
    grid_spec = pltpu.PrefetchScalarGridSpec(
        num_scalar_prefetch=2,
        grid=(qtab.shape[0],),
        in_specs=[pl.BlockSpec((h_n, tq, LANES), lambda t, qt, kt: (0, qt[t], 0)),
                  pl.BlockSpec((h_n, tk, LANES), lambda t, qt, kt: (0, kt[t], 0)),
                  pl.BlockSpec((h_n, tk, LANES), lambda t, qt, kt: (0, kt[t], 0))],
        out_specs=pl.BlockSpec((tq, h_n * HEAD_DIM), lambda t, qt, kt: (qt[t], 0)),
        scratch_shapes=[pltpu.VMEM((h_n, tq, LANES), F32), pltpu.VMEM((h_n, tq, LANES), F32)],
    )
    return pl.pallas_call(
        _moba_attn_kernel,
        grid_spec=grid_spec,
        out_shape=jax.ShapeDtypeStruct((s, h_n * HEAD_DIM), BF16),
        name="moba_attn",
        compiler_params=_cparams("arbitrary"),
    )(qtab, ktab, qaug, kaug, vaug)


def _mm_res_ln_kernel(alpha, a_ref, w_ref, x_ref, g_ref, b_ref, o_ref):
    z = alpha * x_ref[...] + _dot(a_ref[...], w_ref[...])
    mu = jnp.mean(z, axis=-1, keepdims=True)
    zc = z - mu
    var = jnp.mean(zc * zc, axis=-1, keepdims=True)
    o_ref[...] = zc * lax.rsqrt(var + LN_EPS) * g_ref[...] + b_ref[...]


def _mm_res_ln(a, w, x, gain, bias, alpha):
    s, k = a.shape
    d = w.shape[1]
    tm = min(MM_TILE, s)
    row = lambda i: (i, 0)
    whole = lambda i: (0, 0)
    return pl.pallas_call(
        functools.partial(_mm_res_ln_kernel, alpha),
        grid=(s // tm,),
        in_specs=[pl.BlockSpec((tm, k), row), pl.BlockSpec((k, d), whole),
                  pl.BlockSpec((tm, d), row), pl.BlockSpec((1, d), whole), pl.BlockSpec((1, d), whole)],
        out_specs=pl.BlockSpec((tm, d), row),
        out_shape=jax.ShapeDtypeStruct((s, d), F32),
        name="mm_res_ln",
        compiler_params=_cparams("parallel"),
    )(a, w, x, gain.reshape(1, d), bias.reshape(1, d))


def _ffn_up_kernel(n_chunk, x_ref, wg_ref, wu_ref, o_ref):
    xb = x_ref[...].astype(BF16)
    cw = wg_ref.shape[1] // n_chunk
    for c in range(n_chunk):
        sl = slice(c * cw, (c + 1) * cw)
        o_ref[:, sl] = (jax.nn.silu(_dot(xb, wg_ref[:, sl])) * _dot(xb, wu_ref[:, sl])).astype(o_ref.dtype)


def _ffn_up(x, wg, wu):
    s, d = x.shape
    f = wg.shape[1]
    n_chunk = 2 if f % (2 * LANES) == 0 else 1
    tm = min(MM_TILE, s)
    row = lambda i: (i, 0)
    whole = lambda i: (0, 0)
    return pl.pallas_call(
        functools.partial(_ffn_up_kernel, n_chunk),
        grid=(s // tm,),
        in_specs=[pl.BlockSpec((tm, d), row), pl.BlockSpec((d, f), whole), pl.BlockSpec((d, f), whole)],
        out_specs=pl.BlockSpec((tm, f), row),
        out_shape=jax.ShapeDtypeStruct((s, f), BF16),
        name="ffn_up",
        compiler_params=_cparams("parallel"),
    )(x, wg, wu)


def _nsa_mixer(x, tabs, w_in, ck_pos, ck_w1, ck_w2, cv_pos, cv_w1, cv_w2):
    s = x.shape[0]
    g_n, r_n, hd = NSA_KV_HEADS, NSA_GROUP, HEAD_DIM
    q_dim, kv_dim = N_HEADS * hd, NSA_KV_HEADS * hd
    n_sel = s // SEL_BLOCK
    n_var = -(-n_sel // BIAS_BLOCKS)
    assert s % TK == 0 and (BIAS_BLOCKS * SEL_BLOCK) % TK == 0
    wq = w_in[:, :q_dim].astype(BF16)
    wkv = w_in[:, q_dim:q_dim + 6 * kv_dim].astype(BF16)
    wg = w_in[:, q_dim + 6 * kv_dim:].reshape(-1, g_n, 3 * r_n)
    wg = jnp.pad(wg, ((0, 0), (0, 0), (0, LANES - 3 * r_n))).reshape(-1, g_n * LANES).astype(BF16)
    q, cmp_kv, kaug, vaug, kwin, vwin, gates = _nsa_proj(x, tabs, wq, wkv, wg)
    comp = _compress(cmp_kv, jnp.stack([ck_pos, cv_pos]),
                     jnp.stack([ck_w1, cv_w1]).astype(BF16), jnp.stack([ck_w2, cv_w2]).astype(BF16))
    kc = comp[0].astype(BF16)
    ones = jnp.ones(comp.shape[1:3] + (1,), F32)
    zeros = jnp.zeros(comp.shape[1:3] + (LANES - hd - 1,), F32)
    vc = jnp.concatenate([comp[1], ones, zeros], axis=-1).astype(BF16)
    pre, qaug = _nsa_local(q, kc, vc, kwin, vwin, gates, n_sel, n_var)
    return _nsa_sel(qaug, kaug, vaug, pre, gates)


def _moba_mixer(x, tabs, w_in):
    qaug, kaug, vaug = _moba_proj(x, tabs, w_in.astype(BF16))
    return _moba_attn(qaug, kaug, vaug)


def kernel(x, positions, nsa_w_in, nsa_w_o, nsa_ck_pos, nsa_ck_w1, nsa_ck_w2, nsa_cv_pos, nsa_cv_w1, nsa_cv_w2, moba_w_in, moba_w_o, ffn_wg, ffn_wu, ffn_wd, ln1_g, ln1_b, ln2_g, ln2_b):
    b, s, d = x.shape
    depth = ffn_wg.shape[0]
    alpha = (2 * depth) ** 0.25
    outs = []
    for bi in range(b):
        h = x[bi]
        tabs = _rope_tables(positions[bi])
        for i in range(depth):
            j = i // 2
            if i % 2 == 0:
                a = _nsa_mixer(h, tabs, nsa_w_in[j], nsa_ck_pos[j], nsa_ck_w1[j], nsa_ck_w2[j],
                               nsa_cv_pos[j], nsa_cv_w1[j], nsa_cv_w2[j])
                w_o = nsa_w_o[j]
            else:
                a = _moba_mixer(h, tabs, moba_w_in[j])
                w_o = moba_w_o[j]
            h = _mm_res_ln(a, w_o.astype(BF16), h, ln1_g[i], ln1_b[i], alpha)
            u = _ffn_up(h, ffn_wg[i].astype(BF16), ffn_wu[i].astype(BF16))
            h = _mm_res_ln(u, ffn_wd[i].astype(BF16), h, ln2_g[i], ln2_b[i], alpha)
        outs.append(h)
    return jnp.stack(outs)
```

```python
import functools

import numpy as np
import jax
import jax.numpy as jnp
from jax import lax
from jax.experimental import pallas as pl
from jax.experimental.pallas import tpu as pltpu

F32 = jnp.float32
BF16 = jnp.bfloat16

N_HEADS = 16
HEAD_DIM = 64
ROPE_DIM = HEAD_DIM // 4
ROPE_THETA = 500000.0
NSA_KV_HEADS = 4
NSA_GROUP = N_HEADS // NSA_KV_HEADS
CMP_BLOCK = 32
CMP_STRIDE = 16
SEL_BLOCK = 64
SEL_TOPK = 16
WINDOW = 512
MOBA_BLOCK = 256
MOBA_TOPK = 3
LN_EPS = 1e-5
LARGE = 1e30
SCALE = HEAD_DIM ** -0.5 * float(np.log2(np.e))

LANES = 128
VMEM_LIMIT_BYTES = 48 * 1024 * 1024

ROW_TILE = 256
MM_TILE = 512
TQ = 256
TK = 1024
MOBA_TQ = 1024
MOBA_TK = 512
BIAS_BLOCKS = 128
MASKED = -1e30
M_INIT = -5e29

assert WINDOW % TQ == 0 and TK % TQ == 0 and ROW_TILE == MOBA_BLOCK


def _cparams(*sem, flags=None):
    return pltpu.CompilerParams(dimension_semantics=sem, vmem_limit_bytes=VMEM_LIMIT_BYTES, flags=flags)


def _dot(a, b):
    return jnp.dot(a, b, preferred_element_type=F32)


def _dot_t(a, b, precision=None):
    return lax.dot_general(a, b, (((1,), (1,)), ((), ())), preferred_element_type=F32,
                           precision=precision)


def _rope_tables_kernel(pos_ref, c_ref, cos_ref, sa_ref, sb_ref):
    ang = pos_ref[...].astype(F32) * c_ref[0:1, :]
    c = jnp.cos(ang)
    s = jnp.sin(ang)
    cos_ref[...] = jnp.where(c_ref[1:2, :] > 0, c, 1.0)
    sa_ref[...] = jnp.where(c_ref[2:3, :] > 0, s, 0.0)
    sb_ref[...] = jnp.where(c_ref[3:4, :] > 0, -s, 0.0)


def _rope_tables(positions):
    s = positions.shape[0]
    half = ROPE_DIM // 2
    inv = ROPE_THETA ** (-jnp.arange(0, ROPE_DIM, 2, dtype=F32) / ROPE_DIM)
    d = np.arange(LANES) % HEAD_DIM
    consts = jnp.zeros((8, LANES), F32)
    consts = consts.at[0].set(inv[d % half])
    consts = consts.at[1].set(jnp.asarray(d < ROPE_DIM, F32))
    consts = consts.at[2].set(jnp.asarray((d >= half) & (d < ROPE_DIM), F32))
    consts = consts.at[3].set(jnp.asarray(d < half, F32))
    tm = min(MM_TILE, s)
    tab = jax.ShapeDtypeStruct((s, LANES), F32)
    return pl.pallas_call(
        _rope_tables_kernel,
        grid=(s // tm,),
        in_specs=[pl.BlockSpec((tm, 1), lambda i: (i, 0)),
                  pl.BlockSpec((8, LANES), lambda i: (0, 0))],
        out_specs=[pl.BlockSpec((tm, LANES), lambda i: (i, 0))] * 3,
        out_shape=[tab, tab, tab],
        name="rope_tables",
        compiler_params=_cparams("parallel"),
    )(positions.reshape(s, 1), consts)


def _rope(x, cos_t, sin_a, sin_b):
    half = ROPE_DIM // 2
    outs = []
    for c in range(x.shape[1] // LANES):
        xs = x[:, c * LANES:(c + 1) * LANES]
        outs.append(xs * cos_t + pltpu.roll(xs, half, 1) * sin_a
                    + pltpu.roll(xs, LANES - half, 1) * sin_b)
    return outs[0] if len(outs) == 1 else jnp.concatenate(outs, axis=1)


def _ones_col(rows, width):
    return jnp.where(lax.broadcasted_iota(jnp.int32, (rows, width), 1) == 0, 1.0, 0.0).astype(BF16)


def _nsa_proj_kernel(x_ref, cos_ref, sa_ref, sb_ref, wq_ref, wkv_ref, wg_ref,
                     q_ref, cmp_ref, kaug_ref, vaug_ref, kwin_ref, vwin_ref, gate_ref):
    tm = x_ref.shape[0]
    g_n, hd, kvd = NSA_KV_HEADS, HEAD_DIM, NSA_KV_HEADS * HEAD_DIM
    xb = x_ref[...].astype(BF16)
    cos_t, sin_a, sin_b = cos_ref[...], sa_ref[...], sb_ref[...]
    q = _rope(_dot(xb, wq_ref[...]), cos_t, sin_a, sin_b) * SCALE
    q_ref[...] = q.astype(BF16)
    kv = _dot(xb, wkv_ref[...])
    k_cmp = _rope(kv[:, 0:kvd], cos_t, sin_a, sin_b)
    v_cmp = kv[:, kvd:2 * kvd]
    k_sel = _rope(kv[:, 2 * kvd:3 * kvd], cos_t, sin_a, sin_b)
    v_sel = kv[:, 3 * kvd:4 * kvd]
    k_win = _rope(kv[:, 4 * kvd:5 * kvd], cos_t, sin_a, sin_b)
    v_win = kv[:, 5 * kvd:6 * kvd]
    gate = jax.nn.sigmoid(_dot(xb, wg_ref[...]))
    tok = pl.program_id(0) * tm + lax.broadcasted_iota(jnp.int32, (tm, BIAS_BLOCKS), 0)
    blk = (tok // SEL_BLOCK) % BIAS_BLOCKS
    onehot = jnp.where(lax.broadcasted_iota(jnp.int32, (tm, BIAS_BLOCKS), 1) == blk,
                       1.0, 0.0).astype(BF16)
    ones_col = _ones_col(tm, LANES - hd)
    for g in range(g_n):
        sl = slice(g * hd, (g + 1) * hd)
        cmp_ref[0, g] = k_cmp[:, sl]
        cmp_ref[1, g] = v_cmp[:, sl]
        kwin_ref[g] = k_win[:, sl].astype(BF16)
        vwin_ref[g, :, 0:hd] = v_win[:, sl].astype(BF16)
        vwin_ref[g, :, hd:LANES] = ones_col
        gate_ref[g] = gate[:, g * LANES:(g + 1) * LANES]
        kb = g * 2 * LANES
        kaug_ref[:, kb:kb + BIAS_BLOCKS] = onehot
        kaug_ref[:, kb + BIAS_BLOCKS:kb + BIAS_BLOCKS + hd] = k_sel[:, sl].astype(BF16)
        kaug_ref[:, kb + BIAS_BLOCKS + hd:kb + 2 * LANES] = jnp.zeros((tm, 2 * LANES - BIAS_BLOCKS - hd), BF16)
        vb = g * LANES
        vaug_ref[:, vb:vb + hd] = v_sel[:, sl].astype(BF16)
        vaug_ref[:, vb + hd:vb + LANES] = ones_col


def _nsa_proj(x, tabs, wq, wkv, wg):
    s, d = x.shape
    g_n, hd = NSA_KV_HEADS, HEAD_DIM
    tm = min(ROW_TILE, s)
    row = lambda i: (i, 0)
    whole = lambda i: (0, 0)
    out_shape = [
        jax.ShapeDtypeStruct((s, N_HEADS * hd), BF16),
        jax.ShapeDtypeStruct((2, g_n, s, hd), F32),
        jax.ShapeDtypeStruct((s, g_n * 2 * LANES), BF16),
        jax.ShapeDtypeStruct((s, g_n * LANES), BF16),
        jax.ShapeDtypeStruct((g_n, s, hd), BF16),
        jax.ShapeDtypeStruct((g_n, s, LANES), BF16),
        jax.ShapeDtypeStruct((g_n, s, LANES), F32),
    ]
    out_specs = [
        pl.BlockSpec((tm, N_HEADS * hd), row),
        pl.BlockSpec((2, g_n, tm, hd), lambda i: (0, 0, i, 0)),
        pl.BlockSpec((tm, g_n * 2 * LANES), row),
        pl.BlockSpec((tm, g_n * LANES), row),
        pl.BlockSpec((g_n, tm, hd), lambda i: (0, i, 0)),
        pl.BlockSpec((g_n, tm, LANES), lambda i: (0, i, 0)),
        pl.BlockSpec((g_n, tm, LANES), lambda i: (0, i, 0)),
    ]
    return pl.pallas_call(
        _nsa_proj_kernel,
        grid=(s // tm,),
        in_specs=[pl.BlockSpec((tm, d), row)] + [pl.BlockSpec((tm, LANES), row)] * 3
        + [pl.BlockSpec(wq.shape, whole), pl.BlockSpec(wkv.shape, whole),
           pl.BlockSpec(wg.shape, whole)],
        out_specs=out_specs,
        out_shape=out_shape,
        name="nsa_proj",
        compiler_params=_cparams("parallel"),
    )(x, *tabs, wq, wkv, wg)


def _compress_kernel(x_ref, pos_ref, w1_ref, w2_ref, out_ref):
    n16 = x_ref.shape[2]
    half = w1_ref.shape[1] // 2
    x = x_ref[0, 0]
    a = _dot((x + pos_ref[0, 0:1, :]).astype(BF16), w1_ref[0, 0:half, :])
    b = _dot((x + pos_ref[0, 1:2, :]).astype(BF16), w1_ref[0, half:, :])
    hid = a + pltpu.roll(b, n16 - 1, 0)
    out_ref[0, 0] = _dot(jax.nn.gelu(hid).astype(BF16), w2_ref[0])


def _compress(cmp_kv, pos, w1, w2):
    _, g_n, s, hd = cmp_kv.shape
    n16 = s // CMP_STRIDE
    x = cmp_kv.reshape(2, g_n, n16, CMP_STRIDE * hd)
    pos2 = pos.reshape(2, 2, CMP_STRIDE * hd)
    return pl.pallas_call(
        _compress_kernel,
        grid=(2, g_n),
        in_specs=[pl.BlockSpec((1, 1, n16, CMP_STRIDE * hd), lambda a, g: (a, g, 0, 0)),
                  pl.BlockSpec((1, 2, CMP_STRIDE * hd), lambda a, g: (a, 0, 0)),
                  pl.BlockSpec((1,) + w1.shape[1:], lambda a, g: (a, 0, 0)),
                  pl.BlockSpec((1,) + w2.shape[1:], lambda a, g: (a, 0, 0))],
        out_specs=pl.BlockSpec((1, 1, n16, hd), lambda a, g: (a, g, 0, 0)),
        out_shape=jax.ShapeDtypeStruct((2, g_n, n16, hd), F32),
        name="nsa_compress",
        compiler_params=_cparams("parallel", "parallel"),
    )(x, pos2, w1, w2)


def _top_k_axis0(score, n_rounds, sel, enable=None):
    n = score.shape[0]
    row = lax.broadcasted_iota(jnp.int32, score.shape, 0).astype(F32)
    for _ in range(n_rounds):
        m = jnp.max(score, axis=0, keepdims=True)
        idx = jnp.min(jnp.where(score == m, row, float(n)), axis=0, keepdims=True)
        ok = m > -jnp.inf if enable is None else (m > -jnp.inf) & enable
        hit = row == jnp.where(ok, idx, -1.0)
        sel = jnp.where(hit, 1.0, sel)
        score = jnp.where(hit, -jnp.inf, score)
    return sel, score


def _nsa_local_kernel(n_sel, n_top, q_ref, kc_ref, vc_ref, ov_ref, wmask_ref, *rest):
    n_wb = WINDOW // TQ + 1
    kw_refs = rest[:n_wb]
    vw_refs = rest[n_wb:2 * n_wb]
    gate_ref, pre_ref, qaug_ref, s_scr, m_scr, o_acc, imp_acc = rest[2 * n_wb:]
    r_n, hd = NSA_GROUP, HEAD_DIM
    qi = pl.program_id(1)
    qg = q_ref[...]
    qs = jnp.concatenate([qg[:, r * hd:(r + 1) * hd] for r in range(r_n)], axis=0)
    rows = r_n * TQ

    nc = kc_ref.shape[1]
    ct = min(2 * LANES, nc)
    t_max = ((qi * TQ + TQ - CMP_BLOCK) // CMP_STRIDE) // ct
    t_full = (jnp.maximum(qi * TQ - (CMP_BLOCK - 1) + CMP_STRIDE, 0) // CMP_STRIDE) // ct

    def lane_fold_max(s):
        m = s[:, 0:LANES]
        for c in range(1, s.shape[1] // LANES):
            m = jnp.maximum(m, s[:, c * LANES:(c + 1) * LANES])
        return m

    m_scr[...] = jnp.full(m_scr.shape, M_INIT, F32)
    for t in range(nc // ct):
        tile = slice(t * ct, (t + 1) * ct)

        @pl.when(t < t_full)
        def _():
            s = _dot_t(qs, kc_ref[0, tile, :])
            s_scr[:, tile] = s
            m_scr[...] = jnp.maximum(m_scr[...], lane_fold_max(s))

        @pl.when((t >= t_full) & (t <= t_max))
        def _():
            q_off = lax.broadcasted_iota(jnp.int32, (rows, ct), 0) % TQ
            tok = lax.broadcasted_iota(jnp.int32, (rows, ct), 1)
            visible = q_off - CMP_STRIDE * tok >= CMP_STRIDE * ct * t + CMP_BLOCK - 1 - qi * TQ
            s = jnp.where(visible, _dot_t(qs, kc_ref[0, tile, :]), MASKED)
            s_scr[:, tile] = s
            m_scr[...] = jnp.maximum(m_scr[...], lane_fold_max(s))

    m_scr[...] = jnp.broadcast_to(jnp.max(m_scr[...], axis=1, keepdims=True), m_scr.shape)
    o_acc[...] = jnp.zeros(o_acc.shape, F32)
    imp_acc[...] = jnp.zeros(imp_acc.shape, F32)
    for t in range(nc // ct):
        tile = slice(t * ct, (t + 1) * ct)

        @pl.when(t <= t_max)
        def _():
            m = m_scr[...]
            p = jnp.concatenate(
                [jnp.exp2(s_scr[:, t * ct + c * LANES:t * ct + (c + 1) * LANES] - m).astype(BF16)
                 for c in range(ct // LANES)], axis=1)
            o_acc[...] += _dot(p, vc_ref[0, tile, :])
            imp_acc[...] += _dot_t(ov_ref[:, tile], p)

    o_aug = o_acc[...]
    l_col = o_aug[:, hd:hd + 1]
    o_cmp = o_aug[:, 0:hd] * jnp.where(l_col > 0, 1.0 / l_col, 0.0)

    imp_t = imp_acc[...]
    l_row = imp_t[n_sel:n_sel + 1, :]
    imp_t = imp_t[0:n_sel, :] * jnp.where(l_row > 0, 1.0 / l_row, 0.0)
    imp = imp_t[:, 0:TQ]
    for r in range(1, r_n):
        imp = imp + imp_t[:, r * TQ:(r + 1) * TQ]
    blk = lax.broadcasted_iota(jnp.int32, (n_sel, TQ), 0)
    blk_q = (qi * TQ + lax.broadcasted_iota(jnp.int32, (n_sel, TQ), 1)) // SEL_BLOCK
    forced = (blk == 0) | (blk == blk_q) | (blk == blk_q - 1)
    score = jnp.where((blk <= blk_q) & jnp.logical_not(forced), imp, -jnp.inf)
    sel = jnp.where(forced, 1.0, 0.0)
    n_forced_min = 3
    blk_q_row = blk_q[0:1, :]
    sel_tiles = []
    for c in range(TQ // LANES):
        ls = slice(c * LANES, (c + 1) * LANES)
        sel_c, score_c = _top_k_axis0(score[:, ls], n_top - n_forced_min, sel[:, ls])
        for extra in range(1, n_forced_min):
            sel_c, score_c = _top_k_axis0(score_c, 1, sel_c,
                                          enable=blk_q_row[:, ls] < n_forced_min - extra)
        sel_tiles.append(sel_c)
    sel = jnp.concatenate(sel_tiles, axis=1)
    bias = jnp.where(sel > 0, 0.0, MASKED).T.astype(BF16)
    n_var = qaug_ref.shape[0]
    if n_sel < n_var * BIAS_BLOCKS:
        bias = jnp.concatenate(
            [bias, jnp.full((TQ, n_var * BIAS_BLOCKS - n_sel), MASKED, BF16)], axis=1)
    zeros = jnp.zeros((TQ, 2 * LANES - BIAS_BLOCKS - hd), BF16)
    for v in range(n_var):
        for r in range(r_n):
            rs = slice(r * TQ, (r + 1) * TQ)
            qaug_ref[v, 0, 0, rs, 0:BIAS_BLOCKS] = bias[:, v * BIAS_BLOCKS:(v + 1) * BIAS_BLOCKS]
            qaug_ref[v, 0, 0, rs, BIAS_BLOCKS:BIAS_BLOCKS + hd] = qg[:, r * hd:(r + 1) * hd]
            qaug_ref[v, 0, 0, rs, BIAS_BLOCKS + hd:2 * LANES] = zeros

    kcat = jnp.concatenate([kw[0] for kw in kw_refs], axis=0)
    vcat = jnp.concatenate([vw[0] for vw in vw_refs], axis=0)
    s = _dot_t(qs, kcat)
    wmask = wmask_ref[0]
    s = jnp.concatenate([s[r * TQ:(r + 1) * TQ] + wmask for r in range(r_n)], axis=0)
    m = jnp.broadcast_to(jnp.max(s, axis=1, keepdims=True), (rows, LANES))
    p = jnp.concatenate([jnp.exp2(s[:, c * LANES:(c + 1) * LANES] - m).astype(BF16)
                         for c in range(s.shape[1] // LANES)], axis=1)
    o_aug = _dot(p, vcat)
    o_win = o_aug[:, 0:hd] * (1.0 / o_aug[:, hd:hd + 1])

    gate = gate_ref[0]
    for r in range(r_n):
        rs = slice(r * TQ, (r + 1) * TQ)
        pre_ref[:, r * hd:(r + 1) * hd] = (gate[:, 3 * r:3 * r + 1] * o_cmp[rs]
                                          + gate[:, 3 * r + 2:3 * r + 3] * o_win[rs])


def _overlap_matrix(nc, n_sel):
    i = np.arange(nc)[None, :]
    j = np.arange(n_sel)[:, None]
    lo = np.maximum(i * CMP_STRIDE, j * SEL_BLOCK)
    hi = np.minimum(i * CMP_STRIDE + CMP_BLOCK - 1, j * SEL_BLOCK + SEL_BLOCK - 1)
    ov = np.zeros((n_sel + 8, nc), np.float32)
    ov[:n_sel] = np.clip(hi - lo + 1, 0, None) / CMP_STRIDE
    ov[n_sel] = 1.0
    return jnp.asarray(ov, BF16)


def _window_masks(n_wb):
    q = np.arange(TQ)[:, None]
    c = np.arange(n_wb * TQ)[None, :]
    dist = q + (n_wb - 1) * TQ - c
    masks = []
    for v in range(n_wb):
        kpos = (v - (n_wb - 1)) * TQ + c
        masks.append(np.where((kpos >= 0) & (dist >= 0) & (dist < WINDOW), 0.0, MASKED))
    return jnp.asarray(np.stack(masks), F32)


def _nsa_local(q, kc, vc, kwin, vwin, gates, n_sel, n_var):
    s = q.shape[0]
    g_n, r_n, hd = NSA_KV_HEADS, NSA_GROUP, HEAD_DIM
    n_q = s // TQ
    n_wb = WINDOW // TQ + 1
    n_top = min(SEL_TOPK, n_sel)
    nc = kc.shape[1]
    rows = r_n * TQ
    assert n_top >= 3 and nc % min(2 * LANES, nc) == 0

    def win_spec(j, width):
        return pl.BlockSpec((1, TQ, width), lambda g, i: (g, jnp.maximum(i - (n_wb - 1) + j, 0), 0))

    return pl.pallas_call(
        functools.partial(_nsa_local_kernel, n_sel, n_top),
        grid=(g_n, n_q),
        in_specs=[pl.BlockSpec((TQ, r_n * hd), lambda g, i: (i, g)),
                  pl.BlockSpec((1, nc, hd), lambda g, i: (g, 0, 0)),
                  pl.BlockSpec((1, nc, LANES), lambda g, i: (g, 0, 0)),
                  pl.BlockSpec((n_sel + 8, nc), lambda g, i: (0, 0)),
                  pl.BlockSpec((1, TQ, n_wb * TQ), lambda g, i: (jnp.minimum(i, n_wb - 1), 0, 0))]
        + [win_spec(j, hd) for j in range(n_wb)] + [win_spec(j, LANES) for j in range(n_wb)]
        + [pl.BlockSpec((1, TQ, LANES), lambda g, i: (g, i, 0))],
        out_specs=[pl.BlockSpec((TQ, r_n * hd), lambda g, i: (i, g)),
                   pl.BlockSpec((n_var, 1, 1, rows, 2 * LANES), lambda g, i: (0, g, i, 0, 0))],
        out_shape=[jax.ShapeDtypeStruct((s, N_HEADS * hd), F32),
                   jax.ShapeDtypeStruct((n_var, g_n, n_q, rows, 2 * LANES), BF16)],
        scratch_shapes=[pltpu.VMEM((rows, nc), F32), pltpu.VMEM((rows, LANES), F32),
                        pltpu.VMEM((rows, LANES), F32), pltpu.VMEM((n_sel + 8, rows), F32)],
        name="nsa_local",
        compiler_params=_cparams("parallel", "parallel"),
    )(q, kc, vc, _overlap_matrix(nc, n_sel), _window_masks(n_wb),
      *([kwin] * n_wb), *([vwin] * n_wb), gates)


def _causal_pairs(n_q, tq, tk):
    qs, ks = [], []
    for qi in range(n_q):
        for ki in range(((qi + 1) * tq - 1) // tk + 1):
            qs.append(qi)
            ks.append(ki)
    return jnp.asarray(qs, jnp.int32), jnp.asarray(ks, jnp.int32)


def _flash_update(q, k, v, m_ref, acc_ref, idx, causal_mask):
    s = _dot_t(q, k)
    if causal_mask is not None:
        s = jnp.where(causal_mask, s, MASKED)
    m_prev = m_ref[idx]
    m_new = jnp.maximum(m_prev, jnp.max(s, axis=1, keepdims=True))
    p = jnp.concatenate([jnp.exp2(s[:, c * LANES:(c + 1) * LANES] - m_new).astype(BF16)
                         for c in range(s.shape[1] // LANES)], axis=1)
    acc_ref[idx] = jnp.exp2(m_prev - m_new) * acc_ref[idx] + _dot(p, v)
    m_ref[idx] = m_new


def _flash_steps(qi, ki, tq, tk, rows_per_q, n_batch, operands, m_ref, acc_ref, finalize):
    first_diag = (qi * tq) // tk
    last_k = ((qi + 1) * tq - 1) // tk

    @pl.when(ki == 0)
    def _():
        m_ref[...] = jnp.full(m_ref.shape, M_INIT, F32)
        acc_ref[...] = jnp.zeros(acc_ref.shape, F32)

    @pl.when(ki < first_diag)
    def _():
        for b in range(n_batch):
            q, k, v = operands(b)
            _flash_update(q, k, v, m_ref, acc_ref, b, None)

    @pl.when(ki >= first_diag)
    def _():
        shape = (rows_per_q * tq, tk)
        q_off = lax.broadcasted_iota(jnp.int32, shape, 0) % tq
        k_off = lax.broadcasted_iota(jnp.int32, shape, 1)
        mask = q_off - k_off >= ki * tk - qi * tq
        for b in range(n_batch):
            q, k, v = operands(b)
            _flash_update(q, k, v, m_ref, acc_ref, b, mask)

    @pl.when(ki == last_k)
    def _():
        finalize()


def _nsa_sel_kernel(qtab, ktab, qaug_ref, kaug_ref, vaug_ref, pre_ref, gate_ref, o_ref,
                    m_ref, acc_ref):
    g_n, r_n, hd = NSA_KV_HEADS, NSA_GROUP, HEAD_DIM
    t = pl.program_id(0)
    qi, ki = qtab[t], ktab[t]

    def operands(g):
        return (qaug_ref[0, g, 0], kaug_ref[:, g * 2 * LANES:(g + 1) * 2 * LANES],
                vaug_ref[:, g * LANES:(g + 1) * LANES])

    def finalize():
        for g in range(g_n):
            acc = acc_ref[g]
            gate = gate_ref[g]
            o = acc[:, 0:hd] * (1.0 / acc[:, hd:hd + 1])
            for r in range(r_n):
                c = (g * r_n + r) * hd
                o_ref[:, c:c + hd] = (pre_ref[:, c:c + hd] + gate[:, 3 * r + 1:3 * r + 2]
                                      * o[r * TQ:(r + 1) * TQ]).astype(o_ref.dtype)

    _flash_steps(qi, ki, TQ, TK, r_n, g_n, operands, m_ref, acc_ref, finalize)


def _nsa_sel(qaug, kaug, vaug, pre, gates):
    n_var, g_n, n_q, rows, _ = qaug.shape
    s = kaug.shape[0]
    qtab, ktab = _causal_pairs(n_q, TQ, TK)
    tiles_per_var = BIAS_BLOCKS * SEL_BLOCK // TK
    grid_spec = pltpu.PrefetchScalarGridSpec(
        num_scalar_prefetch=2,
        grid=(qtab.shape[0],),
        in_specs=[
            pl.BlockSpec((1, g_n, 1, rows, 2 * LANES),
                         lambda t, qt, kt: (kt[t] // tiles_per_var, 0, qt[t], 0, 0)),
            pl.BlockSpec((TK, g_n * 2 * LANES), lambda t, qt, kt: (kt[t], 0)),
            pl.BlockSpec((TK, g_n * LANES), lambda t, qt, kt: (kt[t], 0)),
            pl.BlockSpec((TQ, N_HEADS * HEAD_DIM), lambda t, qt, kt: (qt[t], 0)),
            pl.BlockSpec((g_n, TQ, LANES), lambda t, qt, kt: (0, qt[t], 0)),
        ],
        out_specs=pl.BlockSpec((TQ, N_HEADS * HEAD_DIM), lambda t, qt, kt: (qt[t], 0)),
        scratch_shapes=[pltpu.VMEM((g_n, rows, LANES), F32), pltpu.VMEM((g_n, rows, LANES), F32)],
    )
    return pl.pallas_call(
        _nsa_sel_kernel,
        grid_spec=grid_spec,
        out_shape=jax.ShapeDtypeStruct((s, N_HEADS * HEAD_DIM), BF16),
        name="nsa_sel",
        compiler_params=_cparams("arbitrary"),
    )(qtab, ktab, qaug, kaug, vaug, pre, gates)


def _moba_proj_kernel(n_blk, k_eff, x_ref, cos_ref, sa_ref, sb_ref, w_ref,
                      qaug_ref, kaug_ref, vaug_ref, kmean_ref):
    tm = x_ref.shape[0]
    hd, d = HEAD_DIM, N_HEADS * HEAD_DIM
    i = pl.program_id(0)

    @pl.when(i == 0)
    def _():
        kmean_ref[...] = jnp.zeros(kmean_ref.shape, F32)

    xb = x_ref[...].astype(BF16)
    cos_t, sin_a, sin_b = cos_ref[...], sa_ref[...], sb_ref[...]
    q = _rope(_dot(xb, w_ref[:, 0:d]), cos_t, sin_a, sin_b)
    k = _rope(_dot(xb, w_ref[:, d:2 * d]), cos_t, sin_a, sin_b)
    v = _dot(xb, w_ref[:, 2 * d:3 * d])
    kmean = kmean_ref[...]
    nb = kmean.shape[0]
    own = jnp.where(lax.broadcasted_iota(jnp.int32, (1, nb), 1) == i, 1.0, 0.0).astype(BF16)
    own_onehot = jnp.broadcast_to(own, (tm, nb))
    ones_col = _ones_col(tm, LANES - hd)
    blk_t = lax.broadcasted_iota(jnp.int32, (nb, tm), 0)
    own_t = jnp.where(blk_t == i, 1.0, 0.0)
    for hp in range(N_HEADS // 2):
        bias_t = []
        for h in (2 * hp, 2 * hp + 1):
            sl = slice(h * hd, (h + 1) * hd)
            gate_t = _dot_t(kmean[:, sl], q[:, sl], precision=lax.Precision.HIGHEST)
            gate_t = jnp.where(blk_t < i, gate_t, -jnp.inf)
            bias_t.append(jnp.where(_top_k_axis0(gate_t, k_eff, own_t)[0] > 0, 0.0, MASKED))
        bias = jnp.concatenate(bias_t, axis=0).T.astype(BF16)
        for j, h in enumerate((2 * hp, 2 * hp + 1)):
            sl = slice(h * hd, (h + 1) * hd)
            qaug_ref[h, :, 0:hd] = (q[:, sl] * SCALE).astype(BF16)
            qaug_ref[h, :, hd:hd + nb] = bias[:, j * nb:(j + 1) * nb]
            kaug_ref[h, :, 0:hd] = k[:, sl].astype(BF16)
            kaug_ref[h, :, hd:hd + nb] = own_onehot
            vaug_ref[h, :, 0:hd] = v[:, sl].astype(BF16)
            vaug_ref[h, :, hd:LANES] = ones_col
    kmean_ref[pl.ds(i, 1), :] = jnp.mean(k, axis=0, keepdims=True)


def _moba_proj(x, tabs, w):
    s, d = x.shape
    n_blk = s // MOBA_BLOCK
    k_eff = min(MOBA_TOPK, max(n_blk - 1, 1))
    nb = LANES - HEAD_DIM
    assert s % MOBA_BLOCK == 0 and n_blk <= nb
    tm = MOBA_BLOCK
    row = lambda i: (i, 0)
    head = jax.ShapeDtypeStruct((N_HEADS, s, LANES), BF16)
    head_spec = pl.BlockSpec((N_HEADS, tm, LANES), lambda i: (0, i, 0))
    return pl.pallas_call(
        functools.partial(_moba_proj_kernel, n_blk, k_eff),
        grid=(s // tm,),
        in_specs=[pl.BlockSpec((tm, d), row)] + [pl.BlockSpec((tm, LANES), row)] * 3
        + [pl.BlockSpec(w.shape, lambda i: (0, 0))],
        out_specs=[head_spec] * 3,
        out_shape=[head] * 3,
        scratch_shapes=[pltpu.VMEM((nb, N_HEADS * HEAD_DIM), F32)],
        name="moba_proj",
        compiler_params=_cparams("arbitrary"),
    )(x, *tabs, w)


def _moba_attn_kernel(qtab, ktab, q_ref, k_ref, v_ref, o_ref, m_ref, acc_ref):
    hd = HEAD_DIM
    t = pl.program_id(0)
    qi, ki = qtab[t], ktab[t]

    def operands(h):
        return q_ref[h], k_ref[h], v_ref[h]

    def finalize():
        for h in range(N_HEADS):
            acc = acc_ref[h]
            o_ref[:, h * hd:(h + 1) * hd] = (acc[:, 0:hd] * (1.0 / acc[:, hd:hd + 1])).astype(o_ref.dtype)

    _flash_steps(qi, ki, q_ref.shape[1], k_ref.shape[1], 1, N_HEADS, operands, m_ref, acc_ref, finalize)


def _moba_attn(qaug, kaug, vaug):
    h_n, s, _ = qaug.shape
    tq, tk = min(MOBA_TQ, s), min(MOBA_TK, s)
    assert s % tq == 0 and s % tk == 0
    qtab, ktab = _causal_pairs(s // tq, tq, tk)
    grid_spec = pltpu.PrefetchScalarGridSpec(
        num_scalar_prefetch=2,
        grid=(qtab.shape[0],),
        in_specs=[pl.BlockSpec((h_n, tq, LANES), lambda t, qt, kt: (0, qt[t], 0)),
                  pl.BlockSpec((h_n, tk, LANES), lambda t, qt, kt: (0, kt[t], 0)),
                  pl.BlockSpec((h_n, tk, LANES), lambda t, qt, kt: (0, kt[t], 0))],
        out_specs=pl.BlockSpec((tq, h_n * HEAD_DIM), lambda t, qt, kt: (qt[t], 0)),
        scratch_shapes=[pltpu.VMEM((h_n, tq, LANES), F32), pltpu.VMEM((h_n, tq, LANES), F32)],
    )
    return pl.pallas_call(
        _moba_attn_kernel,
        grid_spec=grid_spec,
        out_shape=jax.ShapeDtypeStruct((s, h_n * HEAD_DIM), BF16),
        name="moba_attn",
        compiler_params=_cparams("arbitrary"),
    )(qtab, ktab, qaug, kaug, vaug)


def _mm_res_ln_kernel(alpha, a_ref, w_ref, x_ref, g_ref, b_ref, o_ref):
    z = alpha * x_ref[...] + _dot(a_ref[...], w_ref[...])
    mu = jnp.mean(z, axis=-1, keepdims=True)
    zc = z - mu
    var = jnp.mean(zc * zc, axis=-1, keepdims=True)
    o_ref[...] = zc * lax.rsqrt(var + LN_EPS) * g_ref[...] + b_ref[...]


def _mm_res_ln(a, w, x, gain, bias, alpha):
    s, k = a.shape
    d = w.shape[1]
    tm = min(MM_TILE, s)
    row = lambda i: (i, 0)
    whole = lambda i: (0, 0)
    return pl.pallas_call(
        functools.partial(_mm_res_ln_kernel, alpha),
        grid=(s // tm,),
        in_specs=[pl.BlockSpec((tm, k), row), pl.BlockSpec((k, d), whole),
                  pl.BlockSpec((tm, d), row), pl.BlockSpec((1, d), whole), pl.BlockSpec((1, d), whole)],
        out_specs=pl.BlockSpec((tm, d), row),
        out_shape=jax.ShapeDtypeStruct((s, d), F32),
        name="mm_res_ln",
        compiler_params=_cparams("parallel"),
    )(a, w, x, gain.reshape(1, d), bias.reshape(1, d))


def _ffn_up_kernel(n_chunk, x_ref, wg_ref, wu_ref, o_ref):
    xb = x_ref[...].astype(BF16)
    cw = wg_ref.shape[1] // n_chunk
    for c in range(n_chunk):
        sl = slice(c * cw, (c + 1) * cw)
        o_ref[:, sl] = (jax.nn.silu(_dot(xb, wg_ref[:, sl])) * _dot(xb, wu_ref[:, sl])).astype(o_ref.dtype)


def _ffn_up(x, wg, wu):
    s, d = x.shape
    f = wg.shape[1]
    n_chunk = 2 if f % (2 * LANES) == 0 else 1
    tm = min(MM_TILE, s)
    row = lambda i: (i, 0)
    whole = lambda i: (0, 0)
    return pl.pallas_call(
        functools.partial(_ffn_up_kernel, n_chunk),
        grid=(s // tm,),
        in_specs=[pl.BlockSpec((tm, d), row), pl.BlockSpec((d, f), whole), pl.BlockSpec((d, f), whole)],
        out_specs=pl.BlockSpec((tm, f), row),
        out_shape=jax.ShapeDtypeStruct((s, f), BF16),
        name="ffn_up",
        compiler_params=_cparams("parallel"),
    )(x, wg, wu)


def _nsa_mixer(x, tabs, w_in, ck_pos, ck_w1, ck_w2, cv_pos, cv_w1, cv_w2):
    s = x.shape[0]
    g_n, r_n, hd = NSA_KV_HEADS, NSA_GROUP, HEAD_DIM
    q_dim, kv_dim = N_HEADS * hd, NSA_KV_HEADS * hd
    n_sel = s // SEL_BLOCK
    n_var = -(-n_sel // BIAS_BLOCKS)
    assert s % TK == 0 and (BIAS_BLOCKS * SEL_BLOCK) % TK == 0
    wq = w_in[:, :q_dim].astype(BF16)
    wkv = w_in[:, q_dim:q_dim + 6 * kv_dim].astype(BF16)
    wg = w_in[:, q_dim + 6 * kv_dim:].reshape(-1, g_n, 3 * r_n)
    wg = jnp.pad(wg, ((0, 0), (0, 0), (0, LANES - 3 * r_n))).reshape(-1, g_n * LANES).astype(BF16)
    q, cmp_kv, kaug, vaug, kwin, vwin, gates = _nsa_proj(x, tabs, wq, wkv, wg)
    comp = _compress(cmp_kv, jnp.stack([ck_pos, cv_pos]),
                     jnp.stack([ck_w1, cv_w1]).astype(BF16), jnp.stack([ck_w2, cv_w2]).astype(BF16))
    kc = comp[0].astype(BF16)
    ones = jnp.ones(comp.shape[1:3] + (1,), F32)
    zeros = jnp.zeros(comp.shape[1:3] + (LANES - hd - 1,), F32)
    vc = jnp.concatenate([comp[1], ones, zeros], axis=-1).astype(BF16)
    pre, qaug = _nsa_local(q, kc, vc, kwin, vwin, gates, n_sel, n_var)
    return _nsa_sel(qaug, kaug, vaug, pre, gates)


def _moba_mixer(x, tabs, w_in):
    qaug, kaug, vaug = _moba_proj(x, tabs, w_in.astype(BF16))
    return _moba_attn(qaug, kaug, vaug)


def kernel(x, positions, nsa_w_in, nsa_w_o, nsa_ck_pos, nsa_ck_w1, nsa_ck_w2, nsa_cv_pos, nsa_cv_w1, nsa_cv_w2, moba_w_in, moba_w_o, ffn_wg, ffn_wu, ffn_wd, ln1_g, ln1_b, ln2_g, ln2_b):
    b, s, d = x.shape
    depth = ffn_wg.shape[0]
    alpha = (2 * depth) ** 0.25
    outs = []
    for bi in range(b):
        h = x[bi]
        tabs = _rope_tables(positions[bi])
        for i in range(depth):
            j = i // 2
            if i % 2 == 0:
                a = _nsa_mixer(h, tabs, nsa_w_in[j], nsa_ck_pos[j], nsa_ck_w1[j], nsa_ck_w2[j],
                               nsa_cv_pos[j], nsa_cv_w1[j], nsa_cv_w2[j])
                w_o = nsa_w_o[j]
            else:
                a = _moba_mixer(h, tabs, moba_w_in[j])
                w_o = moba_w_o[j]
            h = _mm_res_ln(a, w_o.astype(BF16), h, ln1_g[i], ln1_b[i], alpha)
            u = _ffn_up(h, ffn_wg[i].astype(BF16), ffn_wu[i].astype(BF16))
            h = _mm_res_ln(u, ffn_wd[i].astype(BF16), h, ln2_g[i], ln2_b[i], alpha)
        outs.append(h)
    return jnp.stack(outs)
```

```python
import functools

import numpy as np
import jax
import jax.numpy as jnp
from jax import lax
from jax.experimental import pallas as pl
from jax.experimental.pallas import tpu as pltpu

F32 = jnp.float32
BF16 = jnp.bfloat16

N_HEADS = 16
HEAD_DIM = 64
ROPE_DIM = HEAD_DIM // 4
ROPE_THETA = 500000.0
NSA_KV_HEADS = 4
NSA_GROUP = N_HEADS // NSA_KV_HEADS
CMP_BLOCK = 32
CMP_STRIDE = 16
SEL_BLOCK = 64
SEL_TOPK = 16
WINDOW = 512
MOBA_BLOCK = 256
MOBA_TOPK = 3
LN_EPS = 1e-5
LARGE = 1e30
SCALE = HEAD_DIM ** -0.5 * float(np.log2(np.e))

LANES = 128
VMEM_LIMIT_BYTES = 48 * 1024 * 1024

ROW_TILE = 256
MM_TILE = 512
TQ = 256
TK = 1024
MOBA_TQ = 1024
MOBA_TK = 512
BIAS_BLOCKS = 128
MASKED = -1e30
M_INIT = -5e29

assert WINDOW % TQ == 0 and TK % TQ == 0 and ROW_TILE == MOBA_BLOCK


def _cparams(*sem, flags=None):
    return pltpu.CompilerParams(dimension_semantics=sem, vmem_limit_bytes=VMEM_LIMIT_BYTES, flags=flags)


def _dot(a, b):
    return jnp.dot(a, b, preferred_element_type=F32)


def _dot_t(a, b, precision=None):
    return lax.dot_general(a, b, (((1,), (1,)), ((), ())), preferred_element_type=F32,
                           precision=precision)


def _rope_tables_kernel(pos_ref, c_ref, cos_ref, sa_ref, sb_ref):
    ang = pos_ref[...].astype(F32) * c_ref[0:1, :]
    c = jnp.cos(ang)
    s = jnp.sin(ang)
    cos_ref[...] = jnp.where(c_ref[1:2, :] > 0, c, 1.0)
    sa_ref[...] = jnp.where(c_ref[2:3, :] > 0, s, 0.0)
    sb_ref[...] = jnp.where(c_ref[3:4, :] > 0, -s, 0.0)


def _rope_tables(positions):
    s = positions.shape[0]
    half = ROPE_DIM // 2
    inv = ROPE_THETA ** (-jnp.arange(0, ROPE_DIM, 2, dtype=F32) / ROPE_DIM)
    d = np.arange(LANES) % HEAD_DIM
    consts = jnp.zeros((8, LANES), F32)
    consts = consts.at[0].set(inv[d % half])
    consts = consts.at[1].set(jnp.asarray(d < ROPE_DIM, F32))
    consts = consts.at[2].set(jnp.asarray((d >= half) & (d < ROPE_DIM), F32))
    consts = consts.at[3].set(jnp.asarray(d < half, F32))
    tm = min(MM_TILE, s)
    tab = jax.ShapeDtypeStruct((s, LANES), F32)
    return pl.pallas_call(
        _rope_tables_kernel,
        grid=(s // tm,),
        in_specs=[pl.BlockSpec((tm, 1), lambda i: (i, 0)),
                  pl.BlockSpec((8, LANES), lambda i: (0, 0))],
        out_specs=[pl.BlockSpec((tm, LANES), lambda i: (i, 0))] * 3,
        out_shape=[tab, tab, tab],
        name="rope_tables",
        compiler_params=_cparams("parallel"),
    )(positions.reshape(s, 1), consts)


def _rope(x, cos_t, sin_a, sin_b):
    half = ROPE_DIM // 2
    outs = []
    for c in range(x.shape[1] // LANES):
        xs = x[:, c * LANES:(c + 1) * LANES]
        outs.append(xs * cos_t + pltpu.roll(xs, half, 1) * sin_a
                    + pltpu.roll(xs, LANES - half, 1) * sin_b)
    return outs[0] if len(outs) == 1 else jnp.concatenate(outs, axis=1)


def _ones_col(rows, width):
    return jnp.where(lax.broadcasted_iota(jnp.int32, (rows, width), 1) == 0, 1.0, 0.0).astype(BF16)


def _nsa_proj_kernel(x_ref, cos_ref, sa_ref, sb_ref, wq_ref, wkv_ref, wg_ref,
                     q_ref, cmp_ref, kaug_ref, vaug_ref, kwin_ref, vwin_ref, gate_ref):
    tm = x_ref.shape[0]
    g_n, hd, kvd = NSA_KV_HEADS, HEAD_DIM, NSA_KV_HEADS * HEAD_DIM
    xb = x_ref[...].astype(BF16)
    cos_t, sin_a, sin_b = cos_ref[...], sa_ref[...], sb_ref[...]
    q = _rope(_dot(xb, wq_ref[...]), cos_t, sin_a, sin_b) * SCALE
    q_ref[...] = q.astype(BF16)
    kv = _dot(xb, wkv_ref[...])
    k_cmp = _rope(kv[:, 0:kvd], cos_t, sin_a, sin_b)
    v_cmp = kv[:, kvd:2 * kvd]
    k_sel = _rope(kv[:, 2 * kvd:3 * kvd], cos_t, sin_a, sin_b)
    v_sel = kv[:, 3 * kvd:4 * kvd]
    k_win = _rope(kv[:, 4 * kvd:5 * kvd], cos_t, sin_a, sin_b)
    v_win = kv[:, 5 * kvd:6 * kvd]
    gate = jax.nn.sigmoid(_dot(xb, wg_ref[...]))
    tok = pl.program_id(0) * tm + lax.broadcasted_iota(jnp.int32, (tm, BIAS_BLOCKS), 0)
    blk = (tok // SEL_BLOCK) % BIAS_BLOCKS
    onehot = jnp.where(lax.broadcasted_iota(jnp.int32, (tm, BIAS_BLOCKS), 1) == blk,
                       1.0, 0.0).astype(BF16)
    ones_col = _ones_col(tm, LANES - hd)
    for g in range(g_n):
        sl = slice(g * hd, (g + 1) * hd)
        cmp_ref[0, g] = k_cmp[:, sl]
        cmp_ref[1, g] = v_cmp[:, sl]
        kwin_ref[g] = k_win[:, sl].astype(BF16)
        vwin_ref[g, :, 0:hd] = v_win[:, sl].astype(BF16)
        vwin_ref[g, :, hd:LANES] = ones_col
        gate_ref[g] = gate[:, g * LANES:(g + 1) * LANES]
        kb = g * 2 * LANES
        kaug_ref[:, kb:kb + BIAS_BLOCKS] = onehot
        kaug_ref[:, kb + BIAS_BLOCKS:kb + BIAS_BLOCKS + hd] = k_sel[:, sl].astype(BF16)
        kaug_ref[:, kb + BIAS_BLOCKS + hd:kb + 2 * LANES] = jnp.zeros((tm, 2 * LANES - BIAS_BLOCKS - hd), BF16)
        vb = g * LANES
        vaug_ref[:, vb:vb + hd] = v_sel[:, sl].astype(BF16)
        vaug_ref[:, vb + hd:vb + LANES] = ones_col


def _nsa_proj(x, tabs, wq, wkv, wg):
    s, d = x.shape
    g_n, hd = NSA_KV_HEADS, HEAD_DIM
    tm = min(ROW_TILE, s)
    row = lambda i: (i, 0)
    whole = lambda i: (0, 0)
    out_shape = [
        jax.ShapeDtypeStruct((s, N_HEADS * hd), BF16),
        jax.ShapeDtypeStruct((2, g_n, s, hd), F32),
        jax.ShapeDtypeStruct((s, g_n * 2 * LANES), BF16),
        jax.ShapeDtypeStruct((s, g_n * LANES), BF16),
        jax.ShapeDtypeStruct((g_n, s, hd), BF16),
        jax.ShapeDtypeStruct((g_n, s, LANES), BF16),
        jax.ShapeDtypeStruct((g_n, s, LANES), F32),
    ]
    out_specs = [
        pl.BlockSpec((tm, N_HEADS * hd), row),
        pl.BlockSpec((2, g_n, tm, hd), lambda i: (0, 0, i, 0)),
        pl.BlockSpec((tm, g_n * 2 * LANES), row),
        pl.BlockSpec((tm, g_n * LANES), row),
        pl.BlockSpec((g_n, tm, hd), lambda i: (0, i, 0)),
        pl.BlockSpec((g_n, tm, LANES), lambda i: (0, i, 0)),
        pl.BlockSpec((g_n, tm, LANES), lambda i: (0, i, 0)),
    ]
    return pl.pallas_call(
        _nsa_proj_kernel,
        grid=(s // tm,),
        in_specs=[pl.BlockSpec((tm, d), row)] + [pl.BlockSpec((tm, LANES), row)] * 3
        + [pl.BlockSpec(wq.shape, whole), pl.BlockSpec(wkv.shape, whole),
           pl.BlockSpec(wg.shape, whole)],
        out_specs=out_specs,
        out_shape=out_shape,
        name="nsa_proj",
        compiler_params=_cparams("parallel"),
    )(x, *tabs, wq, wkv, wg)


def _compress_kernel(x_ref, pos_ref, w1_ref, w2_ref, out_ref):
    n16 = x_ref.shape[2]
    half = w1_ref.shape[1] // 2
    x = x_ref[0, 0]
    a = _dot((x + pos_ref[0, 0:1, :]).astype(BF16), w1_ref[0, 0:half, :])
    b = _dot((x + pos_ref[0, 1:2, :]).astype(BF16), w1_ref[0, half:, :])
    hid = a + pltpu.roll(b, n16 - 1, 0)
    out_ref[0, 0] = _dot(jax.nn.gelu(hid).astype(BF16), w2_ref[0])


def _compress(cmp_kv, pos, w1, w2):
    _, g_n, s, hd = cmp_kv.shape
    n16 = s // CMP_STRIDE
    x = cmp_kv.reshape(2, g_n, n16, CMP_STRIDE * hd)
    pos2 = pos.reshape(2, 2, CMP_STRIDE * hd)
    return pl.pallas_call(
        _compress_kernel,
        grid=(2, g_n),
        in_specs=[pl.BlockSpec((1, 1, n16, CMP_STRIDE * hd), lambda a, g: (a, g, 0, 0)),
                  pl.BlockSpec((1, 2, CMP_STRIDE * hd), lambda a, g: (a, 0, 0)),
                  pl.BlockSpec((1,) + w1.shape[1:], lambda a, g: (a, 0, 0)),
                  pl.BlockSpec((1,) + w2.shape[1:], lambda a, g: (a, 0, 0))],
        out_specs=pl.BlockSpec((1, 1, n16, hd), lambda a, g: (a, g, 0, 0)),
        out_shape=jax.ShapeDtypeStruct((2, g_n, n16, hd), F32),
        name="nsa_compress",
        compiler_params=_cparams("parallel", "parallel"),
    )(x, pos2, w1, w2)


def _top_k_axis0(score, n_rounds, sel, enable=None):
    n = score.shape[0]
    row = lax.broadcasted_iota(jnp.int32, score.shape, 0).astype(F32)
    for _ in range(n_rounds):
        m = jnp.max(score, axis=0, keepdims=True)
        idx = jnp.min(jnp.where(score == m, row, float(n)), axis=0, keepdims=True)
        ok = m > -jnp.inf if enable is None else (m > -jnp.inf) & enable
        hit = row == jnp.where(ok, idx, -1.0)
        sel = jnp.where(hit, 1.0, sel)
        score = jnp.where(hit, -jnp.inf, score)
    return sel, score


def _nsa_local_kernel(n_sel, n_top, q_ref, kc_ref, vc_ref, ov_ref, wmask_ref, *rest):
    n_wb = WINDOW // TQ + 1
    kw_refs = rest[:n_wb]
    vw_refs = rest[n_wb:2 * n_wb]
    gate_ref, pre_ref, qaug_ref, s_scr, m_scr, o_acc, imp_acc, sel_scr = rest[2 * n_wb:]
    r_n, hd = NSA_GROUP, HEAD_DIM
    qi = pl.program_id(1)
    qg = q_ref[...]
    qs = jnp.concatenate([qg[:, r * hd:(r + 1) * hd] for r in range(r_n)], axis=0)
    rows = r_n * TQ

    nc = kc_ref.shape[1]
    ct = min(2 * LANES, nc)
    t_max = ((qi * TQ + TQ - CMP_BLOCK) // CMP_STRIDE) // ct
    t_full = (jnp.maximum(qi * TQ - (CMP_BLOCK - 1) + CMP_STRIDE, 0) // CMP_STRIDE) // ct

    def lane_fold_max(s):
        m = s[:, 0:LANES]
        for c in range(1, s.shape[1] // LANES):
            m = jnp.maximum(m, s[:, c * LANES:(c + 1) * LANES])
        return m

    m_scr[...] = jnp.full(m_scr.shape, M_INIT, F32)
    for t in range(nc // ct):
        tile = slice(t * ct, (t + 1) * ct)

        @pl.when(t < t_full)
        def _():
            s = _dot_t(qs, kc_ref[0, tile, :])
            s_scr[:, tile] = s
            m_scr[...] = jnp.maximum(m_scr[...], lane_fold_max(s))

        @pl.when((t >= t_full) & (t <= t_max))
        def _():
            q_off = lax.broadcasted_iota(jnp.int32, (rows, ct), 0) % TQ
            tok = lax.broadcasted_iota(jnp.int32, (rows, ct), 1)
            visible = q_off - CMP_STRIDE * tok >= CMP_STRIDE * ct * t + CMP_BLOCK - 1 - qi * TQ
            s = jnp.where(visible, _dot_t(qs, kc_ref[0, tile, :]), MASKED)
            s_scr[:, tile] = s
            m_scr[...] = jnp.maximum(m_scr[...], lane_fold_max(s))

    m_scr[...] = jnp.broadcast_to(jnp.max(m_scr[...], axis=1, keepdims=True), m_scr.shape)
    o_acc[...] = jnp.zeros(o_acc.shape, F32)
    imp_acc[...] = jnp.zeros(imp_acc.shape, F32)
    for t in range(nc // ct):
        tile = slice(t * ct, (t + 1) * ct)

        @pl.when(t <= t_max)
        def _():
            m = m_scr[...]
            p = jnp.concatenate(
                [jnp.exp2(s_scr[:, t * ct + c * LANES:t * ct + (c + 1) * LANES] - m).astype(BF16)
                 for c in range(ct // LANES)], axis=1)
            o_acc[...] += _dot(p, vc_ref[0, tile, :])
            imp_acc[...] += _dot_t(ov_ref[:, tile], p)

    o_aug = o_acc[...]
    l_col = o_aug[:, hd:hd + 1]
    o_cmp = o_aug[:, 0:hd] * jnp.where(l_col > 0, 1.0 / l_col, 0.0)

    imp_t = imp_acc[...]
    l_row = imp_t[n_sel:n_sel + 1, :]
    imp_t = imp_t[0:n_sel, :] * jnp.where(l_row > 0, 1.0 / l_row, 0.0)
    imp = imp_t[:, 0:TQ]
    for r in range(1, r_n):
        imp = imp + imp_t[:, r * TQ:(r + 1) * TQ]
    blk = lax.broadcasted_iota(jnp.int32, (n_sel, TQ), 0)
    blk_q = (qi * TQ + lax.broadcasted_iota(jnp.int32, (n_sel, TQ), 1)) // SEL_BLOCK
    forced = (blk == 0) | (blk == blk_q) | (blk == blk_q - 1)
    score = jnp.where((blk <= blk_q) & jnp.logical_not(forced), imp, -jnp.inf)
    sel = jnp.where(forced, 1.0, 0.0)
    n_forced_min = 3
    blk_q_row = blk_q[0:1, :]
    n_prefix = 4 if n_sel % 32 == 0 else 1
    prefix = ((qi + 1) * (TQ // SEL_BLOCK) - 1) * n_prefix // n_sel
    for v in range(n_prefix):
        nr = n_sel * (v + 1) // n_prefix

        @pl.when(prefix == v)
        def _():
            sel_tiles = []
            for c in range(TQ // LANES):
                ls = slice(c * LANES, (c + 1) * LANES)
                sel_c, score_c = _top_k_axis0(score[0:nr, ls], n_top - n_forced_min, sel[0:nr, ls])
                if v == 0:
                    for extra in range(1, n_forced_min):
                        sel_c, score_c = _top_k_axis0(score_c, 1, sel_c,
                                                      enable=blk_q_row[:, ls] < n_forced_min - extra)
                sel_tiles.append(sel_c)
            sel_scr[0:nr, :] = jnp.concatenate(sel_tiles, axis=1)
            if nr < n_sel:
                sel_scr[nr:n_sel, :] = jnp.zeros((n_sel - nr, TQ), F32)

    bias = jnp.where(sel_scr[...] > 0, 0.0, MASKED).T.astype(BF16)
    n_var = qaug_ref.shape[0]
    if n_sel < n_var * BIAS_BLOCKS:
        bias = jnp.concatenate(
            [bias, jnp.full((TQ, n_var * BIAS_BLOCKS - n_sel), MASKED, BF16)], axis=1)
    zeros = jnp.zeros((TQ, 2 * LANES - BIAS_BLOCKS - hd), BF16)
    for v in range(n_var):
        for r in range(r_n):
            rs = slice(r * TQ, (r + 1) * TQ)
            qaug_ref[v, 0, 0, rs, 0:BIAS_BLOCKS] = bias[:, v * BIAS_BLOCKS:(v + 1) * BIAS_BLOCKS]
            qaug_ref[v, 0, 0, rs, BIAS_BLOCKS:BIAS_BLOCKS + hd] = qg[:, r * hd:(r + 1) * hd]
            qaug_ref[v, 0, 0, rs, BIAS_BLOCKS + hd:2 * LANES] = zeros

    kcat = jnp.concatenate([kw[0] for kw in kw_refs], axis=0)
    vcat = jnp.concatenate([vw[0] for vw in vw_refs], axis=0)
    s = _dot_t(qs, kcat)
    wmask = wmask_ref[0]
    s = jnp.concatenate([s[r * TQ:(r + 1) * TQ] + wmask for r in range(r_n)], axis=0)
    m = jnp.broadcast_to(jnp.max(s, axis=1, keepdims=True), (rows, LANES))
    p = jnp.concatenate([jnp.exp2(s[:, c * LANES:(c + 1) * LANES] - m).astype(BF16)
                         for c in range(s.shape[1] // LANES)], axis=1)
    o_aug = _dot(p, vcat)
    o_win = o_aug[:, 0:hd] * (1.0 / o_aug[:, hd:hd + 1])

    gate = gate_ref[0]
    for r in range(r_n):
        rs = slice(r * TQ, (r + 1) * TQ)
        pre_ref[:, r * hd:(r + 1) * hd] = (gate[:, 3 * r:3 * r + 1] * o_cmp[rs]
                                          + gate[:, 3 * r + 2:3 * r + 3] * o_win[rs])


def _overlap_matrix(nc, n_sel):
    i = np.arange(nc)[None, :]
    j = np.arange(n_sel)[:, None]
    lo = np.maximum(i * CMP_STRIDE, j * SEL_BLOCK)
    hi = np.minimum(i * CMP_STRIDE + CMP_BLOCK - 1, j * SEL_BLOCK + SEL_BLOCK - 1)
    ov = np.zeros((n_sel + 8, nc), np.float32)
    ov[:n_sel] = np.clip(hi - lo + 1, 0, None) / CMP_STRIDE
    ov[n_sel] = 1.0
    return jnp.asarray(ov, BF16)


def _window_masks(n_wb):
    q = np.arange(TQ)[:, None]
    c = np.arange(n_wb * TQ)[None, :]
    dist = q + (n_wb - 1) * TQ - c
    masks = []
    for v in range(n_wb):
        kpos = (v - (n_wb - 1)) * TQ + c
        masks.append(np.where((kpos >= 0) & (dist >= 0) & (dist < WINDOW), 0.0, MASKED))
    return jnp.asarray(np.stack(masks), F32)


def _nsa_local(q, kc, vc, kwin, vwin, gates, n_sel, n_var):
    s = q.shape[0]
    g_n, r_n, hd = NSA_KV_HEADS, NSA_GROUP, HEAD_DIM
    n_q = s // TQ
    n_wb = WINDOW // TQ + 1
    n_top = min(SEL_TOPK, n_sel)
    nc = kc.shape[1]
    rows = r_n * TQ
    assert n_top >= 3 and nc % min(2 * LANES, nc) == 0

    def win_spec(j, width):
        return pl.BlockSpec((1, TQ, width), lambda g, i: (g, jnp.maximum(i - (n_wb - 1) + j, 0), 0))

    return pl.pallas_call(
        functools.partial(_nsa_local_kernel, n_sel, n_top),
        grid=(g_n, n_q),
        in_specs=[pl.BlockSpec((TQ, r_n * hd), lambda g, i: (i, g)),
                  pl.BlockSpec((1, nc, hd), lambda g, i: (g, 0, 0)),
                  pl.BlockSpec((1, nc, LANES), lambda g, i: (g, 0, 0)),
                  pl.BlockSpec((n_sel + 8, nc), lambda g, i: (0, 0)),
                  pl.BlockSpec((1, TQ, n_wb * TQ), lambda g, i: (jnp.minimum(i, n_wb - 1), 0, 0))]
        + [win_spec(j, hd) for j in range(n_wb)] + [win_spec(j, LANES) for j in range(n_wb)]
        + [pl.BlockSpec((1, TQ, LANES), lambda g, i: (g, i, 0))],
        out_specs=[pl.BlockSpec((TQ, r_n * hd), lambda g, i: (i, g)),
                   pl.BlockSpec((n_var, 1, 1, rows, 2 * LANES), lambda g, i: (0, g, i, 0, 0))],
        out_shape=[jax.ShapeDtypeStruct((s, N_HEADS * hd), F32),
                   jax.ShapeDtypeStruct((n_var, g_n, n_q, rows, 2 * LANES), BF16)],
        scratch_shapes=[pltpu.VMEM((rows, nc), F32), pltpu.VMEM((rows, LANES), F32),
                        pltpu.VMEM((rows, LANES), F32), pltpu.VMEM((n_sel + 8, rows), F32),
                        pltpu.VMEM((n_sel, TQ), F32)],
        name="nsa_local",
        compiler_params=_cparams("parallel", "parallel"),
    )(q, kc, vc, _overlap_matrix(nc, n_sel), _window_masks(n_wb),
      *([kwin] * n_wb), *([vwin] * n_wb), gates)


def _causal_pairs(n_q, tq, tk):
    qs, ks = [], []
    for qi in range(n_q):
        for ki in range(((qi + 1) * tq - 1) // tk + 1):
            qs.append(qi)
            ks.append(ki)
    return jnp.asarray(qs, jnp.int32), jnp.asarray(ks, jnp.int32)


def _flash_scores(q, k, causal_mask):
    s = _dot_t(q, k)
    return s if causal_mask is None else jnp.where(causal_mask, s, MASKED)


def _flash_accumulate(s, v, m_ref, acc_ref, idx):
    m_prev = m_ref[idx]
    m_new = jnp.maximum(m_prev, jnp.max(s, axis=1, keepdims=True))
    p = jnp.concatenate([jnp.exp2(s[:, c * LANES:(c + 1) * LANES] - m_new).astype(BF16)
                         for c in range(s.shape[1] // LANES)], axis=1)
    acc_ref[idx] = jnp.exp2(m_prev - m_new) * acc_ref[idx] + _dot(p, v)
    m_ref[idx] = m_new


def _flash_steps(qi, ki, tq, tk, rows_per_q, n_batch, operands, m_ref, acc_ref, finalize):
    first_diag = (qi * tq) // tk
    last_k = ((qi + 1) * tq - 1) // tk

    @pl.when(ki == 0)
    def _():
        m_ref[...] = jnp.full(m_ref.shape, M_INIT, F32)
        acc_ref[...] = jnp.zeros(acc_ref.shape, F32)

    def run(mask):
        s_next = _flash_scores(*operands(0)[:2], mask)
        for b in range(n_batch):
            s = s_next
            if b + 1 < n_batch:
                s_next = _flash_scores(*operands(b + 1)[:2], mask)
            _flash_accumulate(s, operands(b)[2], m_ref, acc_ref, b)

    @pl.when(ki < first_diag)
    def _():
        run(None)

    @pl.when(ki >= first_diag)
    def _():
        shape = (rows_per_q * tq, tk)
        q_off = lax.broadcasted_iota(jnp.int32, shape, 0) % tq
        k_off = lax.broadcasted_iota(jnp.int32, shape, 1)
        run(q_off - k_off >= ki * tk - qi * tq)

    @pl.when(ki == last_k)
    def _():
        finalize()


def _nsa_sel_kernel(qtab, ktab, qaug_ref, kaug_ref, vaug_ref, pre_ref, gate_ref, o_ref,
                    m_ref, acc_ref):
    g_n, r_n, hd = NSA_KV_HEADS, NSA_GROUP, HEAD_DIM
    t = pl.program_id(0)
    qi, ki = qtab[t], ktab[t]

    def operands(g):
        return (qaug_ref[0, g, 0], kaug_ref[:, g * 2 * LANES:(g + 1) * 2 * LANES],
                vaug_ref[:, g * LANES:(g + 1) * LANES])

    def finalize():
        for g in range(g_n):
            acc = acc_ref[g]
            gate = gate_ref[g]
            o = acc[:, 0:hd] * (1.0 / acc[:, hd:hd + 1])
            for r in range(r_n):
                c = (g * r_n + r) * hd
                o_ref[:, c:c + hd] = (pre_ref[:, c:c + hd] + gate[:, 3 * r + 1:3 * r + 2]
                                      * o[r * TQ:(r + 1) * TQ]).astype(o_ref.dtype)

    _flash_steps(qi, ki, TQ, TK, r_n, g_n, operands, m_ref, acc_ref, finalize)


def _nsa_sel(qaug, kaug, vaug, pre, gates):
    n_var, g_n, n_q, rows, _ = qaug.shape
    s = kaug.shape[0]
    qtab, ktab = _causal_pairs(n_q, TQ, TK)
    tiles_per_var = BIAS_BLOCKS * SEL_BLOCK // TK
    grid_spec = pltpu.PrefetchScalarGridSpec(
        num_scalar_prefetch=2,
        grid=(qtab.shape[0],),
        in_specs=[
            pl.BlockSpec((1, g_n, 1, rows, 2 * LANES),
                         lambda t, qt, kt: (kt[t] // tiles_per_var, 0, qt[t], 0, 0)),
            pl.BlockSpec((TK, g_n * 2 * LANES), lambda t, qt, kt: (kt[t], 0)),
            pl.BlockSpec((TK, g_n * LANES), lambda t, qt, kt: (kt[t], 0)),
            pl.BlockSpec((TQ, N_HEADS * HEAD_DIM), lambda t, qt, kt: (qt[t], 0)),
            pl.BlockSpec((g_n, TQ, LANES), lambda t, qt, kt: (0, qt[t], 0)),
        ],
        out_specs=pl.BlockSpec((TQ, N_HEADS * HEAD_DIM), lambda t, qt, kt: (qt[t], 0)),
        scratch_shapes=[pltpu.VMEM((g_n, rows, LANES), F32), pltpu.VMEM((g_n, rows, LANES), F32)],
    )
    return pl.pallas_call(
        _nsa_sel_kernel,
        grid_spec=grid_spec,
        out_shape=jax.ShapeDtypeStruct((s, N_HEADS * HEAD_DIM), BF16),
        name="nsa_sel",
        compiler_params=_cparams("arbitrary"),
    )(qtab, ktab, qaug, kaug, vaug, pre, gates)


def _moba_proj_kernel(n_blk, k_eff, x_ref, cos_ref, sa_ref, sb_ref, w_ref,
                      qaug_ref, kaug_ref, vaug_ref, kmean_ref):
    tm = x_ref.shape[0]
    hd, d = HEAD_DIM, N_HEADS * HEAD_DIM
    i = pl.program_id(0)

    @pl.when(i == 0)
    def _():
        kmean_ref[...] = jnp.zeros(kmean_ref.shape, F32)

    xb = x_ref[...].astype(BF16)
    cos_t, sin_a, sin_b = cos_ref[...], sa_ref[...], sb_ref[...]
    q = _rope(_dot(xb, w_ref[:, 0:d]), cos_t, sin_a, sin_b)
    k = _rope(_dot(xb, w_ref[:, d:2 * d]), cos_t, sin_a, sin_b)
    v = _dot(xb, w_ref[:, 2 * d:3 * d])
    kmean = kmean_ref[...]
    nb = kmean.shape[0]
    own = jnp.where(lax.broadcasted_iota(jnp.int32, (1, nb), 1) == i, 1.0, 0.0).astype(BF16)
    own_onehot = jnp.broadcast_to(own, (tm, nb))
    ones_col = _ones_col(tm, LANES - hd)
    blk_t = lax.broadcasted_iota(jnp.int32, (nb, tm), 0)
    own_t = jnp.where(blk_t == i, 1.0, 0.0)
    for hp in range(N_HEADS // 2):
        bias_t = []
        for h in (2 * hp, 2 * hp + 1):
            sl = slice(h * hd, (h + 1) * hd)
            gate_t = _dot_t(kmean[:, sl], q[:, sl], precision=lax.Precision.HIGHEST)
            gate_t = jnp.where(blk_t < i, gate_t, -jnp.inf)
            bias_t.append(jnp.where(_top_k_axis0(gate_t, k_eff, own_t)[0] > 0, 0.0, MASKED))
        bias = jnp.concatenate(bias_t, axis=0).T.astype(BF16)
        for j, h in enumerate((2 * hp, 2 * hp + 1)):
            sl = slice(h * hd, (h + 1) * hd)
            qaug_ref[h, :, 0:hd] = (q[:, sl] * SCALE).astype(BF16)
            qaug_ref[h, :, hd:hd + nb] = bias[:, j * nb:(j + 1) * nb]
            kaug_ref[h, :, 0:hd] = k[:, sl].astype(BF16)
            kaug_ref[h, :, hd:hd + nb] = own_onehot
            vaug_ref[h, :, 0:hd] = v[:, sl].astype(BF16)
            vaug_ref[h, :, hd:LANES] = ones_col
    kmean_ref[pl.ds(i, 1), :] = jnp.mean(k, axis=0, keepdims=True)


def _moba_proj(x, tabs, w):
    s, d = x.shape
    n_blk = s // MOBA_BLOCK
    k_eff = min(MOBA_TOPK, max(n_blk - 1, 1))
    nb = LANES - HEAD_DIM
    assert s % MOBA_BLOCK == 0 and n_blk <= nb
    tm = MOBA_BLOCK
    row = lambda i: (i, 0)
    head = jax.ShapeDtypeStruct((N_HEADS, s, LANES), BF16)
    head_spec = pl.BlockSpec((N_HEADS, tm, LANES), lambda i: (0, i, 0))
    return pl.pallas_call(
        functools.partial(_moba_proj_kernel, n_blk, k_eff),
        grid=(s // tm,),
        in_specs=[pl.BlockSpec((tm, d), row)] + [pl.BlockSpec((tm, LANES), row)] * 3
        + [pl.BlockSpec(w.shape, lambda i: (0, 0))],
        out_specs=[head_spec] * 3,
        out_shape=[head] * 3,
        scratch_shapes=[pltpu.VMEM((nb, N_HEADS * HEAD_DIM), F32)],
        name="moba_proj",
        compiler_params=_cparams("arbitrary"),
    )(x, *tabs, w)


def _moba_attn_kernel(qtab, ktab, q_ref, k_ref, v_ref, o_ref, m_ref, acc_ref):
    hd = HEAD_DIM
    t = pl.program_id(0)
    qi, ki = qtab[t], ktab[t]

    def operands(h):
        return q_ref[h], k_ref[h], v_ref[h]

    def finalize():
        for h in range(N_HEADS):
            acc = acc_ref[h]
            o_ref[:, h * hd:(h + 1) * hd] = (acc[:, 0:hd] * (1.0 / acc[:, hd:hd + 1])).astype(o_ref.dtype)

    _flash_steps(qi, ki, q_ref.shape[1], k_ref.shape[1], 1, N_HEADS, operands, m_ref, acc_ref, finalize)


def _moba_attn(qaug, kaug, vaug):
    h_n, s, _ = qaug.shape
    tq, tk = min(MOBA_TQ, s), min(MOBA_TK, s)
    assert s % tq == 0 and s % tk == 0
    qtab, ktab = _causal_pairs(s // tq, tq, tk)
    grid_spec = pltpu.PrefetchScalarGridSpec(
        num_scalar_prefetch=2,
        grid=(qtab.shape[0],),
        in_specs=[pl.BlockSpec((h_n, tq, LANES), lambda t, qt, kt: (0, qt[t], 0)),
                  pl.BlockSpec((h_n, tk, LANES), lambda t, qt, kt: (0, kt[t], 0)),
                  pl.BlockSpec((h_n, tk, LANES), lambda t, qt, kt: (0, kt[t], 0))],
        out_specs=pl.BlockSpec((tq, h_n * HEAD_DIM), lambda t, qt, kt: (qt[t], 0)),
        scratch_shapes=[pltpu.VMEM((h_n, tq, LANES), F32), pltpu.VMEM((h_n, tq, LANES), F32)],
    )
    return pl.pallas_call(
        _moba_attn_kernel,
        grid_spec=grid_spec,
        out_shape=jax.ShapeDtypeStruct((s, h_n * HEAD_DIM), BF16),
        name="moba_attn",
        compiler_params=_cparams("arbitrary"),
    )(qtab, ktab, qaug, kaug, vaug)


def _mm_res_ln_kernel(alpha, a_ref, w_ref, x_ref, g_ref, b_ref, o_ref):
    z = alpha * x_ref[...] + _dot(a_ref[...], w_ref[...])
    mu = jnp.mean(z, axis=-1, keepdims=True)
    zc = z - mu
    var = jnp.mean(zc * zc, axis=-1, keepdims=True)
    o_ref[...] = zc * lax.rsqrt(var + LN_EPS) * g_ref[...] + b_ref[...]


def _mm_res_ln(a, w, x, gain, bias, alpha):
    s, k = a.shape
    d = w.shape[1]
    tm = min(MM_TILE, s)
    row = lambda i: (i, 0)
    whole = lambda i: (0, 0)
    return pl.pallas_call(
        functools.partial(_mm_res_ln_kernel, alpha),
        grid=(s // tm,),
        in_specs=[pl.BlockSpec((tm, k), row), pl.BlockSpec((k, d), whole),
                  pl.BlockSpec((tm, d), row), pl.BlockSpec((1, d), whole), pl.BlockSpec((1, d), whole)],
        out_specs=pl.BlockSpec((tm, d), row),
        out_shape=jax.ShapeDtypeStruct((s, d), F32),
        name="mm_res_ln",
        compiler_params=_cparams("parallel"),
    )(a, w, x, gain.reshape(1, d), bias.reshape(1, d))


def _ffn_up_kernel(n_chunk, x_ref, wg_ref, wu_ref, o_ref):
    xb = x_ref[...].astype(BF16)
    cw = wg_ref.shape[1] // n_chunk
    for c in range(n_chunk):
        sl = slice(c * cw, (c + 1) * cw)
        o_ref[:, sl] = (jax.nn.silu(_dot(xb, wg_ref[:, sl])) * _dot(xb, wu_ref[:, sl])).astype(o_ref.dtype)


def _ffn_up(x, wg, wu):
    s, d = x.shape
    f = wg.shape[1]
    n_chunk = 2 if f % (2 * LANES) == 0 else 1
    tm = min(MM_TILE, s)
    row = lambda i: (i, 0)
    whole = lambda i: (0, 0)
    return pl.pallas_call(
        functools.partial(_ffn_up_kernel, n_chunk),
        grid=(s // tm,),
        in_specs=[pl.BlockSpec((tm, d), row), pl.BlockSpec((d, f), whole), pl.BlockSpec((d, f), whole)],
        out_specs=pl.BlockSpec((tm, f), row),
        out_shape=jax.ShapeDtypeStruct((s, f), BF16),
        name="ffn_up",
        compiler_params=_cparams("parallel"),
    )(x, wg, wu)


def _nsa_mixer(x, tabs, w_in, ck_pos, ck_w1, ck_w2, cv_pos, cv_w1, cv_w2):
    s = x.shape[0]
    g_n, r_n, hd = NSA_KV_HEADS, NSA_GROUP, HEAD_DIM
    q_dim, kv_dim = N_HEADS * hd, NSA_KV_HEADS * hd
    n_sel = s // SEL_BLOCK
    n_var = -(-n_sel // BIAS_BLOCKS)
    assert s % TK == 0 and (BIAS_BLOCKS * SEL_BLOCK) % TK == 0
    wq = w_in[:, :q_dim].astype(BF16)
    wkv = w_in[:, q_dim:q_dim + 6 * kv_dim].astype(BF16)
    wg = w_in[:, q_dim + 6 * kv_dim:].reshape(-1, g_n, 3 * r_n)
    wg = jnp.pad(wg, ((0, 0), (0, 0), (0, LANES - 3 * r_n))).reshape(-1, g_n * LANES).astype(BF16)
    q, cmp_kv, kaug, vaug, kwin, vwin, gates = _nsa_proj(x, tabs, wq, wkv, wg)
    comp = _compress(cmp_kv, jnp.stack([ck_pos, cv_pos]),
                     jnp.stack([ck_w1, cv_w1]).astype(BF16), jnp.stack([ck_w2, cv_w2]).astype(BF16))
    kc = comp[0].astype(BF16)
    ones = jnp.ones(comp.shape[1:3] + (1,), F32)
    zeros = jnp.zeros(comp.shape[1:3] + (LANES - hd - 1,), F32)
    vc = jnp.concatenate([comp[1], ones, zeros], axis=-1).astype(BF16)
    pre, qaug = _nsa_local(q, kc, vc, kwin, vwin, gates, n_sel, n_var)
    return _nsa_sel(qaug, kaug, vaug, pre, gates)


def _moba_mixer(x, tabs, w_in):
    qaug, kaug, vaug = _moba_proj(x, tabs, w_in.astype(BF16))
    return _moba_attn(qaug, kaug, vaug)


def kernel(x, positions, nsa_w_in, nsa_w_o, nsa_ck_pos, nsa_ck_w1, nsa_ck_w2, nsa_cv_pos, nsa_cv_w1, nsa_cv_w2, moba_w_in, moba_w_o, ffn_wg, ffn_wu, ffn_wd, ln1_g, ln1_b, ln2_g, ln2_b):
    b, s, d = x.shape
    depth = ffn_wg.shape[0]
    alpha = (2 * depth) ** 0.25
    outs = []
    for bi in range(b):
        h = x[bi]
        tabs = _rope_tables(positions[bi])
        for i in range(depth):
            j = i // 2
            if i % 2 == 0:
                a = _nsa_mixer(h, tabs, nsa_w_in[j], nsa_ck_pos[j], nsa_ck_w1[j], nsa_ck_w2[j],
                               nsa_cv_pos[j], nsa_cv_w1[j], nsa_cv_w2[j])
                w_o = nsa_w_o[j]
            else:
                a = _moba_mixer(h, tabs, moba_w_in[j])
                w_o = moba_w_o[j]
            h = _mm_res_ln(a, w_o.astype(BF16), h, ln1_g[i], ln1_b[i], alpha)
            u = _ffn_up(h, ffn_wg[i].astype(BF16), ffn_wu[i].astype(BF16))
            h = _mm_res_ln(u, ffn_wd[i].astype(BF16), h, ln2_g[i], ln2_b[i], alpha)
        outs.append(h)
    return jnp.stack(outs)
```

```python
import functools

import numpy as np
import jax
import jax.numpy as jnp
from jax import lax
from jax.experimental import pallas as pl
from jax.experimental.pallas import tpu as pltpu

F32 = jnp.float32
BF16 = jnp.bfloat16

N_HEADS = 16
HEAD_DIM = 64
ROPE_DIM = HEAD_DIM // 4
ROPE_THETA = 500000.0
NSA_KV_HEADS = 4
NSA_GROUP = N_HEADS // NSA_KV_HEADS
CMP_BLOCK = 32
CMP_STRIDE = 16
SEL_BLOCK = 64
SEL_TOPK = 16
WINDOW = 512
MOBA_BLOCK = 256
MOBA_TOPK = 3
LN_EPS = 1e-5
LARGE = 1e30
SCALE = HEAD_DIM ** -0.5 * float(np.log2(np.e))

LANES = 128
VMEM_LIMIT_BYTES = 48 * 1024 * 1024

ROW_TILE = 256
MM_TILE = 512
TQ = 256
TK = 1024
MOBA_TQ = 1024
MOBA_TK = 512
BIAS_BLOCKS = 128
MASKED = -1e30
M_INIT = -5e29

assert WINDOW % TQ == 0 and TK % TQ == 0 and ROW_TILE == MOBA_BLOCK


def _cparams(*sem, flags=None):
    return pltpu.CompilerParams(dimension_semantics=sem, vmem_limit_bytes=VMEM_LIMIT_BYTES, flags=flags)


def _dot(a, b):
    return jnp.dot(a, b, preferred_element_type=F32)


def _dot_t(a, b, precision=None):
    return lax.dot_general(a, b, (((1,), (1,)), ((), ())), preferred_element_type=F32,
                           precision=precision)


def _rope_tables_kernel(pos_ref, c_ref, cos_ref, sa_ref, sb_ref):
    ang = pos_ref[...].astype(F32) * c_ref[0:1, :]
    c = jnp.cos(ang)
    s = jnp.sin(ang)
    cos_ref[...] = jnp.where(c_ref[1:2, :] > 0, c, 1.0)
    sa_ref[...] = jnp.where(c_ref[2:3, :] > 0, s, 0.0)
    sb_ref[...] = jnp.where(c_ref[3:4, :] > 0, -s, 0.0)


def _rope_tables(positions):
    s = positions.shape[0]
    half = ROPE_DIM // 2
    inv = ROPE_THETA ** (-jnp.arange(0, ROPE_DIM, 2, dtype=F32) / ROPE_DIM)
    d = np.arange(LANES) % HEAD_DIM
    consts = jnp.zeros((8, LANES), F32)
    consts = consts.at[0].set(inv[d % half])
    consts = consts.at[1].set(jnp.asarray(d < ROPE_DIM, F32))
    consts = consts.at[2].set(jnp.asarray((d >= half) & (d < ROPE_DIM), F32))
    consts = consts.at[3].set(jnp.asarray(d < half, F32))
    tm = min(MM_TILE, s)
    tab = jax.ShapeDtypeStruct((s, LANES), F32)
    return pl.pallas_call(
        _rope_tables_kernel,
        grid=(s // tm,),
        in_specs=[pl.BlockSpec((tm, 1), lambda i: (i, 0)),
                  pl.BlockSpec((8, LANES), lambda i: (0, 0))],
        out_specs=[pl.BlockSpec((tm, LANES), lambda i: (i, 0))] * 3,
        out_shape=[tab, tab, tab],
        name="rope_tables",
        compiler_params=_cparams("parallel"),
    )(positions.reshape(s, 1), consts)


def _rope(x, cos_t, sin_a, sin_b):
    half = ROPE_DIM // 2
    outs = []
    for c in range(x.shape[1] // LANES):
        xs = x[:, c * LANES:(c + 1) * LANES]
        outs.append(xs * cos_t + pltpu.roll(xs, half, 1) * sin_a
                    + pltpu.roll(xs, LANES - half, 1) * sin_b)
    return outs[0] if len(outs) == 1 else jnp.concatenate(outs, axis=1)


def _ones_col(rows, width):
    return jnp.where(lax.broadcasted_iota(jnp.int32, (rows, width), 1) == 0, 1.0, 0.0).astype(BF16)


def _nsa_proj_kernel(x_ref, cos_ref, sa_ref, sb_ref, wq_ref, wkv_ref, wg_ref,
                     q_ref, cmp_ref, kaug_ref, vaug_ref, kwin_ref, vwin_ref, gate_ref):
    tm = x_ref.shape[0]
    g_n, hd, kvd = NSA_KV_HEADS, HEAD_DIM, NSA_KV_HEADS * HEAD_DIM
    xb = x_ref[...].astype(BF16)
    cos_t, sin_a, sin_b = cos_ref[...], sa_ref[...], sb_ref[...]
    q = _rope(_dot(xb, wq_ref[...]), cos_t, sin_a, sin_b) * SCALE
    q_ref[...] = q.astype(BF16)
    kv = _dot(xb, wkv_ref[...])
    k_cmp = _rope(kv[:, 0:kvd], cos_t, sin_a, sin_b)
    v_cmp = kv[:, kvd:2 * kvd]
    k_sel = _rope(kv[:, 2 * kvd:3 * kvd], cos_t, sin_a, sin_b)
    v_sel = kv[:, 3 * kvd:4 * kvd]
    k_win = _rope(kv[:, 4 * kvd:5 * kvd], cos_t, sin_a, sin_b)
    v_win = kv[:, 5 * kvd:6 * kvd]
    gate = jax.nn.sigmoid(_dot(xb, wg_ref[...]))
    tok = pl.program_id(0) * tm + lax.broadcasted_iota(jnp.int32, (tm, BIAS_BLOCKS), 0)
    blk = (tok // SEL_BLOCK) % BIAS_BLOCKS
    onehot = jnp.where(lax.broadcasted_iota(jnp.int32, (tm, BIAS_BLOCKS), 1) == blk,
                       1.0, 0.0).astype(BF16)
    ones_col = _ones_col(tm, LANES - hd)
    for g in range(g_n):
        sl = slice(g * hd, (g + 1) * hd)
        cmp_ref[0, g] = k_cmp[:, sl]
        cmp_ref[1, g] = v_cmp[:, sl]
        kwin_ref[g] = k_win[:, sl].astype(BF16)
        vwin_ref[g, :, 0:hd] = v_win[:, sl].astype(BF16)
        vwin_ref[g, :, hd:LANES] = ones_col
        gate_ref[g] = gate[:, g * LANES:(g + 1) * LANES]
        kb = g * 2 * LANES
        kaug_ref[:, kb:kb + BIAS_BLOCKS] = onehot
        kaug_ref[:, kb + BIAS_BLOCKS:kb + BIAS_BLOCKS + hd] = k_sel[:, sl].astype(BF16)
        kaug_ref[:, kb + BIAS_BLOCKS + hd:kb + 2 * LANES] = jnp.zeros((tm, 2 * LANES - BIAS_BLOCKS - hd), BF16)
        vb = g * LANES
        vaug_ref[:, vb:vb + hd] = v_sel[:, sl].astype(BF16)
        vaug_ref[:, vb + hd:vb + LANES] = ones_col


def _nsa_proj(x, tabs, wq, wkv, wg):
    s, d = x.shape
    g_n, hd = NSA_KV_HEADS, HEAD_DIM
    tm = min(ROW_TILE, s)
    row = lambda i: (i, 0)
    whole = lambda i: (0, 0)
    out_shape = [
        jax.ShapeDtypeStruct((s, N_HEADS * hd), BF16),
        jax.ShapeDtypeStruct((2, g_n, s, hd), F32),
        jax.ShapeDtypeStruct((s, g_n * 2 * LANES), BF16),
        jax.ShapeDtypeStruct((s, g_n * LANES), BF16),
        jax.ShapeDtypeStruct((g_n, s, hd), BF16),
        jax.ShapeDtypeStruct((g_n, s, LANES), BF16),
        jax.ShapeDtypeStruct((g_n, s, LANES), F32),
    ]
    out_specs = [
        pl.BlockSpec((tm, N_HEADS * hd), row),
        pl.BlockSpec((2, g_n, tm, hd), lambda i: (0, 0, i, 0)),
        pl.BlockSpec((tm, g_n * 2 * LANES), row),
        pl.BlockSpec((tm, g_n * LANES), row),
        pl.BlockSpec((g_n, tm, hd), lambda i: (0, i, 0)),
        pl.BlockSpec((g_n, tm, LANES), lambda i: (0, i, 0)),
        pl.BlockSpec((g_n, tm, LANES), lambda i: (0, i, 0)),
    ]
    return pl.pallas_call(
        _nsa_proj_kernel,
        grid=(s // tm,),
        in_specs=[pl.BlockSpec((tm, d), row)] + [pl.BlockSpec((tm, LANES), row)] * 3
        + [pl.BlockSpec(wq.shape, whole), pl.BlockSpec(wkv.shape, whole),
           pl.BlockSpec(wg.shape, whole)],
        out_specs=out_specs,
        out_shape=out_shape,
        name="nsa_proj",
        compiler_params=_cparams("parallel"),
    )(x, *tabs, wq, wkv, wg)


def _compress_kernel(x_ref, pos_ref, w1_ref, w2_ref, out_ref):
    n16 = x_ref.shape[2]
    half = w1_ref.shape[1] // 2
    x = x_ref[0, 0]
    a = _dot((x + pos_ref[0, 0:1, :]).astype(BF16), w1_ref[0, 0:half, :])
    b = _dot((x + pos_ref[0, 1:2, :]).astype(BF16), w1_ref[0, half:, :])
    hid = a + pltpu.roll(b, n16 - 1, 0)
    out_ref[0, 0] = _dot(jax.nn.gelu(hid).astype(BF16), w2_ref[0])


def _compress(cmp_kv, pos, w1, w2):
    _, g_n, s, hd = cmp_kv.shape
    n16 = s // CMP_STRIDE
    x = cmp_kv.reshape(2, g_n, n16, CMP_STRIDE * hd)
    pos2 = pos.reshape(2, 2, CMP_STRIDE * hd)
    return pl.pallas_call(
        _compress_kernel,
        grid=(2, g_n),
        in_specs=[pl.BlockSpec((1, 1, n16, CMP_STRIDE * hd), lambda a, g: (a, g, 0, 0)),
                  pl.BlockSpec((1, 2, CMP_STRIDE * hd), lambda a, g: (a, 0, 0)),
                  pl.BlockSpec((1,) + w1.shape[1:], lambda a, g: (a, 0, 0)),
                  pl.BlockSpec((1,) + w2.shape[1:], lambda a, g: (a, 0, 0))],
        out_specs=pl.BlockSpec((1, 1, n16, hd), lambda a, g: (a, g, 0, 0)),
        out_shape=jax.ShapeDtypeStruct((2, g_n, n16, hd), F32),
        name="nsa_compress",
        compiler_params=_cparams("parallel", "parallel"),
    )(x, pos2, w1, w2)


def _top_k_axis0(score, n_rounds, sel, enable=None):
    n = score.shape[0]
    row = lax.broadcasted_iota(jnp.int32, score.shape, 0).astype(F32)
    for _ in range(n_rounds):
        m = jnp.max(score, axis=0, keepdims=True)
        idx = jnp.min(jnp.where(score == m, row, float(n)), axis=0, keepdims=True)
        ok = m > -jnp.inf if enable is None else (m > -jnp.inf) & enable
        hit = row == jnp.where(ok, idx, -1.0)
        sel = jnp.where(hit, 1.0, sel)
        score = jnp.where(hit, -jnp.inf, score)
    return sel, score


def _nsa_local_kernel(n_sel, n_top, q_ref, kc_ref, vc_ref, ov_ref, wmask_ref, *rest):
    n_wb = WINDOW // TQ + 1
    kw_refs = rest[:n_wb]
    vw_refs = rest[n_wb:2 * n_wb]
    gate_ref, pre_ref, qaug_ref, s_scr, m_scr, o_acc, imp_acc = rest[2 * n_wb:]
    r_n, hd = NSA_GROUP, HEAD_DIM
    qi = pl.program_id(1)
    qg = q_ref[...]
    qs = jnp.concatenate([qg[:, r * hd:(r + 1) * hd] for r in range(r_n)], axis=0)
    rows = r_n * TQ

    nc = kc_ref.shape[1]
    ct = min(2 * LANES, nc)
    t_max = ((qi * TQ + TQ - CMP_BLOCK) // CMP_STRIDE) // ct
    t_full = (jnp.maximum(qi * TQ - (CMP_BLOCK - 1) + CMP_STRIDE, 0) // CMP_STRIDE) // ct

    def lane_fold_max(s):
        m = s[:, 0:LANES]
        for c in range(1, s.shape[1] // LANES):
            m = jnp.maximum(m, s[:, c * LANES:(c + 1) * LANES])
        return m

    m_scr[...] = jnp.full(m_scr.shape, M_INIT, F32)
    for t in range(nc // ct):
        tile = slice(t * ct, (t + 1) * ct)

        @pl.when(t < t_full)
        def _():
            s = _dot_t(qs, kc_ref[0, tile, :])
            s_scr[:, tile] = s
            m_scr[...] = jnp.maximum(m_scr[...], lane_fold_max(s))

        @pl.when((t >= t_full) & (t <= t_max))
        def _():
            q_off = lax.broadcasted_iota(jnp.int32, (rows, ct), 0) % TQ
            tok = lax.broadcasted_iota(jnp.int32, (rows, ct), 1)
            visible = q_off - CMP_STRIDE * tok >= CMP_STRIDE * ct * t + CMP_BLOCK - 1 - qi * TQ
            s = jnp.where(visible, _dot_t(qs, kc_ref[0, tile, :]), MASKED)
            s_scr[:, tile] = s
            m_scr[...] = jnp.maximum(m_scr[...], lane_fold_max(s))

    m_scr[...] = jnp.broadcast_to(jnp.max(m_scr[...], axis=1, keepdims=True), m_scr.shape)
    o_acc[...] = jnp.zeros(o_acc.shape, F32)
    imp_acc[...] = jnp.zeros(imp_acc.shape, F32)
    for t in range(nc // ct):
        tile = slice(t * ct, (t + 1) * ct)

        @pl.when(t <= t_max)
        def _():
            m = m_scr[...]
            p = jnp.concatenate(
                [jnp.exp2(s_scr[:, t * ct + c * LANES:t * ct + (c + 1) * LANES] - m).astype(BF16)
                 for c in range(ct // LANES)], axis=1)
            o_acc[...] += _dot(p, vc_ref[0, tile, :])
            imp_acc[...] += _dot_t(ov_ref[:, tile], p)

    n_prefix = 4 if n_sel % 32 == 0 else 1
    prefix = ((qi + 1) * (TQ // SEL_BLOCK) - 1) * n_prefix // n_sel
    for v in range(n_prefix):
        @pl.when(prefix == v)
        def _():
            _nsa_local_tail(n_sel * (v + 1) // n_prefix, v == 0, n_sel, n_top, qi, qg, qs,
                            o_acc, imp_acc, wmask_ref, kw_refs, vw_refs, gate_ref, pre_ref, qaug_ref)


def _nsa_local_tail(n_rows, first_tile, n_sel, n_top, qi, qg, qs, o_acc, imp_acc, wmask_ref,
                    kw_refs, vw_refs, gate_ref, pre_ref, qaug_ref):
    r_n, hd = NSA_GROUP, HEAD_DIM
    rows = r_n * TQ
    o_aug = o_acc[...]
    l_col = o_aug[:, hd:hd + 1]
    o_cmp = o_aug[:, 0:hd] * jnp.where(l_col > 0, 1.0 / l_col, 0.0)

    l_row = imp_acc[n_sel:n_sel + 1, :]
    imp_t = imp_acc[0:n_rows, :] * jnp.where(l_row > 0, 1.0 / l_row, 0.0)
    imp = imp_t[:, 0:TQ]
    for r in range(1, r_n):
        imp = imp + imp_t[:, r * TQ:(r + 1) * TQ]
    blk = lax.broadcasted_iota(jnp.int32, (n_rows, TQ), 0)
    blk_q = (qi * TQ + lax.broadcasted_iota(jnp.int32, (n_rows, TQ), 1)) // SEL_BLOCK
    forced = (blk == 0) | (blk == blk_q) | (blk == blk_q - 1)
    score = jnp.where((blk <= blk_q) & jnp.logical_not(forced), imp, -jnp.inf)
    sel = jnp.where(forced, 1.0, 0.0)
    n_forced_min = 3
    blk_q_row = blk_q[0:1, :]
    sel_tiles = []
    for c in range(TQ // LANES):
        ls = slice(c * LANES, (c + 1) * LANES)
        sel_c, score_c = _top_k_axis0(score[:, ls], n_top - n_forced_min, sel[:, ls])
        if first_tile:
            for extra in range(1, n_forced_min):
                sel_c, score_c = _top_k_axis0(score_c, 1, sel_c,
                                              enable=blk_q_row[:, ls] < n_forced_min - extra)
        sel_tiles.append(sel_c)
    sel = jnp.concatenate(sel_tiles, axis=1)
    n_pad = -(-n_rows // LANES) * LANES
    if n_pad > n_rows:
        sel = jnp.concatenate([sel, jnp.zeros((n_pad - n_rows, TQ), F32)], axis=0)
    bias = jnp.where(sel > 0, 0.0, MASKED).T.astype(BF16)
    n_var = qaug_ref.shape[0]
    if n_pad < n_var * BIAS_BLOCKS:
        bias = jnp.concatenate(
            [bias, jnp.full((TQ, n_var * BIAS_BLOCKS - n_pad), MASKED, BF16)], axis=1)
    zeros = jnp.zeros((TQ, 2 * LANES - BIAS_BLOCKS - hd), BF16)
    for v in range(n_var):
        for r in range(r_n):
            rs = slice(r * TQ, (r + 1) * TQ)
            qaug_ref[v, 0, 0, rs, 0:BIAS_BLOCKS] = bias[:, v * BIAS_BLOCKS:(v + 1) * BIAS_BLOCKS]
            qaug_ref[v, 0, 0, rs, BIAS_BLOCKS:BIAS_BLOCKS + hd] = qg[:, r * hd:(r + 1) * hd]
            qaug_ref[v, 0, 0, rs, BIAS_BLOCKS + hd:2 * LANES] = zeros

    kcat = jnp.concatenate([kw[0] for kw in kw_refs], axis=0)
    vcat = jnp.concatenate([vw[0] for vw in vw_refs], axis=0)
    s = _dot_t(qs, kcat)
    wmask = wmask_ref[0]
    s = jnp.concatenate([s[r * TQ:(r + 1) * TQ] + wmask for r in range(r_n)], axis=0)
    m = jnp.broadcast_to(jnp.max(s, axis=1, keepdims=True), (rows, LANES))
    p = jnp.concatenate([jnp.exp2(s[:, c * LANES:(c + 1) * LANES] - m).astype(BF16)
                         for c in range(s.shape[1] // LANES)], axis=1)
    o_aug = _dot(p, vcat)
    o_win = o_aug[:, 0:hd] * (1.0 / o_aug[:, hd:hd + 1])

    gate = gate_ref[0]
    for r in range(r_n):
        rs = slice(r * TQ, (r + 1) * TQ)
        pre_ref[:, r * hd:(r + 1) * hd] = (gate[:, 3 * r:3 * r + 1] * o_cmp[rs]
                                          + gate[:, 3 * r + 2:3 * r + 3] * o_win[rs])


def _overlap_matrix(nc, n_sel):
    i = np.arange(nc)[None, :]
    j = np.arange(n_sel)[:, None]
    lo = np.maximum(i * CMP_STRIDE, j * SEL_BLOCK)
    hi = np.minimum(i * CMP_STRIDE + CMP_BLOCK - 1, j * SEL_BLOCK + SEL_BLOCK - 1)
    ov = np.zeros((n_sel + 8, nc), np.float32)
    ov[:n_sel] = np.clip(hi - lo + 1, 0, None) / CMP_STRIDE
    ov[n_sel] = 1.0
    return jnp.asarray(ov, BF16)


def _window_masks(n_wb):
    q = np.arange(TQ)[:, None]
    c = np.arange(n_wb * TQ)[None, :]
    dist = q + (n_wb - 1) * TQ - c
    masks = []
    for v in range(n_wb):
        kpos = (v - (n_wb - 1)) * TQ + c
        masks.append(np.where((kpos >= 0) & (dist >= 0) & (dist < WINDOW), 0.0, MASKED))
    return jnp.asarray(np.stack(masks), F32)


def _nsa_local(q, kc, vc, kwin, vwin, gates, n_sel, n_var):
    s = q.shape[0]
    g_n, r_n, hd = NSA_KV_HEADS, NSA_GROUP, HEAD_DIM
    n_q = s // TQ
    n_wb = WINDOW // TQ + 1
    n_top = min(SEL_TOPK, n_sel)
    nc = kc.shape[1]
    rows = r_n * TQ
    assert n_top >= 3 and nc % min(2 * LANES, nc) == 0

    def win_spec(j, width):
        return pl.BlockSpec((1, TQ, width), lambda g, i: (g, jnp.maximum(i - (n_wb - 1) + j, 0), 0))

    return pl.pallas_call(
        functools.partial(_nsa_local_kernel, n_sel, n_top),
        grid=(g_n, n_q),
        in_specs=[pl.BlockSpec((TQ, r_n * hd), lambda g, i: (i, g)),
                  pl.BlockSpec((1, nc, hd), lambda g, i: (g, 0, 0)),
                  pl.BlockSpec((1, nc, LANES), lambda g, i: (g, 0, 0)),
                  pl.BlockSpec((n_sel + 8, nc), lambda g, i: (0, 0)),
                  pl.BlockSpec((1, TQ, n_wb * TQ), lambda g, i: (jnp.minimum(i, n_wb - 1), 0, 0))]
        + [win_spec(j, hd) for j in range(n_wb)] + [win_spec(j, LANES) for j in range(n_wb)]
        + [pl.BlockSpec((1, TQ, LANES), lambda g, i: (g, i, 0))],
        out_specs=[pl.BlockSpec((TQ, r_n * hd), lambda g, i: (i, g)),
                   pl.BlockSpec((n_var, 1, 1, rows, 2 * LANES), lambda g, i: (0, g, i, 0, 0))],
        out_shape=[jax.ShapeDtypeStruct((s, N_HEADS * hd), F32),
                   jax.ShapeDtypeStruct((n_var, g_n, n_q, rows, 2 * LANES), BF16)],
        scratch_shapes=[pltpu.VMEM((rows, nc), F32), pltpu.VMEM((rows, LANES), F32),
                        pltpu.VMEM((rows, LANES), F32), pltpu.VMEM((n_sel + 8, rows), F32)],
        name="nsa_local",
        compiler_params=_cparams("parallel", "parallel"),
    )(q, kc, vc, _overlap_matrix(nc, n_sel), _window_masks(n_wb),
      *([kwin] * n_wb), *([vwin] * n_wb), gates)


def _causal_pairs(n_q, tq, tk):
    qs, ks = [], []
    for qi in range(n_q):
        for ki in range(((qi + 1) * tq - 1) // tk + 1):
            qs.append(qi)
            ks.append(ki)
    return jnp.asarray(qs, jnp.int32), jnp.asarray(ks, jnp.int32)


def _flash_scores(q, k, causal_mask):
    s = _dot_t(q, k)
    return s if causal_mask is None else jnp.where(causal_mask, s, MASKED)


def _flash_accumulate(s, v, m_ref, acc_ref, idx):
    m_prev = m_ref[idx]
    m_new = jnp.maximum(m_prev, jnp.max(s, axis=1, keepdims=True))
    p = jnp.concatenate([jnp.exp2(s[:, c * LANES:(c + 1) * LANES] - m_new).astype(BF16)
                         for c in range(s.shape[1] // LANES)], axis=1)
    acc_ref[idx] = jnp.exp2(m_prev - m_new) * acc_ref[idx] + _dot(p, v)
    m_ref[idx] = m_new


def _flash_steps(qi, ki, tq, tk, rows_per_q, n_batch, operands, m_ref, acc_ref, finalize):
    first_diag = (qi * tq) // tk
    last_k = ((qi + 1) * tq - 1) // tk

    @pl.when(ki == 0)
    def _():
        m_ref[...] = jnp.full(m_ref.shape, M_INIT, F32)
        acc_ref[...] = jnp.zeros(acc_ref.shape, F32)

    def run(mask):
        s_next = _flash_scores(*operands(0)[:2], mask)
        for b in range(n_batch):
            s = s_next
            if b + 1 < n_batch:
                s_next = _flash_scores(*operands(b + 1)[:2], mask)
            _flash_accumulate(s, operands(b)[2], m_ref, acc_ref, b)

    @pl.when(ki < first_diag)
    def _():
        run(None)

    @pl.when(ki >= first_diag)
    def _():
        shape = (rows_per_q * tq, tk)
        q_off = lax.broadcasted_iota(jnp.int32, shape, 0) % tq
        k_off = lax.broadcasted_iota(jnp.int32, shape, 1)
        run(q_off - k_off >= ki * tk - qi * tq)

    @pl.when(ki == last_k)
    def _():
        finalize()


def _nsa_sel_kernel(qtab, ktab, qaug_ref, kaug_ref, vaug_ref, pre_ref, gate_ref, o_ref,
                    m_ref, acc_ref):
    g_n, r_n, hd = NSA_KV_HEADS, NSA_GROUP, HEAD_DIM
    t = pl.program_id(0)
    qi, ki = qtab[t], ktab[t]

    def operands(g):
        return (qaug_ref[0, g, 0], kaug_ref[:, g * 2 * LANES:(g + 1) * 2 * LANES],
                vaug_ref[:, g * LANES:(g + 1) * LANES])

    def finalize():
        for g in range(g_n):
            acc = acc_ref[g]
            gate = gate_ref[g]
            o = acc[:, 0:hd] * (1.0 / acc[:, hd:hd + 1])
            for r in range(r_n):
                c = (g * r_n + r) * hd
                o_ref[:, c:c + hd] = (pre_ref[:, c:c + hd] + gate[:, 3 * r + 1:3 * r + 2]
                                      * o[r * TQ:(r + 1) * TQ]).astype(o_ref.dtype)

    _flash_steps(qi, ki, TQ, TK, r_n, g_n, operands, m_ref, acc_ref, finalize)


def _nsa_sel(qaug, kaug, vaug, pre, gates):
    n_var, g_n, n_q, rows, _ = qaug.shape
    s = kaug.shape[0]
    qtab, ktab = _causal_pairs(n_q, TQ, TK)
    tiles_per_var = BIAS_BLOCKS * SEL_BLOCK // TK
    grid_spec = pltpu.PrefetchScalarGridSpec(
        num_scalar_prefetch=2,
        grid=(qtab.shape[0],),
        in_specs=[
            pl.BlockSpec((1, g_n, 1, rows, 2 * LANES),
                         lambda t, qt, kt: (kt[t] // tiles_per_var, 0, qt[t], 0, 0)),
            pl.BlockSpec((TK, g_n * 2 * LANES), lambda t, qt, kt: (kt[t], 0)),
            pl.BlockSpec((TK, g_n * LANES), lambda t, qt, kt: (kt[t], 0)),
            pl.BlockSpec((TQ, N_HEADS * HEAD_DIM), lambda t, qt, kt: (qt[t], 0)),
            pl.BlockSpec((g_n, TQ, LANES), lambda t, qt, kt: (0, qt[t], 0)),
        ],
        out_specs=pl.BlockSpec((TQ, N_HEADS * HEAD_DIM), lambda t, qt, kt: (qt[t], 0)),
        scratch_shapes=[pltpu.VMEM((g_n, rows, LANES), F32), pltpu.VMEM((g_n, rows, LANES), F32)],
    )
    return pl.pallas_call(
        _nsa_sel_kernel,
        grid_spec=grid_spec,
        out_shape=jax.ShapeDtypeStruct((s, N_HEADS * HEAD_DIM), BF16),
        name="nsa_sel",
        compiler_params=_cparams("arbitrary"),
    )(qtab, ktab, qaug, kaug, vaug, pre, gates)


def _moba_proj_kernel(n_blk, k_eff, x_ref, cos_ref, sa_ref, sb_ref, w_ref,
                      qaug_ref, kaug_ref, vaug_ref, kmean_ref):
    tm = x_ref.shape[0]
    hd, d = HEAD_DIM, N_HEADS * HEAD_DIM
    i = pl.program_id(0)

    @pl.when(i == 0)
    def _():
        kmean_ref[...] = jnp.zeros(kmean_ref.shape, F32)

    xb = x_ref[...].astype(BF16)
    cos_t, sin_a, sin_b = cos_ref[...], sa_ref[...], sb_ref[...]
    q = _rope(_dot(xb, w_ref[:, 0:d]), cos_t, sin_a, sin_b)
    k = _rope(_dot(xb, w_ref[:, d:2 * d]), cos_t, sin_a, sin_b)
    v = _dot(xb, w_ref[:, 2 * d:3 * d])
    kmean = kmean_ref[...]
    nb = kmean.shape[0]
    own = jnp.where(lax.broadcasted_iota(jnp.int32, (1, nb), 1) == i, 1.0, 0.0).astype(BF16)
    own_onehot = jnp.broadcast_to(own, (tm, nb))
    ones_col = _ones_col(tm, LANES - hd)
    blk_t = lax.broadcasted_iota(jnp.int32, (nb, tm), 0)
    own_t = jnp.where(blk_t == i, 1.0, 0.0)
    for hp in range(N_HEADS // 2):
        bias_t = []
        for h in (2 * hp, 2 * hp + 1):
            sl = slice(h * hd, (h + 1) * hd)
            gate_t = _dot_t(kmean[:, sl], q[:, sl], precision=lax.Precision.HIGHEST)
            gate_t = jnp.where(blk_t < i, gate_t, -jnp.inf)
            bias_t.append(jnp.where(_top_k_axis0(gate_t, k_eff, own_t)[0] > 0, 0.0, MASKED))
        bias = jnp.concatenate(bias_t, axis=0).T.astype(BF16)
        for j, h in enumerate((2 * hp, 2 * hp + 1)):
            sl = slice(h * hd, (h + 1) * hd)
            qaug_ref[h, :, 0:hd] = (q[:, sl] * SCALE).astype(BF16)
            qaug_ref[h, :, hd:hd + nb] = bias[:, j * nb:(j + 1) * nb]
            kaug_ref[h, :, 0:hd] = k[:, sl].astype(BF16)
            kaug_ref[h, :, hd:hd + nb] = own_onehot
            vaug_ref[h, :, 0:hd] = v[:, sl].astype(BF16)
            vaug_ref[h, :, hd:LANES] = ones_col
    kmean_ref[pl.ds(i, 1), :] = jnp.mean(k, axis=0, keepdims=True)


def _moba_proj(x, tabs, w):
    s, d = x.shape
    n_blk = s // MOBA_BLOCK
    k_eff = min(MOBA_TOPK, max(n_blk - 1, 1))
    nb = LANES - HEAD_DIM
    assert s % MOBA_BLOCK == 0 and n_blk <= nb
    tm = MOBA_BLOCK
    row = lambda i: (i, 0)
    head = jax.ShapeDtypeStruct((N_HEADS, s, LANES), BF16)
    head_spec = pl.BlockSpec((N_HEADS, tm, LANES), lambda i: (0, i, 0))
    return pl.pallas_call(
        functools.partial(_moba_proj_kernel, n_blk, k_eff),
        grid=(s // tm,),
        in_specs=[pl.BlockSpec((tm, d), row)] + [pl.BlockSpec((tm, LANES), row)] * 3
        + [pl.BlockSpec(w.shape, lambda i: (0, 0))],
        out_specs=[head_spec] * 3,
        out_shape=[head] * 3,
        scratch_shapes=[pltpu.VMEM((nb, N_HEADS * HEAD_DIM), F32)],
        name="moba_proj",
        compiler_params=_cparams("arbitrary"),
    )(x, *tabs, w)


def _moba_attn_kernel(qtab, ktab, q_ref, k_ref, v_ref, o_ref, m_ref, acc_ref):
    hd = HEAD_DIM
    t = pl.program_id(0)
    qi, ki = qtab[t], ktab[t]

    def operands(h):
        return q_ref[h], k_ref[h], v_ref[h]

    def finalize():
        for h in range(N_HEADS):
            acc = acc_ref[h]
            o_ref[:, h * hd:(h + 1) * hd] = (acc[:, 0:hd] * (1.0 / acc[:, hd:hd + 1])).astype(o_ref.dtype)

    _flash_steps(qi, ki, q_ref.shape[1], k_ref.shape[1], 1, N_HEADS, operands, m_ref, acc_ref, finalize)


def _moba_attn(qaug, kaug, vaug):
    h_n, s, _ = qaug.shape
    tq, tk = min(MOBA_TQ, s), min(MOBA_TK, s)
    assert s % tq == 0 and s % tk == 0
    qtab, ktab = _causal_pairs(s // tq, tq, tk)
    grid_spec = pltpu.PrefetchScalarGridSpec(
        num_scalar_prefetch=2,
        grid=(qtab.shape[0],),
        in_specs=[pl.BlockSpec((h_n, tq, LANES), lambda t, qt, kt: (0, qt[t], 0)),
                  pl.BlockSpec((h_n, tk, LANES), lambda t, qt, kt: (0, kt[t], 0)),
                  pl.BlockSpec((h_n, tk, LANES), lambda t, qt, kt: (0, kt[t], 0))],
        out_specs=pl.BlockSpec((tq, h_n * HEAD_DIM), lambda t, qt, kt: (qt[t], 0)),
        scratch_shapes=[pltpu.VMEM((h_n, tq, LANES), F32), pltpu.VMEM((h_n, tq, LANES), F32)],
    )
    return pl.pallas_call(
        _moba_attn_kernel,
        grid_spec=grid_spec,
        out_shape=jax.ShapeDtypeStruct((s, h_n * HEAD_DIM), BF16),
        name="moba_attn",
        compiler_params=_cparams("arbitrary"),
    )(qtab, ktab, qaug, kaug, vaug)


def _mm_res_ln_kernel(alpha, a_ref, w_ref, x_ref, g_ref, b_ref, o_ref):
    z = alpha * x_ref[...] + _dot(a_ref[...], w_ref[...])
    mu = jnp.mean(z, axis=-1, keepdims=True)
    zc = z - mu
    var = jnp.mean(zc * zc, axis=-1, keepdims=True)
    o_ref[...] = zc * lax.rsqrt(var + LN_EPS) * g_ref[...] + b_ref[...]


def _mm_res_ln(a, w, x, gain, bias, alpha):
    s, k = a.shape
    d = w.shape[1]
    tm = min(MM_TILE, s)
    row = lambda i: (i, 0)
    whole = lambda i: (0, 0)
    return pl.pallas_call(
        functools.partial(_mm_res_ln_kernel, alpha),
        grid=(s // tm,),
        in_specs=[pl.BlockSpec((tm, k), row), pl.BlockSpec((k, d), whole),
                  pl.BlockSpec((tm, d), row), pl.BlockSpec((1, d), whole), pl.BlockSpec((1, d), whole)],
        out_specs=pl.BlockSpec((tm, d), row),
        out_shape=jax.ShapeDtypeStruct((s, d), F32),
        name="mm_res_ln",
        compiler_params=_cparams("parallel"),
    )(a, w, x, gain.reshape(1, d), bias.reshape(1, d))


def _ffn_up_kernel(n_chunk, x_ref, wg_ref, wu_ref, o_ref):
    xb = x_ref[...].astype(BF16)
    cw = wg_ref.shape[1] // n_chunk
    for c in range(n_chunk):
        sl = slice(c * cw, (c + 1) * cw)
        o_ref[:, sl] = (jax.nn.silu(_dot(xb, wg_ref[:, sl])) * _dot(xb, wu_ref[:, sl])).astype(o_ref.dtype)


def _ffn_up(x, wg, wu):
    s, d = x.shape
    f = wg.shape[1]
    n_chunk = 2 if f % (2 * LANES) == 0 else 1
    tm = min(MM_TILE, s)
    row = lambda i: (i, 0)
    whole = lambda i: (0, 0)
    return pl.pallas_call(
        functools.partial(_ffn_up_kernel, n_chunk),
        grid=(s // tm,),
        in_specs=[pl.BlockSpec((tm, d), row), pl.BlockSpec((d, f), whole), pl.BlockSpec((d, f), whole)],
        out_specs=pl.BlockSpec((tm, f), row),
        out_shape=jax.ShapeDtypeStruct((s, f), BF16),
        name="ffn_up",
        compiler_params=_cparams("parallel"),
    )(x, wg, wu)


def _nsa_mixer(x, tabs, w_in, ck_pos, ck_w1, ck_w2, cv_pos, cv_w1, cv_w2):
    s = x.shape[0]
    g_n, r_n, hd = NSA_KV_HEADS, NSA_GROUP, HEAD_DIM
    q_dim, kv_dim = N_HEADS * hd, NSA_KV_HEADS * hd
    n_sel = s // SEL_BLOCK
    n_var = -(-n_sel // BIAS_BLOCKS)
    assert s % TK == 0 and (BIAS_BLOCKS * SEL_BLOCK) % TK == 0
    wq = w_in[:, :q_dim].astype(BF16)
    wkv = w_in[:, q_dim:q_dim + 6 * kv_dim].astype(BF16)
    wg = w_in[:, q_dim + 6 * kv_dim:].reshape(-1, g_n, 3 * r_n)
    wg = jnp.pad(wg, ((0, 0), (0, 0), (0, LANES - 3 * r_n))).reshape(-1, g_n * LANES).astype(BF16)
    q, cmp_kv, kaug, vaug, kwin, vwin, gates = _nsa_proj(x, tabs, wq, wkv, wg)
    comp = _compress(cmp_kv, jnp.stack([ck_pos, cv_pos]),
                     jnp.stack([ck_w1, cv_w1]).astype(BF16), jnp.stack([ck_w2, cv_w2]).astype(BF16))
    kc = comp[0].astype(BF16)
    ones = jnp.ones(comp.shape[1:3] + (1,), F32)
    zeros = jnp.zeros(comp.shape[1:3] + (LANES - hd - 1,), F32)
    vc = jnp.concatenate([comp[1], ones, zeros], axis=-1).astype(BF16)
    pre, qaug = _nsa_local(q, kc, vc, kwin, vwin, gates, n_sel, n_var)
    return _nsa_sel(qaug, kaug, vaug, pre, gates)


def _moba_mixer(x, tabs, w_in):
    qaug, kaug, vaug = _moba_proj(x, tabs, w_in.astype(BF16))
    return _moba_attn(qaug, kaug, vaug)


def kernel(x, positions, nsa_w_in, nsa_w_o, nsa_ck_pos, nsa_ck_w1, nsa_ck_w2, nsa_cv_pos, nsa_cv_w1, nsa_cv_w2, moba_w_in, moba_w_o, ffn_wg, ffn_wu, ffn_wd, ln1_g, ln1_b, ln2_g, ln2_b):
    b, s, d = x.shape
    depth = ffn_wg.shape[0]
    alpha = (2 * depth) ** 0.25
    outs = []
    for bi in range(b):
        h = x[bi]
        tabs = _rope_tables(positions[bi])
        for i in range(depth):
            j = i // 2
            if i % 2 == 0:
                a = _nsa_mixer(h, tabs, nsa_w_in[j], nsa_ck_pos[j], nsa_ck_w1[j], nsa_ck_w2[j],
                               nsa_cv_pos[j], nsa_cv_w1[j], nsa_cv_w2[j])
                w_o = nsa_w_o[j]
            else:
                a = _moba_mixer(h, tabs, moba_w_in[j])
                w_o = moba_w_o[j]
            h = _mm_res_ln(a, w_o.astype(BF16), h, ln1_g[i], ln1_b[i], alpha)
            u = _ffn_up(h, ffn_wg[i].astype(BF16), ffn_wu[i].astype(BF16))
            h = _mm_res_ln(u, ffn_wd[i].astype(BF16), h, ln2_g[i], ln2_b[i], alpha)
        outs.append(h)
    return jnp.stack(outs)
```

```python
import functools

import numpy as np
import jax
import jax.numpy as jnp
from jax import lax
from jax.experimental import pallas as pl
from jax.experimental.pallas import tpu as pltpu

F32 = jnp.float32
BF16 = jnp.bfloat16

N_HEADS = 16
HEAD_DIM = 64
ROPE_DIM = HEAD_DIM // 4
ROPE_THETA = 500000.0
NSA_KV_HEADS = 4
NSA_GROUP = N_HEADS // NSA_KV_HEADS
CMP_BLOCK = 32
CMP_STRIDE = 16
SEL_BLOCK = 64
SEL_TOPK = 16
WINDOW = 512
MOBA_BLOCK = 256
MOBA_TOPK = 3
LN_EPS = 1e-5
LARGE = 1e30
SCALE = HEAD_DIM ** -0.5 * float(np.log2(np.e))

LANES = 128
VMEM_LIMIT_BYTES = 48 * 1024 * 1024

ROW_TILE = 256
MM_TILE = 512
TQ = 256
TK = 1024
MOBA_TQ = 1024
MOBA_TK = 512
QK_AHEAD = 1
BIAS_BLOCKS = 128
MASKED = -1e30
M_INIT = -5e29

assert WINDOW % TQ == 0 and TK % TQ == 0 and ROW_TILE == MOBA_BLOCK


def _cparams(*sem, flags=None):
    return pltpu.CompilerParams(dimension_semantics=sem, vmem_limit_bytes=VMEM_LIMIT_BYTES, flags=flags)


def _dot(a, b):
    return jnp.dot(a, b, preferred_element_type=F32)


def _dot_t(a, b, precision=None):
    return lax.dot_general(a, b, (((1,), (1,)), ((), ())), preferred_element_type=F32,
                           precision=precision)


def _rope_tables_kernel(pos_ref, c_ref, cos_ref, sa_ref, sb_ref):
    ang = pos_ref[...].astype(F32) * c_ref[0:1, :]
    c = jnp.cos(ang)
    s = jnp.sin(ang)
    cos_ref[...] = jnp.where(c_ref[1:2, :] > 0, c, 1.0)
    sa_ref[...] = jnp.where(c_ref[2:3, :] > 0, s, 0.0)
    sb_ref[...] = jnp.where(c_ref[3:4, :] > 0, -s, 0.0)


def _rope_tables(positions):
    s = positions.shape[0]
    half = ROPE_DIM // 2
    inv = ROPE_THETA ** (-jnp.arange(0, ROPE_DIM, 2, dtype=F32) / ROPE_DIM)
    d = np.arange(LANES) % HEAD_DIM
    consts = jnp.zeros((8, LANES), F32)
    consts = consts.at[0].set(inv[d % half])
    consts = consts.at[1].set(jnp.asarray(d < ROPE_DIM, F32))
    consts = consts.at[2].set(jnp.asarray((d >= half) & (d < ROPE_DIM), F32))
    consts = consts.at[3].set(jnp.asarray(d < half, F32))
    tm = min(MM_TILE, s)
    tab = jax.ShapeDtypeStruct((s, LANES), F32)
    return pl.pallas_call(
        _rope_tables_kernel,
        grid=(s // tm,),
        in_specs=[pl.BlockSpec((tm, 1), lambda i: (i, 0)),
                  pl.BlockSpec((8, LANES), lambda i: (0, 0))],
        out_specs=[pl.BlockSpec((tm, LANES), lambda i: (i, 0))] * 3,
        out_shape=[tab, tab, tab],
        name="rope_tables",
        compiler_params=_cparams("parallel"),
    )(positions.reshape(s, 1), consts)


def _rope(x, cos_t, sin_a, sin_b):
    half = ROPE_DIM // 2
    outs = []
    for c in range(x.shape[1] // LANES):
        xs = x[:, c * LANES:(c + 1) * LANES]
        outs.append(xs * cos_t + pltpu.roll(xs, half, 1) * sin_a
                    + pltpu.roll(xs, LANES - half, 1) * sin_b)
    return outs[0] if len(outs) == 1 else jnp.concatenate(outs, axis=1)


def _ones_col(rows, width):
    return jnp.where(lax.broadcasted_iota(jnp.int32, (rows, width), 1) == 0, 1.0, 0.0).astype(BF16)


def _nsa_proj_kernel(x_ref, cos_ref, sa_ref, sb_ref, wq_ref, wkv_ref, wg_ref,
                     q_ref, cmp_ref, kaug_ref, vaug_ref, kwin_ref, vwin_ref, gate_ref):
    tm = x_ref.shape[0]
    g_n, hd, kvd = NSA_KV_HEADS, HEAD_DIM, NSA_KV_HEADS * HEAD_DIM
    xb = x_ref[...].astype(BF16)
    cos_t, sin_a, sin_b = cos_ref[...], sa_ref[...], sb_ref[...]
    q = _rope(_dot(xb, wq_ref[...]), cos_t, sin_a, sin_b) * SCALE
    q_ref[...] = q.astype(BF16)
    kv = _dot(xb, wkv_ref[...])
    k_cmp = _rope(kv[:, 0:kvd], cos_t, sin_a, sin_b)
    v_cmp = kv[:, kvd:2 * kvd]
    k_sel = _rope(kv[:, 2 * kvd:3 * kvd], cos_t, sin_a, sin_b)
    v_sel = kv[:, 3 * kvd:4 * kvd]
    k_win = _rope(kv[:, 4 * kvd:5 * kvd], cos_t, sin_a, sin_b)
    v_win = kv[:, 5 * kvd:6 * kvd]
    gate = jax.nn.sigmoid(_dot(xb, wg_ref[...]))
    tok = pl.program_id(0) * tm + lax.broadcasted_iota(jnp.int32, (tm, BIAS_BLOCKS), 0)
    blk = (tok // SEL_BLOCK) % BIAS_BLOCKS
    onehot = jnp.where(lax.broadcasted_iota(jnp.int32, (tm, BIAS_BLOCKS), 1) == blk,
                       1.0, 0.0).astype(BF16)
    ones_col = _ones_col(tm, LANES - hd)
    for g in range(g_n):
        sl = slice(g * hd, (g + 1) * hd)
        cmp_ref[0, g] = k_cmp[:, sl]
        cmp_ref[1, g] = v_cmp[:, sl]
        kwin_ref[g] = k_win[:, sl].astype(BF16)
        vwin_ref[g, :, 0:hd] = v_win[:, sl].astype(BF16)
        vwin_ref[g, :, hd:LANES] = ones_col
        gate_ref[g] = gate[:, g * LANES:(g + 1) * LANES]
        kb = g * 2 * LANES
        kaug_ref[:, kb:kb + BIAS_BLOCKS] = onehot
        kaug_ref[:, kb + BIAS_BLOCKS:kb + BIAS_BLOCKS + hd] = k_sel[:, sl].astype(BF16)
        kaug_ref[:, kb + BIAS_BLOCKS + hd:kb + 2 * LANES] = jnp.zeros((tm, 2 * LANES - BIAS_BLOCKS - hd), BF16)
        vb = g * LANES
        vaug_ref[:, vb:vb + hd] = v_sel[:, sl].astype(BF16)
        vaug_ref[:, vb + hd:vb + LANES] = ones_col


def _nsa_proj(x, tabs, wq, wkv, wg):
    s, d = x.shape
    g_n, hd = NSA_KV_HEADS, HEAD_DIM
    tm = min(ROW_TILE, s)
    row = lambda i: (i, 0)
    whole = lambda i: (0, 0)
    out_shape = [
        jax.ShapeDtypeStruct((s, N_HEADS * hd), BF16),
        jax.ShapeDtypeStruct((2, g_n, s, hd), F32),
        jax.ShapeDtypeStruct((s, g_n * 2 * LANES), BF16),
        jax.ShapeDtypeStruct((s, g_n * LANES), BF16),
        jax.ShapeDtypeStruct((g_n, s, hd), BF16),
        jax.ShapeDtypeStruct((g_n, s, LANES), BF16),
        jax.ShapeDtypeStruct((g_n, s, LANES), F32),
    ]
    out_specs = [
        pl.BlockSpec((tm, N_HEADS * hd), row),
        pl.BlockSpec((2, g_n, tm, hd), lambda i: (0, 0, i, 0)),
        pl.BlockSpec((tm, g_n * 2 * LANES), row),
        pl.BlockSpec((tm, g_n * LANES), row),
        pl.BlockSpec((g_n, tm, hd), lambda i: (0, i, 0)),
        pl.BlockSpec((g_n, tm, LANES), lambda i: (0, i, 0)),
        pl.BlockSpec((g_n, tm, LANES), lambda i: (0, i, 0)),
    ]
    return pl.pallas_call(
        _nsa_proj_kernel,
        grid=(s // tm,),
        in_specs=[pl.BlockSpec((tm, d), row)] + [pl.BlockSpec((tm, LANES), row)] * 3
        + [pl.BlockSpec(wq.shape, whole), pl.BlockSpec(wkv.shape, whole),
           pl.BlockSpec(wg.shape, whole)],
        out_specs=out_specs,
        out_shape=out_shape,
        name="nsa_proj",
        compiler_params=_cparams("parallel"),
    )(x, *tabs, wq, wkv, wg)


def _compress_kernel(x_ref, pos_ref, w1_ref, w2_ref, out_ref):
    n16 = x_ref.shape[2]
    half = w1_ref.shape[1] // 2
    x = x_ref[0, 0]
    a = _dot((x + pos_ref[0, 0:1, :]).astype(BF16), w1_ref[0, 0:half, :])
    b = _dot((x + pos_ref[0, 1:2, :]).astype(BF16), w1_ref[0, half:, :])
    hid = a + pltpu.roll(b, n16 - 1, 0)
    out_ref[0, 0] = _dot(jax.nn.gelu(hid).astype(BF16), w2_ref[0])


def _compress(cmp_kv, pos, w1, w2):
    _, g_n, s, hd = cmp_kv.shape
    n16 = s // CMP_STRIDE
    x = cmp_kv.reshape(2, g_n, n16, CMP_STRIDE * hd)
    pos2 = pos.reshape(2, 2, CMP_STRIDE * hd)
    return pl.pallas_call(
        _compress_kernel,
        grid=(2, g_n),
        in_specs=[pl.BlockSpec((1, 1, n16, CMP_STRIDE * hd), lambda a, g: (a, g, 0, 0)),
                  pl.BlockSpec((1, 2, CMP_STRIDE * hd), lambda a, g: (a, 0, 0)),
                  pl.BlockSpec((1,) + w1.shape[1:], lambda a, g: (a, 0, 0)),
                  pl.BlockSpec((1,) + w2.shape[1:], lambda a, g: (a, 0, 0))],
        out_specs=pl.BlockSpec((1, 1, n16, hd), lambda a, g: (a, g, 0, 0)),
        out_shape=jax.ShapeDtypeStruct((2, g_n, n16, hd), F32),
        name="nsa_compress",
        compiler_params=_cparams("parallel", "parallel"),
    )(x, pos2, w1, w2)


def _top_k_axis0(score, n_rounds, sel, enable=None):
    n = score.shape[0]
    row = lax.broadcasted_iota(jnp.int32, score.shape, 0).astype(F32)
    for _ in range(n_rounds):
        m = jnp.max(score, axis=0, keepdims=True)
        idx = jnp.min(jnp.where(score == m, row, float(n)), axis=0, keepdims=True)
        ok = m > -jnp.inf if enable is None else (m > -jnp.inf) & enable
        hit = row == jnp.where(ok, idx, -1.0)
        sel = jnp.where(hit, 1.0, sel)
        score = jnp.where(hit, -jnp.inf, score)
    return sel, score


def _nsa_local_kernel(n_sel, n_top, q_ref, kc_ref, vc_ref, ov_ref, wmask_ref, *rest):
    n_wb = WINDOW // TQ + 1
    kw_refs = rest[:n_wb]
    vw_refs = rest[n_wb:2 * n_wb]
    gate_ref, pre_ref, qaug_ref, s_scr, m_scr, o_acc, imp_acc = rest[2 * n_wb:]
    r_n, hd = NSA_GROUP, HEAD_DIM
    qi = pl.program_id(1)
    qg = q_ref[...]
    qs = jnp.concatenate([qg[:, r * hd:(r + 1) * hd] for r in range(r_n)], axis=0)
    rows = r_n * TQ

    nc = kc_ref.shape[1]
    ct = min(2 * LANES, nc)
    t_max = ((qi * TQ + TQ - CMP_BLOCK) // CMP_STRIDE) // ct
    t_full = (jnp.maximum(qi * TQ - (CMP_BLOCK - 1) + CMP_STRIDE, 0) // CMP_STRIDE) // ct

    def lane_fold_max(s):
        m = s[:, 0:LANES]
        for c in range(1, s.shape[1] // LANES):
            m = jnp.maximum(m, s[:, c * LANES:(c + 1) * LANES])
        return m

    m_scr[...] = jnp.full(m_scr.shape, M_INIT, F32)
    for t in range(nc // ct):
        tile = slice(t * ct, (t + 1) * ct)

        @pl.when(t < t_full)
        def _():
            s = _dot_t(qs, kc_ref[0, tile, :])
            s_scr[:, tile] = s
            m_scr[...] = jnp.maximum(m_scr[...], lane_fold_max(s))

        @pl.when((t >= t_full) & (t <= t_max))
        def _():
            q_off = lax.broadcasted_iota(jnp.int32, (rows, ct), 0) % TQ
            tok = lax.broadcasted_iota(jnp.int32, (rows, ct), 1)
            visible = q_off - CMP_STRIDE * tok >= CMP_STRIDE * ct * t + CMP_BLOCK - 1 - qi * TQ
            s = jnp.where(visible, _dot_t(qs, kc_ref[0, tile, :]), MASKED)
            s_scr[:, tile] = s
            m_scr[...] = jnp.maximum(m_scr[...], lane_fold_max(s))

    m_scr[...] = jnp.broadcast_to(jnp.max(m_scr[...], axis=1, keepdims=True), m_scr.shape)
    o_acc[...] = jnp.zeros(o_acc.shape, F32)
    imp_acc[...] = jnp.zeros(imp_acc.shape, F32)
    for t in range(nc // ct):
        tile = slice(t * ct, (t + 1) * ct)

        @pl.when(t <= t_max)
        def _():
            m = m_scr[...]
            p = jnp.concatenate(
                [jnp.exp2(s_scr[:, t * ct + c * LANES:t * ct + (c + 1) * LANES] - m).astype(BF16)
                 for c in range(ct // LANES)], axis=1)
            o_acc[...] += _dot(p, vc_ref[0, tile, :])
            imp_acc[...] += _dot_t(ov_ref[:, tile], p)

    n_prefix = 4 if n_sel % 32 == 0 else 1
    prefix = ((qi + 1) * (TQ // SEL_BLOCK) - 1) * n_prefix // n_sel
    for v in range(n_prefix):
        @pl.when(prefix == v)
        def _():
            _nsa_local_tail(n_sel * (v + 1) // n_prefix, v == 0, n_sel, n_top, qi, qg, qs,
                            o_acc, imp_acc, wmask_ref, kw_refs, vw_refs, gate_ref, pre_ref, qaug_ref)


def _nsa_local_tail(n_rows, first_tile, n_sel, n_top, qi, qg, qs, o_acc, imp_acc, wmask_ref,
                    kw_refs, vw_refs, gate_ref, pre_ref, qaug_ref):
    r_n, hd = NSA_GROUP, HEAD_DIM
    rows = r_n * TQ
    o_aug = o_acc[...]
    l_col = o_aug[:, hd:hd + 1]
    o_cmp = o_aug[:, 0:hd] * jnp.where(l_col > 0, 1.0 / l_col, 0.0)

    l_row = imp_acc[n_sel:n_sel + 1, :]
    imp_t = imp_acc[0:n_rows, :] * jnp.where(l_row > 0, 1.0 / l_row, 0.0)
    imp = imp_t[:, 0:TQ]
    for r in range(1, r_n):
        imp = imp + imp_t[:, r * TQ:(r + 1) * TQ]
    blk = lax.broadcasted_iota(jnp.int32, (n_rows, TQ), 0)
    blk_q = (qi * TQ + lax.broadcasted_iota(jnp.int32, (n_rows, TQ), 1)) // SEL_BLOCK
    forced = (blk == 0) | (blk == blk_q) | (blk == blk_q - 1)
    score = jnp.where((blk <= blk_q) & jnp.logical_not(forced), imp, -jnp.inf)
    sel = jnp.where(forced, 1.0, 0.0)
    n_forced_min = 3
    blk_q_row = blk_q[0:1, :]
    sel_tiles = []
    for c in range(TQ // LANES):
        ls = slice(c * LANES, (c + 1) * LANES)
        sel_c, score_c = _top_k_axis0(score[:, ls], n_top - n_forced_min, sel[:, ls])
        if first_tile:
            for extra in range(1, n_forced_min):
                sel_c, score_c = _top_k_axis0(score_c, 1, sel_c,
                                              enable=blk_q_row[:, ls] < n_forced_min - extra)
        sel_tiles.append(sel_c)
    sel = jnp.concatenate(sel_tiles, axis=1)
    n_pad = -(-n_rows // LANES) * LANES
    if n_pad > n_rows:
        sel = jnp.concatenate([sel, jnp.zeros((n_pad - n_rows, TQ), F32)], axis=0)
    bias = jnp.where(sel > 0, 0.0, MASKED).T.astype(BF16)
    n_var = qaug_ref.shape[0]
    if n_pad < n_var * BIAS_BLOCKS:
        bias = jnp.concatenate(
            [bias, jnp.full((TQ, n_var * BIAS_BLOCKS - n_pad), MASKED, BF16)], axis=1)
    zeros = jnp.zeros((TQ, 2 * LANES - BIAS_BLOCKS - hd), BF16)
    for v in range(n_var):
        for r in range(r_n):
            rs = slice(r * TQ, (r + 1) * TQ)
            qaug_ref[v, 0, 0, rs, 0:BIAS_BLOCKS] = bias[:, v * BIAS_BLOCKS:(v + 1) * BIAS_BLOCKS]
            qaug_ref[v, 0, 0, rs, BIAS_BLOCKS:BIAS_BLOCKS + hd] = qg[:, r * hd:(r + 1) * hd]
            qaug_ref[v, 0, 0, rs, BIAS_BLOCKS + hd:2 * LANES] = zeros

    kcat = jnp.concatenate([kw[0] for kw in kw_refs], axis=0)
    vcat = jnp.concatenate([vw[0] for vw in vw_refs], axis=0)
    s = _dot_t(qs, kcat)
    wmask = wmask_ref[0]
    s = jnp.concatenate([s[r * TQ:(r + 1) * TQ] + wmask for r in range(r_n)], axis=0)
    m = jnp.broadcast_to(jnp.max(s, axis=1, keepdims=True), (rows, LANES))
    p = jnp.concatenate([jnp.exp2(s[:, c * LANES:(c + 1) * LANES] - m).astype(BF16)
                         for c in range(s.shape[1] // LANES)], axis=1)
    o_aug = _dot(p, vcat)
    o_win = o_aug[:, 0:hd] * (1.0 / o_aug[:, hd:hd + 1])

    gate = gate_ref[0]
    for r in range(r_n):
        rs = slice(r * TQ, (r + 1) * TQ)
        pre_ref[:, r * hd:(r + 1) * hd] = (gate[:, 3 * r:3 * r + 1] * o_cmp[rs]
                                          + gate[:, 3 * r + 2:3 * r + 3] * o_win[rs])


def _overlap_matrix(nc, n_sel):
    i = np.arange(nc)[None, :]
    j = np.arange(n_sel)[:, None]
    lo = np.maximum(i * CMP_STRIDE, j * SEL_BLOCK)
    hi = np.minimum(i * CMP_STRIDE + CMP_BLOCK - 1, j * SEL_BLOCK + SEL_BLOCK - 1)
    ov = np.zeros((n_sel + 8, nc), np.float32)
    ov[:n_sel] = np.clip(hi - lo + 1, 0, None) / CMP_STRIDE
    ov[n_sel] = 1.0
    return jnp.asarray(ov, BF16)


def _window_masks(n_wb):
    q = np.arange(TQ)[:, None]
    c = np.arange(n_wb * TQ)[None, :]
    dist = q + (n_wb - 1) * TQ - c
    masks = []
    for v in range(n_wb):
        kpos = (v - (n_wb - 1)) * TQ + c
        masks.append(np.where((kpos >= 0) & (dist >= 0) & (dist < WINDOW), 0.0, MASKED))
    return jnp.asarray(np.stack(masks), F32)


def _nsa_local(q, kc, vc, kwin, vwin, gates, n_sel, n_var):
    s = q.shape[0]
    g_n, r_n, hd = NSA_KV_HEADS, NSA_GROUP, HEAD_DIM
    n_q = s // TQ
    n_wb = WINDOW // TQ + 1
    n_top = min(SEL_TOPK, n_sel)
    nc = kc.shape[1]
    rows = r_n * TQ
    assert n_top >= 3 and nc % min(2 * LANES, nc) == 0

    def win_spec(j, width):
        return pl.BlockSpec((1, TQ, width), lambda g, i: (g, jnp.maximum(i - (n_wb - 1) + j, 0), 0))

    return pl.pallas_call(
        functools.partial(_nsa_local_kernel, n_sel, n_top),
        grid=(g_n, n_q),
        in_specs=[pl.BlockSpec((TQ, r_n * hd), lambda g, i: (i, g)),
                  pl.BlockSpec((1, nc, hd), lambda g, i: (g, 0, 0)),
                  pl.BlockSpec((1, nc, LANES), lambda g, i: (g, 0, 0)),
                  pl.BlockSpec((n_sel + 8, nc), lambda g, i: (0, 0)),
                  pl.BlockSpec((1, TQ, n_wb * TQ), lambda g, i: (jnp.minimum(i, n_wb - 1), 0, 0))]
        + [win_spec(j, hd) for j in range(n_wb)] + [win_spec(j, LANES) for j in range(n_wb)]
        + [pl.BlockSpec((1, TQ, LANES), lambda g, i: (g, i, 0))],
        out_specs=[pl.BlockSpec((TQ, r_n * hd), lambda g, i: (i, g)),
                   pl.BlockSpec((n_var, 1, 1, rows, 2 * LANES), lambda g, i: (0, g, i, 0, 0))],
        out_shape=[jax.ShapeDtypeStruct((s, N_HEADS * hd), F32),
                   jax.ShapeDtypeStruct((n_var, g_n, n_q, rows, 2 * LANES), BF16)],
        scratch_shapes=[pltpu.VMEM((rows, nc), F32), pltpu.VMEM((rows, LANES), F32),
                        pltpu.VMEM((rows, LANES), F32), pltpu.VMEM((n_sel + 8, rows), F32)],
        name="nsa_local",
        compiler_params=_cparams("parallel", "parallel"),
    )(q, kc, vc, _overlap_matrix(nc, n_sel), _window_masks(n_wb),
      *([kwin] * n_wb), *([vwin] * n_wb), gates)


def _causal_pairs(n_q, tq, tk):
    qs, ks = [], []
    for qi in range(n_q):
        for ki in range(((qi + 1) * tq - 1) // tk + 1):
            qs.append(qi)
            ks.append(ki)
    return jnp.asarray(qs, jnp.int32), jnp.asarray(ks, jnp.int32)


def _flash_scores(q, k, causal_mask):
    s = _dot_t(q, k)
    return s if causal_mask is None else jnp.where(causal_mask, s, MASKED)


def _flash_accumulate(s, v, m_ref, acc_ref, idx, row0):
    m_prev = m_ref[idx, row0:, :]
    m_new = jnp.maximum(m_prev, jnp.max(s, axis=1, keepdims=True))
    p = jnp.concatenate([jnp.exp2(s[:, c * LANES:(c + 1) * LANES] - m_new).astype(BF16)
                         for c in range(s.shape[1] // LANES)], axis=1)
    acc_ref[idx, row0:, :] = jnp.exp2(m_prev - m_new) * acc_ref[idx, row0:, :] + _dot(p, v)
    m_ref[idx, row0:, :] = m_new


def _flash_steps(qi, ki, tq, tk, rows_per_q, n_batch, operands, m_ref, acc_ref, finalize):
    first_diag = (qi * tq) // tk
    last_k = ((qi + 1) * tq - 1) // tk
    assert rows_per_q == 1 or tq <= tk

    @pl.when(ki == 0)
    def _():
        m_ref[...] = jnp.full(m_ref.shape, M_INIT, F32)
        acc_ref[...] = jnp.zeros(acc_ref.shape, F32)

    def run(masked, row0, n_keys):
        mask = None
        if masked:
            shape = (rows_per_q * tq - row0, n_keys)
            q_off = lax.broadcasted_iota(jnp.int32, shape, 0) % tq + row0
            k_off = lax.broadcasted_iota(jnp.int32, shape, 1)
            mask = q_off - k_off >= ki * tk - qi * tq
        def scores(b):
            q, k, _ = operands(b, row0, n_keys)
            return _flash_scores(q, k, mask)

        ahead = [scores(b) for b in range(min(QK_AHEAD, n_batch))]
        for b in range(n_batch):
            s = ahead.pop(0)
            if b + QK_AHEAD < n_batch:
                ahead.append(scores(b + QK_AHEAD))
            _flash_accumulate(s, operands(b, row0, n_keys)[2], m_ref, acc_ref, b, row0)

    @pl.when(ki < first_diag)
    def _():
        run(False, 0, tk)

    if tq >= tk:
        variants = [(ki - first_diag == j, j * tk, tk) for j in range(tq // tk)]
    else:
        variants = [(qi % (tk // tq) == j, 0, (j + 1) * tq) for j in range(tk // tq)]
    for cond, row0, n_keys in variants:
        @pl.when((ki >= first_diag) & cond)
        def _():
            run(True, row0, n_keys)

    @pl.when(ki == last_k)
    def _():
        finalize()


def _nsa_sel_kernel(qtab, ktab, qaug_ref, kaug_ref, vaug_ref, pre_ref, gate_ref, o_ref,
                    m_ref, acc_ref):
    g_n, r_n, hd = NSA_KV_HEADS, NSA_GROUP, HEAD_DIM
    t = pl.program_id(0)
    qi, ki = qtab[t], ktab[t]

    def operands(g, row0, n_keys):
        return (qaug_ref[0, g, 0, row0:, :], kaug_ref[0:n_keys, g * 2 * LANES:(g + 1) * 2 * LANES],
                vaug_ref[0:n_keys, g * LANES:(g + 1) * LANES])

    def finalize():
        for g in range(g_n):
            acc = acc_ref[g]
            gate = gate_ref[g]
            o = acc[:, 0:hd] * (1.0 / acc[:, hd:hd + 1])
            for r in range(r_n):
                c = (g * r_n + r) * hd
                o_ref[:, c:c + hd] = (pre_ref[:, c:c + hd] + gate[:, 3 * r + 1:3 * r + 2]
                                      * o[r * TQ:(r + 1) * TQ]).astype(o_ref.dtype)

    _flash_steps(qi, ki, TQ, TK, r_n, g_n, operands, m_ref, acc_ref, finalize)


def _nsa_sel(qaug, kaug, vaug, pre, gates):
    n_var, g_n, n_q, rows, _ = qaug.shape
    s = kaug.shape[0]
    qtab, ktab = _causal_pairs(n_q, TQ, TK)
    tiles_per_var = BIAS_BLOCKS * SEL_BLOCK // TK
    grid_spec = pltpu.PrefetchScalarGridSpec(
        num_scalar_prefetch=2,
        grid=(qtab.shape[0],),
        in_specs=[
            pl.BlockSpec((1, g_n, 1, rows, 2 * LANES),
                         lambda t, qt, kt: (kt[t] // tiles_per_var, 0, qt[t], 0, 0)),
            pl.BlockSpec((TK, g_n * 2 * LANES), lambda t, qt, kt: (kt[t], 0)),
            pl.BlockSpec((TK, g_n * LANES), lambda t, qt, kt: (kt[t], 0)),
            pl.BlockSpec((TQ, N_HEADS * HEAD_DIM), lambda t, qt, kt: (qt[t], 0)),
            pl.BlockSpec((g_n, TQ, LANES), lambda t, qt, kt: (0, qt[t], 0)),
        ],
        out_specs=pl.BlockSpec((TQ, N_HEADS * HEAD_DIM), lambda t, qt, kt: (qt[t], 0)),
        scratch_shapes=[pltpu.VMEM((g_n, rows, LANES), F32), pltpu.VMEM((g_n, rows, LANES), F32)],
    )
    return pl.pallas_call(
        _nsa_sel_kernel,
        grid_spec=grid_spec,
        out_shape=jax.ShapeDtypeStruct((s, N_HEADS * HEAD_DIM), BF16),
        name="nsa_sel",
        compiler_params=_cparams("arbitrary"),
    )(qtab, ktab, qaug, kaug, vaug, pre, gates)


def _moba_proj_kernel(n_blk, k_eff, x_ref, cos_ref, sa_ref, sb_ref, w_ref,
                      qaug_ref, kaug_ref, vaug_ref, kmean_ref):
    tm = x_ref.shape[0]
    hd, d = HEAD_DIM, N_HEADS * HEAD_DIM
    i = pl.program_id(0)

    @pl.when(i == 0)
    def _():
        kmean_ref[...] = jnp.zeros(kmean_ref.shape, F32)

    xb = x_ref[...].astype(BF16)
    cos_t, sin_a, sin_b = cos_ref[...], sa_ref[...], sb_ref[...]
    q = _rope(_dot(xb, w_ref[:, 0:d]), cos_t, sin_a, sin_b)
    k = _rope(_dot(xb, w_ref[:, d:2 * d]), cos_t, sin_a, sin_b)
    v = _dot(xb, w_ref[:, 2 * d:3 * d])
    kmean = kmean_ref[...]
    nb = kmean.shape[0]
    own = jnp.where(lax.broadcasted_iota(jnp.int32, (1, nb), 1) == i, 1.0, 0.0).astype(BF16)
    own_onehot = jnp.broadcast_to(own, (tm, nb))
    ones_col = _ones_col(tm, LANES - hd)
    blk_t = lax.broadcasted_iota(jnp.int32, (nb, tm), 0)
    own_t = jnp.where(blk_t == i, 1.0, 0.0)
    for hp in range(N_HEADS // 2):
        bias_t = []
        for h in (2 * hp, 2 * hp + 1):
            sl = slice(h * hd, (h + 1) * hd)
            gate_t = _dot_t(kmean[:, sl], q[:, sl], precision=lax.Precision.HIGHEST)
            gate_t = jnp.where(blk_t < i, gate_t, -jnp.inf)
            bias_t.append(jnp.where(_top_k_axis0(gate_t, k_eff, own_t)[0] > 0, 0.0, MASKED))
        bias = jnp.concatenate(bias_t, axis=0).T.astype(BF16)
        for j, h in enumerate((2 * hp, 2 * hp + 1)):
            sl = slice(h * hd, (h + 1) * hd)
            qaug_ref[h, :, 0:hd] = (q[:, sl] * SCALE).astype(BF16)
            qaug_ref[h, :, hd:hd + nb] = bias[:, j * nb:(j + 1) * nb]
            kaug_ref[h, :, 0:hd] = k[:, sl].astype(BF16)
            kaug_ref[h, :, hd:hd + nb] = own_onehot
            vaug_ref[h, :, 0:hd] = v[:, sl].astype(BF16)
            vaug_ref[h, :, hd:LANES] = ones_col
    kmean_ref[pl.ds(i, 1), :] = jnp.mean(k, axis=0, keepdims=True)


def _moba_proj(x, tabs, w):
    s, d = x.shape
    n_blk = s // MOBA_BLOCK
    k_eff = min(MOBA_TOPK, max(n_blk - 1, 1))
    nb = LANES - HEAD_DIM
    assert s % MOBA_BLOCK == 0 and n_blk <= nb
    tm = MOBA_BLOCK
    row = lambda i: (i, 0)
    head = jax.ShapeDtypeStruct((N_HEADS, s, LANES), BF16)
    head_spec = pl.BlockSpec((N_HEADS, tm, LANES), lambda i: (0, i, 0))
    return pl.pallas_call(
        functools.partial(_moba_proj_kernel, n_blk, k_eff),
        grid=(s // tm,),
        in_specs=[pl.BlockSpec((tm, d), row)] + [pl.BlockSpec((tm, LANES), row)] * 3
        + [pl.BlockSpec(w.shape, lambda i: (0, 0))],
        out_specs=[head_spec] * 3,
        out_shape=[head] * 3,
        scratch_shapes=[pltpu.VMEM((nb, N_HEADS * HEAD_DIM), F32)],
        name="moba_proj",
        compiler_params=_cparams("arbitrary"),
    )(x, *tabs, w)


def _moba_attn_kernel(qtab, ktab, q_ref, k_ref, v_ref, o_ref, m_ref, acc_ref):
    hd = HEAD_DIM
    t = pl.program_id(0)
    qi, ki = qtab[t], ktab[t]

    def operands(h, row0, n_keys):
        return q_ref[h, row0:, :], k_ref[h, 0:n_keys, :], v_ref[h, 0:n_keys, :]

    def finalize():
        for h in range(N_HEADS):
            acc = acc_ref[h]
            o_ref[:, h * hd:(h + 1) * hd] = (acc[:, 0:hd] * (1.0 / acc[:, hd:hd + 1])).astype(o_ref.dtype)

    _flash_steps(qi, ki, q_ref.shape[1], k_ref.shape[1], 1, N_HEADS, operands, m_ref, acc_ref, finalize)


def _moba_attn(qaug, kaug, vaug):
    h_n, s, _ = qaug.shape
    tq, tk = min(MOBA_TQ, s), min(MOBA_TK, s)
    assert s % tq == 0 and s % tk == 0
    qtab, ktab = _causal_pairs(s // tq, tq, tk)
    grid_spec = pltpu.PrefetchScalarGridSpec(
        num_scalar_prefetch=2,
        grid=(qtab.shape[0],),
        in_specs=[pl.BlockSpec((h_n, tq, LANES), lambda t, qt, kt: (0, qt[t], 0)),
                  pl.BlockSpec((h_n, tk, LANES), lambda t, qt, kt: (0, kt[t], 0)),
                  pl.BlockSpec((h_n, tk, LANES), lambda t, qt, kt: (0, kt[t], 0))],
        out_specs=pl.BlockSpec((tq, h_n * HEAD_DIM), lambda t, qt, kt: (qt[t], 0)),
        scratch_shapes=[pltpu.VMEM((h_n, tq, LANES), F32), pltpu.VMEM((h_n, tq, LANES), F32)],
    )
    return pl.pallas_call(
        _moba_attn_kernel,
        grid_spec=grid_spec,
        out_shape=jax.ShapeDtypeStruct((s, h_n * HEAD_DIM), BF16),
        name="moba_attn",
        compiler_params=_cparams("arbitrary"),
    )(qtab, ktab, qaug, kaug, vaug)


def _mm_res_ln_kernel(alpha, a_ref, w_ref, x_ref, g_ref, b_ref, o_ref):
    z = alpha * x_ref[...] + _dot(a_ref[...], w_ref[...])
    mu = jnp.mean(z, axis=-1, keepdims=True)
    zc = z - mu
    var = jnp.mean(zc * zc, axis=-1, keepdims=True)
    o_ref[...] = zc * lax.rsqrt(var + LN_EPS) * g_ref[...] + b_ref[...]


def _mm_res_ln(a, w, x, gain, bias, alpha):
    s, k = a.shape
    d = w.shape[1]
    tm = min(MM_TILE, s)
    row = lambda i: (i, 0)
    whole = lambda i: (0, 0)
    return pl.pallas_call(
        functools.partial(_mm_res_ln_kernel, alpha),
        grid=(s // tm,),
        in_specs=[pl.BlockSpec((tm, k), row), pl.BlockSpec((k, d), whole),
                  pl.BlockSpec((tm, d), row), pl.BlockSpec((1, d), whole), pl.BlockSpec((1, d), whole)],
        out_specs=pl.BlockSpec((tm, d), row),
        out_shape=jax.ShapeDtypeStruct((s, d), F32),
        name="mm_res_ln",
        compiler_params=_cparams("parallel"),
    )(a, w, x, gain.reshape(1, d), bias.reshape(1, d))


def _ffn_up_kernel(n_chunk, x_ref, wg_ref, wu_ref, o_ref):
    xb = x_ref[...].astype(BF16)
    cw = wg_ref.shape[1] // n_chunk
    for c in range(n_chunk):
        sl = slice(c * cw, (c + 1) * cw)
        o_ref[:, sl] = (jax.nn.silu(_dot(xb, wg_ref[:, sl])) * _dot(xb, wu_ref[:, sl])).astype(o_ref.dtype)


def _ffn_up(x, wg, wu):
    s, d = x.shape
    f = wg.shape[1]
    n_chunk = 2 if f % (2 * LANES) == 0 else 1
    tm = min(MM_TILE, s)
    row = lambda i: (i, 0)
    whole = lambda i: (0, 0)
    return pl.pallas_call(
        functools.partial(_ffn_up_kernel, n_chunk),
        grid=(s // tm,),
        in_specs=[pl.BlockSpec((tm, d), row), pl.BlockSpec((d, f), whole), pl.BlockSpec((d, f), whole)],
        out_specs=pl.BlockSpec((tm, f), row),
        out_shape=jax.ShapeDtypeStruct((s, f), BF16),
        name="ffn_up",
        compiler_params=_cparams("parallel"),
    )(x, wg, wu)


def _nsa_mixer(x, tabs, w_in, ck_pos, ck_w1, ck_w2, cv_pos, cv_w1, cv_w2):
    s = x.shape[0]
    g_n, r_n, hd = NSA_KV_HEADS, NSA_GROUP, HEAD_DIM
    q_dim, kv_dim = N_HEADS * hd, NSA_KV_HEADS * hd
    n_sel = s // SEL_BLOCK
    n_var = -(-n_sel // BIAS_BLOCKS)
    assert s % TK == 0 and (BIAS_BLOCKS * SEL_BLOCK) % TK == 0
    wq = w_in[:, :q_dim].astype(BF16)
    wkv = w_in[:, q_dim:q_dim + 6 * kv_dim].astype(BF16)
    wg = w_in[:, q_dim + 6 * kv_dim:].reshape(-1, g_n, 3 * r_n)
    wg = jnp.pad(wg, ((0, 0), (0, 0), (0, LANES - 3 * r_n))).reshape(-1, g_n * LANES).astype(BF16)
    q, cmp_kv, kaug, vaug, kwin, vwin, gates = _nsa_proj(x, tabs, wq, wkv, wg)
    comp = _compress(cmp_kv, jnp.stack([ck_pos, cv_pos]),
                     jnp.stack([ck_w1, cv_w1]).astype(BF16), jnp.stack([ck_w2, cv_w2]).astype(BF16))
    kc = comp[0].astype(BF16)
    ones = jnp.ones(comp.shape[1:3] + (1,), F32)
    zeros = jnp.zeros(comp.shape[1:3] + (LANES - hd - 1,), F32)
    vc = jnp.concatenate([comp[1], ones, zeros], axis=-1).astype(BF16)
    pre, qaug = _nsa_local(q, kc, vc, kwin, vwin, gates, n_sel, n_var)
    return _nsa_sel(qaug, kaug, vaug, pre, gates)


def _moba_mixer(x, tabs, w_in):
    qaug, kaug, vaug = _moba_proj(x, tabs, w_in.astype(BF16))
    return _moba_attn(qaug, kaug, vaug)


def kernel(x, positions, nsa_w_in, nsa_w_o, nsa_ck_pos, nsa_ck_w1, nsa_ck_w2, nsa_cv_pos, nsa_cv_w1, nsa_cv_w2, moba_w_in, moba_w_o, ffn_wg, ffn_wu, ffn_wd, ln1_g, ln1_b, ln2_g, ln2_b):
    b, s, d = x.shape
    depth = ffn_wg.shape[0]
    alpha = (2 * depth) ** 0.25
    outs = []
    for bi in range(b):
        h = x[bi]
        tabs = _rope_tables(positions[bi])
        for i in range(depth):
            j = i // 2
            if i % 2 == 0:
                a = _nsa_mixer(h, tabs, nsa_w_in[j], nsa_ck_pos[j], nsa_ck_w1[j], nsa_ck_w2[j],
                               nsa_cv_pos[j], nsa_cv_w1[j], nsa_cv_w2[j])
                w_o = nsa_w_o[j]
            else:
                a = _moba_mixer(h, tabs, moba_w_in[j])
                w_o = moba_w_o[j]
            h = _mm_res_ln(a, w_o.astype(BF16), h, ln1_g[i], ln1_b[i], alpha)
            u = _ffn_up(h, ffn_wg[i].astype(BF16), ffn_wu[i].astype(BF16))
            h = _mm_res_ln(u, ffn_wd[i].astype(BF16), h, ln2_g[i], ln2_b[i], alpha)
        outs.append(h)
    return jnp.stack(outs)
```

```python
import functools

import numpy as np
import jax
import jax.numpy as jnp
from jax import lax
from jax.experimental import pallas as pl
from jax.experimental.pallas import tpu as pltpu

F32 = jnp.float32
BF16 = jnp.bfloat16

N_HEADS = 16
HEAD_DIM = 64
ROPE_DIM = HEAD_DIM // 4
ROPE_THETA = 500000.0
NSA_KV_HEADS = 4
NSA_GROUP = N_HEADS // NSA_KV_HEADS
CMP_BLOCK = 32
CMP_STRIDE = 16
SEL_BLOCK = 64
SEL_TOPK = 16
WINDOW = 512
MOBA_BLOCK = 256
MOBA_TOPK = 3
LN_EPS = 1e-5
LARGE = 1e30
SCALE = HEAD_DIM ** -0.5 * float(np.log2(np.e))

LANES = 128
SUBLANES = 8
VMEM_LIMIT_BYTES = 48 * 1024 * 1024

ROW_TILE = 256
MM_TILE = 512
TQ = 256
TK = 1024
MOBA_TQ = 1024
MOBA_TK = 512
QK_AHEAD = 1
SEL_PREFIXES = 4
BIAS_BLOCKS = 128
MASKED = -1e30
M_INIT = -5e29

assert WINDOW % TQ == 0 and TK % TQ == 0 and ROW_TILE == MOBA_BLOCK


def _cparams(*sem, flags=None):
    return pltpu.CompilerParams(dimension_semantics=sem, vmem_limit_bytes=VMEM_LIMIT_BYTES, flags=flags)


def _dot(a, b):
    return jnp.dot(a, b, preferred_element_type=F32)


def _dot_t(a, b, precision=None):
    return lax.dot_general(a, b, (((1,), (1,)), ((), ())), preferred_element_type=F32,
                           precision=precision)


def _rope_tables_kernel(pos_ref, c_ref, cos_ref, sa_ref, sb_ref):
    ang = pos_ref[...].astype(F32) * c_ref[0:1, :]
    c = jnp.cos(ang)
    s = jnp.sin(ang)
    cos_ref[...] = jnp.where(c_ref[1:2, :] > 0, c, 1.0)
    sa_ref[...] = jnp.where(c_ref[2:3, :] > 0, s, 0.0)
    sb_ref[...] = jnp.where(c_ref[3:4, :] > 0, -s, 0.0)


def _rope_tables(positions):
    s = positions.shape[0]
    half = ROPE_DIM // 2
    inv = ROPE_THETA ** (-jnp.arange(0, ROPE_DIM, 2, dtype=F32) / ROPE_DIM)
    d = np.arange(LANES) % HEAD_DIM
    consts = jnp.zeros((8, LANES), F32)
    consts = consts.at[0].set(inv[d % half])
    consts = consts.at[1].set(jnp.asarray(d < ROPE_DIM, F32))
    consts = consts.at[2].set(jnp.asarray((d >= half) & (d < ROPE_DIM), F32))
    consts = consts.at[3].set(jnp.asarray(d < half, F32))
    tm = min(MM_TILE, s)
    tab = jax.ShapeDtypeStruct((s, LANES), F32)
    return pl.pallas_call(
        _rope_tables_kernel,
        grid=(s // tm,),
        in_specs=[pl.BlockSpec((tm, 1), lambda i: (i, 0)),
                  pl.BlockSpec((8, LANES), lambda i: (0, 0))],
        out_specs=[pl.BlockSpec((tm, LANES), lambda i: (i, 0))] * 3,
        out_shape=[tab, tab, tab],
        name="rope_tables",
        compiler_params=_cparams("parallel"),
    )(positions.reshape(s, 1), consts)


def _rope(x, cos_t, sin_a, sin_b):
    half = ROPE_DIM // 2
    outs = []
    for c in range(x.shape[1] // LANES):
        xs = x[:, c * LANES:(c + 1) * LANES]
        outs.append(xs * cos_t + pltpu.roll(xs, half, 1) * sin_a
                    + pltpu.roll(xs, LANES - half, 1) * sin_b)
    return outs[0] if len(outs) == 1 else jnp.concatenate(outs, axis=1)


def _ones_col(rows, width):
    return jnp.where(lax.broadcasted_iota(jnp.int32, (rows, width), 1) == 0, 1.0, 0.0).astype(BF16)


def _nsa_proj_kernel(x_ref, cos_ref, sa_ref, sb_ref, wq_ref, wkv_ref, wg_ref,
                     q_ref, cmp_ref, kaug_ref, vaug_ref, kwin_ref, vwin_ref, gate_ref):
    tm = x_ref.shape[0]
    g_n, hd, kvd = NSA_KV_HEADS, HEAD_DIM, NSA_KV_HEADS * HEAD_DIM
    xb = x_ref[...].astype(BF16)
    cos_t, sin_a, sin_b = cos_ref[...], sa_ref[...], sb_ref[...]
    q = _rope(_dot(xb, wq_ref[...]), cos_t, sin_a, sin_b) * SCALE
    q_ref[...] = q.astype(BF16)
    kv = _dot(xb, wkv_ref[...])
    k_cmp = _rope(kv[:, 0:kvd], cos_t, sin_a, sin_b)
    v_cmp = kv[:, kvd:2 * kvd]
    k_sel = _rope(kv[:, 2 * kvd:3 * kvd], cos_t, sin_a, sin_b)
    v_sel = kv[:, 3 * kvd:4 * kvd]
    k_win = _rope(kv[:, 4 * kvd:5 * kvd], cos_t, sin_a, sin_b)
    v_win = kv[:, 5 * kvd:6 * kvd]
    gate = jax.nn.sigmoid(_dot(xb, wg_ref[...]))
    tok = pl.program_id(0) * tm + lax.broadcasted_iota(jnp.int32, (tm, BIAS_BLOCKS), 0)
    blk = (tok // SEL_BLOCK) % BIAS_BLOCKS
    onehot = jnp.where(lax.broadcasted_iota(jnp.int32, (tm, BIAS_BLOCKS), 1) == blk,
                       1.0, 0.0).astype(BF16)
    ones_col = _ones_col(tm, LANES - hd)
    for g in range(g_n):
        sl = slice(g * hd, (g + 1) * hd)
        cmp_ref[0, g] = k_cmp[:, sl]
        cmp_ref[1, g] = v_cmp[:, sl]
        kwin_ref[g] = k_win[:, sl].astype(BF16)
        vwin_ref[g, :, 0:hd] = v_win[:, sl].astype(BF16)
        vwin_ref[g, :, hd:LANES] = ones_col
        gate_ref[g] = gate[:, g * LANES:(g + 1) * LANES]
        kb = g * 2 * LANES
        kaug_ref[:, kb:kb + BIAS_BLOCKS] = onehot
        kaug_ref[:, kb + BIAS_BLOCKS:kb + BIAS_BLOCKS + hd] = k_sel[:, sl].astype(BF16)
        kaug_ref[:, kb + BIAS_BLOCKS + hd:kb + 2 * LANES] = jnp.zeros((tm, 2 * LANES - BIAS_BLOCKS - hd), BF16)
        vb = g * LANES
        vaug_ref[:, vb:vb + hd] = v_sel[:, sl].astype(BF16)
        vaug_ref[:, vb + hd:vb + LANES] = ones_col


def _nsa_proj(x, tabs, wq, wkv, wg):
    s, d = x.shape
    g_n, hd = NSA_KV_HEADS, HEAD_DIM
    tm = min(ROW_TILE, s)
    row = lambda i: (i, 0)
    whole = lambda i: (0, 0)
    out_shape = [
        jax.ShapeDtypeStruct((s, N_HEADS * hd), BF16),
        jax.ShapeDtypeStruct((2, g_n, s, hd), F32),
        jax.ShapeDtypeStruct((s, g_n * 2 * LANES), BF16),
        jax.ShapeDtypeStruct((s, g_n * LANES), BF16),
        jax.ShapeDtypeStruct((g_n, s, hd), BF16),
        jax.ShapeDtypeStruct((g_n, s, LANES), BF16),
        jax.ShapeDtypeStruct((g_n, s, LANES), F32),
    ]
    out_specs = [
        pl.BlockSpec((tm, N_HEADS * hd), row),
        pl.BlockSpec((2, g_n, tm, hd), lambda i: (0, 0, i, 0)),
        pl.BlockSpec((tm, g_n * 2 * LANES), row),
        pl.BlockSpec((tm, g_n * LANES), row),
        pl.BlockSpec((g_n, tm, hd), lambda i: (0, i, 0)),
        pl.BlockSpec((g_n, tm, LANES), lambda i: (0, i, 0)),
        pl.BlockSpec((g_n, tm, LANES), lambda i: (0, i, 0)),
    ]
    return pl.pallas_call(
        _nsa_proj_kernel,
        grid=(s // tm,),
        in_specs=[pl.BlockSpec((tm, d), row)] + [pl.BlockSpec((tm, LANES), row)] * 3
        + [pl.BlockSpec(wq.shape, whole), pl.BlockSpec(wkv.shape, whole),
           pl.BlockSpec(wg.shape, whole)],
        out_specs=out_specs,
        out_shape=out_shape,
        name="nsa_proj",
        compiler_params=_cparams("parallel"),
    )(x, *tabs, wq, wkv, wg)


def _compress_kernel(x_ref, pos_ref, w1_ref, w2_ref, out_ref):
    n16 = x_ref.shape[2]
    half = w1_ref.shape[1] // 2
    x = x_ref[0, 0]
    a = _dot((x + pos_ref[0, 0:1, :]).astype(BF16), w1_ref[0, 0:half, :])
    b = _dot((x + pos_ref[0, 1:2, :]).astype(BF16), w1_ref[0, half:, :])
    hid = a + pltpu.roll(b, n16 - 1, 0)
    out_ref[0, 0] = _dot(jax.nn.gelu(hid).astype(BF16), w2_ref[0])


def _compress(cmp_kv, pos, w1, w2):
    _, g_n, s, hd = cmp_kv.shape
    n16 = s // CMP_STRIDE
    x = cmp_kv.reshape(2, g_n, n16, CMP_STRIDE * hd)
    pos2 = pos.reshape(2, 2, CMP_STRIDE * hd)
    return pl.pallas_call(
        _compress_kernel,
        grid=(2, g_n),
        in_specs=[pl.BlockSpec((1, 1, n16, CMP_STRIDE * hd), lambda a, g: (a, g, 0, 0)),
                  pl.BlockSpec((1, 2, CMP_STRIDE * hd), lambda a, g: (a, 0, 0)),
                  pl.BlockSpec((1,) + w1.shape[1:], lambda a, g: (a, 0, 0)),
                  pl.BlockSpec((1,) + w2.shape[1:], lambda a, g: (a, 0, 0))],
        out_specs=pl.BlockSpec((1, 1, n16, hd), lambda a, g: (a, g, 0, 0)),
        out_shape=jax.ShapeDtypeStruct((2, g_n, n16, hd), F32),
        name="nsa_compress",
        compiler_params=_cparams("parallel", "parallel"),
    )(x, pos2, w1, w2)


def _top_k_axis0(score, n_rounds, sel, enable=None):
    n = score.shape[0]
    row = lax.broadcasted_iota(jnp.int32, score.shape, 0).astype(F32)
    for _ in range(n_rounds):
        m = jnp.max(score, axis=0, keepdims=True)
        idx = jnp.min(jnp.where(score == m, row, float(n)), axis=0, keepdims=True)
        ok = m > -jnp.inf if enable is None else (m > -jnp.inf) & enable
        hit = row == jnp.where(ok, idx, -1.0)
        sel = jnp.where(hit, 1.0, sel)
        score = jnp.where(hit, -jnp.inf, score)
    return sel, score


def _nsa_local_kernel(n_sel, n_top, q_ref, kc_ref, vc_ref, ov_ref, wmask_ref, *rest):
    n_wb = WINDOW // TQ + 1
    kw_refs = rest[:n_wb]
    vw_refs = rest[n_wb:2 * n_wb]
    gate_ref, pre_ref, qaug_ref, s_scr, m_scr, o_acc, imp_acc = rest[2 * n_wb:]
    r_n, hd = NSA_GROUP, HEAD_DIM
    qi = pl.program_id(1)
    qg = q_ref[...]
    qs = jnp.concatenate([qg[:, r * hd:(r + 1) * hd] for r in range(r_n)], axis=0)
    rows = r_n * TQ

    nc = kc_ref.shape[1]
    ct = min(2 * LANES, nc)
    t_max = ((qi * TQ + TQ - CMP_BLOCK) // CMP_STRIDE) // ct
    t_full = (jnp.maximum(qi * TQ - (CMP_BLOCK - 1) + CMP_STRIDE, 0) // CMP_STRIDE) // ct

    def lane_fold_max(s):
        m = s[:, 0:LANES]
        for c in range(1, s.shape[1] // LANES):
            m = jnp.maximum(m, s[:, c * LANES:(c + 1) * LANES])
        return m

    m_scr[...] = jnp.full(m_scr.shape, M_INIT, F32)
    for t in range(nc // ct):
        tile = slice(t * ct, (t + 1) * ct)

        @pl.when(t < t_full)
        def _():
            s = _dot_t(qs, kc_ref[0, tile, :])
            s_scr[:, tile] = s
            m_scr[...] = jnp.maximum(m_scr[...], lane_fold_max(s))

        @pl.when((t >= t_full) & (t <= t_max))
        def _():
            q_off = lax.broadcasted_iota(jnp.int32, (rows, ct), 0) % TQ
            tok = lax.broadcasted_iota(jnp.int32, (rows, ct), 1)
            visible = q_off - CMP_STRIDE * tok >= CMP_STRIDE * ct * t + CMP_BLOCK - 1 - qi * TQ
            s = jnp.where(visible, _dot_t(qs, kc_ref[0, tile, :]), MASKED)
            s_scr[:, tile] = s
            m_scr[...] = jnp.maximum(m_scr[...], lane_fold_max(s))

    m_scr[...] = jnp.broadcast_to(jnp.max(m_scr[...], axis=1, keepdims=True), m_scr.shape)
    o_acc[...] = jnp.zeros(o_acc.shape, F32)
    imp_acc[...] = jnp.zeros(imp_acc.shape, F32)
    for t in range(nc // ct):
        tile = slice(t * ct, (t + 1) * ct)

        @pl.when(t <= t_max)
        def _():
            m = m_scr[...]
            p = jnp.concatenate(
                [jnp.exp2(s_scr[:, t * ct + c * LANES:t * ct + (c + 1) * LANES] - m).astype(BF16)
                 for c in range(ct // LANES)], axis=1)
            o_acc[...] += _dot(p, vc_ref[0, tile, :])
            imp_acc[...] += _dot_t(ov_ref[:, tile], p)

    n_prefix = SEL_PREFIXES if n_sel % (SEL_PREFIXES * SUBLANES) == 0 else 1
    prefix = ((qi + 1) * (TQ // SEL_BLOCK) - 1) * n_prefix // n_sel
    for v in range(n_prefix):
        @pl.when(prefix == v)
        def _():
            _nsa_local_tail(n_sel * (v + 1) // n_prefix, v == 0, n_sel, n_top, qi, qg, qs,
                            o_acc, imp_acc, wmask_ref, kw_refs, vw_refs, gate_ref, pre_ref, qaug_ref)


def _nsa_local_tail(n_rows, first_tile, n_sel, n_top, qi, qg, qs, o_acc, imp_acc, wmask_ref,
                    kw_refs, vw_refs, gate_ref, pre_ref, qaug_ref):
    r_n, hd = NSA_GROUP, HEAD_DIM
    rows = r_n * TQ
    o_aug = o_acc[...]
    l_col = o_aug[:, hd:hd + 1]
    o_cmp = o_aug[:, 0:hd] * jnp.where(l_col > 0, 1.0 / l_col, 0.0)

    l_row = imp_acc[n_sel:n_sel + 1, :]
    imp_t = imp_acc[0:n_rows, :] * jnp.where(l_row > 0, 1.0 / l_row, 0.0)
    imp = imp_t[:, 0:TQ]
    for r in range(1, r_n):
        imp = imp + imp_t[:, r * TQ:(r + 1) * TQ]
    blk = lax.broadcasted_iota(jnp.int32, (n_rows, TQ), 0)
    blk_q = (qi * TQ + lax.broadcasted_iota(jnp.int32, (n_rows, TQ), 1)) // SEL_BLOCK
    forced = (blk == 0) | (blk == blk_q) | (blk == blk_q - 1)
    score = jnp.where((blk <= blk_q) & jnp.logical_not(forced), imp, -jnp.inf)
    sel = jnp.where(forced, 1.0, 0.0)
    n_forced_min = 3
    blk_q_row = blk_q[0:1, :]
    sel_tiles = []
    for c in range(TQ // LANES):
        ls = slice(c * LANES, (c + 1) * LANES)
        sel_c, score_c = _top_k_axis0(score[:, ls], n_top - n_forced_min, sel[:, ls])
        if first_tile:
            for extra in range(1, n_forced_min):
                sel_c, score_c = _top_k_axis0(score_c, 1, sel_c,
                                              enable=blk_q_row[:, ls] < n_forced_min - extra)
        sel_tiles.append(sel_c)
    sel = jnp.concatenate(sel_tiles, axis=1)
    n_pad = -(-n_rows // LANES) * LANES
    if n_pad > n_rows:
        sel = jnp.concatenate([sel, jnp.zeros((n_pad - n_rows, TQ), F32)], axis=0)
    bias = jnp.where(sel > 0, 0.0, MASKED).T.astype(BF16)
    n_var = qaug_ref.shape[0]
    if n_pad < n_var * BIAS_BLOCKS:
        bias = jnp.concatenate(
            [bias, jnp.full((TQ, n_var * BIAS_BLOCKS - n_pad), MASKED, BF16)], axis=1)
    zeros = jnp.zeros((TQ, 2 * LANES - BIAS_BLOCKS - hd), BF16)
    for v in range(n_var):
        for r in range(r_n):
            rs = slice(r * TQ, (r + 1) * TQ)
            qaug_ref[v, 0, 0, rs, 0:BIAS_BLOCKS] = bias[:, v * BIAS_BLOCKS:(v + 1) * BIAS_BLOCKS]
            qaug_ref[v, 0, 0, rs, BIAS_BLOCKS:BIAS_BLOCKS + hd] = qg[:, r * hd:(r + 1) * hd]
            qaug_ref[v, 0, 0, rs, BIAS_BLOCKS + hd:2 * LANES] = zeros

    kcat = jnp.concatenate([kw[0] for kw in kw_refs], axis=0)
    vcat = jnp.concatenate([vw[0] for vw in vw_refs], axis=0)
    s = _dot_t(qs, kcat)
    wmask = wmask_ref[0]
    s = jnp.concatenate([s[r * TQ:(r + 1) * TQ] + wmask for r in range(r_n)], axis=0)
    m = jnp.broadcast_to(jnp.max(s, axis=1, keepdims=True), (rows, LANES))
    p = jnp.concatenate([jnp.exp2(s[:, c * LANES:(c + 1) * LANES] - m).astype(BF16)
                         for c in range(s.shape[1] // LANES)], axis=1)
    o_aug = _dot(p, vcat)
    o_win = o_aug[:, 0:hd] * (1.0 / o_aug[:, hd:hd + 1])

    gate = gate_ref[0]
    for r in range(r_n):
        rs = slice(r * TQ, (r + 1) * TQ)
        pre_ref[:, r * hd:(r + 1) * hd] = (gate[:, 3 * r:3 * r + 1] * o_cmp[rs]
                                          + gate[:, 3 * r + 2:3 * r + 3] * o_win[rs])


def _overlap_matrix(nc, n_sel):
    i = np.arange(nc)[None, :]
    j = np.arange(n_sel)[:, None]
    lo = np.maximum(i * CMP_STRIDE, j * SEL_BLOCK)
    hi = np.minimum(i * CMP_STRIDE + CMP_BLOCK - 1, j * SEL_BLOCK + SEL_BLOCK - 1)
    ov = np.zeros((n_sel + SUBLANES, nc), np.float32)
    ov[:n_sel] = np.clip(hi - lo + 1, 0, None) / CMP_STRIDE
    ov[n_sel] = 1.0
    return jnp.asarray(ov, BF16)


def _window_masks(n_wb):
    q = np.arange(TQ)[:, None]
    c = np.arange(n_wb * TQ)[None, :]
    dist = q + (n_wb - 1) * TQ - c
    masks = []
    for v in range(n_wb):
        kpos = (v - (n_wb - 1)) * TQ + c
        masks.append(np.where((kpos >= 0) & (dist >= 0) & (dist < WINDOW), 0.0, MASKED))
    return jnp.asarray(np.stack(masks), F32)


def _nsa_local(q, kc, vc, kwin, vwin, gates, n_sel, n_var):
    s = q.shape[0]
    g_n, r_n, hd = NSA_KV_HEADS, NSA_GROUP, HEAD_DIM
    n_q = s // TQ
    n_wb = WINDOW // TQ + 1
    n_top = min(SEL_TOPK, n_sel)
    nc = kc.shape[1]
    rows = r_n * TQ
    assert n_top >= 3 and nc % min(2 * LANES, nc) == 0

    def win_spec(j, width):
        return pl.BlockSpec((1, TQ, width), lambda g, i: (g, jnp.maximum(i - (n_wb - 1) + j, 0), 0))

    return pl.pallas_call(
        functools.partial(_nsa_local_kernel, n_sel, n_top),
        grid=(g_n, n_q),
        in_specs=[pl.BlockSpec((TQ, r_n * hd), lambda g, i: (i, g)),
                  pl.BlockSpec((1, nc, hd), lambda g, i: (g, 0, 0)),
                  pl.BlockSpec((1, nc, LANES), lambda g, i: (g, 0, 0)),
                  pl.BlockSpec((n_sel + SUBLANES, nc), lambda g, i: (0, 0)),
                  pl.BlockSpec((1, TQ, n_wb * TQ), lambda g, i: (jnp.minimum(i, n_wb - 1), 0, 0))]
        + [win_spec(j, hd) for j in range(n_wb)] + [win_spec(j, LANES) for j in range(n_wb)]
        + [pl.BlockSpec((1, TQ, LANES), lambda g, i: (g, i, 0))],
        out_specs=[pl.BlockSpec((TQ, r_n * hd), lambda g, i: (i, g)),
                   pl.BlockSpec((n_var, 1, 1, rows, 2 * LANES), lambda g, i: (0, g, i, 0, 0))],
        out_shape=[jax.ShapeDtypeStruct((s, N_HEADS * hd), F32),
                   jax.ShapeDtypeStruct((n_var, g_n, n_q, rows, 2 * LANES), BF16)],
        scratch_shapes=[pltpu.VMEM((rows, nc), F32), pltpu.VMEM((rows, LANES), F32),
                        pltpu.VMEM((rows, LANES), F32), pltpu.VMEM((n_sel + SUBLANES, rows), F32)],
        name="nsa_local",
        compiler_params=_cparams("parallel", "parallel"),
    )(q, kc, vc, _overlap_matrix(nc, n_sel), _window_masks(n_wb),
      *([kwin] * n_wb), *([vwin] * n_wb), gates)


def _causal_pairs(n_q, tq, tk):
    qs, ks = [], []
    for qi in range(n_q):
        for ki in range(((qi + 1) * tq - 1) // tk + 1):
            qs.append(qi)
            ks.append(ki)
    return jnp.asarray(qs, jnp.int32), jnp.asarray(ks, jnp.int32)


def _flash_scores(q, k, causal_mask):
    s = _dot_t(q, k)
    return s if causal_mask is None else jnp.where(causal_mask, s, MASKED)


def _flash_accumulate(s, v, m_ref, acc_ref, idx, row0):
    m_prev = m_ref[idx, row0:, :]
    m_new = jnp.maximum(m_prev, jnp.max(s, axis=1, keepdims=True))
    p = jnp.concatenate([jnp.exp2(s[:, c * LANES:(c + 1) * LANES] - m_new).astype(BF16)
                         for c in range(s.shape[1] // LANES)], axis=1)
    acc_ref[idx, row0:, :] = jnp.exp2(m_prev - m_new) * acc_ref[idx, row0:, :] + _dot(p, v)
    m_ref[idx, row0:, :] = m_new


def _flash_steps(qi, ki, tq, tk, rows_per_q, n_batch, operands, m_ref, acc_ref, finalize):
    first_diag = (qi * tq) // tk
    last_k = ((qi + 1) * tq - 1) // tk
    assert rows_per_q == 1 or tq <= tk

    @pl.when(ki == 0)
    def _():
        m_ref[...] = jnp.full(m_ref.shape, M_INIT, F32)
        acc_ref[...] = jnp.zeros(acc_ref.shape, F32)

    def run(masked, row0, n_keys):
        mask = None
        if masked:
            shape = (rows_per_q * tq - row0, n_keys)
            q_off = lax.broadcasted_iota(jnp.int32, shape, 0) % tq + row0
            k_off = lax.broadcasted_iota(jnp.int32, shape, 1)
            mask = q_off - k_off >= ki * tk - qi * tq
        def scores(b):
            q, k, _ = operands(b, row0, n_keys)
            return _flash_scores(q, k, mask)

        ahead = [scores(b) for b in range(min(QK_AHEAD, n_batch))]
        for b in range(n_batch):
            s = ahead.pop(0)
            if b + QK_AHEAD < n_batch:
                ahead.append(scores(b + QK_AHEAD))
            _flash_accumulate(s, operands(b, row0, n_keys)[2], m_ref, acc_ref, b, row0)

    @pl.when(ki < first_diag)
    def _():
        run(False, 0, tk)

    if tq >= tk:
        variants = [(ki - first_diag == j, j * tk, tk) for j in range(tq // tk)]
    else:
        variants = [(qi % (tk // tq) == j, 0, (j + 1) * tq) for j in range(tk // tq)]
    for cond, row0, n_keys in variants:
        @pl.when((ki >= first_diag) & cond)
        def _():
            run(True, row0, n_keys)

    @pl.when(ki == last_k)
    def _():
        finalize()


def _nsa_sel_kernel(qtab, ktab, qaug_ref, kaug_ref, vaug_ref, pre_ref, gate_ref, o_ref,
                    m_ref, acc_ref):
    g_n, r_n, hd = NSA_KV_HEADS, NSA_GROUP, HEAD_DIM
    t = pl.program_id(0)
    qi, ki = qtab[t], ktab[t]

    def operands(g, row0, n_keys):
        return (qaug_ref[0, g, 0, row0:, :], kaug_ref[0:n_keys, g * 2 * LANES:(g + 1) * 2 * LANES],
                vaug_ref[0:n_keys, g * LANES:(g + 1) * LANES])

    def finalize():
        for g in range(g_n):
            acc = acc_ref[g]
            gate = gate_ref[g]
            o = acc[:, 0:hd] * (1.0 / acc[:, hd:hd + 1])
            for r in range(r_n):
                c = (g * r_n + r) * hd
                o_ref[:, c:c + hd] = (pre_ref[:, c:c + hd] + gate[:, 3 * r + 1:3 * r + 2]
                                      * o[r * TQ:(r + 1) * TQ]).astype(o_ref.dtype)

    _flash_steps(qi, ki, TQ, TK, r_n, g_n, operands, m_ref, acc_ref, finalize)


def _nsa_sel(qaug, kaug, vaug, pre, gates):
    n_var, g_n, n_q, rows, _ = qaug.shape
    s = kaug.shape[0]
    qtab, ktab = _causal_pairs(n_q, TQ, TK)
    tiles_per_var = BIAS_BLOCKS * SEL_BLOCK // TK
    grid_spec = pltpu.PrefetchScalarGridSpec(
        num_scalar_prefetch=2,
        grid=(qtab.shape[0],),
        in_specs=[
            pl.BlockSpec((1, g_n, 1, rows, 2 * LANES),
                         lambda t, qt, kt: (kt[t] // tiles_per_var, 0, qt[t], 0, 0)),
            pl.BlockSpec((TK, g_n * 2 * LANES), lambda t, qt, kt: (kt[t], 0)),
            pl.BlockSpec((TK, g_n * LANES), lambda t, qt, kt: (kt[t], 0)),
            pl.BlockSpec((TQ, N_HEADS * HEAD_DIM), lambda t, qt, kt: (qt[t], 0)),
            pl.BlockSpec((g_n, TQ, LANES), lambda t, qt, kt: (0, qt[t], 0)),
        ],
        out_specs=pl.BlockSpec((TQ, N_HEADS * HEAD_DIM), lambda t, qt, kt: (qt[t], 0)),
        scratch_shapes=[pltpu.VMEM((g_n, rows, LANES), F32), pltpu.VMEM((g_n, rows, LANES), F32)],
    )
    return pl.pallas_call(
        _nsa_sel_kernel,
        grid_spec=grid_spec,
        out_shape=jax.ShapeDtypeStruct((s, N_HEADS * HEAD_DIM), BF16),
        name="nsa_sel",
        compiler_params=_cparams("arbitrary"),
    )(qtab, ktab, qaug, kaug, vaug, pre, gates)


def _moba_proj_kernel(n_blk, k_eff, x_ref, cos_ref, sa_ref, sb_ref, w_ref,
                      qaug_ref, kaug_ref, vaug_ref, kmean_ref):
    tm = x_ref.shape[0]
    hd, d = HEAD_DIM, N_HEADS * HEAD_DIM
    i = pl.program_id(0)

    @pl.when(i == 0)
    def _():
        kmean_ref[...] = jnp.zeros(kmean_ref.shape, F32)

    xb = x_ref[...].astype(BF16)
    cos_t, sin_a, sin_b = cos_ref[...], sa_ref[...], sb_ref[...]
    q = _rope(_dot(xb, w_ref[:, 0:d]), cos_t, sin_a, sin_b)
    k = _rope(_dot(xb, w_ref[:, d:2 * d]), cos_t, sin_a, sin_b)
    v = _dot(xb, w_ref[:, 2 * d:3 * d])
    kmean = kmean_ref[...]
    nb = kmean.shape[0]
    own = jnp.where(lax.broadcasted_iota(jnp.int32, (1, nb), 1) == i, 1.0, 0.0).astype(BF16)
    own_onehot = jnp.broadcast_to(own, (tm, nb))
    ones_col = _ones_col(tm, LANES - hd)
    blk_t = lax.broadcasted_iota(jnp.int32, (nb, tm), 0)
    own_t = jnp.where(blk_t == i, 1.0, 0.0)
    for hp in range(N_HEADS // 2):
        bias_t = []
        for h in (2 * hp, 2 * hp + 1):
            sl = slice(h * hd, (h + 1) * hd)
            gate_t = _dot_t(kmean[:, sl], q[:, sl], precision=lax.Precision.HIGHEST)
            gate_t = jnp.where(blk_t < i, gate_t, -jnp.inf)
            bias_t.append(jnp.where(_top_k_axis0(gate_t, k_eff, own_t)[0] > 0, 0.0, MASKED))
        bias = jnp.concatenate(bias_t, axis=0).T.astype(BF16)
        for j, h in enumerate((2 * hp, 2 * hp + 1)):
            sl = slice(h * hd, (h + 1) * hd)
            qaug_ref[h, :, 0:hd] = (q[:, sl] * SCALE).astype(BF16)
            qaug_ref[h, :, hd:hd + nb] = bias[:, j * nb:(j + 1) * nb]
            kaug_ref[h, :, 0:hd] = k[:, sl].astype(BF16)
            kaug_ref[h, :, hd:hd + nb] = own_onehot
            vaug_ref[h, :, 0:hd] = v[:, sl].astype(BF16)
            vaug_ref[h, :, hd:LANES] = ones_col
    kmean_ref[pl.ds(i, 1), :] = jnp.mean(k, axis=0, keepdims=True)


def _moba_proj(x, tabs, w):
    s, d = x.shape
    n_blk = s // MOBA_BLOCK
    k_eff = min(MOBA_TOPK, max(n_blk - 1, 1))
    nb = LANES - HEAD_DIM
    assert s % MOBA_BLOCK == 0 and n_blk <= nb
    tm = MOBA_BLOCK
    row = lambda i: (i, 0)
    head = jax.ShapeDtypeStruct((N_HEADS, s, LANES), BF16)
    head_spec = pl.BlockSpec((N_HEADS, tm, LANES), lambda i: (0, i, 0))
    return pl.pallas_call(
        functools.partial(_moba_proj_kernel, n_blk, k_eff),
        grid=(s // tm,),
        in_specs=[pl.BlockSpec((tm, d), row)] + [pl.BlockSpec((tm, LANES), row)] * 3
        + [pl.BlockSpec(w.shape, lambda i: (0, 0))],
        out_specs=[head_spec] * 3,
        out_shape=[head] * 3,
        scratch_shapes=[pltpu.VMEM((nb, N_HEADS * HEAD_DIM), F32)],
        name="moba_proj",
        compiler_params=_cparams("arbitrary"),
    )(x, *tabs, w)


def _moba_attn_kernel(qtab, ktab, q_ref, k_ref, v_ref, o_ref, m_ref, acc_ref):
    hd = HEAD_DIM
    t = pl.program_id(0)
    qi, ki = qtab[t], ktab[t]

    def operands(h, row0, n_keys):
        return q_ref[h, row0:, :], k_ref[h, 0:n_keys, :], v_ref[h, 0:n_keys, :]

    def finalize():
        for h in range(N_HEADS):
            acc = acc_ref[h]
            o_ref[:, h * hd:(h + 1) * hd] = (acc[:, 0:hd] * (1.0 / acc[:, hd:hd + 1])).astype(o_ref.dtype)

    _flash_steps(qi, ki, q_ref.shape[1], k_ref.shape[1], 1, N_HEADS, operands, m_ref, acc_ref, finalize)


def _moba_attn(qaug, kaug, vaug):
    h_n, s, _ = qaug.shape
    tq, tk = min(MOBA_TQ, s), min(MOBA_TK, s)
    assert s % tq == 0 and s % tk == 0
    qtab, ktab = _causal_pairs(s // tq, tq, tk)
    grid_spec = pltpu.PrefetchScalarGridSpec(
        num_scalar_prefetch=2,
        grid=(qtab.shape[0],),
        in_specs=[pl.BlockSpec((h_n, tq, LANES), lambda t, qt, kt: (0, qt[t], 0)),
                  pl.BlockSpec((h_n, tk, LANES), lambda t, qt, kt: (0, kt[t], 0)),
                  pl.BlockSpec((h_n, tk, LANES), lambda t, qt, kt: (0, kt[t], 0))],
        out_specs=pl.BlockSpec((tq, h_n * HEAD_DIM), lambda t, qt, kt: (qt[t], 0)),
        scratch_shapes=[pltpu.VMEM((h_n, tq, LANES), F32), pltpu.VMEM((h_n, tq, LANES), F32)],
    )
    return pl.pallas_call(
        _moba_attn_kernel,
        grid_spec=grid_spec,
        out_shape=jax.ShapeDtypeStruct((s, h_n * HEAD_DIM), BF16),
        name="moba_attn",
        compiler_params=_cparams("arbitrary"),
    )(qtab, ktab, qaug, kaug, vaug)


def _mm_res_ln_kernel(alpha, a_ref, w_ref, x_ref, g_ref, b_ref, o_ref):
    z = alpha * x_ref[...] + _dot(a_ref[...], w_ref[...])
    mu = jnp.mean(z, axis=-1, keepdims=True)
    zc = z - mu
    var = jnp.mean(zc * zc, axis=-1, keepdims=True)
    o_ref[...] = zc * lax.rsqrt(var + LN_EPS) * g_ref[...] + b_ref[...]


def _mm_res_ln(a, w, x, gain, bias, alpha):
    s, k = a.shape
    d = w.shape[1]
    tm = min(MM_TILE, s)
    row = lambda i: (i, 0)
    whole = lambda i: (0, 0)
    return pl.pallas_call(
        functools.partial(_mm_res_ln_kernel, alpha),
        grid=(s // tm,),
        in_specs=[pl.BlockSpec((tm, k), row), pl.BlockSpec((k, d), whole),
                  pl.BlockSpec((tm, d), row), pl.BlockSpec((1, d), whole), pl.BlockSpec((1, d), whole)],
        out_specs=pl.BlockSpec((tm, d), row),
        out_shape=jax.ShapeDtypeStruct((s, d), F32),
        name="mm_res_ln",
        compiler_params=_cparams("parallel"),
    )(a, w, x, gain.reshape(1, d), bias.reshape(1, d))


def _ffn_up_kernel(n_chunk, x_ref, wg_ref, wu_ref, o_ref):
    xb = x_ref[...].astype(BF16)
    cw = wg_ref.shape[1] // n_chunk
    for c in range(n_chunk):
        sl = slice(c * cw, (c + 1) * cw)
        o_ref[:, sl] = (jax.nn.silu(_dot(xb, wg_ref[:, sl])) * _dot(xb, wu_ref[:, sl])).astype(o_ref.dtype)


def _ffn_up(x, wg, wu):
    s, d = x.shape
    f = wg.shape[1]
    n_chunk = 2 if f % (2 * LANES) == 0 else 1
    tm = min(MM_TILE, s)
    row = lambda i: (i, 0)
    whole = lambda i: (0, 0)
    return pl.pallas_call(
        functools.partial(_ffn_up_kernel, n_chunk),
        grid=(s // tm,),
        in_specs=[pl.BlockSpec((tm, d), row), pl.BlockSpec((d, f), whole), pl.BlockSpec((d, f), whole)],
        out_specs=pl.BlockSpec((tm, f), row),
        out_shape=jax.ShapeDtypeStruct((s, f), BF16),
        name="ffn_up",
        compiler_params=_cparams("parallel"),
    )(x, wg, wu)


def _layer_norm(z, gain, bias):
    mu = jnp.mean(z, axis=-1, keepdims=True)
    zc = z - mu
    var = jnp.mean(zc * zc, axis=-1, keepdims=True)
    return zc * lax.rsqrt(var + LN_EPS) * gain + bias


def _post_mixer_kernel(alpha, n_chunk, a_ref, x_ref, wo_ref, g1_ref, b1_ref, wg_ref, wu_ref, wd_ref,
                       g2_ref, b2_ref, o_ref):
    x1 = _layer_norm(alpha * x_ref[...] + _dot(a_ref[...], wo_ref[...]), g1_ref[...], b1_ref[...])
    xb = x1.astype(BF16)
    cw = wg_ref.shape[1] // n_chunk
    f = None
    for c in range(n_chunk):
        sl = slice(c * cw, (c + 1) * cw)
        u = (jax.nn.silu(_dot(xb, wg_ref[:, sl])) * _dot(xb, wu_ref[:, sl])).astype(BF16)
        part = _dot(u, wd_ref[sl, :])
        f = part if f is None else f + part
    o_ref[...] = _layer_norm(alpha * x1 + f, g2_ref[...], b2_ref[...])


def _post_mixer(a, x, w_o, g1, b1, wg, wu, wd, g2, b2, alpha):
    s, d = x.shape
    f = wg.shape[1]
    n_chunk = 2 if f % (2 * LANES) == 0 else 1
    tm = min(MM_TILE, s)
    row = lambda i: (i, 0)

    def resident(shape):
        return pl.BlockSpec(shape, lambda i: (0, 0), pipeline_mode=pl.Buffered(1))

    vec = lambda v: v.reshape(1, d)
    return pl.pallas_call(
        functools.partial(_post_mixer_kernel, alpha, n_chunk),
        grid=(s // tm,),
        in_specs=[pl.BlockSpec((tm, a.shape[1]), row), pl.BlockSpec((tm, d), row),
                  resident(w_o.shape), resident((1, d)), resident((1, d)),
                  resident(wg.shape), resident(wu.shape), resident(wd.shape),
                  resident((1, d)), resident((1, d))],
        out_specs=pl.BlockSpec((tm, d), row),
        out_shape=jax.ShapeDtypeStruct((s, d), F32),
        name="post_mixer",
        compiler_params=_cparams("parallel"),
    )(a, x, w_o, vec(g1), vec(b1), wg, wu, wd, vec(g2), vec(b2))


def _nsa_mixer(x, tabs, w_in, ck_pos, ck_w1, ck_w2, cv_pos, cv_w1, cv_w2):
    s = x.shape[0]
    g_n, r_n, hd = NSA_KV_HEADS, NSA_GROUP, HEAD_DIM
    q_dim, kv_dim = N_HEADS * hd, NSA_KV_HEADS * hd
    n_sel = s // SEL_BLOCK
    n_var = -(-n_sel // BIAS_BLOCKS)
    assert s % TK == 0 and (BIAS_BLOCKS * SEL_BLOCK) % TK == 0
    wq = w_in[:, :q_dim].astype(BF16)
    wkv = w_in[:, q_dim:q_dim + 6 * kv_dim].astype(BF16)
    wg = w_in[:, q_dim + 6 * kv_dim:].reshape(-1, g_n, 3 * r_n)
    wg = jnp.pad(wg, ((0, 0), (0, 0), (0, LANES - 3 * r_n))).reshape(-1, g_n * LANES).astype(BF16)
    q, cmp_kv, kaug, vaug, kwin, vwin, gates = _nsa_proj(x, tabs, wq, wkv, wg)
    comp = _compress(cmp_kv, jnp.stack([ck_pos, cv_pos]),
                     jnp.stack([ck_w1, cv_w1]).astype(BF16), jnp.stack([ck_w2, cv_w2]).astype(BF16))
    kc = comp[0].astype(BF16)
    ones = jnp.ones(comp.shape[1:3] + (1,), F32)
    zeros = jnp.zeros(comp.shape[1:3] + (LANES - hd - 1,), F32)
    vc = jnp.concatenate([comp[1], ones, zeros], axis=-1).astype(BF16)
    pre, qaug = _nsa_local(q, kc, vc, kwin, vwin, gates, n_sel, n_var)
    return _nsa_sel(qaug, kaug, vaug, pre, gates)


def _moba_mixer(x, tabs, w_in):
    qaug, kaug, vaug = _moba_proj(x, tabs, w_in.astype(BF16))
    return _moba_attn(qaug, kaug, vaug)


def kernel(x, positions, nsa_w_in, nsa_w_o, nsa_ck_pos, nsa_ck_w1, nsa_ck_w2, nsa_cv_pos, nsa_cv_w1, nsa_cv_w2, moba_w_in, moba_w_o, ffn_wg, ffn_wu, ffn_wd, ln1_g, ln1_b, ln2_g, ln2_b):
    b, s, d = x.shape
    depth = ffn_wg.shape[0]
    alpha = (2 * depth) ** 0.25
    outs = []
    for bi in range(b):
        h = x[bi]
        tabs = _rope_tables(positions[bi])
        for i in range(depth):
            j = i // 2
            if i % 2 == 0:
                a = _nsa_mixer(h, tabs, nsa_w_in[j], nsa_ck_pos[j], nsa_ck_w1[j], nsa_ck_w2[j],
                               nsa_cv_pos[j], nsa_cv_w1[j], nsa_cv_w2[j])
                w_o = nsa_w_o[j]
            else:
                a = _moba_mixer(h, tabs, moba_w_in[j])
                w_o = moba_w_o[j]
            h = _post_mixer(a, h, w_o.astype(BF16), ln1_g[i], ln1_b[i], ffn_wg[i].astype(BF16),
                            ffn_wu[i].astype(BF16), ffn_wd[i].astype(BF16), ln2_g[i], ln2_b[i], alpha)
        outs.append(h)
    return jnp.stack(outs)
```

```python
import functools

import numpy as np
import jax
import jax.numpy as jnp
from jax import lax
from jax.experimental import pallas as pl
from jax.experimental.pallas import tpu as pltpu

F32 = jnp.float32
BF16 = jnp.bfloat16

N_HEADS = 16
HEAD_DIM = 64
ROPE_DIM = HEAD_DIM // 4
ROPE_THETA = 500000.0
NSA_KV_HEADS = 4
NSA_GROUP = N_HEADS // NSA_KV_HEADS
CMP_BLOCK = 32
CMP_STRIDE = 16
SEL_BLOCK = 64
SEL_TOPK = 16
WINDOW = 512
MOBA_BLOCK = 256
MOBA_TOPK = 3
LN_EPS = 1e-5
LARGE = 1e30
SCALE = HEAD_DIM ** -0.5 * float(np.log2(np.e))

LANES = 128
SUBLANES = 8
VMEM_LIMIT_BYTES = 48 * 1024 * 1024

ROW_TILE = 256
MM_TILE = 512
TQ = 256
TK = 1024
MOBA_TQ = 1024
MOBA_TK = 512
QK_AHEAD = 1
SEL_PREFIXES = 4
NSA_LOCAL_GROUPS = 1
BIAS_BLOCKS = 128
MASKED = -1e30
M_INIT = -5e29

assert WINDOW % TQ == 0 and TK % TQ == 0 and ROW_TILE == MOBA_BLOCK


def _cparams(*sem, flags=None):
    return pltpu.CompilerParams(dimension_semantics=sem, vmem_limit_bytes=VMEM_LIMIT_BYTES, flags=flags)


def _dot(a, b):
    return jnp.dot(a, b, preferred_element_type=F32)


def _dot_t(a, b, precision=None):
    return lax.dot_general(a, b, (((1,), (1,)), ((), ())), preferred_element_type=F32,
                           precision=precision)


def _rope_tables_kernel(pos_ref, c_ref, cos_ref, sa_ref, sb_ref):
    ang = pos_ref[...].astype(F32) * c_ref[0:1, :]
    c = jnp.cos(ang)
    s = jnp.sin(ang)
    cos_ref[...] = jnp.where(c_ref[1:2, :] > 0, c, 1.0)
    sa_ref[...] = jnp.where(c_ref[2:3, :] > 0, s, 0.0)
    sb_ref[...] = jnp.where(c_ref[3:4, :] > 0, -s, 0.0)


def _rope_tables(positions):
    s = positions.shape[0]
    half = ROPE_DIM // 2
    inv = ROPE_THETA ** (-jnp.arange(0, ROPE_DIM, 2, dtype=F32) / ROPE_DIM)
    d = np.arange(LANES) % HEAD_DIM
    consts = jnp.zeros((8, LANES), F32)
    consts = consts.at[0].set(inv[d % half])
    consts = consts.at[1].set(jnp.asarray(d < ROPE_DIM, F32))
    consts = consts.at[2].set(jnp.asarray((d >= half) & (d < ROPE_DIM), F32))
    consts = consts.at[3].set(jnp.asarray(d < half, F32))
    tm = min(MM_TILE, s)
    tab = jax.ShapeDtypeStruct((s, LANES), F32)
    return pl.pallas_call(
        _rope_tables_kernel,
        grid=(s // tm,),
        in_specs=[pl.BlockSpec((tm, 1), lambda i: (i, 0)),
                  pl.BlockSpec((8, LANES), lambda i: (0, 0))],
        out_specs=[pl.BlockSpec((tm, LANES), lambda i: (i, 0))] * 3,
        out_shape=[tab, tab, tab],
        name="rope_tables",
        compiler_params=_cparams("parallel"),
    )(positions.reshape(s, 1), consts)


def _rope(x, cos_t, sin_a, sin_b):
    half = ROPE_DIM // 2
    outs = []
    for c in range(x.shape[1] // LANES):
        xs = x[:, c * LANES:(c + 1) * LANES]
        outs.append(xs * cos_t + pltpu.roll(xs, half, 1) * sin_a
                    + pltpu.roll(xs, LANES - half, 1) * sin_b)
    return outs[0] if len(outs) == 1 else jnp.concatenate(outs, axis=1)


def _ones_col(rows, width):
    return jnp.where(lax.broadcasted_iota(jnp.int32, (rows, width), 1) == 0, 1.0, 0.0).astype(BF16)


def _nsa_proj_kernel(x_ref, cos_ref, sa_ref, sb_ref, wq_ref, wkv_ref, wg_ref,
                     q_ref, cmp_ref, kaug_ref, vaug_ref, kwin_ref, vwin_ref, gate_ref):
    tm = x_ref.shape[0]
    g_n, hd, kvd = NSA_KV_HEADS, HEAD_DIM, NSA_KV_HEADS * HEAD_DIM
    xb = x_ref[...].astype(BF16)
    cos_t, sin_a, sin_b = cos_ref[...], sa_ref[...], sb_ref[...]
    q = _rope(_dot(xb, wq_ref[...]), cos_t, sin_a, sin_b) * SCALE
    q_ref[...] = q.astype(BF16)
    kv = _dot(xb, wkv_ref[...])
    k_cmp = _rope(kv[:, 0:kvd], cos_t, sin_a, sin_b)
    v_cmp = kv[:, kvd:2 * kvd]
    k_sel = _rope(kv[:, 2 * kvd:3 * kvd], cos_t, sin_a, sin_b)
    v_sel = kv[:, 3 * kvd:4 * kvd]
    k_win = _rope(kv[:, 4 * kvd:5 * kvd], cos_t, sin_a, sin_b)
    v_win = kv[:, 5 * kvd:6 * kvd]
    gate = jax.nn.sigmoid(_dot(xb, wg_ref[...]))
    tok = pl.program_id(0) * tm + lax.broadcasted_iota(jnp.int32, (tm, BIAS_BLOCKS), 0)
    blk = (tok // SEL_BLOCK) % BIAS_BLOCKS
    onehot = jnp.where(lax.broadcasted_iota(jnp.int32, (tm, BIAS_BLOCKS), 1) == blk,
                       1.0, 0.0).astype(BF16)
    ones_col = _ones_col(tm, LANES - hd)
    for g in range(g_n):
        sl = slice(g * hd, (g + 1) * hd)
        cmp_ref[0, g] = k_cmp[:, sl]
        cmp_ref[1, g] = v_cmp[:, sl]
        kwin_ref[g] = k_win[:, sl].astype(BF16)
        vwin_ref[g, :, 0:hd] = v_win[:, sl].astype(BF16)
        vwin_ref[g, :, hd:LANES] = ones_col
        gate_ref[g] = gate[:, g * LANES:(g + 1) * LANES]
        kb = g * 2 * LANES
        kaug_ref[:, kb:kb + BIAS_BLOCKS] = onehot
        kaug_ref[:, kb + BIAS_BLOCKS:kb + BIAS_BLOCKS + hd] = k_sel[:, sl].astype(BF16)
        kaug_ref[:, kb + BIAS_BLOCKS + hd:kb + 2 * LANES] = jnp.zeros((tm, 2 * LANES - BIAS_BLOCKS - hd), BF16)
        vb = g * LANES
        vaug_ref[:, vb:vb + hd] = v_sel[:, sl].astype(BF16)
        vaug_ref[:, vb + hd:vb + LANES] = ones_col


def _nsa_proj(x, tabs, wq, wkv, wg):
    s, d = x.shape
    g_n, hd = NSA_KV_HEADS, HEAD_DIM
    tm = min(ROW_TILE, s)
    row = lambda i: (i, 0)
    whole = lambda i: (0, 0)
    out_shape = [
        jax.ShapeDtypeStruct((s, N_HEADS * hd), BF16),
        jax.ShapeDtypeStruct((2, g_n, s, hd), F32),
        jax.ShapeDtypeStruct((s, g_n * 2 * LANES), BF16),
        jax.ShapeDtypeStruct((s, g_n * LANES), BF16),
        jax.ShapeDtypeStruct((g_n, s, hd), BF16),
        jax.ShapeDtypeStruct((g_n, s, LANES), BF16),
        jax.ShapeDtypeStruct((g_n, s, LANES), F32),
    ]
    out_specs = [
        pl.BlockSpec((tm, N_HEADS * hd), row),
        pl.BlockSpec((2, g_n, tm, hd), lambda i: (0, 0, i, 0)),
        pl.BlockSpec((tm, g_n * 2 * LANES), row),
        pl.BlockSpec((tm, g_n * LANES), row),
        pl.BlockSpec((g_n, tm, hd), lambda i: (0, i, 0)),
        pl.BlockSpec((g_n, tm, LANES), lambda i: (0, i, 0)),
        pl.BlockSpec((g_n, tm, LANES), lambda i: (0, i, 0)),
    ]
    return pl.pallas_call(
        _nsa_proj_kernel,
        grid=(s // tm,),
        in_specs=[pl.BlockSpec((tm, d), row)] + [pl.BlockSpec((tm, LANES), row)] * 3
        + [pl.BlockSpec(wq.shape, whole), pl.BlockSpec(wkv.shape, whole),
           pl.BlockSpec(wg.shape, whole)],
        out_specs=out_specs,
        out_shape=out_shape,
        name="nsa_proj",
        compiler_params=_cparams("parallel"),
    )(x, *tabs, wq, wkv, wg)


def _compress_kernel(x_ref, pos_ref, w1_ref, w2_ref, out_ref):
    n16 = x_ref.shape[2]
    half = w1_ref.shape[1] // 2
    x = x_ref[0, 0]
    a = _dot((x + pos_ref[0, 0:1, :]).astype(BF16), w1_ref[0, 0:half, :])
    b = _dot((x + pos_ref[0, 1:2, :]).astype(BF16), w1_ref[0, half:, :])
    hid = a + pltpu.roll(b, n16 - 1, 0)
    out_ref[0, 0] = _dot(jax.nn.gelu(hid).astype(BF16), w2_ref[0])


def _compress(cmp_kv, pos, w1, w2):
    _, g_n, s, hd = cmp_kv.shape
    n16 = s // CMP_STRIDE
    x = cmp_kv.reshape(2, g_n, n16, CMP_STRIDE * hd)
    pos2 = pos.reshape(2, 2, CMP_STRIDE * hd)
    return pl.pallas_call(
        _compress_kernel,
        grid=(2, g_n),
        in_specs=[pl.BlockSpec((1, 1, n16, CMP_STRIDE * hd), lambda a, g: (a, g, 0, 0)),
                  pl.BlockSpec((1, 2, CMP_STRIDE * hd), lambda a, g: (a, 0, 0)),
                  pl.BlockSpec((1,) + w1.shape[1:], lambda a, g: (a, 0, 0)),
                  pl.BlockSpec((1,) + w2.shape[1:], lambda a, g: (a, 0, 0))],
        out_specs=pl.BlockSpec((1, 1, n16, hd), lambda a, g: (a, g, 0, 0)),
        out_shape=jax.ShapeDtypeStruct((2, g_n, n16, hd), F32),
        name="nsa_compress",
        compiler_params=_cparams("parallel", "parallel"),
    )(x, pos2, w1, w2)


def _top_k_axis0(score, n_rounds, sel, enable=None):
    n = score.shape[0]
    row = lax.broadcasted_iota(jnp.int32, score.shape, 0).astype(F32)
    for _ in range(n_rounds):
        m = jnp.max(score, axis=0, keepdims=True)
        idx = jnp.min(jnp.where(score == m, row, float(n)), axis=0, keepdims=True)
        ok = m > -jnp.inf if enable is None else (m > -jnp.inf) & enable
        hit = row == jnp.where(ok, idx, -1.0)
        sel = jnp.where(hit, 1.0, sel)
        score = jnp.where(hit, -jnp.inf, score)
    return sel, score


def _nsa_local_kernel(n_sel, n_top, q_ref, kc_ref, vc_ref, ov_ref, wmask_ref, *rest):
    n_wb = WINDOW // TQ + 1
    kw_refs = rest[:n_wb]
    vw_refs = rest[n_wb:2 * n_wb]
    gate_ref, pre_ref, qaug_ref, s_scr, m_scr, o_acc, imp_acc = rest[2 * n_wb:]
    r_n, hd = NSA_GROUP, HEAD_DIM
    gps = kc_ref.shape[0]
    qi = pl.program_id(1)
    qgs = [q_ref[:, j * r_n * hd:(j + 1) * r_n * hd] for j in range(gps)]
    qss = [jnp.concatenate([qg[:, r * hd:(r + 1) * hd] for r in range(r_n)], axis=0) for qg in qgs]
    rows = r_n * TQ

    nc = kc_ref.shape[1]
    ct = min(2 * LANES, nc)
    t_max = ((qi * TQ + TQ - CMP_BLOCK) // CMP_STRIDE) // ct
    t_full = (jnp.maximum(qi * TQ - (CMP_BLOCK - 1) + CMP_STRIDE, 0) // CMP_STRIDE) // ct

    def lane_fold_max(s):
        m = s[:, 0:LANES]
        for c in range(1, s.shape[1] // LANES):
            m = jnp.maximum(m, s[:, c * LANES:(c + 1) * LANES])
        return m

    m_scr[...] = jnp.full(m_scr.shape, M_INIT, F32)
    for t in range(nc // ct):
        tile = slice(t * ct, (t + 1) * ct)

        @pl.when(t < t_full)
        def _():
            for j in range(gps):
                s = _dot_t(qss[j], kc_ref[j, tile, :])
                s_scr[j, :, tile] = s
                m_scr[j] = jnp.maximum(m_scr[j], lane_fold_max(s))

        @pl.when((t >= t_full) & (t <= t_max))
        def _():
            q_off = lax.broadcasted_iota(jnp.int32, (rows, ct), 0) % TQ
            tok = lax.broadcasted_iota(jnp.int32, (rows, ct), 1)
            visible = q_off - CMP_STRIDE * tok >= CMP_STRIDE * ct * t + CMP_BLOCK - 1 - qi * TQ
            for j in range(gps):
                s = jnp.where(visible, _dot_t(qss[j], kc_ref[j, tile, :]), MASKED)
                s_scr[j, :, tile] = s
                m_scr[j] = jnp.maximum(m_scr[j], lane_fold_max(s))

    for j in range(gps):
        m_scr[j] = jnp.broadcast_to(jnp.max(m_scr[j], axis=1, keepdims=True), (rows, LANES))
    o_acc[...] = jnp.zeros(o_acc.shape, F32)
    imp_acc[...] = jnp.zeros(imp_acc.shape, F32)
    for t in range(nc // ct):
        tile = slice(t * ct, (t + 1) * ct)

        @pl.when(t <= t_max)
        def _():
            for j in range(gps):
                m = m_scr[j]
                p = jnp.concatenate(
                    [jnp.exp2(s_scr[j, :, t * ct + c * LANES:t * ct + (c + 1) * LANES] - m).astype(BF16)
                     for c in range(ct // LANES)], axis=1)
                o_acc[j] += _dot(p, vc_ref[j, tile, :])
                imp_acc[j] += _dot_t(ov_ref[:, tile], p)

    n_prefix = SEL_PREFIXES if n_sel % (SEL_PREFIXES * SUBLANES) == 0 else 1
    prefix = ((qi + 1) * (TQ // SEL_BLOCK) - 1) * n_prefix // n_sel
    for v in range(n_prefix):
        @pl.when(prefix == v)
        def _():
            for j in range(gps):
                _nsa_local_tail(j, n_sel * (v + 1) // n_prefix, v == 0, n_sel, n_top, qi, qgs[j], qss[j],
                                o_acc, imp_acc, wmask_ref, kw_refs, vw_refs, gate_ref, pre_ref, qaug_ref)


def _nsa_local_tail(j, n_rows, first_tile, n_sel, n_top, qi, qg, qs, o_acc, imp_acc, wmask_ref,
                    kw_refs, vw_refs, gate_ref, pre_ref, qaug_ref):
    r_n, hd = NSA_GROUP, HEAD_DIM
    rows = r_n * TQ
    o_aug = o_acc[j]
    l_col = o_aug[:, hd:hd + 1]
    o_cmp = o_aug[:, 0:hd] * jnp.where(l_col > 0, 1.0 / l_col, 0.0)

    l_row = imp_acc[j, n_sel:n_sel + 1, :]
    imp_t = imp_acc[j, 0:n_rows, :] * jnp.where(l_row > 0, 1.0 / l_row, 0.0)
    imp = imp_t[:, 0:TQ]
    for r in range(1, r_n):
        imp = imp + imp_t[:, r * TQ:(r + 1) * TQ]
    blk = lax.broadcasted_iota(jnp.int32, (n_rows, TQ), 0)
    blk_q = (qi * TQ + lax.broadcasted_iota(jnp.int32, (n_rows, TQ), 1)) // SEL_BLOCK
    forced = (blk == 0) | (blk == blk_q) | (blk == blk_q - 1)
    score = jnp.where((blk <= blk_q) & jnp.logical_not(forced), imp, -jnp.inf)
    sel = jnp.where(forced, 1.0, 0.0)
    n_forced_min = 3
    blk_q_row = blk_q[0:1, :]
    sel_tiles = []
    for c in range(TQ // LANES):
        ls = slice(c * LANES, (c + 1) * LANES)
        sel_c, score_c = _top_k_axis0(score[:, ls], n_top - n_forced_min, sel[:, ls])
        if first_tile:
            for extra in range(1, n_forced_min):
                sel_c, score_c = _top_k_axis0(score_c, 1, sel_c,
                                              enable=blk_q_row[:, ls] < n_forced_min - extra)
        sel_tiles.append(sel_c)
    sel = jnp.concatenate(sel_tiles, axis=1)
    n_pad = -(-n_rows // LANES) * LANES
    if n_pad > n_rows:
        sel = jnp.concatenate([sel, jnp.zeros((n_pad - n_rows, TQ), F32)], axis=0)
    bias = jnp.where(sel > 0, 0.0, MASKED).T.astype(BF16)
    n_var = qaug_ref.shape[0]
    if n_pad < n_var * BIAS_BLOCKS:
        bias = jnp.concatenate(
            [bias, jnp.full((TQ, n_var * BIAS_BLOCKS - n_pad), MASKED, BF16)], axis=1)
    zeros = jnp.zeros((TQ, 2 * LANES - BIAS_BLOCKS - hd), BF16)
    for v in range(n_var):
        for r in range(r_n):
            rs = slice(r * TQ, (r + 1) * TQ)
            qaug_ref[v, j, 0, rs, 0:BIAS_BLOCKS] = bias[:, v * BIAS_BLOCKS:(v + 1) * BIAS_BLOCKS]
            qaug_ref[v, j, 0, rs, BIAS_BLOCKS:BIAS_BLOCKS + hd] = qg[:, r * hd:(r + 1) * hd]
            qaug_ref[v, j, 0, rs, BIAS_BLOCKS + hd:2 * LANES] = zeros

    kcat = jnp.concatenate([kw[j] for kw in kw_refs], axis=0)
    vcat = jnp.concatenate([vw[j] for vw in vw_refs], axis=0)
    s = _dot_t(qs, kcat)
    wmask = wmask_ref[0]
    s = jnp.concatenate([s[r * TQ:(r + 1) * TQ] + wmask for r in range(r_n)], axis=0)
    m = jnp.broadcast_to(jnp.max(s, axis=1, keepdims=True), (rows, LANES))
    p = jnp.concatenate([jnp.exp2(s[:, c * LANES:(c + 1) * LANES] - m).astype(BF16)
                         for c in range(s.shape[1] // LANES)], axis=1)
    o_aug = _dot(p, vcat)
    o_win = o_aug[:, 0:hd] * (1.0 / o_aug[:, hd:hd + 1])

    gate = gate_ref[j]
    for r in range(r_n):
        rs = slice(r * TQ, (r + 1) * TQ)
        c = (j * r_n + r) * hd
        pre_ref[:, c:c + hd] = (gate[:, 3 * r:3 * r + 1] * o_cmp[rs]
                                + gate[:, 3 * r + 2:3 * r + 3] * o_win[rs])


def _overlap_matrix(nc, n_sel):
    i = np.arange(nc)[None, :]
    j = np.arange(n_sel)[:, None]
    lo = np.maximum(i * CMP_STRIDE, j * SEL_BLOCK)
    hi = np.minimum(i * CMP_STRIDE + CMP_BLOCK - 1, j * SEL_BLOCK + SEL_BLOCK - 1)
    ov = np.zeros((n_sel + SUBLANES, nc), np.float32)
    ov[:n_sel] = np.clip(hi - lo + 1, 0, None) / CMP_STRIDE
    ov[n_sel] = 1.0
    return jnp.asarray(ov, BF16)


def _window_masks(n_wb):
    q = np.arange(TQ)[:, None]
    c = np.arange(n_wb * TQ)[None, :]
    dist = q + (n_wb - 1) * TQ - c
    masks = []
    for v in range(n_wb):
        kpos = (v - (n_wb - 1)) * TQ + c
        masks.append(np.where((kpos >= 0) & (dist >= 0) & (dist < WINDOW), 0.0, MASKED))
    return jnp.asarray(np.stack(masks), F32)


def _nsa_local(q, kc, vc, kwin, vwin, gates, n_sel, n_var):
    s = q.shape[0]
    g_n, r_n, hd = NSA_KV_HEADS, NSA_GROUP, HEAD_DIM
    n_q = s // TQ
    n_wb = WINDOW // TQ + 1
    n_top = min(SEL_TOPK, n_sel)
    nc = kc.shape[1]
    rows = r_n * TQ
    gps = NSA_LOCAL_GROUPS
    assert n_top >= 3 and nc % min(2 * LANES, nc) == 0 and g_n % gps == 0

    def win_spec(j, width):
        return pl.BlockSpec((gps, TQ, width), lambda g, i: (g, jnp.maximum(i - (n_wb - 1) + j, 0), 0))

    return pl.pallas_call(
        functools.partial(_nsa_local_kernel, n_sel, n_top),
        grid=(g_n // gps, n_q),
        in_specs=[pl.BlockSpec((TQ, gps * r_n * hd), lambda g, i: (i, g)),
                  pl.BlockSpec((gps, nc, hd), lambda g, i: (g, 0, 0)),
                  pl.BlockSpec((gps, nc, LANES), lambda g, i: (g, 0, 0)),
                  pl.BlockSpec((n_sel + SUBLANES, nc), lambda g, i: (0, 0)),
                  pl.BlockSpec((1, TQ, n_wb * TQ), lambda g, i: (jnp.minimum(i, n_wb - 1), 0, 0))]
        + [win_spec(j, hd) for j in range(n_wb)] + [win_spec(j, LANES) for j in range(n_wb)]
        + [pl.BlockSpec((gps, TQ, LANES), lambda g, i: (g, i, 0))],
        out_specs=[pl.BlockSpec((TQ, gps * r_n * hd), lambda g, i: (i, g)),
                   pl.BlockSpec((n_var, gps, 1, rows, 2 * LANES), lambda g, i: (0, g, i, 0, 0))],
        out_shape=[jax.ShapeDtypeStruct((s, N_HEADS * hd), F32),
                   jax.ShapeDtypeStruct((n_var, g_n, n_q, rows, 2 * LANES), BF16)],
        scratch_shapes=[pltpu.VMEM((gps, rows, nc), F32), pltpu.VMEM((gps, rows, LANES), F32),
                        pltpu.VMEM((gps, rows, LANES), F32),
                        pltpu.VMEM((gps, n_sel + SUBLANES, rows), F32)],
        name="nsa_local",
        compiler_params=_cparams("parallel", "parallel"),
    )(q, kc, vc, _overlap_matrix(nc, n_sel), _window_masks(n_wb),
      *([kwin] * n_wb), *([vwin] * n_wb), gates)


def _causal_pairs(n_q, tq, tk):
    qs, ks = [], []
    for qi in range(n_q):
        for ki in range(((qi + 1) * tq - 1) // tk + 1):
            qs.append(qi)
            ks.append(ki)
    return jnp.asarray(qs, jnp.int32), jnp.asarray(ks, jnp.int32)


def _flash_scores(q, k, causal_mask):
    s = _dot_t(q, k)
    return s if causal_mask is None else jnp.where(causal_mask, s, MASKED)


def _flash_accumulate(s, v, m_ref, acc_ref, idx, row0):
    m_prev = m_ref[idx, row0:, :]
    m_new = jnp.maximum(m_prev, jnp.max(s, axis=1, keepdims=True))
    p = jnp.concatenate([jnp.exp2(s[:, c * LANES:(c + 1) * LANES] - m_new).astype(BF16)
                         for c in range(s.shape[1] // LANES)], axis=1)
    acc_ref[idx, row0:, :] = jnp.exp2(m_prev - m_new) * acc_ref[idx, row0:, :] + _dot(p, v)
    m_ref[idx, row0:, :] = m_new


def _flash_steps(qi, ki, tq, tk, rows_per_q, n_batch, operands, m_ref, acc_ref, finalize):
    first_diag = (qi * tq) // tk
    last_k = ((qi + 1) * tq - 1) // tk
    assert rows_per_q == 1 or tq <= tk

    @pl.when(ki == 0)
    def _():
        m_ref[...] = jnp.full(m_ref.shape, M_INIT, F32)
        acc_ref[...] = jnp.zeros(acc_ref.shape, F32)

    def run(masked, row0, n_keys):
        mask = None
        if masked:
            shape = (rows_per_q * tq - row0, n_keys)
            q_off = lax.broadcasted_iota(jnp.int32, shape, 0) % tq + row0
            k_off = lax.broadcasted_iota(jnp.int32, shape, 1)
            mask = q_off - k_off >= ki * tk - qi * tq
        def scores(b):
            q, k, _ = operands(b, row0, n_keys)
            return _flash_scores(q, k, mask)

        ahead = [scores(b) for b in range(min(QK_AHEAD, n_batch))]
        for b in range(n_batch):
            s = ahead.pop(0)
            if b + QK_AHEAD < n_batch:
                ahead.append(scores(b + QK_AHEAD))
            _flash_accumulate(s, operands(b, row0, n_keys)[2], m_ref, acc_ref, b, row0)

    @pl.when(ki < first_diag)
    def _():
        run(False, 0, tk)

    if tq >= tk:
        variants = [(ki - first_diag == j, j * tk, tk) for j in range(tq // tk)]
    else:
        variants = [(qi % (tk // tq) == j, 0, (j + 1) * tq) for j in range(tk // tq)]
    for cond, row0, n_keys in variants:
        @pl.when((ki >= first_diag) & cond)
        def _():
            run(True, row0, n_keys)

    @pl.when(ki == last_k)
    def _():
        finalize()


def _nsa_sel_kernel(qtab, ktab, qaug_ref, kaug_ref, vaug_ref, pre_ref, gate_ref, o_ref,
                    m_ref, acc_ref):
    g_n, r_n, hd = NSA_KV_HEADS, NSA_GROUP, HEAD_DIM
    t = pl.program_id(0)
    qi, ki = qtab[t], ktab[t]

    def operands(g, row0, n_keys):
        return (qaug_ref[0, g, 0, row0:, :], kaug_ref[0:n_keys, g * 2 * LANES:(g + 1) * 2 * LANES],
                vaug_ref[0:n_keys, g * LANES:(g + 1) * LANES])

    def finalize():
        for g in range(g_n):
            acc = acc_ref[g]
            gate = gate_ref[g]
            o = acc[:, 0:hd] * (1.0 / acc[:, hd:hd + 1])
            for r in range(r_n):
                c = (g * r_n + r) * hd
                o_ref[:, c:c + hd] = (pre_ref[:, c:c + hd] + gate[:, 3 * r + 1:3 * r + 2]
                                      * o[r * TQ:(r + 1) * TQ]).astype(o_ref.dtype)

    _flash_steps(qi, ki, TQ, TK, r_n, g_n, operands, m_ref, acc_ref, finalize)


def _nsa_sel(qaug, kaug, vaug, pre, gates):
    n_var, g_n, n_q, rows, _ = qaug.shape
    s = kaug.shape[0]
    qtab, ktab = _causal_pairs(n_q, TQ, TK)
    tiles_per_var = BIAS_BLOCKS * SEL_BLOCK // TK
    grid_spec = pltpu.PrefetchScalarGridSpec(
        num_scalar_prefetch=2,
        grid=(qtab.shape[0],),
        in_specs=[
            pl.BlockSpec((1, g_n, 1, rows, 2 * LANES),
                         lambda t, qt, kt: (kt[t] // tiles_per_var, 0, qt[t], 0, 0)),
            pl.BlockSpec((TK, g_n * 2 * LANES), lambda t, qt, kt: (kt[t], 0)),
            pl.BlockSpec((TK, g_n * LANES), lambda t, qt, kt: (kt[t], 0)),
            pl.BlockSpec((TQ, N_HEADS * HEAD_DIM), lambda t, qt, kt: (qt[t], 0)),
            pl.BlockSpec((g_n, TQ, LANES), lambda t, qt, kt: (0, qt[t], 0)),
        ],
        out_specs=pl.BlockSpec((TQ, N_HEADS * HEAD_DIM), lambda t, qt, kt: (qt[t], 0)),
        scratch_shapes=[pltpu.VMEM((g_n, rows, LANES), F32), pltpu.VMEM((g_n, rows, LANES), F32)],
    )
    return pl.pallas_call(
        _nsa_sel_kernel,
        grid_spec=grid_spec,
        out_shape=jax.ShapeDtypeStruct((s, N_HEADS * HEAD_DIM), BF16),
        name="nsa_sel",
        compiler_params=_cparams("arbitrary"),
    )(qtab, ktab, qaug, kaug, vaug, pre, gates)


def _moba_proj_kernel(n_blk, k_eff, x_ref, cos_ref, sa_ref, sb_ref, w_ref,
                      qaug_ref, kaug_ref, vaug_ref, kmean_ref):
    tm = x_ref.shape[0]
    hd, d = HEAD_DIM, N_HEADS * HEAD_DIM
    i = pl.program_id(0)

    @pl.when(i == 0)
    def _():
        kmean_ref[...] = jnp.zeros(kmean_ref.shape, F32)

    xb = x_ref[...].astype(BF16)
    cos_t, sin_a, sin_b = cos_ref[...], sa_ref[...], sb_ref[...]
    q = _rope(_dot(xb, w_ref[:, 0:d]), cos_t, sin_a, sin_b)
    k = _rope(_dot(xb, w_ref[:, d:2 * d]), cos_t, sin_a, sin_b)
    v = _dot(xb, w_ref[:, 2 * d:3 * d])
    kmean = kmean_ref[...]
    nb = kmean.shape[0]
    own = jnp.where(lax.broadcasted_iota(jnp.int32, (1, nb), 1) == i, 1.0, 0.0).astype(BF16)
    own_onehot = jnp.broadcast_to(own, (tm, nb))
    ones_col = _ones_col(tm, LANES - hd)
    blk_t = lax.broadcasted_iota(jnp.int32, (nb, tm), 0)
    own_t = jnp.where(blk_t == i, 1.0, 0.0)
    for hp in range(N_HEADS // 2):
        bias_t = []
        for h in (2 * hp, 2 * hp + 1):
            sl = slice(h * hd, (h + 1) * hd)
            gate_t = _dot_t(kmean[:, sl], q[:, sl], precision=lax.Precision.HIGHEST)
            gate_t = jnp.where(blk_t < i, gate_t, -jnp.inf)
            bias_t.append(jnp.where(_top_k_axis0(gate_t, k_eff, own_t)[0] > 0, 0.0, MASKED))
        bias = jnp.concatenate(bias_t, axis=0).T.astype(BF16)
        for j, h in enumerate((2 * hp, 2 * hp + 1)):
            sl = slice(h * hd, (h + 1) * hd)
            qaug_ref[h, :, 0:hd] = (q[:, sl] * SCALE).astype(BF16)
            qaug_ref[h, :, hd:hd + nb] = bias[:, j * nb:(j + 1) * nb]
            kaug_ref[h, :, 0:hd] = k[:, sl].astype(BF16)
            kaug_ref[h, :, hd:hd + nb] = own_onehot
            vaug_ref[h, :, 0:hd] = v[:, sl].astype(BF16)
            vaug_ref[h, :, hd:LANES] = ones_col
    kmean_ref[pl.ds(i, 1), :] = jnp.mean(k, axis=0, keepdims=True)


def _moba_proj(x, tabs, w):
    s, d = x.shape
    n_blk = s // MOBA_BLOCK
    k_eff = min(MOBA_TOPK, max(n_blk - 1, 1))
    nb = LANES - HEAD_DIM
    assert s % MOBA_BLOCK == 0 and n_blk <= nb
    tm = MOBA_BLOCK
    row = lambda i: (i, 0)
    head = jax.ShapeDtypeStruct((N_HEADS, s, LANES), BF16)
    head_spec = pl.BlockSpec((N_HEADS, tm, LANES), lambda i: (0, i, 0))
    return pl.pallas_call(
        functools.partial(_moba_proj_kernel, n_blk, k_eff),
        grid=(s // tm,),
        in_specs=[pl.BlockSpec((tm, d), row)] + [pl.BlockSpec((tm, LANES), row)] * 3
        + [pl.BlockSpec(w.shape, lambda i: (0, 0))],
        out_specs=[head_spec] * 3,
        out_shape=[head] * 3,
        scratch_shapes=[pltpu.VMEM((nb, N_HEADS * HEAD_DIM), F32)],
        name="moba_proj",
        compiler_params=_cparams("arbitrary"),
    )(x, *tabs, w)


def _moba_attn_kernel(qtab, ktab, q_ref, k_ref, v_ref, o_ref, m_ref, acc_ref):
    hd = HEAD_DIM
    t = pl.program_id(0)
    qi, ki = qtab[t], ktab[t]

    def operands(h, row0, n_keys):
        return q_ref[h, row0:, :], k_ref[h, 0:n_keys, :], v_ref[h, 0:n_keys, :]

    def finalize():
        for h in range(N_HEADS):
            acc = acc_ref[h]
            o_ref[:, h * hd:(h + 1) * hd] = (acc[:, 0:hd] * (1.0 / acc[:, hd:hd + 1])).astype(o_ref.dtype)

    _flash_steps(qi, ki, q_ref.shape[1], k_ref.shape[1], 1, N_HEADS, operands, m_ref, acc_ref, finalize)


def _moba_attn(qaug, kaug, vaug):
    h_n, s, _ = qaug.shape
    tq, tk = min(MOBA_TQ, s), min(MOBA_TK, s)
    assert s % tq == 0 and s % tk == 0
    qtab, ktab = _causal_pairs(s // tq, tq, tk)
    grid_spec = pltpu.PrefetchScalarGridSpec(
        num_scalar_prefetch=2,
        grid=(qtab.shape[0],),
        in_specs=[pl.BlockSpec((h_n, tq, LANES), lambda t, qt, kt: (0, qt[t], 0)),
                  pl.BlockSpec((h_n, tk, LANES), lambda t, qt, kt: (0, kt[t], 0)),
                  pl.BlockSpec((h_n, tk, LANES), lambda t, qt, kt: (0, kt[t], 0))],
        out_specs=pl.BlockSpec((tq, h_n * HEAD_DIM), lambda t, qt, kt: (qt[t], 0)),
        scratch_shapes=[pltpu.VMEM((h_n, tq, LANES), F32), pltpu.VMEM((h_n, tq, LANES), F32)],
    )
    return pl.pallas_call(
        _moba_attn_kernel,
        grid_spec=grid_spec,
        out_shape=jax.ShapeDtypeStruct((s, h_n * HEAD_DIM), BF16),
        name="moba_attn",
        compiler_params=_cparams("arbitrary"),
    )(qtab, ktab, qaug, kaug, vaug)


def _layer_norm(z, gain, bias):
    mu = jnp.mean(z, axis=-1, keepdims=True)
    zc = z - mu
    var = jnp.mean(zc * zc, axis=-1, keepdims=True)
    return zc * lax.rsqrt(var + LN_EPS) * gain + bias


def _post_mixer_kernel(alpha, n_chunk, a_ref, x_ref, wo_ref, g1_ref, b1_ref, wg_ref, wu_ref, wd_ref,
                       g2_ref, b2_ref, o_ref):
    x1 = _layer_norm(alpha * x_ref[...] + _dot(a_ref[...], wo_ref[...]), g1_ref[...], b1_ref[...])
    xb = x1.astype(BF16)
    cw = wg_ref.shape[1] // n_chunk
    f = None
    for c in range(n_chunk):
        sl = slice(c * cw, (c + 1) * cw)
        u = (jax.nn.silu(_dot(xb, wg_ref[:, sl])) * _dot(xb, wu_ref[:, sl])).astype(BF16)
        part = _dot(u, wd_ref[sl, :])
        f = part if f is None else f + part
    o_ref[...] = _layer_norm(alpha * x1 + f, g2_ref[...], b2_ref[...])


def _post_mixer(a, x, w_o, g1, b1, wg, wu, wd, g2, b2, alpha):
    s, d = x.shape
    f = wg.shape[1]
    n_chunk = 2 if f % (2 * LANES) == 0 else 1
    tm = min(MM_TILE, s)
    row = lambda i: (i, 0)

    def resident(shape):
        return pl.BlockSpec(shape, lambda i: (0, 0), pipeline_mode=pl.Buffered(1))

    vec = lambda v: v.reshape(1, d)
    return pl.pallas_call(
        functools.partial(_post_mixer_kernel, alpha, n_chunk),
        grid=(s // tm,),
        in_specs=[pl.BlockSpec((tm, a.shape[1]), row), pl.BlockSpec((tm, d), row),
                  resident(w_o.shape), resident((1, d)), resident((1, d)),
                  resident(wg.shape), resident(wu.shape), resident(wd.shape),
                  resident((1, d)), resident((1, d))],
        out_specs=pl.BlockSpec((tm, d), row),
        out_shape=jax.ShapeDtypeStruct((s, d), F32),
        name="post_mixer",
        compiler_params=_cparams("parallel"),
    )(a, x, w_o, vec(g1), vec(b1), wg, wu, wd, vec(g2), vec(b2))


def _nsa_mixer(x, tabs, w_in, ck_pos, ck_w1, ck_w2, cv_pos, cv_w1, cv_w2):
    s = x.shape[0]
    g_n, r_n, hd = NSA_KV_HEADS, NSA_GROUP, HEAD_DIM
    q_dim, kv_dim = N_HEADS * hd, NSA_KV_HEADS * hd
    n_sel = s // SEL_BLOCK
    n_var = -(-n_sel // BIAS_BLOCKS)
    assert s % TK == 0 and (BIAS_BLOCKS * SEL_BLOCK) % TK == 0
    wq = w_in[:, :q_dim].astype(BF16)
    wkv = w_in[:, q_dim:q_dim + 6 * kv_dim].astype(BF16)
    wg = w_in[:, q_dim + 6 * kv_dim:].reshape(-1, g_n, 3 * r_n)
    wg = jnp.pad(wg, ((0, 0), (0, 0), (0, LANES - 3 * r_n))).reshape(-1, g_n * LANES).astype(BF16)
    q, cmp_kv, kaug, vaug, kwin, vwin, gates = _nsa_proj(x, tabs, wq, wkv, wg)
    comp = _compress(cmp_kv, jnp.stack([ck_pos, cv_pos]),
                     jnp.stack([ck_w1, cv_w1]).astype(BF16), jnp.stack([ck_w2, cv_w2]).astype(BF16))
    kc = comp[0].astype(BF16)
    ones = jnp.ones(comp.shape[1:3] + (1,), F32)
    zeros = jnp.zeros(comp.shape[1:3] + (LANES - hd - 1,), F32)
    vc = jnp.concatenate([comp[1], ones, zeros], axis=-1).astype(BF16)
    pre, qaug = _nsa_local(q, kc, vc, kwin, vwin, gates, n_sel, n_var)
    return _nsa_sel(qaug, kaug, vaug, pre, gates)


def _moba_mixer(x, tabs, w_in):
    qaug, kaug, vaug = _moba_proj(x, tabs, w_in.astype(BF16))
    return _moba_attn(qaug, kaug, vaug)


def kernel(x, positions, nsa_w_in, nsa_w_o, nsa_ck_pos, nsa_ck_w1, nsa_ck_w2, nsa_cv_pos, nsa_cv_w1, nsa_cv_w2, moba_w_in, moba_w_o, ffn_wg, ffn_wu, ffn_wd, ln1_g, ln1_b, ln2_g, ln2_b):
    b, s, d = x.shape
    depth = ffn_wg.shape[0]
    alpha = (2 * depth) ** 0.25
    outs = []
    for bi in range(b):
        h = x[bi]
        tabs = _rope_tables(positions[bi])
        for i in range(depth):
            j = i // 2
            if i % 2 == 0:
                a = _nsa_mixer(h, tabs, nsa_w_in[j], nsa_ck_pos[j], nsa_ck_w1[j], nsa_ck_w2[j],
                               nsa_cv_pos[j], nsa_cv_w1[j], nsa_cv_w2[j])
                w_o = nsa_w_o[j]
            else:
                a = _moba_mixer(h, tabs, moba_w_in[j])
                w_o = moba_w_o[j]
            h = _post_mixer(a, h, w_o.astype(BF16), ln1_g[i], ln1_b[i], ffn_wg[i].astype(BF16),
                            ffn_wu[i].astype(BF16), ffn_wd[i].astype(BF16), ln2_g[i], ln2_b[i], alpha)
        outs.append(h)
    return jnp.stack(outs)
```

```python
import functools

import numpy as np
import jax
import jax.numpy as jnp
from jax import lax
from jax.experimental import pallas as pl
from jax.experimental.pallas import tpu as pltpu

F32 = jnp.float32
BF16 = jnp.bfloat16

N_HEADS = 16
HEAD_DIM = 64
ROPE_DIM = HEAD_DIM // 4
ROPE_THETA = 500000.0
NSA_KV_HEADS = 4
NSA_GROUP = N_HEADS // NSA_KV_HEADS
CMP_BLOCK = 32
CMP_STRIDE = 16
SEL_BLOCK = 64
SEL_TOPK = 16
WINDOW = 512
MOBA_BLOCK = 256
MOBA_TOPK = 3
LN_EPS = 1e-5
LARGE = 1e30
SCALE = HEAD_DIM ** -0.5 * float(np.log2(np.e))

LANES = 128
SUBLANES = 8
VMEM_LIMIT_BYTES = 48 * 1024 * 1024

ROW_TILE = 256
MM_TILE = 512
TQ = 256
TK = 1024
MOBA_TQ = 1024
MOBA_TK = 512
QK_AHEAD = 1
SEL_PREFIXES = 4
NSA_LOCAL_GROUPS = 1
BIAS_BLOCKS = 128
MASKED = -1e30
M_INIT = -5e29

assert WINDOW % TQ == 0 and TK % TQ == 0 and ROW_TILE == MOBA_BLOCK


def _cparams(*sem, flags=None):
    return pltpu.CompilerParams(dimension_semantics=sem, vmem_limit_bytes=VMEM_LIMIT_BYTES, flags=flags)


def _dot(a, b):
    return jnp.dot(a, b, preferred_element_type=F32)


def _dot_t(a, b, precision=None):
    return lax.dot_general(a, b, (((1,), (1,)), ((), ())), preferred_element_type=F32,
                           precision=precision)


def _rope_tables_kernel(pos_ref, c_ref, cos_ref, sa_ref, sb_ref):
    ang = pos_ref[...].astype(F32) * c_ref[0:1, :]
    c = jnp.cos(ang)
    s = jnp.sin(ang)
    cos_ref[...] = jnp.where(c_ref[1:2, :] > 0, c, 1.0)
    sa_ref[...] = jnp.where(c_ref[2:3, :] > 0, s, 0.0)
    sb_ref[...] = jnp.where(c_ref[3:4, :] > 0, -s, 0.0)


def _rope_tables(positions):
    s = positions.shape[0]
    half = ROPE_DIM // 2
    inv = ROPE_THETA ** (-jnp.arange(0, ROPE_DIM, 2, dtype=F32) / ROPE_DIM)
    d = np.arange(LANES) % HEAD_DIM
    consts = jnp.zeros((8, LANES), F32)
    consts = consts.at[0].set(inv[d % half])
    consts = consts.at[1].set(jnp.asarray(d < ROPE_DIM, F32))
    consts = consts.at[2].set(jnp.asarray((d >= half) & (d < ROPE_DIM), F32))
    consts = consts.at[3].set(jnp.asarray(d < half, F32))
    tm = min(MM_TILE, s)
    tab = jax.ShapeDtypeStruct((s, LANES), F32)
    return pl.pallas_call(
        _rope_tables_kernel,
        grid=(s // tm,),
        in_specs=[pl.BlockSpec((tm, 1), lambda i: (i, 0)),
                  pl.BlockSpec((8, LANES), lambda i: (0, 0))],
        out_specs=[pl.BlockSpec((tm, LANES), lambda i: (i, 0))] * 3,
        out_shape=[tab, tab, tab],
        name="rope_tables",
        compiler_params=_cparams("parallel"),
    )(positions.reshape(s, 1), consts)


def _rope(x, cos_t, sin_a, sin_b):
    half = ROPE_DIM // 2
    outs = []
    for c in range(x.shape[1] // LANES):
        xs = x[:, c * LANES:(c + 1) * LANES]
        outs.append(xs * cos_t + pltpu.roll(xs, half, 1) * sin_a
                    + pltpu.roll(xs, LANES - half, 1) * sin_b)
    return outs[0] if len(outs) == 1 else jnp.concatenate(outs, axis=1)


def _ones_col(rows, width):
    return jnp.where(lax.broadcasted_iota(jnp.int32, (rows, width), 1) == 0, 1.0, 0.0).astype(BF16)


def _nsa_proj_kernel(x_ref, cos_ref, sa_ref, sb_ref, wq_ref, wkv_ref, wg_ref,
                     q_ref, cmp_ref, kaug_ref, vaug_ref, kwin_ref, vwin_ref, gate_ref):
    tm = x_ref.shape[0]
    g_n, hd, kvd = NSA_KV_HEADS, HEAD_DIM, NSA_KV_HEADS * HEAD_DIM
    xb = x_ref[...].astype(BF16)
    cos_t, sin_a, sin_b = cos_ref[...], sa_ref[...], sb_ref[...]
    q = _rope(_dot(xb, wq_ref[...]), cos_t, sin_a, sin_b) * SCALE
    q_ref[...] = q.astype(BF16)
    kv = _dot(xb, wkv_ref[...])
    k_cmp = _rope(kv[:, 0:kvd], cos_t, sin_a, sin_b)
    v_cmp = kv[:, kvd:2 * kvd]
    k_sel = _rope(kv[:, 2 * kvd:3 * kvd], cos_t, sin_a, sin_b)
    v_sel = kv[:, 3 * kvd:4 * kvd]
    k_win = _rope(kv[:, 4 * kvd:5 * kvd], cos_t, sin_a, sin_b)
    v_win = kv[:, 5 * kvd:6 * kvd]
    gate = jax.nn.sigmoid(_dot(xb, wg_ref[...]))
    tok = pl.program_id(0) * tm + lax.broadcasted_iota(jnp.int32, (tm, BIAS_BLOCKS), 0)
    blk = (tok // SEL_BLOCK) % BIAS_BLOCKS
    onehot = jnp.where(lax.broadcasted_iota(jnp.int32, (tm, BIAS_BLOCKS), 1) == blk,
                       1.0, 0.0).astype(BF16)
    ones_col = _ones_col(tm, LANES - hd)
    for g in range(g_n):
        sl = slice(g * hd, (g + 1) * hd)
        cmp_ref[0, g] = k_cmp[:, sl]
        cmp_ref[1, g] = v_cmp[:, sl]
        kwin_ref[g] = k_win[:, sl].astype(BF16)
        vwin_ref[g, :, 0:hd] = v_win[:, sl].astype(BF16)
        vwin_ref[g, :, hd:LANES] = ones_col
        gate_ref[g] = gate[:, g * LANES:(g + 1) * LANES]
        kb = g * 2 * LANES
        kaug_ref[:, kb:kb + BIAS_BLOCKS] = onehot
        kaug_ref[:, kb + BIAS_BLOCKS:kb + BIAS_BLOCKS + hd] = k_sel[:, sl].astype(BF16)
        kaug_ref[:, kb + BIAS_BLOCKS + hd:kb + 2 * LANES] = jnp.zeros((tm, 2 * LANES - BIAS_BLOCKS - hd), BF16)
        vb = g * LANES
        vaug_ref[:, vb:vb + hd] = v_sel[:, sl].astype(BF16)
        vaug_ref[:, vb + hd:vb + LANES] = ones_col


def _nsa_proj(x, tabs, wq, wkv, wg):
    s, d = x.shape
    g_n, hd = NSA_KV_HEADS, HEAD_DIM
    tm = min(ROW_TILE, s)
    row = lambda i: (i, 0)
    whole = lambda i: (0, 0)
    out_shape = [
        jax.ShapeDtypeStruct((s, N_HEADS * hd), BF16),
        jax.ShapeDtypeStruct((2, g_n, s, hd), F32),
        jax.ShapeDtypeStruct((s, g_n * 2 * LANES), BF16),
        jax.ShapeDtypeStruct((s, g_n * LANES), BF16),
        jax.ShapeDtypeStruct((g_n, s, hd), BF16),
        jax.ShapeDtypeStruct((g_n, s, LANES), BF16),
        jax.ShapeDtypeStruct((g_n, s, LANES), F32),
    ]
    out_specs = [
        pl.BlockSpec((tm, N_HEADS * hd), row),
        pl.BlockSpec((2, g_n, tm, hd), lambda i: (0, 0, i, 0)),
        pl.BlockSpec((tm, g_n * 2 * LANES), row),
        pl.BlockSpec((tm, g_n * LANES), row),
        pl.BlockSpec((g_n, tm, hd), lambda i: (0, i, 0)),
        pl.BlockSpec((g_n, tm, LANES), lambda i: (0, i, 0)),
        pl.BlockSpec((g_n, tm, LANES), lambda i: (0, i, 0)),
    ]
    return pl.pallas_call(
        _nsa_proj_kernel,
        grid=(s // tm,),
        in_specs=[pl.BlockSpec((tm, d), row)] + [pl.BlockSpec((tm, LANES), row)] * 3
        + [pl.BlockSpec(wq.shape, whole), pl.BlockSpec(wkv.shape, whole),
           pl.BlockSpec(wg.shape, whole)],
        out_specs=out_specs,
        out_shape=out_shape,
        name="nsa_proj",
        compiler_params=_cparams("parallel"),
    )(x, *tabs, wq, wkv, wg)


def _compress_kernel(x_ref, pos_ref, w1_ref, w2_ref, out_ref):
    n16 = x_ref.shape[2]
    half = w1_ref.shape[1] // 2
    x = x_ref[0, 0]
    a = _dot((x + pos_ref[0, 0:1, :]).astype(BF16), w1_ref[0, 0:half, :])
    b = _dot((x + pos_ref[0, 1:2, :]).astype(BF16), w1_ref[0, half:, :])
    hid = a + pltpu.roll(b, n16 - 1, 0)
    out_ref[0, 0] = _dot(jax.nn.gelu(hid).astype(BF16), w2_ref[0])


def _compress(cmp_kv, pos, w1, w2):
    _, g_n, s, hd = cmp_kv.shape
    n16 = s // CMP_STRIDE
    x = cmp_kv.reshape(2, g_n, n16, CMP_STRIDE * hd)
    pos2 = pos.reshape(2, 2, CMP_STRIDE * hd)
    return pl.pallas_call(
        _compress_kernel,
        grid=(2, g_n),
        in_specs=[pl.BlockSpec((1, 1, n16, CMP_STRIDE * hd), lambda a, g: (a, g, 0, 0)),
                  pl.BlockSpec((1, 2, CMP_STRIDE * hd), lambda a, g: (a, 0, 0)),
                  pl.BlockSpec((1,) + w1.shape[1:], lambda a, g: (a, 0, 0)),
                  pl.BlockSpec((1,) + w2.shape[1:], lambda a, g: (a, 0, 0))],
        out_specs=pl.BlockSpec((1, 1, n16, hd), lambda a, g: (a, g, 0, 0)),
        out_shape=jax.ShapeDtypeStruct((2, g_n, n16, hd), F32),
        name="nsa_compress",
        compiler_params=_cparams("parallel", "parallel"),
    )(x, pos2, w1, w2)


def _top_k_axis0(score, n_rounds, sel, enable=None):
    n = score.shape[0]
    row = lax.broadcasted_iota(jnp.int32, score.shape, 0).astype(F32)
    for _ in range(n_rounds):
        m = jnp.max(score, axis=0, keepdims=True)
        idx = jnp.min(jnp.where(score == m, row, float(n)), axis=0, keepdims=True)
        ok = m > -jnp.inf if enable is None else (m > -jnp.inf) & enable
        hit = row == jnp.where(ok, idx, -1.0)
        sel = jnp.where(hit, 1.0, sel)
        score = jnp.where(hit, -jnp.inf, score)
    return sel, score


def _nsa_local_kernel(n_sel, n_top, q_ref, kc_ref, vc_ref, ov_ref, wmask_ref, *rest):
    n_wb = WINDOW // TQ + 1
    kw_refs = rest[:n_wb]
    vw_refs = rest[n_wb:2 * n_wb]
    gate_ref, pre_ref, qaug_ref, m_scr, o_acc, imp_acc = rest[2 * n_wb:]
    r_n, hd = NSA_GROUP, HEAD_DIM
    gps = kc_ref.shape[0]
    qi = pl.program_id(1)
    qgs = [q_ref[:, j * r_n * hd:(j + 1) * r_n * hd] for j in range(gps)]
    qss = [jnp.concatenate([qg[:, r * hd:(r + 1) * hd] for r in range(r_n)], axis=0) for qg in qgs]
    rows = r_n * TQ

    nc = kc_ref.shape[1]
    ct = min(2 * LANES, nc)
    t_max = ((qi * TQ + TQ - CMP_BLOCK) // CMP_STRIDE) // ct
    t_full = (jnp.maximum(qi * TQ - (CMP_BLOCK - 1) + CMP_STRIDE, 0) // CMP_STRIDE) // ct

    m_scr[...] = jnp.full(m_scr.shape, M_INIT, F32)
    o_acc[...] = jnp.zeros(o_acc.shape, F32)
    imp_acc[...] = jnp.zeros(imp_acc.shape, F32)

    def cmp_tile(t, masked):
        tile = slice(t * ct, (t + 1) * ct)
        if masked:
            q_off = lax.broadcasted_iota(jnp.int32, (rows, ct), 0) % TQ
            tok = lax.broadcasted_iota(jnp.int32, (rows, ct), 1)
            visible = q_off - CMP_STRIDE * tok >= CMP_STRIDE * ct * t + CMP_BLOCK - 1 - qi * TQ
        for j in range(gps):
            s = _dot_t(qss[j], kc_ref[j, tile, :])
            if masked:
                s = jnp.where(visible, s, MASKED)
            m_prev = m_scr[j]
            m_new = jnp.maximum(m_prev, jnp.max(s, axis=1, keepdims=True))
            alpha = jnp.exp2(m_prev - m_new)
            p = jnp.concatenate([jnp.exp2(s[:, c * LANES:(c + 1) * LANES] - m_new).astype(BF16)
                                 for c in range(ct // LANES)], axis=1)
            o_acc[j] = alpha * o_acc[j] + _dot(p, vc_ref[j, tile, :])
            imp_new = _dot(p, ov_ref[tile, :])
            imp_acc[j] = jnp.concatenate(
                [alpha * imp_acc[j, :, c * LANES:(c + 1) * LANES] + imp_new[:, c * LANES:(c + 1) * LANES]
                 for c in range(imp_new.shape[1] // LANES)], axis=1)
            m_scr[j] = m_new

    for t in range(nc // ct):
        pl.when(t < t_full)(functools.partial(cmp_tile, t, False))
        pl.when((t >= t_full) & (t <= t_max))(functools.partial(cmp_tile, t, True))

    n_prefix = SEL_PREFIXES if n_sel % (SEL_PREFIXES * SUBLANES) == 0 else 1
    prefix = ((qi + 1) * (TQ // SEL_BLOCK) - 1) * n_prefix // n_sel
    for v in range(n_prefix):
        @pl.when(prefix == v)
        def _():
            for j in range(gps):
                _nsa_local_tail(j, n_sel * (v + 1) // n_prefix, v == 0, n_sel, n_top, qi, qgs[j], qss[j],
                                o_acc, imp_acc, wmask_ref, kw_refs, vw_refs, gate_ref, pre_ref, qaug_ref)


def _nsa_local_tail(j, n_rows, first_tile, n_sel, n_top, qi, qg, qs, o_acc, imp_acc, wmask_ref,
                    kw_refs, vw_refs, gate_ref, pre_ref, qaug_ref):
    r_n, hd = NSA_GROUP, HEAD_DIM
    rows = r_n * TQ
    o_aug = o_acc[j]
    l_col = o_aug[:, hd:hd + 1]
    inv_l = jnp.broadcast_to(jnp.where(l_col > 0, 1.0 / l_col, 0.0), (rows, LANES))
    o_cmp = o_aug[:, 0:hd] * inv_l[:, 0:hd]

    n_pad = -(-n_rows // LANES) * LANES
    imp_q = None
    for r in range(r_n):
        rs = slice(r * TQ, (r + 1) * TQ)
        imp_r = jnp.concatenate([imp_acc[j, rs, c * LANES:(c + 1) * LANES] * inv_l[rs]
                                 for c in range(n_pad // LANES)], axis=1)
        imp_q = imp_r if imp_q is None else imp_q + imp_r
    imp = imp_q.T[0:n_rows]
    blk = lax.broadcasted_iota(jnp.int32, (n_rows, TQ), 0)
    blk_q = (qi * TQ + lax.broadcasted_iota(jnp.int32, (n_rows, TQ), 1)) // SEL_BLOCK
    forced = (blk == 0) | (blk == blk_q) | (blk == blk_q - 1)
    score = jnp.where((blk <= blk_q) & jnp.logical_not(forced), imp, -jnp.inf)
    sel = jnp.where(forced, 1.0, 0.0)
    n_forced_min = 3
    blk_q_row = blk_q[0:1, :]
    sel_tiles = []
    for c in range(TQ // LANES):
        ls = slice(c * LANES, (c + 1) * LANES)
        sel_c, score_c = _top_k_axis0(score[:, ls], n_top - n_forced_min, sel[:, ls])
        if first_tile:
            for extra in range(1, n_forced_min):
                sel_c, score_c = _top_k_axis0(score_c, 1, sel_c,
                                              enable=blk_q_row[:, ls] < n_forced_min - extra)
        sel_tiles.append(sel_c)
    sel = jnp.concatenate(sel_tiles, axis=1)
    n_pad = -(-n_rows // LANES) * LANES
    if n_pad > n_rows:
        sel = jnp.concatenate([sel, jnp.zeros((n_pad - n_rows, TQ), F32)], axis=0)
    bias = jnp.where(sel > 0, 0.0, MASKED).T.astype(BF16)
    n_var = qaug_ref.shape[0]
    if n_pad < n_var * BIAS_BLOCKS:
        bias = jnp.concatenate(
            [bias, jnp.full((TQ, n_var * BIAS_BLOCKS - n_pad), MASKED, BF16)], axis=1)
    zeros = jnp.zeros((TQ, 2 * LANES - BIAS_BLOCKS - hd), BF16)
    for v in range(n_var):
        for r in range(r_n):
            rs = slice(r * TQ, (r + 1) * TQ)
            qaug_ref[v, j, 0, rs, 0:BIAS_BLOCKS] = bias[:, v * BIAS_BLOCKS:(v + 1) * BIAS_BLOCKS]
            qaug_ref[v, j, 0, rs, BIAS_BLOCKS:BIAS_BLOCKS + hd] = qg[:, r * hd:(r + 1) * hd]
            qaug_ref[v, j, 0, rs, BIAS_BLOCKS + hd:2 * LANES] = zeros

    kcat = jnp.concatenate([kw[j] for kw in kw_refs], axis=0)
    vcat = jnp.concatenate([vw[j] for vw in vw_refs], axis=0)
    s = _dot_t(qs, kcat)
    wmask = wmask_ref[0]
    s = jnp.concatenate([s[r * TQ:(r + 1) * TQ] + wmask for r in range(r_n)], axis=0)
    m = jnp.broadcast_to(jnp.max(s, axis=1, keepdims=True), (rows, LANES))
    p = jnp.concatenate([jnp.exp2(s[:, c * LANES:(c + 1) * LANES] - m).astype(BF16)
                         for c in range(s.shape[1] // LANES)], axis=1)
    o_aug = _dot(p, vcat)
    o_win = o_aug[:, 0:hd] * (1.0 / o_aug[:, hd:hd + 1])

    gate = gate_ref[j]
    for r in range(r_n):
        rs = slice(r * TQ, (r + 1) * TQ)
        c = (j * r_n + r) * hd
        pre_ref[:, c:c + hd] = (gate[:, 3 * r:3 * r + 1] * o_cmp[rs]
                                + gate[:, 3 * r + 2:3 * r + 3] * o_win[rs])


def _overlap_matrix(nc, n_sel):
    i = np.arange(nc)[:, None]
    j = np.arange(n_sel)[None, :]
    lo = np.maximum(i * CMP_STRIDE, j * SEL_BLOCK)
    hi = np.minimum(i * CMP_STRIDE + CMP_BLOCK - 1, j * SEL_BLOCK + SEL_BLOCK - 1)
    ov = np.zeros((nc, -(-n_sel // LANES) * LANES), np.float32)
    ov[:, :n_sel] = np.clip(hi - lo + 1, 0, None) / CMP_STRIDE
    return jnp.asarray(ov, BF16)


def _window_masks(n_wb):
    q = np.arange(TQ)[:, None]
    c = np.arange(n_wb * TQ)[None, :]
    dist = q + (n_wb - 1) * TQ - c
    masks = []
    for v in range(n_wb):
        kpos = (v - (n_wb - 1)) * TQ + c
        masks.append(np.where((kpos >= 0) & (dist >= 0) & (dist < WINDOW), 0.0, MASKED))
    return jnp.asarray(np.stack(masks), F32)


def _nsa_local(q, kc, vc, kwin, vwin, gates, n_sel, n_var):
    s = q.shape[0]
    g_n, r_n, hd = NSA_KV_HEADS, NSA_GROUP, HEAD_DIM
    n_q = s // TQ
    n_wb = WINDOW // TQ + 1
    n_top = min(SEL_TOPK, n_sel)
    nc = kc.shape[1]
    rows = r_n * TQ
    gps = NSA_LOCAL_GROUPS
    assert n_top >= 3 and nc % min(2 * LANES, nc) == 0 and g_n % gps == 0
    overlap = _overlap_matrix(nc, n_sel)

    def win_spec(j, width):
        return pl.BlockSpec((gps, TQ, width), lambda g, i: (g, jnp.maximum(i - (n_wb - 1) + j, 0), 0))

    return pl.pallas_call(
        functools.partial(_nsa_local_kernel, n_sel, n_top),
        grid=(g_n // gps, n_q),
        in_specs=[pl.BlockSpec((TQ, gps * r_n * hd), lambda g, i: (i, g)),
                  pl.BlockSpec((gps, nc, hd), lambda g, i: (g, 0, 0)),
                  pl.BlockSpec((gps, nc, LANES), lambda g, i: (g, 0, 0)),
                  pl.BlockSpec(overlap.shape, lambda g, i: (0, 0)),
                  pl.BlockSpec((1, TQ, n_wb * TQ), lambda g, i: (jnp.minimum(i, n_wb - 1), 0, 0))]
        + [win_spec(j, hd) for j in range(n_wb)] + [win_spec(j, LANES) for j in range(n_wb)]
        + [pl.BlockSpec((gps, TQ, LANES), lambda g, i: (g, i, 0))],
        out_specs=[pl.BlockSpec((TQ, gps * r_n * hd), lambda g, i: (i, g)),
                   pl.BlockSpec((n_var, gps, 1, rows, 2 * LANES), lambda g, i: (0, g, i, 0, 0))],
        out_shape=[jax.ShapeDtypeStruct((s, N_HEADS * hd), F32),
                   jax.ShapeDtypeStruct((n_var, g_n, n_q, rows, 2 * LANES), BF16)],
        scratch_shapes=[pltpu.VMEM((gps, rows, LANES), F32), pltpu.VMEM((gps, rows, LANES), F32),
                        pltpu.VMEM((gps, rows, overlap.shape[1]), F32)],
        name="nsa_local",
        compiler_params=_cparams("parallel", "parallel"),
    )(q, kc, vc, overlap, _window_masks(n_wb),
      *([kwin] * n_wb), *([vwin] * n_wb), gates)


def _causal_pairs(n_q, tq, tk):
    qs, ks = [], []
    for qi in range(n_q):
        for ki in range(((qi + 1) * tq - 1) // tk + 1):
            qs.append(qi)
            ks.append(ki)
    return jnp.asarray(qs, jnp.int32), jnp.asarray(ks, jnp.int32)


def _flash_scores(q, k, causal_mask):
    s = _dot_t(q, k)
    return s if causal_mask is None else jnp.where(causal_mask, s, MASKED)


def _flash_accumulate(s, v, m_ref, acc_ref, idx, row0):
    m_prev = m_ref[idx, row0:, :]
    m_new = jnp.maximum(m_prev, jnp.max(s, axis=1, keepdims=True))
    p = jnp.concatenate([jnp.exp2(s[:, c * LANES:(c + 1) * LANES] - m_new).astype(BF16)
                         for c in range(s.shape[1] // LANES)], axis=1)
    acc_ref[idx, row0:, :] = jnp.exp2(m_prev - m_new) * acc_ref[idx, row0:, :] + _dot(p, v)
    m_ref[idx, row0:, :] = m_new


def _flash_steps(qi, ki, tq, tk, rows_per_q, n_batch, operands, m_ref, acc_ref, finalize):
    first_diag = (qi * tq) // tk
    last_k = ((qi + 1) * tq - 1) // tk
    assert rows_per_q == 1 or tq <= tk

    @pl.when(ki == 0)
    def _():
        m_ref[...] = jnp.full(m_ref.shape, M_INIT, F32)
        acc_ref[...] = jnp.zeros(acc_ref.shape, F32)

    def run(masked, row0, n_keys):
        mask = None
        if masked:
            shape = (rows_per_q * tq - row0, n_keys)
            q_off = lax.broadcasted_iota(jnp.int32, shape, 0) % tq + row0
            k_off = lax.broadcasted_iota(jnp.int32, shape, 1)
            mask = q_off - k_off >= ki * tk - qi * tq
        def scores(b):
            q, k, _ = operands(b, row0, n_keys)
            return _flash_scores(q, k, mask)

        ahead = [scores(b) for b in range(min(QK_AHEAD, n_batch))]
        for b in range(n_batch):
            s = ahead.pop(0)
            if b + QK_AHEAD < n_batch:
                ahead.append(scores(b + QK_AHEAD))
            _flash_accumulate(s, operands(b, row0, n_keys)[2], m_ref, acc_ref, b, row0)

    @pl.when(ki < first_diag)
    def _():
        run(False, 0, tk)

    if tq >= tk:
        variants = [(ki - first_diag == j, j * tk, tk) for j in range(tq // tk)]
    else:
        variants = [(qi % (tk // tq) == j, 0, (j + 1) * tq) for j in range(tk // tq)]
    for cond, row0, n_keys in variants:
        @pl.when((ki >= first_diag) & cond)
        def _():
            run(True, row0, n_keys)

    @pl.when(ki == last_k)
    def _():
        finalize()


def _nsa_sel_kernel(qtab, ktab, qaug_ref, kaug_ref, vaug_ref, pre_ref, gate_ref, o_ref,
                    m_ref, acc_ref):
    g_n, r_n, hd = NSA_KV_HEADS, NSA_GROUP, HEAD_DIM
    t = pl.program_id(0)
    qi, ki = qtab[t], ktab[t]

    def operands(g, row0, n_keys):
        return (qaug_ref[0, g, 0, row0:, :], kaug_ref[0:n_keys, g * 2 * LANES:(g + 1) * 2 * LANES],
                vaug_ref[0:n_keys, g * LANES:(g + 1) * LANES])

    def finalize():
        for g in range(g_n):
            acc = acc_ref[g]
            gate = gate_ref[g]
            o = acc[:, 0:hd] * (1.0 / acc[:, hd:hd + 1])
            for r in range(r_n):
                c = (g * r_n + r) * hd
                o_ref[:, c:c + hd] = (pre_ref[:, c:c + hd] + gate[:, 3 * r + 1:3 * r + 2]
                                      * o[r * TQ:(r + 1) * TQ]).astype(o_ref.dtype)

    _flash_steps(qi, ki, TQ, TK, r_n, g_n, operands, m_ref, acc_ref, finalize)


def _nsa_sel(qaug, kaug, vaug, pre, gates):
    n_var, g_n, n_q, rows, _ = qaug.shape
    s = kaug.shape[0]
    qtab, ktab = _causal_pairs(n_q, TQ, TK)
    tiles_per_var = BIAS_BLOCKS * SEL_BLOCK // TK
    grid_spec = pltpu.PrefetchScalarGridSpec(
        num_scalar_prefetch=2,
        grid=(qtab.shape[0],),
        in_specs=[
            pl.BlockSpec((1, g_n, 1, rows, 2 * LANES),
                         lambda t, qt, kt: (kt[t] // tiles_per_var, 0, qt[t], 0, 0)),
            pl.BlockSpec((TK, g_n * 2 * LANES), lambda t, qt, kt: (kt[t], 0)),
            pl.BlockSpec((TK, g_n * LANES), lambda t, qt, kt: (kt[t], 0)),
            pl.BlockSpec((TQ, N_HEADS * HEAD_DIM), lambda t, qt, kt: (qt[t], 0)),
            pl.BlockSpec((g_n, TQ, LANES), lambda t, qt, kt: (0, qt[t], 0)),
        ],
        out_specs=pl.BlockSpec((TQ, N_HEADS * HEAD_DIM), lambda t, qt, kt: (qt[t], 0)),
        scratch_shapes=[pltpu.VMEM((g_n, rows, LANES), F32), pltpu.VMEM((g_n, rows, LANES), F32)],
    )
    return pl.pallas_call(
        _nsa_sel_kernel,
        grid_spec=grid_spec,
        out_shape=jax.ShapeDtypeStruct((s, N_HEADS * HEAD_DIM), BF16),
        name="nsa_sel",
        compiler_params=_cparams("arbitrary"),
    )(qtab, ktab, qaug, kaug, vaug, pre, gates)


def _moba_proj_kernel(n_blk, k_eff, x_ref, cos_ref, sa_ref, sb_ref, w_ref,
                      qaug_ref, kaug_ref, vaug_ref, kmean_ref):
    tm = x_ref.shape[0]
    hd, d = HEAD_DIM, N_HEADS * HEAD_DIM
    i = pl.program_id(0)

    @pl.when(i == 0)
    def _():
        kmean_ref[...] = jnp.zeros(kmean_ref.shape, F32)

    xb = x_ref[...].astype(BF16)
    cos_t, sin_a, sin_b = cos_ref[...], sa_ref[...], sb_ref[...]
    q = _rope(_dot(xb, w_ref[:, 0:d]), cos_t, sin_a, sin_b)
    k = _rope(_dot(xb, w_ref[:, d:2 * d]), cos_t, sin_a, sin_b)
    v = _dot(xb, w_ref[:, 2 * d:3 * d])
    kmean = kmean_ref[...]
    nb = kmean.shape[0]
    own = jnp.where(lax.broadcasted_iota(jnp.int32, (1, nb), 1) == i, 1.0, 0.0).astype(BF16)
    own_onehot = jnp.broadcast_to(own, (tm, nb))
    ones_col = _ones_col(tm, LANES - hd)
    blk_t = lax.broadcasted_iota(jnp.int32, (nb, tm), 0)
    own_t = jnp.where(blk_t == i, 1.0, 0.0)
    for hp in range(N_HEADS // 2):
        bias_t = []
        for h in (2 * hp, 2 * hp + 1):
            sl = slice(h * hd, (h + 1) * hd)
            gate_t = _dot_t(kmean[:, sl], q[:, sl], precision=lax.Precision.HIGHEST)
            gate_t = jnp.where(blk_t < i, gate_t, -jnp.inf)
            bias_t.append(jnp.where(_top_k_axis0(gate_t, k_eff, own_t)[0] > 0, 0.0, MASKED))
        bias = jnp.concatenate(bias_t, axis=0).T.astype(BF16)
        for j, h in enumerate((2 * hp, 2 * hp + 1)):
            sl = slice(h * hd, (h + 1) * hd)
            qaug_ref[h, :, 0:hd] = (q[:, sl] * SCALE).astype(BF16)
            qaug_ref[h, :, hd:hd + nb] = bias[:, j * nb:(j + 1) * nb]
            kaug_ref[h, :, 0:hd] = k[:, sl].astype(BF16)
            kaug_ref[h, :, hd:hd + nb] = own_onehot
            vaug_ref[h, :, 0:hd] = v[:, sl].astype(BF16)
            vaug_ref[h, :, hd:LANES] = ones_col
    kmean_ref[pl.ds(i, 1), :] = jnp.mean(k, axis=0, keepdims=True)


def _moba_proj(x, tabs, w):
    s, d = x.shape
    n_blk = s // MOBA_BLOCK
    k_eff = min(MOBA_TOPK, max(n_blk - 1, 1))
    nb = LANES - HEAD_DIM
    assert s % MOBA_BLOCK == 0 and n_blk <= nb
    tm = MOBA_BLOCK
    row = lambda i: (i, 0)
    head = jax.ShapeDtypeStruct((N_HEADS, s, LANES), BF16)
    head_spec = pl.BlockSpec((N_HEADS, tm, LANES), lambda i: (0, i, 0))
    return pl.pallas_call(
        functools.partial(_moba_proj_kernel, n_blk, k_eff),
        grid=(s // tm,),
        in_specs=[pl.BlockSpec((tm, d), row)] + [pl.BlockSpec((tm, LANES), row)] * 3
        + [pl.BlockSpec(w.shape, lambda i: (0, 0))],
        out_specs=[head_spec] * 3,
        out_shape=[head] * 3,
        scratch_shapes=[pltpu.VMEM((nb, N_HEADS * HEAD_DIM), F32)],
        name="moba_proj",
        compiler_params=_cparams("arbitrary"),
    )(x, *tabs, w)


def _moba_attn_kernel(qtab, ktab, q_ref, k_ref, v_ref, o_ref, m_ref, acc_ref):
    hd = HEAD_DIM
    t = pl.program_id(0)
    qi, ki = qtab[t], ktab[t]

    def operands(h, row0, n_keys):
        return q_ref[h, row0:, :], k_ref[h, 0:n_keys, :], v_ref[h, 0:n_keys, :]

    def finalize():
        for h in range(N_HEADS):
            acc = acc_ref[h]
            o_ref[:, h * hd:(h + 1) * hd] = (acc[:, 0:hd] * (1.0 / acc[:, hd:hd + 1])).astype(o_ref.dtype)

    _flash_steps(qi, ki, q_ref.shape[1], k_ref.shape[1], 1, N_HEADS, operands, m_ref, acc_ref, finalize)


def _moba_attn(qaug, kaug, vaug):
    h_n, s, _ = qaug.shape
    tq, tk = min(MOBA_TQ, s), min(MOBA_TK, s)
    assert s % tq == 0 and s % tk == 0
    qtab, ktab = _causal_pairs(s // tq, tq, tk)
    grid_spec = pltpu.PrefetchScalarGridSpec(
        num_scalar_prefetch=2,
        grid=(qtab.shape[0],),
        in_specs=[pl.BlockSpec((h_n, tq, LANES), lambda t, qt, kt: (0, qt[t], 0)),
                  pl.BlockSpec((h_n, tk, LANES), lambda t, qt, kt: (0, kt[t], 0)),
                  pl.BlockSpec((h_n, tk, LANES), lambda t, qt, kt: (0, kt[t], 0))],
        out_specs=pl.BlockSpec((tq, h_n * HEAD_DIM), lambda t, qt, kt: (qt[t], 0)),
        scratch_shapes=[pltpu.VMEM((h_n, tq, LANES), F32), pltpu.VMEM((h_n, tq, LANES), F32)],
    )
    return pl.pallas_call(
        _moba_attn_kernel,
        grid_spec=grid_spec,
        out_shape=jax.ShapeDtypeStruct((s, h_n * HEAD_DIM), BF16),
        name="moba_attn",
        compiler_params=_cparams("arbitrary"),
    )(qtab, ktab, qaug, kaug, vaug)


def _layer_norm(z, gain, bias):
    mu = jnp.mean(z, axis=-1, keepdims=True)
    zc = z - mu
    var = jnp.mean(zc * zc, axis=-1, keepdims=True)
    return zc * lax.rsqrt(var + LN_EPS) * gain + bias


def _post_mixer_kernel(alpha, n_chunk, a_ref, x_ref, wo_ref, g1_ref, b1_ref, wg_ref, wu_ref, wd_ref,
                       g2_ref, b2_ref, o_ref):
    x1 = _layer_norm(alpha * x_ref[...] + _dot(a_ref[...], wo_ref[...]), g1_ref[...], b1_ref[...])
    xb = x1.astype(BF16)
    cw = wg_ref.shape[1] // n_chunk
    f = None
    for c in range(n_chunk):
        sl = slice(c * cw, (c + 1) * cw)
        u = (jax.nn.silu(_dot(xb, wg_ref[:, sl])) * _dot(xb, wu_ref[:, sl])).astype(BF16)
        part = _dot(u, wd_ref[sl, :])
        f = part if f is None else f + part
    o_ref[...] = _layer_norm(alpha * x1 + f, g2_ref[...], b2_ref[...])


def _post_mixer(a, x, w_o, g1, b1, wg, wu, wd, g2, b2, alpha):
    s, d = x.shape
    f = wg.shape[1]
    n_chunk = 2 if f % (2 * LANES) == 0 else 1
    tm = min(MM_TILE, s)
    row = lambda i: (i, 0)

    def resident(shape):
        return pl.BlockSpec(shape, lambda i: (0, 0), pipeline_mode=pl.Buffered(1))

    vec = lambda v: v.reshape(1, d)
    return pl.pallas_call(
        functools.partial(_post_mixer_kernel, alpha, n_chunk),
        grid=(s // tm,),
        in_specs=[pl.BlockSpec((tm, a.shape[1]), row), pl.BlockSpec((tm, d), row),
                  resident(w_o.shape), resident((1, d)), resident((1, d)),
                  resident(wg.shape), resident(wu.shape), resident(wd.shape),
                  resident((1, d)), resident((1, d))],
        out_specs=pl.BlockSpec((tm, d), row),
        out_shape=jax.ShapeDtypeStruct((s, d), F32),
        name="post_mixer",
        compiler_params=_cparams("parallel"),
    )(a, x, w_o, vec(g1), vec(b1), wg, wu, wd, vec(g2), vec(b2))


def _nsa_mixer(x, tabs, w_in, ck_pos, ck_w1, ck_w2, cv_pos, cv_w1, cv_w2):
    s = x.shape[0]
    g_n, r_n, hd = NSA_KV_HEADS, NSA_GROUP, HEAD_DIM
    q_dim, kv_dim = N_HEADS * hd, NSA_KV_HEADS * hd
    n_sel = s // SEL_BLOCK
    n_var = -(-n_sel // BIAS_BLOCKS)
    assert s % TK == 0 and (BIAS_BLOCKS * SEL_BLOCK) % TK == 0
    wq = w_in[:, :q_dim].astype(BF16)
    wkv = w_in[:, q_dim:q_dim + 6 * kv_dim].astype(BF16)
    wg = w_in[:, q_dim + 6 * kv_dim:].reshape(-1, g_n, 3 * r_n)
    wg = jnp.pad(wg, ((0, 0), (0, 0), (0, LANES - 3 * r_n))).reshape(-1, g_n * LANES).astype(BF16)
    q, cmp_kv, kaug, vaug, kwin, vwin, gates = _nsa_proj(x, tabs, wq, wkv, wg)
    comp = _compress(cmp_kv, jnp.stack([ck_pos, cv_pos]),
                     jnp.stack([ck_w1, cv_w1]).astype(BF16), jnp.stack([ck_w2, cv_w2]).astype(BF16))
    kc = comp[0].astype(BF16)
    ones = jnp.ones(comp.shape[1:3] + (1,), F32)
    zeros = jnp.zeros(comp.shape[1:3] + (LANES - hd - 1,), F32)
    vc = jnp.concatenate([comp[1], ones, zeros], axis=-1).astype(BF16)
    pre, qaug = _nsa_local(q, kc, vc, kwin, vwin, gates, n_sel, n_var)
    return _nsa_sel(qaug, kaug, vaug, pre, gates)


def _moba_mixer(x, tabs, w_in):
    qaug, kaug, vaug = _moba_proj(x, tabs, w_in.astype(BF16))
    return _moba_attn(qaug, kaug, vaug)


def kernel(x, positions, nsa_w_in, nsa_w_o, nsa_ck_pos, nsa_ck_w1, nsa_ck_w2, nsa_cv_pos, nsa_cv_w1, nsa_cv_w2, moba_w_in, moba_w_o, ffn_wg, ffn_wu, ffn_wd, ln1_g, ln1_b, ln2_g, ln2_b):
    b, s, d = x.shape
    depth = ffn_wg.shape[0]
    alpha = (2 * depth) ** 0.25
    outs = []
    for bi in range(b):
        h = x[bi]
        tabs = _rope_tables(positions[bi])
        for i in range(depth):
            j = i // 2
            if i % 2 == 0:
                a = _nsa_mixer(h, tabs, nsa_w_in[j], nsa_ck_pos[j], nsa_ck_w1[j], nsa_ck_w2[j],
                               nsa_cv_pos[j], nsa_cv_w1[j], nsa_cv_w2[j])
                w_o = nsa_w_o[j]
            else:
                a = _moba_mixer(h, tabs, moba_w_in[j])
                w_o = moba_w_o[j]
            h = _post_mixer(a, h, w_o.astype(BF16), ln1_g[i], ln1_b[i], ffn_wg[i].astype(BF16),
                            ffn_wu[i].astype(BF16), ffn_wd[i].astype(BF16), ln2_g[i], ln2_b[i], alpha)
        outs.append(h)
    return jnp.stack(outs)
```

```python
import functools

import numpy as np
import jax
import jax.numpy as jnp
from jax import lax
from jax.experimental import pallas as pl
from jax.experimental.pallas import tpu as pltpu

F32 = jnp.float32
BF16 = jnp.bfloat16

N_HEADS = 16
HEAD_DIM = 64
ROPE_DIM = HEAD_DIM // 4
ROPE_THETA = 500000.0
NSA_KV_HEADS = 4
NSA_GROUP = N_HEADS // NSA_KV_HEADS
CMP_BLOCK = 32
CMP_STRIDE = 16
SEL_BLOCK = 64
SEL_TOPK = 16
WINDOW = 512
MOBA_BLOCK = 256
MOBA_TOPK = 3
LN_EPS = 1e-5
SCALE = HEAD_DIM ** -0.5 * float(np.log2(np.e))

LANES = 128
SUBLANES = 8
VMEM_LIMIT_BYTES = 48 * 1024 * 1024

ROW_TILE = 256
MM_TILE = 512
TQ = 256
TK = 1024
MOBA_TQ = 1024
MOBA_TK = 512
QK_AHEAD = 1
SEL_PREFIXES = 4
NSA_LOCAL_GROUPS = 1
BIAS_BLOCKS = 128
MASKED = -1e30
M_INIT = -5e29

assert WINDOW % TQ == 0 and TK % TQ == 0 and ROW_TILE == MOBA_BLOCK


def _cparams(*sem):
    return pltpu.CompilerParams(dimension_semantics=sem, vmem_limit_bytes=VMEM_LIMIT_BYTES)


def _dot(a, b):
    return jnp.dot(a, b, preferred_element_type=F32)


def _dot_t(a, b, precision=None):
    return lax.dot_general(a, b, (((1,), (1,)), ((), ())), preferred_element_type=F32,
                           precision=precision)


def _rope_tables_kernel(pos_ref, c_ref, cos_ref, sa_ref, sb_ref):
    ang = pos_ref[...].astype(F32) * c_ref[0:1, :]
    c = jnp.cos(ang)
    s = jnp.sin(ang)
    cos_ref[...] = jnp.where(c_ref[1:2, :] > 0, c, 1.0)
    sa_ref[...] = jnp.where(c_ref[2:3, :] > 0, s, 0.0)
    sb_ref[...] = jnp.where(c_ref[3:4, :] > 0, -s, 0.0)


def _rope_tables(positions):
    s = positions.shape[0]
    half = ROPE_DIM // 2
    inv = ROPE_THETA ** (-jnp.arange(0, ROPE_DIM, 2, dtype=F32) / ROPE_DIM)
    d = np.arange(LANES) % HEAD_DIM
    consts = jnp.zeros((SUBLANES, LANES), F32)
    consts = consts.at[0].set(inv[d % half])
    consts = consts.at[1].set(jnp.asarray(d < ROPE_DIM, F32))
    consts = consts.at[2].set(jnp.asarray((d >= half) & (d < ROPE_DIM), F32))
    consts = consts.at[3].set(jnp.asarray(d < half, F32))
    tm = min(MM_TILE, s)
    tab = jax.ShapeDtypeStruct((s, LANES), F32)
    return pl.pallas_call(
        _rope_tables_kernel,
        grid=(s // tm,),
        in_specs=[pl.BlockSpec((tm, 1), lambda i: (i, 0)),
                  pl.BlockSpec((SUBLANES, LANES), lambda i: (0, 0))],
        out_specs=[pl.BlockSpec((tm, LANES), lambda i: (i, 0))] * 3,
        out_shape=[tab, tab, tab],
        name="rope_tables",
        compiler_params=_cparams("parallel"),
    )(positions.reshape(s, 1), consts)


def _rope(x, cos_t, sin_a, sin_b):
    half = ROPE_DIM // 2
    outs = []
    for c in range(x.shape[1] // LANES):
        xs = x[:, c * LANES:(c + 1) * LANES]
        outs.append(xs * cos_t + pltpu.roll(xs, half, 1) * sin_a
                    + pltpu.roll(xs, LANES - half, 1) * sin_b)
    return outs[0] if len(outs) == 1 else jnp.concatenate(outs, axis=1)


def _ones_col(rows, width):
    return jnp.where(lax.broadcasted_iota(jnp.int32, (rows, width), 1) == 0, 1.0, 0.0).astype(BF16)


def _nsa_proj_kernel(x_ref, cos_ref, sa_ref, sb_ref, wq_ref, wkv_ref, wg_ref,
                     q_ref, cmp_ref, kaug_ref, vaug_ref, kwin_ref, vwin_ref, gate_ref):
    tm = x_ref.shape[0]
    g_n, hd, kvd = NSA_KV_HEADS, HEAD_DIM, NSA_KV_HEADS * HEAD_DIM
    xb = x_ref[...].astype(BF16)
    cos_t, sin_a, sin_b = cos_ref[...], sa_ref[...], sb_ref[...]
    q = _rope(_dot(xb, wq_ref[...]), cos_t, sin_a, sin_b) * SCALE
    q_ref[...] = q.astype(BF16)
    kv = _dot(xb, wkv_ref[...])
    k_cmp = _rope(kv[:, 0:kvd], cos_t, sin_a, sin_b)
    v_cmp = kv[:, kvd:2 * kvd]
    k_sel = _rope(kv[:, 2 * kvd:3 * kvd], cos_t, sin_a, sin_b)
    v_sel = kv[:, 3 * kvd:4 * kvd]
    k_win = _rope(kv[:, 4 * kvd:5 * kvd], cos_t, sin_a, sin_b)
    v_win = kv[:, 5 * kvd:6 * kvd]
    gate = jax.nn.sigmoid(_dot(xb, wg_ref[...]))
    tok = pl.program_id(0) * tm + lax.broadcasted_iota(jnp.int32, (tm, BIAS_BLOCKS), 0)
    blk = (tok // SEL_BLOCK) % BIAS_BLOCKS
    onehot = jnp.where(lax.broadcasted_iota(jnp.int32, (tm, BIAS_BLOCKS), 1) == blk,
                       1.0, 0.0).astype(BF16)
    ones_col = _ones_col(tm, LANES - hd)
    for g in range(g_n):
        sl = slice(g * hd, (g + 1) * hd)
        cmp_ref[0, g] = k_cmp[:, sl]
        cmp_ref[1, g] = v_cmp[:, sl]
        kwin_ref[g] = k_win[:, sl].astype(BF16)
        vwin_ref[g, :, 0:hd] = v_win[:, sl].astype(BF16)
        vwin_ref[g, :, hd:LANES] = ones_col
        gate_ref[g] = gate[:, g * LANES:(g + 1) * LANES]
        kb = g * 2 * LANES
        kaug_ref[:, kb:kb + BIAS_BLOCKS] = onehot
        kaug_ref[:, kb + BIAS_BLOCKS:kb + BIAS_BLOCKS + hd] = k_sel[:, sl].astype(BF16)
        kaug_ref[:, kb + BIAS_BLOCKS + hd:kb + 2 * LANES] = jnp.zeros((tm, 2 * LANES - BIAS_BLOCKS - hd), BF16)
        vb = g * LANES
        vaug_ref[:, vb:vb + hd] = v_sel[:, sl].astype(BF16)
        vaug_ref[:, vb + hd:vb + LANES] = ones_col


def _nsa_proj(x, tabs, wq, wkv, wg):
    s, d = x.shape
    g_n, hd = NSA_KV_HEADS, HEAD_DIM
    tm = min(ROW_TILE, s)
    row = lambda i: (i, 0)
    whole = lambda i: (0, 0)
    out_shape = [
        jax.ShapeDtypeStruct((s, N_HEADS * hd), BF16),
        jax.ShapeDtypeStruct((2, g_n, s, hd), F32),
        jax.ShapeDtypeStruct((s, g_n * 2 * LANES), BF16),
        jax.ShapeDtypeStruct((s, g_n * LANES), BF16),
        jax.ShapeDtypeStruct((g_n, s, hd), BF16),
        jax.ShapeDtypeStruct((g_n, s, LANES), BF16),
        jax.ShapeDtypeStruct((g_n, s, LANES), F32),
    ]
    out_specs = [
        pl.BlockSpec((tm, N_HEADS * hd), row),
        pl.BlockSpec((2, g_n, tm, hd), lambda i: (0, 0, i, 0)),
        pl.BlockSpec((tm, g_n * 2 * LANES), row),
        pl.BlockSpec((tm, g_n * LANES), row),
        pl.BlockSpec((g_n, tm, hd), lambda i: (0, i, 0)),
        pl.BlockSpec((g_n, tm, LANES), lambda i: (0, i, 0)),
        pl.BlockSpec((g_n, tm, LANES), lambda i: (0, i, 0)),
    ]
    return pl.pallas_call(
        _nsa_proj_kernel,
        grid=(s // tm,),
        in_specs=[pl.BlockSpec((tm, d), row)] + [pl.BlockSpec((tm, LANES), row)] * 3
        + [pl.BlockSpec(wq.shape, whole), pl.BlockSpec(wkv.shape, whole),
           pl.BlockSpec(wg.shape, whole)],
        out_specs=out_specs,
        out_shape=out_shape,
        name="nsa_proj",
        compiler_params=_cparams("parallel"),
    )(x, *tabs, wq, wkv, wg)


def _compress_kernel(x_ref, pos_ref, w1_ref, w2_ref, out_ref):
    n16 = x_ref.shape[2]
    half = w1_ref.shape[1] // 2
    x = x_ref[0, 0]
    a = _dot((x + pos_ref[0, 0:1, :]).astype(BF16), w1_ref[0, 0:half, :])
    b = _dot((x + pos_ref[0, 1:2, :]).astype(BF16), w1_ref[0, half:, :])
    hid = a + pltpu.roll(b, n16 - 1, 0)
    out_ref[0, 0] = _dot(jax.nn.gelu(hid).astype(BF16), w2_ref[0])


def _compress(cmp_kv, pos, w1, w2):
    _, g_n, s, hd = cmp_kv.shape
    n16 = s // CMP_STRIDE
    x = cmp_kv.reshape(2, g_n, n16, CMP_STRIDE * hd)
    pos2 = pos.reshape(2, 2, CMP_STRIDE * hd)
    return pl.pallas_call(
        _compress_kernel,
        grid=(2, g_n),
        in_specs=[pl.BlockSpec((1, 1, n16, CMP_STRIDE * hd), lambda a, g: (a, g, 0, 0)),
                  pl.BlockSpec((1, 2, CMP_STRIDE * hd), lambda a, g: (a, 0, 0)),
                  pl.BlockSpec((1,) + w1.shape[1:], lambda a, g: (a, 0, 0)),
                  pl.BlockSpec((1,) + w2.shape[1:], lambda a, g: (a, 0, 0))],
        out_specs=pl.BlockSpec((1, 1, n16, hd), lambda a, g: (a, g, 0, 0)),
        out_shape=jax.ShapeDtypeStruct((2, g_n, n16, hd), F32),
        name="nsa_compress",
        compiler_params=_cparams("parallel", "parallel"),
    )(x, pos2, w1, w2)


def _top_k_axis0(score, n_rounds, sel, enable=None):
    n = score.shape[0]
    row = lax.broadcasted_iota(jnp.int32, score.shape, 0).astype(F32)
    for _ in range(n_rounds):
        m = jnp.max(score, axis=0, keepdims=True)
        idx = jnp.min(jnp.where(score == m, row, float(n)), axis=0, keepdims=True)
        ok = m > -jnp.inf if enable is None else (m > -jnp.inf) & enable
        hit = row == jnp.where(ok, idx, -1.0)
        sel = jnp.where(hit, 1.0, sel)
        score = jnp.where(hit, -jnp.inf, score)
    return sel, score


def _nsa_local_kernel(n_sel, n_top, q_ref, kc_ref, vc_ref, ov_ref, wmask_ref, *rest):
    n_wb = WINDOW // TQ + 1
    kw_refs = rest[:n_wb]
    vw_refs = rest[n_wb:2 * n_wb]
    gate_ref, pre_ref, qaug_ref, m_scr, o_acc, imp_acc = rest[2 * n_wb:]
    r_n, hd = NSA_GROUP, HEAD_DIM
    gps = kc_ref.shape[0]
    qi = pl.program_id(1)
    qgs = [q_ref[:, j * r_n * hd:(j + 1) * r_n * hd] for j in range(gps)]
    qss = [jnp.concatenate([qg[:, r * hd:(r + 1) * hd] for r in range(r_n)], axis=0) for qg in qgs]
    rows = r_n * TQ

    nc = kc_ref.shape[1]
    ct = min(2 * LANES, nc)
    t_max = ((qi * TQ + TQ - CMP_BLOCK) // CMP_STRIDE) // ct
    t_full = (jnp.maximum(qi * TQ - (CMP_BLOCK - 1) + CMP_STRIDE, 0) // CMP_STRIDE) // ct

    m_scr[...] = jnp.full(m_scr.shape, M_INIT, F32)
    o_acc[...] = jnp.zeros(o_acc.shape, F32)
    imp_acc[...] = jnp.zeros(imp_acc.shape, F32)

    def cmp_tile(t, masked):
        tile = slice(t * ct, (t + 1) * ct)
        if masked:
            q_off = lax.broadcasted_iota(jnp.int32, (rows, ct), 0) % TQ
            tok = lax.broadcasted_iota(jnp.int32, (rows, ct), 1)
            visible = q_off - CMP_STRIDE * tok >= CMP_STRIDE * ct * t + CMP_BLOCK - 1 - qi * TQ
        for j in range(gps):
            s = _dot_t(qss[j], kc_ref[j, tile, :])
            if masked:
                s = jnp.where(visible, s, MASKED)
            m_prev = m_scr[j]
            m_new = jnp.maximum(m_prev, jnp.max(s, axis=1, keepdims=True))
            alpha = jnp.exp2(m_prev - m_new)
            p = jnp.concatenate([jnp.exp2(s[:, c * LANES:(c + 1) * LANES] - m_new).astype(BF16)
                                 for c in range(ct // LANES)], axis=1)
            o_acc[j] = alpha * o_acc[j] + _dot(p, vc_ref[j, tile, :])
            imp_new = _dot(p, ov_ref[tile, :])
            imp_acc[j] = jnp.concatenate(
                [alpha * imp_acc[j, :, c * LANES:(c + 1) * LANES] + imp_new[:, c * LANES:(c + 1) * LANES]
                 for c in range(imp_new.shape[1] // LANES)], axis=1)
            m_scr[j] = m_new

    for t in range(nc // ct):
        pl.when(t < t_full)(functools.partial(cmp_tile, t, False))
        pl.when((t >= t_full) & (t <= t_max))(functools.partial(cmp_tile, t, True))

    n_prefix = SEL_PREFIXES if n_sel % (SEL_PREFIXES * SUBLANES) == 0 else 1
    prefix = ((qi + 1) * (TQ // SEL_BLOCK) - 1) * n_prefix // n_sel
    for v in range(n_prefix):
        @pl.when(prefix == v)
        def _():
            for j in range(gps):
                _nsa_local_tail(j, n_sel * (v + 1) // n_prefix, v == 0, n_sel, n_top, qi, qgs[j], qss[j],
                                o_acc, imp_acc, wmask_ref, kw_refs, vw_refs, gate_ref, pre_ref, qaug_ref)


def _nsa_local_tail(j, n_rows, first_tile, n_sel, n_top, qi, qg, qs, o_acc, imp_acc, wmask_ref,
                    kw_refs, vw_refs, gate_ref, pre_ref, qaug_ref):
    r_n, hd = NSA_GROUP, HEAD_DIM
    rows = r_n * TQ
    o_aug = o_acc[j]
    l_col = o_aug[:, hd:hd + 1]
    inv_l = jnp.broadcast_to(jnp.where(l_col > 0, 1.0 / l_col, 0.0), (rows, LANES))
    o_cmp = o_aug[:, 0:hd] * inv_l[:, 0:hd]

    n_pad = -(-n_rows // LANES) * LANES
    imp_q = None
    for r in range(r_n):
        rs = slice(r * TQ, (r + 1) * TQ)
        imp_r = jnp.concatenate([imp_acc[j, rs, c * LANES:(c + 1) * LANES] * inv_l[rs]
                                 for c in range(n_pad // LANES)], axis=1)
        imp_q = imp_r if imp_q is None else imp_q + imp_r
    imp = imp_q.T[0:n_rows]
    blk = lax.broadcasted_iota(jnp.int32, (n_rows, TQ), 0)
    blk_q = (qi * TQ + lax.broadcasted_iota(jnp.int32, (n_rows, TQ), 1)) // SEL_BLOCK
    forced = (blk == 0) | (blk == blk_q) | (blk == blk_q - 1)
    score = jnp.where((blk <= blk_q) & jnp.logical_not(forced), imp, -jnp.inf)
    sel = jnp.where(forced, 1.0, 0.0)
    n_forced_min = 3
    blk_q_row = blk_q[0:1, :]
    sel_tiles = []
    for c in range(TQ // LANES):
        ls = slice(c * LANES, (c + 1) * LANES)
        sel_c, score_c = _top_k_axis0(score[:, ls], n_top - n_forced_min, sel[:, ls])
        if first_tile:
            for extra in range(1, n_forced_min):
                sel_c, score_c = _top_k_axis0(score_c, 1, sel_c,
                                              enable=blk_q_row[:, ls] < n_forced_min - extra)
        sel_tiles.append(sel_c)
    sel = jnp.concatenate(sel_tiles, axis=1)
    n_pad = -(-n_rows // LANES) * LANES
    if n_pad > n_rows:
        sel = jnp.concatenate([sel, jnp.zeros((n_pad - n_rows, TQ), F32)], axis=0)
    bias = jnp.where(sel > 0, 0.0, MASKED).T.astype(BF16)
    n_var = qaug_ref.shape[0]
    if n_pad < n_var * BIAS_BLOCKS:
        bias = jnp.concatenate(
            [bias, jnp.full((TQ, n_var * BIAS_BLOCKS - n_pad), MASKED, BF16)], axis=1)
    zeros = jnp.zeros((TQ, 2 * LANES - BIAS_BLOCKS - hd), BF16)
    for v in range(n_var):
        for r in range(r_n):
            rs = slice(r * TQ, (r + 1) * TQ)
            qaug_ref[v, j, 0, rs, 0:BIAS_BLOCKS] = bias[:, v * BIAS_BLOCKS:(v + 1) * BIAS_BLOCKS]
            qaug_ref[v, j, 0, rs, BIAS_BLOCKS:BIAS_BLOCKS + hd] = qg[:, r * hd:(r + 1) * hd]
            qaug_ref[v, j, 0, rs, BIAS_BLOCKS + hd:2 * LANES] = zeros

    kcat = jnp.concatenate([kw[j] for kw in kw_refs], axis=0)
    vcat = jnp.concatenate([vw[j] for vw in vw_refs], axis=0)
    s = _dot_t(qs, kcat)
    wmask = wmask_ref[0]
    s = jnp.concatenate([s[r * TQ:(r + 1) * TQ] + wmask for r in range(r_n)], axis=0)
    m = jnp.broadcast_to(jnp.max(s, axis=1, keepdims=True), (rows, LANES))
    p = jnp.concatenate([jnp.exp2(s[:, c * LANES:(c + 1) * LANES] - m).astype(BF16)
                         for c in range(s.shape[1] // LANES)], axis=1)
    o_aug = _dot(p, vcat)
    o_win = o_aug[:, 0:hd] * (1.0 / o_aug[:, hd:hd + 1])

    gate = gate_ref[j]
    for r in range(r_n):
        rs = slice(r * TQ, (r + 1) * TQ)
        c = (j * r_n + r) * hd
        pre_ref[:, c:c + hd] = (gate[:, 3 * r:3 * r + 1] * o_cmp[rs]
                                + gate[:, 3 * r + 2:3 * r + 3] * o_win[rs])


def _overlap_matrix(nc, n_sel):
    i = np.arange(nc)[:, None]
    j = np.arange(n_sel)[None, :]
    lo = np.maximum(i * CMP_STRIDE, j * SEL_BLOCK)
    hi = np.minimum(i * CMP_STRIDE + CMP_BLOCK - 1, j * SEL_BLOCK + SEL_BLOCK - 1)
    ov = np.zeros((nc, -(-n_sel // LANES) * LANES), np.float32)
    ov[:, :n_sel] = np.clip(hi - lo + 1, 0, None) / CMP_STRIDE
    return jnp.asarray(ov, BF16)


def _window_masks(n_wb):
    q = np.arange(TQ)[:, None]
    c = np.arange(n_wb * TQ)[None, :]
    dist = q + (n_wb - 1) * TQ - c
    masks = []
    for v in range(n_wb):
        kpos = (v - (n_wb - 1)) * TQ + c
        masks.append(np.where((kpos >= 0) & (dist >= 0) & (dist < WINDOW), 0.0, MASKED))
    return jnp.asarray(np.stack(masks), F32)


def _nsa_local(q, kc, vc, kwin, vwin, gates, n_sel, n_var):
    s = q.shape[0]
    g_n, r_n, hd = NSA_KV_HEADS, NSA_GROUP, HEAD_DIM
    n_q = s // TQ
    n_wb = WINDOW // TQ + 1
    n_top = min(SEL_TOPK, n_sel)
    nc = kc.shape[1]
    rows = r_n * TQ
    gps = NSA_LOCAL_GROUPS
    assert n_top >= 3 and nc % min(2 * LANES, nc) == 0 and g_n % gps == 0
    overlap = _overlap_matrix(nc, n_sel)

    def win_spec(j, width):
        return pl.BlockSpec((gps, TQ, width), lambda g, i: (g, jnp.maximum(i - (n_wb - 1) + j, 0), 0))

    return pl.pallas_call(
        functools.partial(_nsa_local_kernel, n_sel, n_top),
        grid=(g_n // gps, n_q),
        in_specs=[pl.BlockSpec((TQ, gps * r_n * hd), lambda g, i: (i, g)),
                  pl.BlockSpec((gps, nc, hd), lambda g, i: (g, 0, 0)),
                  pl.BlockSpec((gps, nc, LANES), lambda g, i: (g, 0, 0)),
                  pl.BlockSpec(overlap.shape, lambda g, i: (0, 0)),
                  pl.BlockSpec((1, TQ, n_wb * TQ), lambda g, i: (jnp.minimum(i, n_wb - 1), 0, 0))]
        + [win_spec(j, hd) for j in range(n_wb)] + [win_spec(j, LANES) for j in range(n_wb)]
        + [pl.BlockSpec((gps, TQ, LANES), lambda g, i: (g, i, 0))],
        out_specs=[pl.BlockSpec((TQ, gps * r_n * hd), lambda g, i: (i, g)),
                   pl.BlockSpec((n_var, gps, 1, rows, 2 * LANES), lambda g, i: (0, g, i, 0, 0))],
        out_shape=[jax.ShapeDtypeStruct((s, N_HEADS * hd), F32),
                   jax.ShapeDtypeStruct((n_var, g_n, n_q, rows, 2 * LANES), BF16)],
        scratch_shapes=[pltpu.VMEM((gps, rows, LANES), F32), pltpu.VMEM((gps, rows, LANES), F32),
                        pltpu.VMEM((gps, rows, overlap.shape[1]), F32)],
        name="nsa_local",
        compiler_params=_cparams("parallel", "parallel"),
    )(q, kc, vc, overlap, _window_masks(n_wb),
      *([kwin] * n_wb), *([vwin] * n_wb), gates)


def _causal_pairs(n_q, tq, tk):
    qs, ks = [], []
    for qi in range(n_q):
        for ki in range(((qi + 1) * tq - 1) // tk + 1):
            qs.append(qi)
            ks.append(ki)
    return jnp.asarray(qs, jnp.int32), jnp.asarray(ks, jnp.int32)


def _flash_scores(q, k, causal_mask):
    s = _dot_t(q, k)
    return s if causal_mask is None else jnp.where(causal_mask, s, MASKED)


def _flash_accumulate(s, v, m_ref, acc_ref, idx, row0):
    m_prev = m_ref[idx, row0:, :]
    m_new = jnp.maximum(m_prev, jnp.max(s, axis=1, keepdims=True))
    p = jnp.concatenate([jnp.exp2(s[:, c * LANES:(c + 1) * LANES] - m_new).astype(BF16)
                         for c in range(s.shape[1] // LANES)], axis=1)
    acc_ref[idx, row0:, :] = jnp.exp2(m_prev - m_new) * acc_ref[idx, row0:, :] + _dot(p, v)
    m_ref[idx, row0:, :] = m_new


def _flash_steps(qi, ki, tq, tk, rows_per_q, n_batch, operands, m_ref, acc_ref, finalize):
    first_diag = (qi * tq) // tk
    last_k = ((qi + 1) * tq - 1) // tk
    assert rows_per_q == 1 or tq <= tk

    @pl.when(ki == 0)
    def _():
        m_ref[...] = jnp.full(m_ref.shape, M_INIT, F32)
        acc_ref[...] = jnp.zeros(acc_ref.shape, F32)

    def run(masked, row0, n_keys):
        mask = None
        if masked:
            shape = (rows_per_q * tq - row0, n_keys)
            q_off = lax.broadcasted_iota(jnp.int32, shape, 0) % tq + row0
            k_off = lax.broadcasted_iota(jnp.int32, shape, 1)
            mask = q_off - k_off >= ki * tk - qi * tq
        def scores(b):
            q, k, _ = operands(b, row0, n_keys)
            return _flash_scores(q, k, mask)

        ahead = [scores(b) for b in range(min(QK_AHEAD, n_batch))]
        for b in range(n_batch):
            s = ahead.pop(0)
            if b + QK_AHEAD < n_batch:
                ahead.append(scores(b + QK_AHEAD))
            _flash_accumulate(s, operands(b, row0, n_keys)[2], m_ref, acc_ref, b, row0)

    @pl.when(ki < first_diag)
    def _():
        run(False, 0, tk)

    if tq >= tk:
        variants = [(ki - first_diag == j, j * tk, tk) for j in range(tq // tk)]
    else:
        variants = [(qi % (tk // tq) == j, 0, (j + 1) * tq) for j in range(tk // tq)]
    for cond, row0, n_keys in variants:
        @pl.when((ki >= first_diag) & cond)
        def _():
            run(True, row0, n_keys)

    @pl.when(ki == last_k)
    def _():
        finalize()


def _nsa_sel_kernel(qtab, ktab, qaug_ref, kaug_ref, vaug_ref, pre_ref, gate_ref, o_ref,
                    m_ref, acc_ref):
    g_n, r_n, hd = NSA_KV_HEADS, NSA_GROUP, HEAD_DIM
    t = pl.program_id(0)
    qi, ki = qtab[t], ktab[t]

    def operands(g, row0, n_keys):
        return (qaug_ref[0, g, 0, row0:, :], kaug_ref[0:n_keys, g * 2 * LANES:(g + 1) * 2 * LANES],
                vaug_ref[0:n_keys, g * LANES:(g + 1) * LANES])

    def finalize():
        for g in range(g_n):
            acc = acc_ref[g]
            gate = gate_ref[g]
            o = acc[:, 0:hd] * (1.0 / acc[:, hd:hd + 1])
            for r in range(r_n):
                c = (g * r_n + r) * hd
                o_ref[:, c:c + hd] = (pre_ref[:, c:c + hd] + gate[:, 3 * r + 1:3 * r + 2]
                                      * o[r * TQ:(r + 1) * TQ]).astype(o_ref.dtype)

    _flash_steps(qi, ki, TQ, TK, r_n, g_n, operands, m_ref, acc_ref, finalize)


def _nsa_sel(qaug, kaug, vaug, pre, gates):
    n_var, g_n, n_q, rows, _ = qaug.shape
    s = kaug.shape[0]
    qtab, ktab = _causal_pairs(n_q, TQ, TK)
    tiles_per_var = BIAS_BLOCKS * SEL_BLOCK // TK
    grid_spec = pltpu.PrefetchScalarGridSpec(
        num_scalar_prefetch=2,
        grid=(qtab.shape[0],),
        in_specs=[
            pl.BlockSpec((1, g_n, 1, rows, 2 * LANES),
                         lambda t, qt, kt: (kt[t] // tiles_per_var, 0, qt[t], 0, 0)),
            pl.BlockSpec((TK, g_n * 2 * LANES), lambda t, qt, kt: (kt[t], 0)),
            pl.BlockSpec((TK, g_n * LANES), lambda t, qt, kt: (kt[t], 0)),
            pl.BlockSpec((TQ, N_HEADS * HEAD_DIM), lambda t, qt, kt: (qt[t], 0)),
            pl.BlockSpec((g_n, TQ, LANES), lambda t, qt, kt: (0, qt[t], 0)),
        ],
        out_specs=pl.BlockSpec((TQ, N_HEADS * HEAD_DIM), lambda t, qt, kt: (qt[t], 0)),
        scratch_shapes=[pltpu.VMEM((g_n, rows, LANES), F32), pltpu.VMEM((g_n, rows, LANES), F32)],
    )
    return pl.pallas_call(
        _nsa_sel_kernel,
        grid_spec=grid_spec,
        out_shape=jax.ShapeDtypeStruct((s, N_HEADS * HEAD_DIM), BF16),
        name="nsa_sel",
        compiler_params=_cparams("arbitrary"),
    )(qtab, ktab, qaug, kaug, vaug, pre, gates)


def _moba_proj_kernel(k_eff, x_ref, cos_ref, sa_ref, sb_ref, w_ref,
                      qaug_ref, kaug_ref, vaug_ref, kmean_ref):
    tm = x_ref.shape[0]
    hd, d = HEAD_DIM, N_HEADS * HEAD_DIM
    i = pl.program_id(0)

    @pl.when(i == 0)
    def _():
        kmean_ref[...] = jnp.zeros(kmean_ref.shape, F32)

    xb = x_ref[...].astype(BF16)
    cos_t, sin_a, sin_b = cos_ref[...], sa_ref[...], sb_ref[...]
    q = _rope(_dot(xb, w_ref[:, 0:d]), cos_t, sin_a, sin_b)
    k = _rope(_dot(xb, w_ref[:, d:2 * d]), cos_t, sin_a, sin_b)
    v = _dot(xb, w_ref[:, 2 * d:3 * d])
    kmean = kmean_ref[...]
    nb = kmean.shape[0]
    own = jnp.where(lax.broadcasted_iota(jnp.int32, (1, nb), 1) == i, 1.0, 0.0).astype(BF16)
    own_onehot = jnp.broadcast_to(own, (tm, nb))
    ones_col = _ones_col(tm, LANES - hd)
    blk_t = lax.broadcasted_iota(jnp.int32, (nb, tm), 0)
    own_t = jnp.where(blk_t == i, 1.0, 0.0)
    for hp in range(N_HEADS // 2):
        bias_t = []
        for h in (2 * hp, 2 * hp + 1):
            sl = slice(h * hd, (h + 1) * hd)
            gate_t = _dot_t(kmean[:, sl], q[:, sl], precision=lax.Precision.HIGHEST)
            gate_t = jnp.where(blk_t < i, gate_t, -jnp.inf)
            bias_t.append(jnp.where(_top_k_axis0(gate_t, k_eff, own_t)[0] > 0, 0.0, MASKED))
        bias = jnp.concatenate(bias_t, axis=0).T.astype(BF16)
        for j, h in enumerate((2 * hp, 2 * hp + 1)):
            sl = slice(h * hd, (h + 1) * hd)
            qaug_ref[h, :, 0:hd] = (q[:, sl] * SCALE).astype(BF16)
            qaug_ref[h, :, hd:hd + nb] = bias[:, j * nb:(j + 1) * nb]
            kaug_ref[h, :, 0:hd] = k[:, sl].astype(BF16)
            kaug_ref[h, :, hd:hd + nb] = own_onehot
            vaug_ref[h, :, 0:hd] = v[:, sl].astype(BF16)
            vaug_ref[h, :, hd:LANES] = ones_col
    kmean_ref[pl.ds(i, 1), :] = jnp.mean(k, axis=0, keepdims=True)


def _moba_proj(x, tabs, w):
    s, d = x.shape
    n_blk = s // MOBA_BLOCK
    k_eff = min(MOBA_TOPK, max(n_blk - 1, 1))
    nb = LANES - HEAD_DIM
    assert s % MOBA_BLOCK == 0 and n_blk <= nb
    tm = MOBA_BLOCK
    row = lambda i: (i, 0)
    head = jax.ShapeDtypeStruct((N_HEADS, s, LANES), BF16)
    head_spec = pl.BlockSpec((N_HEADS, tm, LANES), lambda i: (0, i, 0))
    return pl.pallas_call(
        functools.partial(_moba_proj_kernel, k_eff),
        grid=(s // tm,),
        in_specs=[pl.BlockSpec((tm, d), row)] + [pl.BlockSpec((tm, LANES), row)] * 3
        + [pl.BlockSpec(w.shape, lambda i: (0, 0))],
        out_specs=[head_spec] * 3,
        out_shape=[head] * 3,
        scratch_shapes=[pltpu.VMEM((nb, N_HEADS * HEAD_DIM), F32)],
        name="moba_proj",
        compiler_params=_cparams("arbitrary"),
    )(x, *tabs, w)


def _moba_attn_kernel(qtab, ktab, q_ref, k_ref, v_ref, o_ref, m_ref, acc_ref):
    hd = HEAD_DIM
    t = pl.program_id(0)
    qi, ki = qtab[t], ktab[t]

    def operands(h, row0, n_keys):
        return q_ref[h, row0:, :], k_ref[h, 0:n_keys, :], v_ref[h, 0:n_keys, :]

    def finalize():
        for h in range(N_HEADS):
            acc = acc_ref[h]
            o_ref[:, h * hd:(h + 1) * hd] = (acc[:, 0:hd] * (1.0 / acc[:, hd:hd + 1])).astype(o_ref.dtype)

    _flash_steps(qi, ki, q_ref.shape[1], k_ref.shape[1], 1, N_HEADS, operands, m_ref, acc_ref, finalize)


def _moba_attn(qaug, kaug, vaug):
    h_n, s, _ = qaug.shape
    tq, tk = min(MOBA_TQ, s), min(MOBA_TK, s)
    assert s % tq == 0 and s % tk == 0
    qtab, ktab = _causal_pairs(s // tq, tq, tk)
    grid_spec = pltpu.PrefetchScalarGridSpec(
        num_scalar_prefetch=2,
        grid=(qtab.shape[0],),
        in_specs=[pl.BlockSpec((h_n, tq, LANES), lambda t, qt, kt: (0, qt[t], 0)),
                  pl.BlockSpec((h_n, tk, LANES), lambda t, qt, kt: (0, kt[t], 0)),
                  pl.BlockSpec((h_n, tk, LANES), lambda t, qt, kt: (0, kt[t], 0))],
        out_specs=pl.BlockSpec((tq, h_n * HEAD_DIM), lambda t, qt, kt: (qt[t], 0)),
        scratch_shapes=[pltpu.VMEM((h_n, tq, LANES), F32), pltpu.VMEM((h_n, tq, LANES), F32)],
    )
    return pl.pallas_call(
        _moba_attn_kernel,
        grid_spec=grid_spec,
        out_shape=jax.ShapeDtypeStruct((s, h_n * HEAD_DIM), BF16),
        name="moba_attn",
        compiler_params=_cparams("arbitrary"),
    )(qtab, ktab, qaug, kaug, vaug)


def _layer_norm(z, gain, bias):
    mu = jnp.mean(z, axis=-1, keepdims=True)
    zc = z - mu
    var = jnp.mean(zc * zc, axis=-1, keepdims=True)
    return zc * lax.rsqrt(var + LN_EPS) * gain + bias


def _post_mixer_kernel(alpha, n_chunk, a_ref, x_ref, wo_ref, g1_ref, b1_ref, wg_ref, wu_ref, wd_ref,
                       g2_ref, b2_ref, o_ref):
    x1 = _layer_norm(alpha * x_ref[...] + _dot(a_ref[...], wo_ref[...]), g1_ref[...], b1_ref[...])
    xb = x1.astype(BF16)
    cw = wg_ref.shape[1] // n_chunk
    f = None
    for c in range(n_chunk):
        sl = slice(c * cw, (c + 1) * cw)
        u = (jax.nn.silu(_dot(xb, wg_ref[:, sl])) * _dot(xb, wu_ref[:, sl])).astype(BF16)
        part = _dot(u, wd_ref[sl, :])
        f = part if f is None else f + part
    o_ref[...] = _layer_norm(alpha * x1 + f, g2_ref[...], b2_ref[...])


def _post_mixer(a, x, w_o, g1, b1, wg, wu, wd, g2, b2, alpha):
    s, d = x.shape
    f = wg.shape[1]
    n_chunk = 2 if f % (2 * LANES) == 0 else 1
    tm = min(MM_TILE, s)
    row = lambda i: (i, 0)

    def resident(shape):
        return pl.BlockSpec(shape, lambda i: (0, 0), pipeline_mode=pl.Buffered(1))

    vec = lambda v: v.reshape(1, d)
    return pl.pallas_call(
        functools.partial(_post_mixer_kernel, alpha, n_chunk),
        grid=(s // tm,),
        in_specs=[pl.BlockSpec((tm, a.shape[1]), row), pl.BlockSpec((tm, d), row),
                  resident(w_o.shape), resident((1, d)), resident((1, d)),
                  resident(wg.shape), resident(wu.shape), resident(wd.shape),
                  resident((1, d)), resident((1, d))],
        out_specs=pl.BlockSpec((tm, d), row),
        out_shape=jax.ShapeDtypeStruct((s, d), F32),
        name="post_mixer",
        compiler_params=_cparams("parallel"),
    )(a, x, w_o, vec(g1), vec(b1), wg, wu, wd, vec(g2), vec(b2))


def _nsa_mixer(x, tabs, w_in, ck_pos, ck_w1, ck_w2, cv_pos, cv_w1, cv_w2):
    s = x.shape[0]
    g_n, r_n, hd = NSA_KV_HEADS, NSA_GROUP, HEAD_DIM
    q_dim, kv_dim = N_HEADS * hd, NSA_KV_HEADS * hd
    n_sel = s // SEL_BLOCK
    n_var = -(-n_sel // BIAS_BLOCKS)
    assert s % TK == 0 and (BIAS_BLOCKS * SEL_BLOCK) % TK == 0
    wq = w_in[:, :q_dim].astype(BF16)
    wkv = w_in[:, q_dim:q_dim + 6 * kv_dim].astype(BF16)
    wg = w_in[:, q_dim + 6 * kv_dim:].reshape(-1, g_n, 3 * r_n)
    wg = jnp.pad(wg, ((0, 0), (0, 0), (0, LANES - 3 * r_n))).reshape(-1, g_n * LANES).astype(BF16)
    q, cmp_kv, kaug, vaug, kwin, vwin, gates = _nsa_proj(x, tabs, wq, wkv, wg)
    comp = _compress(cmp_kv, jnp.stack([ck_pos, cv_pos]),
                     jnp.stack([ck_w1, cv_w1]).astype(BF16), jnp.stack([ck_w2, cv_w2]).astype(BF16))
    kc = comp[0].astype(BF16)
    ones = jnp.ones(comp.shape[1:3] + (1,), F32)
    zeros = jnp.zeros(comp.shape[1:3] + (LANES - hd - 1,), F32)
    vc = jnp.concatenate([comp[1], ones, zeros], axis=-1).astype(BF16)
    pre, qaug = _nsa_local(q, kc, vc, kwin, vwin, gates, n_sel, n_var)
    return _nsa_sel(qaug, kaug, vaug, pre, gates)


def _moba_mixer(x, tabs, w_in):
    qaug, kaug, vaug = _moba_proj(x, tabs, w_in.astype(BF16))
    return _moba_attn(qaug, kaug, vaug)


def kernel(x, positions, nsa_w_in, nsa_w_o, nsa_ck_pos, nsa_ck_w1, nsa_ck_w2, nsa_cv_pos, nsa_cv_w1, nsa_cv_w2, moba_w_in, moba_w_o, ffn_wg, ffn_wu, ffn_wd, ln1_g, ln1_b, ln2_g, ln2_b):
    b, s, d = x.shape
    depth = ffn_wg.shape[0]
    alpha = (2 * depth) ** 0.25
    outs = []
    for bi in range(b):
        h = x[bi]
        tabs = _rope_tables(positions[bi])
        for i in range(depth):
            j = i // 2
            if i % 2 == 0:
                a = _nsa_mixer(h, tabs, nsa_w_in[j], nsa_ck_pos[j], nsa_ck_w1[j], nsa_ck_w2[j],
                               nsa_cv_pos[j], nsa_cv_w1[j], nsa_cv_w2[j])
                w_o = nsa_w_o[j]
            else:
                a = _moba_mixer(h, tabs, moba_w_in[j])
                w_o = moba_w_o[j]
            h = _post_mixer(a, h, w_o.astype(BF16), ln1_g[i], ln1_b[i], ffn_wg[i].astype(BF16),
                            ffn_wu[i].astype(BF16), ffn_wd[i].astype(BF16), ln2_g[i], ln2_b[i], alpha)
        outs.append(h)
    return jnp.stack(outs)
```

```python
import functools

import numpy as np
import jax
import jax.numpy as jnp
from jax import lax
from jax.experimental import pallas as pl
from jax.experimental.pallas import tpu as pltpu

F32 = jnp.float32
BF16 = jnp.bfloat16

N_HEADS = 16
HEAD_DIM = 64
ROPE_DIM = HEAD_DIM // 4
ROPE_THETA = 500000.0
NSA_KV_HEADS = 4
NSA_GROUP = N_HEADS // NSA_KV_HEADS
CMP_BLOCK = 32
CMP_STRIDE = 16
SEL_BLOCK = 64
SEL_TOPK = 16
WINDOW = 512
MOBA_BLOCK = 256
MOBA_TOPK = 3
LN_EPS = 1e-5
SCALE = HEAD_DIM ** -0.5 * float(np.log2(np.e))

LANES = 128
SUBLANES = 8
VMEM_LIMIT_BYTES = 52 * 1024 * 1024

ROW_TILE = 256
MM_TILE = 512
TQ = 256
TK = 2048
MOBA_TQ = 1024
MOBA_TK = 512
QK_AHEAD = 1
SEL_PREFIXES = 4
NSA_LOCAL_GROUPS = 1
BIAS_BLOCKS = 128
MASKED = -1e30
M_INIT = -5e29

assert WINDOW % TQ == 0 and TK % TQ == 0 and ROW_TILE == MOBA_BLOCK


def _cparams(*sem):
    return pltpu.CompilerParams(dimension_semantics=sem, vmem_limit_bytes=VMEM_LIMIT_BYTES)


def _dot(a, b):
    return jnp.dot(a, b, preferred_element_type=F32)


def _dot_t(a, b, precision=None):
    return lax.dot_general(a, b, (((1,), (1,)), ((), ())), preferred_element_type=F32,
                           precision=precision)


def _rope_tables_kernel(pos_ref, c_ref, cos_ref, sa_ref, sb_ref):
    ang = pos_ref[...].astype(F32) * c_ref[0:1, :]
    c = jnp.cos(ang)
    s = jnp.sin(ang)
    cos_ref[...] = jnp.where(c_ref[1:2, :] > 0, c, 1.0)
    sa_ref[...] = jnp.where(c_ref[2:3, :] > 0, s, 0.0)
    sb_ref[...] = jnp.where(c_ref[3:4, :] > 0, -s, 0.0)


def _rope_tables(positions):
    s = positions.shape[0]
    half = ROPE_DIM // 2
    inv = ROPE_THETA ** (-jnp.arange(0, ROPE_DIM, 2, dtype=F32) / ROPE_DIM)
    d = np.arange(LANES) % HEAD_DIM
    consts = jnp.zeros((SUBLANES, LANES), F32)
    consts = consts.at[0].set(inv[d % half])
    consts = consts.at[1].set(jnp.asarray(d < ROPE_DIM, F32))
    consts = consts.at[2].set(jnp.asarray((d >= half) & (d < ROPE_DIM), F32))
    consts = consts.at[3].set(jnp.asarray(d < half, F32))
    tm = min(MM_TILE, s)
    tab = jax.ShapeDtypeStruct((s, LANES), F32)
    return pl.pallas_call(
        _rope_tables_kernel,
        grid=(s // tm,),
        in_specs=[pl.BlockSpec((tm, 1), lambda i: (i, 0)),
                  pl.BlockSpec((SUBLANES, LANES), lambda i: (0, 0))],
        out_specs=[pl.BlockSpec((tm, LANES), lambda i: (i, 0))] * 3,
        out_shape=[tab, tab, tab],
        name="rope_tables",
        compiler_params=_cparams("parallel"),
    )(positions.reshape(s, 1), consts)


def _rope(x, cos_t, sin_a, sin_b):
    half = ROPE_DIM // 2
    outs = []
    for c in range(x.shape[1] // LANES):
        xs = x[:, c * LANES:(c + 1) * LANES]
        outs.append(xs * cos_t + pltpu.roll(xs, half, 1) * sin_a
                    + pltpu.roll(xs, LANES - half, 1) * sin_b)
    return outs[0] if len(outs) == 1 else jnp.concatenate(outs, axis=1)


def _ones_col(rows, width):
    return jnp.where(lax.broadcasted_iota(jnp.int32, (rows, width), 1) == 0, 1.0, 0.0).astype(BF16)


def _nsa_proj_kernel(x_ref, cos_ref, sa_ref, sb_ref, wq_ref, wkv_ref, wg_ref,
                     q_ref, cmp_ref, kaug_ref, vaug_ref, kwin_ref, vwin_ref, gate_ref):
    tm = x_ref.shape[0]
    g_n, hd, kvd = NSA_KV_HEADS, HEAD_DIM, NSA_KV_HEADS * HEAD_DIM
    xb = x_ref[...].astype(BF16)
    cos_t, sin_a, sin_b = cos_ref[...], sa_ref[...], sb_ref[...]
    q = _rope(_dot(xb, wq_ref[...]), cos_t, sin_a, sin_b) * SCALE
    q_ref[...] = q.astype(BF16)
    kv = _dot(xb, wkv_ref[...])
    k_cmp = _rope(kv[:, 0:kvd], cos_t, sin_a, sin_b)
    v_cmp = kv[:, kvd:2 * kvd]
    k_sel = _rope(kv[:, 2 * kvd:3 * kvd], cos_t, sin_a, sin_b)
    v_sel = kv[:, 3 * kvd:4 * kvd]
    k_win = _rope(kv[:, 4 * kvd:5 * kvd], cos_t, sin_a, sin_b)
    v_win = kv[:, 5 * kvd:6 * kvd]
    gate = jax.nn.sigmoid(_dot(xb, wg_ref[...]))
    tok = pl.program_id(0) * tm + lax.broadcasted_iota(jnp.int32, (tm, BIAS_BLOCKS), 0)
    blk = (tok // SEL_BLOCK) % BIAS_BLOCKS
    onehot = jnp.where(lax.broadcasted_iota(jnp.int32, (tm, BIAS_BLOCKS), 1) == blk,
                       1.0, 0.0).astype(BF16)
    ones_col = _ones_col(tm, LANES - hd)
    for g in range(g_n):
        sl = slice(g * hd, (g + 1) * hd)
        cmp_ref[0, g] = k_cmp[:, sl]
        cmp_ref[1, g] = v_cmp[:, sl]
        kwin_ref[g] = k_win[:, sl].astype(BF16)
        vwin_ref[g, :, 0:hd] = v_win[:, sl].astype(BF16)
        vwin_ref[g, :, hd:LANES] = ones_col
        gate_ref[g] = gate[:, g * LANES:(g + 1) * LANES]
        kb = g * 2 * LANES
        kaug_ref[:, kb:kb + BIAS_BLOCKS] = onehot
        kaug_ref[:, kb + BIAS_BLOCKS:kb + BIAS_BLOCKS + hd] = k_sel[:, sl].astype(BF16)
        kaug_ref[:, kb + BIAS_BLOCKS + hd:kb + 2 * LANES] = jnp.zeros((tm, 2 * LANES - BIAS_BLOCKS - hd), BF16)
        vb = g * LANES
        vaug_ref[:, vb:vb + hd] = v_sel[:, sl].astype(BF16)
        vaug_ref[:, vb + hd:vb + LANES] = ones_col


def _nsa_proj(x, tabs, wq, wkv, wg):
    s, d = x.shape
    g_n, hd = NSA_KV_HEADS, HEAD_DIM
    tm = min(ROW_TILE, s)
    row = lambda i: (i, 0)
    whole = lambda i: (0, 0)
    out_shape = [
        jax.ShapeDtypeStruct((s, N_HEADS * hd), BF16),
        jax.ShapeDtypeStruct((2, g_n, s, hd), F32),
        jax.ShapeDtypeStruct((s, g_n * 2 * LANES), BF16),
        jax.ShapeDtypeStruct((s, g_n * LANES), BF16),
        jax.ShapeDtypeStruct((g_n, s, hd), BF16),
        jax.ShapeDtypeStruct((g_n, s, LANES), BF16),
        jax.ShapeDtypeStruct((g_n, s, LANES), F32),
    ]
    out_specs = [
        pl.BlockSpec((tm, N_HEADS * hd), row),
        pl.BlockSpec((2, g_n, tm, hd), lambda i: (0, 0, i, 0)),
        pl.BlockSpec((tm, g_n * 2 * LANES), row),
        pl.BlockSpec((tm, g_n * LANES), row),
        pl.BlockSpec((g_n, tm, hd), lambda i: (0, i, 0)),
        pl.BlockSpec((g_n, tm, LANES), lambda i: (0, i, 0)),
        pl.BlockSpec((g_n, tm, LANES), lambda i: (0, i, 0)),
    ]
    return pl.pallas_call(
        _nsa_proj_kernel,
        grid=(s // tm,),
        in_specs=[pl.BlockSpec((tm, d), row)] + [pl.BlockSpec((tm, LANES), row)] * 3
        + [pl.BlockSpec(wq.shape, whole), pl.BlockSpec(wkv.shape, whole),
           pl.BlockSpec(wg.shape, whole)],
        out_specs=out_specs,
        out_shape=out_shape,
        name="nsa_proj",
        compiler_params=_cparams("parallel"),
    )(x, *tabs, wq, wkv, wg)


def _compress_kernel(x_ref, pos_ref, w1_ref, w2_ref, out_ref):
    n16 = x_ref.shape[2]
    half = w1_ref.shape[1] // 2
    x = x_ref[0, 0]
    a = _dot((x + pos_ref[0, 0:1, :]).astype(BF16), w1_ref[0, 0:half, :])
    b = _dot((x + pos_ref[0, 1:2, :]).astype(BF16), w1_ref[0, half:, :])
    hid = a + pltpu.roll(b, n16 - 1, 0)
    out_ref[0, 0] = _dot(jax.nn.gelu(hid).astype(BF16), w2_ref[0])


def _compress(cmp_kv, pos, w1, w2):
    _, g_n, s, hd = cmp_kv.shape
    n16 = s // CMP_STRIDE
    x = cmp_kv.reshape(2, g_n, n16, CMP_STRIDE * hd)
    pos2 = pos.reshape(2, 2, CMP_STRIDE * hd)
    return pl.pallas_call(
        _compress_kernel,
        grid=(2, g_n),
        in_specs=[pl.BlockSpec((1, 1, n16, CMP_STRIDE * hd), lambda a, g: (a, g, 0, 0)),
                  pl.BlockSpec((1, 2, CMP_STRIDE * hd), lambda a, g: (a, 0, 0)),
                  pl.BlockSpec((1,) + w1.shape[1:], lambda a, g: (a, 0, 0)),
                  pl.BlockSpec((1,) + w2.shape[1:], lambda a, g: (a, 0, 0))],
        out_specs=pl.BlockSpec((1, 1, n16, hd), lambda a, g: (a, g, 0, 0)),
        out_shape=jax.ShapeDtypeStruct((2, g_n, n16, hd), F32),
        name="nsa_compress",
        compiler_params=_cparams("parallel", "parallel"),
    )(x, pos2, w1, w2)


def _top_k_axis0(score, n_rounds, sel, enable=None):
    n = score.shape[0]
    row = lax.broadcasted_iota(jnp.int32, score.shape, 0).astype(F32)
    for _ in range(n_rounds):
        m = jnp.max(score, axis=0, keepdims=True)
        idx = jnp.min(jnp.where(score == m, row, float(n)), axis=0, keepdims=True)
        ok = m > -jnp.inf if enable is None else (m > -jnp.inf) & enable
        hit = row == jnp.where(ok, idx, -1.0)
        sel = jnp.where(hit, 1.0, sel)
        score = jnp.where(hit, -jnp.inf, score)
    return sel, score


def _nsa_local_kernel(n_sel, n_top, q_ref, kc_ref, vc_ref, ov_ref, wmask_ref, *rest):
    n_wb = WINDOW // TQ + 1
    kw_refs = rest[:n_wb]
    vw_refs = rest[n_wb:2 * n_wb]
    gate_ref, pre_ref, qaug_ref, m_scr, o_acc, imp_acc = rest[2 * n_wb:]
    r_n, hd = NSA_GROUP, HEAD_DIM
    gps = kc_ref.shape[0]
    qi = pl.program_id(1)
    qgs = [q_ref[:, j * r_n * hd:(j + 1) * r_n * hd] for j in range(gps)]
    qss = [jnp.concatenate([qg[:, r * hd:(r + 1) * hd] for r in range(r_n)], axis=0) for qg in qgs]
    rows = r_n * TQ

    nc = kc_ref.shape[1]
    ct = min(2 * LANES, nc)
    t_max = ((qi * TQ + TQ - CMP_BLOCK) // CMP_STRIDE) // ct
    t_full = (jnp.maximum(qi * TQ - (CMP_BLOCK - 1) + CMP_STRIDE, 0) // CMP_STRIDE) // ct

    m_scr[...] = jnp.full(m_scr.shape, M_INIT, F32)
    o_acc[...] = jnp.zeros(o_acc.shape, F32)
    imp_acc[...] = jnp.zeros(imp_acc.shape, F32)

    def cmp_tile(t, masked):
        tile = slice(t * ct, (t + 1) * ct)
        if masked:
            q_off = lax.broadcasted_iota(jnp.int32, (rows, ct), 0) % TQ
            tok = lax.broadcasted_iota(jnp.int32, (rows, ct), 1)
            visible = q_off - CMP_STRIDE * tok >= CMP_STRIDE * ct * t + CMP_BLOCK - 1 - qi * TQ
        for j in range(gps):
            s = _dot_t(qss[j], kc_ref[j, tile, :])
            if masked:
                s = jnp.where(visible, s, MASKED)
            m_prev = m_scr[j]
            m_new = jnp.maximum(m_prev, jnp.max(s, axis=1, keepdims=True))
            alpha = jnp.exp2(m_prev - m_new)
            p = jnp.concatenate([jnp.exp2(s[:, c * LANES:(c + 1) * LANES] - m_new).astype(BF16)
                                 for c in range(ct // LANES)], axis=1)
            o_acc[j] = alpha * o_acc[j] + _dot(p, vc_ref[j, tile, :])
            imp_new = _dot(p, ov_ref[tile, :])
            imp_acc[j] = jnp.concatenate(
                [alpha * imp_acc[j, :, c * LANES:(c + 1) * LANES] + imp_new[:, c * LANES:(c + 1) * LANES]
                 for c in range(imp_new.shape[1] // LANES)], axis=1)
            m_scr[j] = m_new

    for t in range(nc // ct):
        pl.when(t < t_full)(functools.partial(cmp_tile, t, False))
        pl.when((t >= t_full) & (t <= t_max))(functools.partial(cmp_tile, t, True))

    n_prefix = SEL_PREFIXES if n_sel % (SEL_PREFIXES * SUBLANES) == 0 else 1
    prefix = ((qi + 1) * (TQ // SEL_BLOCK) - 1) * n_prefix // n_sel
    for v in range(n_prefix):
        @pl.when(prefix == v)
        def _():
            for j in range(gps):
                _nsa_local_tail(j, n_sel * (v + 1) // n_prefix, v == 0, n_sel, n_top, qi, qgs[j], qss[j],
                                o_acc, imp_acc, wmask_ref, kw_refs, vw_refs, gate_ref, pre_ref, qaug_ref)


def _nsa_local_tail(j, n_rows, first_tile, n_sel, n_top, qi, qg, qs, o_acc, imp_acc, wmask_ref,
                    kw_refs, vw_refs, gate_ref, pre_ref, qaug_ref):
    r_n, hd = NSA_GROUP, HEAD_DIM
    rows = r_n * TQ
    o_aug = o_acc[j]
    l_col = o_aug[:, hd:hd + 1]
    inv_l = jnp.broadcast_to(jnp.where(l_col > 0, 1.0 / l_col, 0.0), (rows, LANES))
    o_cmp = o_aug[:, 0:hd] * inv_l[:, 0:hd]

    n_pad = -(-n_rows // LANES) * LANES
    imp_q = None
    for r in range(r_n):
        rs = slice(r * TQ, (r + 1) * TQ)
        imp_r = jnp.concatenate([imp_acc[j, rs, c * LANES:(c + 1) * LANES] * inv_l[rs]
                                 for c in range(n_pad // LANES)], axis=1)
        imp_q = imp_r if imp_q is None else imp_q + imp_r
    imp = imp_q.T[0:n_rows]
    blk = lax.broadcasted_iota(jnp.int32, (n_rows, TQ), 0)
    blk_q = (qi * TQ + lax.broadcasted_iota(jnp.int32, (n_rows, TQ), 1)) // SEL_BLOCK
    forced = (blk == 0) | (blk == blk_q) | (blk == blk_q - 1)
    score = jnp.where((blk <= blk_q) & jnp.logical_not(forced), imp, -jnp.inf)
    sel = jnp.where(forced, 1.0, 0.0)
    n_forced_min = 3
    blk_q_row = blk_q[0:1, :]
    sel_tiles = []
    for c in range(TQ // LANES):
        ls = slice(c * LANES, (c + 1) * LANES)
        sel_c, score_c = _top_k_axis0(score[:, ls], n_top - n_forced_min, sel[:, ls])
        if first_tile:
            for extra in range(1, n_forced_min):
                sel_c, score_c = _top_k_axis0(score_c, 1, sel_c,
                                              enable=blk_q_row[:, ls] < n_forced_min - extra)
        sel_tiles.append(sel_c)
    sel = jnp.concatenate(sel_tiles, axis=1)
    n_pad = -(-n_rows // LANES) * LANES
    if n_pad > n_rows:
        sel = jnp.concatenate([sel, jnp.zeros((n_pad - n_rows, TQ), F32)], axis=0)
    bias = jnp.where(sel > 0, 0.0, MASKED).T.astype(BF16)
    n_var = qaug_ref.shape[0]
    if n_pad < n_var * BIAS_BLOCKS:
        bias = jnp.concatenate(
            [bias, jnp.full((TQ, n_var * BIAS_BLOCKS - n_pad), MASKED, BF16)], axis=1)
    zeros = jnp.zeros((TQ, 2 * LANES - BIAS_BLOCKS - hd), BF16)
    for v in range(n_var):
        for r in range(r_n):
            rs = slice(r * TQ, (r + 1) * TQ)
            qaug_ref[v, j, 0, rs, 0:BIAS_BLOCKS] = bias[:, v * BIAS_BLOCKS:(v + 1) * BIAS_BLOCKS]
            qaug_ref[v, j, 0, rs, BIAS_BLOCKS:BIAS_BLOCKS + hd] = qg[:, r * hd:(r + 1) * hd]
            qaug_ref[v, j, 0, rs, BIAS_BLOCKS + hd:2 * LANES] = zeros

    kcat = jnp.concatenate([kw[j] for kw in kw_refs], axis=0)
    vcat = jnp.concatenate([vw[j] for vw in vw_refs], axis=0)
    s = _dot_t(qs, kcat)
    wmask = wmask_ref[0]
    s = jnp.concatenate([s[r * TQ:(r + 1) * TQ] + wmask for r in range(r_n)], axis=0)
    m = jnp.broadcast_to(jnp.max(s, axis=1, keepdims=True), (rows, LANES))
    p = jnp.concatenate([jnp.exp2(s[:, c * LANES:(c + 1) * LANES] - m).astype(BF16)
                         for c in range(s.shape[1] // LANES)], axis=1)
    o_aug = _dot(p, vcat)
    o_win = o_aug[:, 0:hd] * (1.0 / o_aug[:, hd:hd + 1])

    gate = gate_ref[j]
    for r in range(r_n):
        rs = slice(r * TQ, (r + 1) * TQ)
        c = (j * r_n + r) * hd
        pre_ref[:, c:c + hd] = (gate[:, 3 * r:3 * r + 1] * o_cmp[rs]
                                + gate[:, 3 * r + 2:3 * r + 3] * o_win[rs])


def _overlap_matrix(nc, n_sel):
    i = np.arange(nc)[:, None]
    j = np.arange(n_sel)[None, :]
    lo = np.maximum(i * CMP_STRIDE, j * SEL_BLOCK)
    hi = np.minimum(i * CMP_STRIDE + CMP_BLOCK - 1, j * SEL_BLOCK + SEL_BLOCK - 1)
    ov = np.zeros((nc, -(-n_sel // LANES) * LANES), np.float32)
    ov[:, :n_sel] = np.clip(hi - lo + 1, 0, None) / CMP_STRIDE
    return jnp.asarray(ov, BF16)


def _window_masks(n_wb):
    q = np.arange(TQ)[:, None]
    c = np.arange(n_wb * TQ)[None, :]
    dist = q + (n_wb - 1) * TQ - c
    masks = []
    for v in range(n_wb):
        kpos = (v - (n_wb - 1)) * TQ + c
        masks.append(np.where((kpos >= 0) & (dist >= 0) & (dist < WINDOW), 0.0, MASKED))
    return jnp.asarray(np.stack(masks), F32)


def _nsa_local(q, kc, vc, kwin, vwin, gates, n_sel, n_var):
    s = q.shape[0]
    g_n, r_n, hd = NSA_KV_HEADS, NSA_GROUP, HEAD_DIM
    n_q = s // TQ
    n_wb = WINDOW // TQ + 1
    n_top = min(SEL_TOPK, n_sel)
    nc = kc.shape[1]
    rows = r_n * TQ
    gps = NSA_LOCAL_GROUPS
    assert n_top >= 3 and nc % min(2 * LANES, nc) == 0 and g_n % gps == 0
    overlap = _overlap_matrix(nc, n_sel)

    def win_spec(j, width):
        return pl.BlockSpec((gps, TQ, width), lambda g, i: (g, jnp.maximum(i - (n_wb - 1) + j, 0), 0))

    return pl.pallas_call(
        functools.partial(_nsa_local_kernel, n_sel, n_top),
        grid=(g_n // gps, n_q),
        in_specs=[pl.BlockSpec((TQ, gps * r_n * hd), lambda g, i: (i, g)),
                  pl.BlockSpec((gps, nc, hd), lambda g, i: (g, 0, 0)),
                  pl.BlockSpec((gps, nc, LANES), lambda g, i: (g, 0, 0)),
                  pl.BlockSpec(overlap.shape, lambda g, i: (0, 0)),
                  pl.BlockSpec((1, TQ, n_wb * TQ), lambda g, i: (jnp.minimum(i, n_wb - 1), 0, 0))]
        + [win_spec(j, hd) for j in range(n_wb)] + [win_spec(j, LANES) for j in range(n_wb)]
        + [pl.BlockSpec((gps, TQ, LANES), lambda g, i: (g, i, 0))],
        out_specs=[pl.BlockSpec((TQ, gps * r_n * hd), lambda g, i: (i, g)),
                   pl.BlockSpec((n_var, gps, 1, rows, 2 * LANES), lambda g, i: (0, g, i, 0, 0))],
        out_shape=[jax.ShapeDtypeStruct((s, N_HEADS * hd), F32),
                   jax.ShapeDtypeStruct((n_var, g_n, n_q, rows, 2 * LANES), BF16)],
        scratch_shapes=[pltpu.VMEM((gps, rows, LANES), F32), pltpu.VMEM((gps, rows, LANES), F32),
                        pltpu.VMEM((gps, rows, overlap.shape[1]), F32)],
        name="nsa_local",
        compiler_params=_cparams("parallel", "parallel"),
    )(q, kc, vc, overlap, _window_masks(n_wb),
      *([kwin] * n_wb), *([vwin] * n_wb), gates)


def _causal_pairs(n_q, tq, tk):
    qs, ks = [], []
    for qi in range(n_q):
        for ki in range(((qi + 1) * tq - 1) // tk + 1):
            qs.append(qi)
            ks.append(ki)
    return jnp.asarray(qs, jnp.int32), jnp.asarray(ks, jnp.int32)


def _flash_scores(q, k, causal_mask):
    s = _dot_t(q, k)
    return s if causal_mask is None else jnp.where(causal_mask, s, MASKED)


def _flash_accumulate(s, v, m_ref, acc_ref, idx, row0):
    m_prev = m_ref[idx, row0:, :]
    m_new = jnp.maximum(m_prev, jnp.max(s, axis=1, keepdims=True))
    p = jnp.concatenate([jnp.exp2(s[:, c * LANES:(c + 1) * LANES] - m_new).astype(BF16)
                         for c in range(s.shape[1] // LANES)], axis=1)
    acc_ref[idx, row0:, :] = jnp.exp2(m_prev - m_new) * acc_ref[idx, row0:, :] + _dot(p, v)
    m_ref[idx, row0:, :] = m_new


def _flash_steps(qi, ki, tq, tk, rows_per_q, n_batch, operands, m_ref, acc_ref, finalize):
    first_diag = (qi * tq) // tk
    last_k = ((qi + 1) * tq - 1) // tk
    assert rows_per_q == 1 or tq <= tk

    @pl.when(ki == 0)
    def _():
        m_ref[...] = jnp.full(m_ref.shape, M_INIT, F32)
        acc_ref[...] = jnp.zeros(acc_ref.shape, F32)

    def run(masked, row0, n_keys):
        mask = None
        if masked:
            shape = (rows_per_q * tq - row0, n_keys)
            q_off = lax.broadcasted_iota(jnp.int32, shape, 0) % tq + row0
            k_off = lax.broadcasted_iota(jnp.int32, shape, 1)
            mask = q_off - k_off >= ki * tk - qi * tq
        def scores(b):
            q, k, _ = operands(b, row0, n_keys)
            return _flash_scores(q, k, mask)

        ahead = [scores(b) for b in range(min(QK_AHEAD, n_batch))]
        for b in range(n_batch):
            s = ahead.pop(0)
            if b + QK_AHEAD < n_batch:
                ahead.append(scores(b + QK_AHEAD))
            _flash_accumulate(s, operands(b, row0, n_keys)[2], m_ref, acc_ref, b, row0)

    @pl.when(ki < first_diag)
    def _():
        run(False, 0, tk)

    if tq >= tk:
        variants = [(ki - first_diag == j, j * tk, tk) for j in range(tq // tk)]
    else:
        variants = [(qi % (tk // tq) == j, 0, (j + 1) * tq) for j in range(tk // tq)]
    for cond, row0, n_keys in variants:
        @pl.when((ki >= first_diag) & cond)
        def _():
            run(True, row0, n_keys)

    @pl.when(ki == last_k)
    def _():
        finalize()


def _nsa_sel_kernel(qtab, ktab, qaug_ref, kaug_ref, vaug_ref, pre_ref, gate_ref, o_ref,
                    m_ref, acc_ref):
    g_n, r_n, hd = NSA_KV_HEADS, NSA_GROUP, HEAD_DIM
    t = pl.program_id(0)
    qi, ki = qtab[t], ktab[t]

    def operands(g, row0, n_keys):
        return (qaug_ref[0, g, 0, row0:, :], kaug_ref[0:n_keys, g * 2 * LANES:(g + 1) * 2 * LANES],
                vaug_ref[0:n_keys, g * LANES:(g + 1) * LANES])

    def finalize():
        for g in range(g_n):
            acc = acc_ref[g]
            gate = gate_ref[g]
            o = acc[:, 0:hd] * (1.0 / acc[:, hd:hd + 1])
            for r in range(r_n):
                c = (g * r_n + r) * hd
                o_ref[:, c:c + hd] = (pre_ref[:, c:c + hd] + gate[:, 3 * r + 1:3 * r + 2]
                                      * o[r * TQ:(r + 1) * TQ]).astype(o_ref.dtype)

    _flash_steps(qi, ki, TQ, TK, r_n, g_n, operands, m_ref, acc_ref, finalize)


def _nsa_sel(qaug, kaug, vaug, pre, gates):
    n_var, g_n, n_q, rows, _ = qaug.shape
    s = kaug.shape[0]
    qtab, ktab = _causal_pairs(n_q, TQ, TK)
    tiles_per_var = BIAS_BLOCKS * SEL_BLOCK // TK
    grid_spec = pltpu.PrefetchScalarGridSpec(
        num_scalar_prefetch=2,
        grid=(qtab.shape[0],),
        in_specs=[
            pl.BlockSpec((1, g_n, 1, rows, 2 * LANES),
                         lambda t, qt, kt: (kt[t] // tiles_per_var, 0, qt[t], 0, 0)),
            pl.BlockSpec((TK, g_n * 2 * LANES), lambda t, qt, kt: (kt[t], 0)),
            pl.BlockSpec((TK, g_n * LANES), lambda t, qt, kt: (kt[t], 0)),
            pl.BlockSpec((TQ, N_HEADS * HEAD_DIM), lambda t, qt, kt: (qt[t], 0)),
            pl.BlockSpec((g_n, TQ, LANES), lambda t, qt, kt: (0, qt[t], 0)),
        ],
        out_specs=pl.BlockSpec((TQ, N_HEADS * HEAD_DIM), lambda t, qt, kt: (qt[t], 0)),
        scratch_shapes=[pltpu.VMEM((g_n, rows, LANES), F32), pltpu.VMEM((g_n, rows, LANES), F32)],
    )
    return pl.pallas_call(
        _nsa_sel_kernel,
        grid_spec=grid_spec,
        out_shape=jax.ShapeDtypeStruct((s, N_HEADS * HEAD_DIM), BF16),
        name="nsa_sel",
        compiler_params=_cparams("arbitrary"),
    )(qtab, ktab, qaug, kaug, vaug, pre, gates)


def _moba_proj_kernel(k_eff, x_ref, cos_ref, sa_ref, sb_ref, w_ref,
                      qaug_ref, kaug_ref, vaug_ref, kmean_ref):
    tm = x_ref.shape[0]
    hd, d = HEAD_DIM, N_HEADS * HEAD_DIM
    i = pl.program_id(0)

    @pl.when(i == 0)
    def _():
        kmean_ref[...] = jnp.zeros(kmean_ref.shape, F32)

    xb = x_ref[...].astype(BF16)
    cos_t, sin_a, sin_b = cos_ref[...], sa_ref[...], sb_ref[...]
    q = _rope(_dot(xb, w_ref[:, 0:d]), cos_t, sin_a, sin_b)
    k = _rope(_dot(xb, w_ref[:, d:2 * d]), cos_t, sin_a, sin_b)
    v = _dot(xb, w_ref[:, 2 * d:3 * d])
    kmean = kmean_ref[...]
    nb = kmean.shape[0]
    own = jnp.where(lax.broadcasted_iota(jnp.int32, (1, nb), 1) == i, 1.0, 0.0).astype(BF16)
    own_onehot = jnp.broadcast_to(own, (tm, nb))
    ones_col = _ones_col(tm, LANES - hd)
    blk_t = lax.broadcasted_iota(jnp.int32, (nb, tm), 0)
    own_t = jnp.where(blk_t == i, 1.0, 0.0)
    for hp in range(N_HEADS // 2):
        bias_t = []
        for h in (2 * hp, 2 * hp + 1):
            sl = slice(h * hd, (h + 1) * hd)
            gate_t = _dot_t(kmean[:, sl], q[:, sl], precision=lax.Precision.HIGHEST)
            gate_t = jnp.where(blk_t < i, gate_t, -jnp.inf)
            bias_t.append(jnp.where(_top_k_axis0(gate_t, k_eff, own_t)[0] > 0, 0.0, MASKED))
        bias = jnp.concatenate(bias_t, axis=0).T.astype(BF16)
        for j, h in enumerate((2 * hp, 2 * hp + 1)):
            sl = slice(h * hd, (h + 1) * hd)
            qaug_ref[h, :, 0:hd] = (q[:, sl] * SCALE).astype(BF16)
            qaug_ref[h, :, hd:hd + nb] = bias[:, j * nb:(j + 1) * nb]
            kaug_ref[h, :, 0:hd] = k[:, sl].astype(BF16)
            kaug_ref[h, :, hd:hd + nb] = own_onehot
            vaug_ref[h, :, 0:hd] = v[:, sl].astype(BF16)
            vaug_ref[h, :, hd:LANES] = ones_col
    kmean_ref[pl.ds(i, 1), :] = jnp.mean(k, axis=0, keepdims=True)


def _moba_proj(x, tabs, w):
    s, d = x.shape
    n_blk = s // MOBA_BLOCK
    k_eff = min(MOBA_TOPK, max(n_blk - 1, 1))
    nb = LANES - HEAD_DIM
    assert s % MOBA_BLOCK == 0 and n_blk <= nb
    tm = MOBA_BLOCK
    row = lambda i: (i, 0)
    head = jax.ShapeDtypeStruct((N_HEADS, s, LANES), BF16)
    head_spec = pl.BlockSpec((N_HEADS, tm, LANES), lambda i: (0, i, 0))
    return pl.pallas_call(
        functools.partial(_moba_proj_kernel, k_eff),
        grid=(s // tm,),
        in_specs=[pl.BlockSpec((tm, d), row)] + [pl.BlockSpec((tm, LANES), row)] * 3
        + [pl.BlockSpec(w.shape, lambda i: (0, 0))],
        out_specs=[head_spec] * 3,
        out_shape=[head] * 3,
        scratch_shapes=[pltpu.VMEM((nb, N_HEADS * HEAD_DIM), F32)],
        name="moba_proj",
        compiler_params=_cparams("arbitrary"),
    )(x, *tabs, w)


def _moba_attn_kernel(qtab, ktab, q_ref, k_ref, v_ref, o_ref, m_ref, acc_ref):
    hd = HEAD_DIM
    t = pl.program_id(0)
    qi, ki = qtab[t], ktab[t]

    def operands(h, row0, n_keys):
        return q_ref[h, row0:, :], k_ref[h, 0:n_keys, :], v_ref[h, 0:n_keys, :]

    def finalize():
        for h in range(N_HEADS):
            acc = acc_ref[h]
            o_ref[:, h * hd:(h + 1) * hd] = (acc[:, 0:hd] * (1.0 / acc[:, hd:hd + 1])).astype(o_ref.dtype)

    _flash_steps(qi, ki, q_ref.shape[1], k_ref.shape[1], 1, N_HEADS, operands, m_ref, acc_ref, finalize)


def _moba_attn(qaug, kaug, vaug):
    h_n, s, _ = qaug.shape
    tq, tk = min(MOBA_TQ, s), min(MOBA_TK, s)
    assert s % tq == 0 and s % tk == 0
    qtab, ktab = _causal_pairs(s // tq, tq, tk)
    grid_spec = pltpu.PrefetchScalarGridSpec(
        num_scalar_prefetch=2,
        grid=(qtab.shape[0],),
        in_specs=[pl.BlockSpec((h_n, tq, LANES), lambda t, qt, kt: (0, qt[t], 0)),
                  pl.BlockSpec((h_n, tk, LANES), lambda t, qt, kt: (0, kt[t], 0)),
                  pl.BlockSpec((h_n, tk, LANES), lambda t, qt, kt: (0, kt[t], 0))],
        out_specs=pl.BlockSpec((tq, h_n * HEAD_DIM), lambda t, qt, kt: (qt[t], 0)),
        scratch_shapes=[pltpu.VMEM((h_n, tq, LANES), F32), pltpu.VMEM((h_n, tq, LANES), F32)],
    )
    return pl.pallas_call(
        _moba_attn_kernel,
        grid_spec=grid_spec,
        out_shape=jax.ShapeDtypeStruct((s, h_n * HEAD_DIM), BF16),
        name="moba_attn",
        compiler_params=_cparams("arbitrary"),
    )(qtab, ktab, qaug, kaug, vaug)


def _layer_norm(z, gain, bias):
    mu = jnp.mean(z, axis=-1, keepdims=True)
    zc = z - mu
    var = jnp.mean(zc * zc, axis=-1, keepdims=True)
    return zc * lax.rsqrt(var + LN_EPS) * gain + bias


def _post_mixer_kernel(alpha, n_chunk, a_ref, x_ref, wo_ref, g1_ref, b1_ref, wg_ref, wu_ref, wd_ref,
                       g2_ref, b2_ref, o_ref):
    x1 = _layer_norm(alpha * x_ref[...] + _dot(a_ref[...], wo_ref[...]), g1_ref[...], b1_ref[...])
    xb = x1.astype(BF16)
    cw = wg_ref.shape[1] // n_chunk
    f = None
    for c in range(n_chunk):
        sl = slice(c * cw, (c + 1) * cw)
        u = (jax.nn.silu(_dot(xb, wg_ref[:, sl])) * _dot(xb, wu_ref[:, sl])).astype(BF16)
        part = _dot(u, wd_ref[sl, :])
        f = part if f is None else f + part
    o_ref[...] = _layer_norm(alpha * x1 + f, g2_ref[...], b2_ref[...])


def _post_mixer(a, x, w_o, g1, b1, wg, wu, wd, g2, b2, alpha):
    s, d = x.shape
    f = wg.shape[1]
    n_chunk = 2 if f % (2 * LANES) == 0 else 1
    tm = min(MM_TILE, s)
    row = lambda i: (i, 0)

    def resident(shape):
        return pl.BlockSpec(shape, lambda i: (0, 0), pipeline_mode=pl.Buffered(1))

    vec = lambda v: v.reshape(1, d)
    return pl.pallas_call(
        functools.partial(_post_mixer_kernel, alpha, n_chunk),
        grid=(s // tm,),
        in_specs=[pl.BlockSpec((tm, a.shape[1]), row), pl.BlockSpec((tm, d), row),
                  resident(w_o.shape), resident((1, d)), resident((1, d)),
                  resident(wg.shape), resident(wu.shape), resident(wd.shape),
                  resident((1, d)), resident((1, d))],
        out_specs=pl.BlockSpec((tm, d), row),
        out_shape=jax.ShapeDtypeStruct((s, d), F32),
        name="post_mixer",
        compiler_params=_cparams("parallel"),
    )(a, x, w_o, vec(g1), vec(b1), wg, wu, wd, vec(g2), vec(b2))


def _nsa_mixer(x, tabs, w_in, ck_pos, ck_w1, ck_w2, cv_pos, cv_w1, cv_w2):
    s = x.shape[0]
    g_n, r_n, hd = NSA_KV_HEADS, NSA_GROUP, HEAD_DIM
    q_dim, kv_dim = N_HEADS * hd, NSA_KV_HEADS * hd
    n_sel = s // SEL_BLOCK
    n_var = -(-n_sel // BIAS_BLOCKS)
    assert s % TK == 0 and (BIAS_BLOCKS * SEL_BLOCK) % TK == 0
    wq = w_in[:, :q_dim].astype(BF16)
    wkv = w_in[:, q_dim:q_dim + 6 * kv_dim].astype(BF16)
    wg = w_in[:, q_dim + 6 * kv_dim:].reshape(-1, g_n, 3 * r_n)
    wg = jnp.pad(wg, ((0, 0), (0, 0), (0, LANES - 3 * r_n))).reshape(-1, g_n * LANES).astype(BF16)
    q, cmp_kv, kaug, vaug, kwin, vwin, gates = _nsa_proj(x, tabs, wq, wkv, wg)
    comp = _compress(cmp_kv, jnp.stack([ck_pos, cv_pos]),
                     jnp.stack([ck_w1, cv_w1]).astype(BF16), jnp.stack([ck_w2, cv_w2]).astype(BF16))
    kc = comp[0].astype(BF16)
    ones = jnp.ones(comp.shape[1:3] + (1,), F32)
    zeros = jnp.zeros(comp.shape[1:3] + (LANES - hd - 1,), F32)
    vc = jnp.concatenate([comp[1], ones, zeros], axis=-1).astype(BF16)
    pre, qaug = _nsa_local(q, kc, vc, kwin, vwin, gates, n_sel, n_var)
    return _nsa_sel(qaug, kaug, vaug, pre, gates)


def _moba_mixer(x, tabs, w_in):
    qaug, kaug, vaug = _moba_proj(x, tabs, w_in.astype(BF16))
    return _moba_attn(qaug, kaug, vaug)


def kernel(x, positions, nsa_w_in, nsa_w_o, nsa_ck_pos, nsa_ck_w1, nsa_ck_w2, nsa_cv_pos, nsa_cv_w1, nsa_cv_w2, moba_w_in, moba_w_o, ffn_wg, ffn_wu, ffn_wd, ln1_g, ln1_b, ln2_g, ln2_b):
    b, s, d = x.shape
    depth = ffn_wg.shape[0]
    alpha = (2 * depth) ** 0.25
    outs = []
    for bi in range(b):
        h = x[bi]
        tabs = _rope_tables(positions[bi])
        for i in range(depth):
            j = i // 2
            if i % 2 == 0:
                a = _nsa_mixer(h, tabs, nsa_w_in[j], nsa_ck_pos[j], nsa_ck_w1[j], nsa_ck_w2[j],
                               nsa_cv_pos[j], nsa_cv_w1[j], nsa_cv_w2[j])
                w_o = nsa_w_o[j]
            else:
                a = _moba_mixer(h, tabs, moba_w_in[j])
                w_o = moba_w_o[j]
            h = _post_mixer(a, h, w_o.astype(BF16), ln1_g[i], ln1_b[i], ffn_wg[i].astype(BF16),
                            ffn_wu[i].astype(BF16), ffn_wd[i].astype(BF16), ln2_g[i], ln2_b[i], alpha)
        outs.append(h)
    return jnp.stack(outs)
```

```python
import functools

import numpy as np
import jax
import jax.numpy as jnp
from jax import lax
from jax.experimental import pallas as pl
from jax.experimental.pallas import tpu as pltpu

F32 = jnp.float32
BF16 = jnp.bfloat16

N_HEADS = 16
HEAD_DIM = 64
ROPE_DIM = HEAD_DIM // 4
ROPE_THETA = 500000.0
NSA_KV_HEADS = 4
NSA_GROUP = N_HEADS // NSA_KV_HEADS
CMP_BLOCK = 32
CMP_STRIDE = 16
SEL_BLOCK = 64
SEL_TOPK = 16
WINDOW = 512
MOBA_BLOCK = 256
MOBA_TOPK = 3
LN_EPS = 1e-5
SCALE = HEAD_DIM ** -0.5 * float(np.log2(np.e))

LANES = 128
SUBLANES = 8
VMEM_LIMIT_BYTES = 48 * 1024 * 1024

ROW_TILE = 256
MM_TILE = 512
TQ = 256
TK = 1024
MOBA_TQ = 1024
MOBA_TK = 512
QK_AHEAD = 1
SEL_PREFIXES = 4
NSA_LOCAL_GROUPS = 1
BIAS_BLOCKS = 128
MASKED = -1e30
M_INIT = -5e29

assert WINDOW % TQ == 0 and TK % TQ == 0 and ROW_TILE == MOBA_BLOCK


def _cparams(*sem):
    return pltpu.CompilerParams(dimension_semantics=sem, vmem_limit_bytes=VMEM_LIMIT_BYTES)


def _dot(a, b):
    return jnp.dot(a, b, preferred_element_type=F32)


def _dot_t(a, b, precision=None):
    return lax.dot_general(a, b, (((1,), (1,)), ((), ())), preferred_element_type=F32,
                           precision=precision)


def _rope_tables_kernel(pos_ref, c_ref, cos_ref, sa_ref, sb_ref):
    ang = pos_ref[...].astype(F32) * c_ref[0:1, :]
    c = jnp.cos(ang)
    s = jnp.sin(ang)
    cos_ref[...] = jnp.where(c_ref[1:2, :] > 0, c, 1.0)
    sa_ref[...] = jnp.where(c_ref[2:3, :] > 0, s, 0.0)
    sb_ref[...] = jnp.where(c_ref[3:4, :] > 0, -s, 0.0)


def _rope_tables(positions):
    s = positions.shape[0]
    half = ROPE_DIM // 2
    inv = ROPE_THETA ** (-jnp.arange(0, ROPE_DIM, 2, dtype=F32) / ROPE_DIM)
    d = np.arange(LANES) % HEAD_DIM
    consts = jnp.zeros((SUBLANES, LANES), F32)
    consts = consts.at[0].set(inv[d % half])
    consts = consts.at[1].set(jnp.asarray(d < ROPE_DIM, F32))
    consts = consts.at[2].set(jnp.asarray((d >= half) & (d < ROPE_DIM), F32))
    consts = consts.at[3].set(jnp.asarray(d < half, F32))
    tm = min(MM_TILE, s)
    tab = jax.ShapeDtypeStruct((s, LANES), F32)
    return pl.pallas_call(
        _rope_tables_kernel,
        grid=(s // tm,),
        in_specs=[pl.BlockSpec((tm, 1), lambda i: (i, 0)),
                  pl.BlockSpec((SUBLANES, LANES), lambda i: (0, 0))],
        out_specs=[pl.BlockSpec((tm, LANES), lambda i: (i, 0))] * 3,
        out_shape=[tab, tab, tab],
        name="rope_tables",
        compiler_params=_cparams("parallel"),
    )(positions.reshape(s, 1), consts)


def _rope(x, cos_t, sin_a, sin_b):
    half = ROPE_DIM // 2
    outs = []
    for c in range(x.shape[1] // LANES):
        xs = x[:, c * LANES:(c + 1) * LANES]
        outs.append(xs * cos_t + pltpu.roll(xs, half, 1) * sin_a
                    + pltpu.roll(xs, LANES - half, 1) * sin_b)
    return outs[0] if len(outs) == 1 else jnp.concatenate(outs, axis=1)


def _ones_col(rows, width):
    return jnp.where(lax.broadcasted_iota(jnp.int32, (rows, width), 1) == 0, 1.0, 0.0).astype(BF16)


def _nsa_proj_kernel(x_ref, cos_ref, sa_ref, sb_ref, wq_ref, wkv_ref, wg_ref,
                     q_ref, cmp_ref, kaug_ref, vaug_ref, kwin_ref, vwin_ref, gate_ref):
    tm = x_ref.shape[0]
    g_n, hd, kvd = NSA_KV_HEADS, HEAD_DIM, NSA_KV_HEADS * HEAD_DIM
    xb = x_ref[...].astype(BF16)
    cos_t, sin_a, sin_b = cos_ref[...], sa_ref[...], sb_ref[...]
    q = _rope(_dot(xb, wq_ref[...]), cos_t, sin_a, sin_b) * SCALE
    q_ref[...] = q.astype(BF16)
    kv = _dot(xb, wkv_ref[...])
    k_cmp = _rope(kv[:, 0:kvd], cos_t, sin_a, sin_b)
    v_cmp = kv[:, kvd:2 * kvd]
    k_sel = _rope(kv[:, 2 * kvd:3 * kvd], cos_t, sin_a, sin_b)
    v_sel = kv[:, 3 * kvd:4 * kvd]
    k_win = _rope(kv[:, 4 * kvd:5 * kvd], cos_t, sin_a, sin_b)
    v_win = kv[:, 5 * kvd:6 * kvd]
    gate = jax.nn.sigmoid(_dot(xb, wg_ref[...]))
    tok = pl.program_id(0) * tm + lax.broadcasted_iota(jnp.int32, (tm, BIAS_BLOCKS), 0)
    blk = (tok // SEL_BLOCK) % BIAS_BLOCKS
    onehot = jnp.where(lax.broadcasted_iota(jnp.int32, (tm, BIAS_BLOCKS), 1) == blk,
                       1.0, 0.0).astype(BF16)
    ones_col = _ones_col(tm, LANES - hd)
    for g in range(g_n):
        sl = slice(g * hd, (g + 1) * hd)
        cmp_ref[0, g] = k_cmp[:, sl]
        cmp_ref[1, g] = v_cmp[:, sl]
        kwin_ref[g] = k_win[:, sl].astype(BF16)
        vwin_ref[g, :, 0:hd] = v_win[:, sl].astype(BF16)
        vwin_ref[g, :, hd:LANES] = ones_col
        gate_ref[g] = gate[:, g * LANES:(g + 1) * LANES]
        kb = g * 2 * LANES
        kaug_ref[:, kb:kb + BIAS_BLOCKS] = onehot
        kaug_ref[:, kb + BIAS_BLOCKS:kb + BIAS_BLOCKS + hd] = k_sel[:, sl].astype(BF16)
        kaug_ref[:, kb + BIAS_BLOCKS + hd:kb + 2 * LANES] = jnp.zeros((tm, 2 * LANES - BIAS_BLOCKS - hd), BF16)
        vb = g * LANES
        vaug_ref[:, vb:vb + hd] = v_sel[:, sl].astype(BF16)
        vaug_ref[:, vb + hd:vb + LANES] = ones_col


def _nsa_proj(x, tabs, wq, wkv, wg):
    s, d = x.shape
    g_n, hd = NSA_KV_HEADS, HEAD_DIM
    tm = min(ROW_TILE, s)
    row = lambda i: (i, 0)
    whole = lambda i: (0, 0)
    out_shape = [
        jax.ShapeDtypeStruct((s, N_HEADS * hd), BF16),
        jax.ShapeDtypeStruct((2, g_n, s, hd), F32),
        jax.ShapeDtypeStruct((s, g_n * 2 * LANES), BF16),
        jax.ShapeDtypeStruct((s, g_n * LANES), BF16),
        jax.ShapeDtypeStruct((g_n, s, hd), BF16),
        jax.ShapeDtypeStruct((g_n, s, LANES), BF16),
        jax.ShapeDtypeStruct((g_n, s, LANES), F32),
    ]
    out_specs = [
        pl.BlockSpec((tm, N_HEADS * hd), row),
        pl.BlockSpec((2, g_n, tm, hd), lambda i: (0, 0, i, 0)),
        pl.BlockSpec((tm, g_n * 2 * LANES), row),
        pl.BlockSpec((tm, g_n * LANES), row),
        pl.BlockSpec((g_n, tm, hd), lambda i: (0, i, 0)),
        pl.BlockSpec((g_n, tm, LANES), lambda i: (0, i, 0)),
        pl.BlockSpec((g_n, tm, LANES), lambda i: (0, i, 0)),
    ]
    return pl.pallas_call(
        _nsa_proj_kernel,
        grid=(s // tm,),
        in_specs=[pl.BlockSpec((tm, d), row)] + [pl.BlockSpec((tm, LANES), row)] * 3
        + [pl.BlockSpec(wq.shape, whole), pl.BlockSpec(wkv.shape, whole),
           pl.BlockSpec(wg.shape, whole)],
        out_specs=out_specs,
        out_shape=out_shape,
        name="nsa_proj",
        compiler_params=_cparams("parallel"),
    )(x, *tabs, wq, wkv, wg)


def _compress_kernel(x_ref, pos_ref, w1_ref, w2_ref, out_ref):
    n16 = out_ref.shape[2]
    hd = x_ref.shape[3]
    a = b = None
    for t in range(CMP_STRIDE):
        xt = x_ref[0, 0, pl.ds(t, n16, stride=CMP_STRIDE), :]
        lo = _dot((xt + pos_ref[0, t:t + 1, :]).astype(BF16), w1_ref[0, t * hd:(t + 1) * hd, :])
        t2 = CMP_STRIDE + t
        hi = _dot((xt + pos_ref[0, t2:t2 + 1, :]).astype(BF16), w1_ref[0, t2 * hd:(t2 + 1) * hd, :])
        a = lo if a is None else a + lo
        b = hi if b is None else b + hi
    hid = a + pltpu.roll(b, n16 - 1, 0)
    out_ref[0, 0] = _dot(jax.nn.gelu(hid).astype(BF16), w2_ref[0])


def _compress(cmp_kv, pos, w1, w2):
    _, g_n, s, hd = cmp_kv.shape
    n16 = s // CMP_STRIDE
    assert CMP_BLOCK == 2 * CMP_STRIDE
    return pl.pallas_call(
        _compress_kernel,
        grid=(2, g_n),
        in_specs=[pl.BlockSpec((1, 1, s, hd), lambda a, g: (a, g, 0, 0)),
                  pl.BlockSpec((1,) + pos.shape[1:], lambda a, g: (a, 0, 0)),
                  pl.BlockSpec((1,) + w1.shape[1:], lambda a, g: (a, 0, 0)),
                  pl.BlockSpec((1,) + w2.shape[1:], lambda a, g: (a, 0, 0))],
        out_specs=pl.BlockSpec((1, 1, n16, hd), lambda a, g: (a, g, 0, 0)),
        out_shape=jax.ShapeDtypeStruct((2, g_n, n16, hd), F32),
        name="nsa_compress",
        compiler_params=_cparams("parallel", "parallel"),
    )(cmp_kv, pos, w1, w2)


def _top_k_axis0(score, n_rounds, sel, enable=None):
    n = score.shape[0]
    row = lax.broadcasted_iota(jnp.int32, score.shape, 0).astype(F32)
    for _ in range(n_rounds):
        m = jnp.max(score, axis=0, keepdims=True)
        idx = jnp.min(jnp.where(score == m, row, float(n)), axis=0, keepdims=True)
        ok = m > -jnp.inf if enable is None else (m > -jnp.inf) & enable
        hit = row == jnp.where(ok, idx, -1.0)
        sel = jnp.where(hit, 1.0, sel)
        score = jnp.where(hit, -jnp.inf, score)
    return sel, score


def _nsa_local_kernel(n_sel, n_top, q_ref, kc_ref, vc_ref, ov_ref, wmask_ref, *rest):
    n_wb = WINDOW // TQ + 1
    kw_refs = rest[:n_wb]
    vw_refs = rest[n_wb:2 * n_wb]
    gate_ref, pre_ref, qaug_ref, m_scr, o_acc, imp_acc = rest[2 * n_wb:]
    r_n, hd = NSA_GROUP, HEAD_DIM
    gps = kc_ref.shape[0]
    qi = pl.program_id(1)
    qgs = [q_ref[:, j * r_n * hd:(j + 1) * r_n * hd] for j in range(gps)]
    qss = [jnp.concatenate([qg[:, r * hd:(r + 1) * hd] for r in range(r_n)], axis=0) for qg in qgs]
    rows = r_n * TQ

    nc = kc_ref.shape[1]
    ct = min(2 * LANES, nc)
    t_max = ((qi * TQ + TQ - CMP_BLOCK) // CMP_STRIDE) // ct
    t_full = (jnp.maximum(qi * TQ - (CMP_BLOCK - 1) + CMP_STRIDE, 0) // CMP_STRIDE) // ct

    m_scr[...] = jnp.full(m_scr.shape, M_INIT, F32)
    o_acc[...] = jnp.zeros(o_acc.shape, F32)
    imp_acc[...] = jnp.zeros(imp_acc.shape, F32)

    def cmp_tile(t, masked):
        tile = slice(t * ct, (t + 1) * ct)
        if masked:
            q_off = lax.broadcasted_iota(jnp.int32, (rows, ct), 0) % TQ
            tok = lax.broadcasted_iota(jnp.int32, (rows, ct), 1)
            visible = q_off - CMP_STRIDE * tok >= CMP_STRIDE * ct * t + CMP_BLOCK - 1 - qi * TQ
        for j in range(gps):
            s = _dot_t(qss[j], kc_ref[j, tile, :])
            if masked:
                s = jnp.where(visible, s, MASKED)
            m_prev = m_scr[j]
            m_new = jnp.maximum(m_prev, jnp.max(s, axis=1, keepdims=True))
            alpha = jnp.exp2(m_prev - m_new)
            p = jnp.concatenate([jnp.exp2(s[:, c * LANES:(c + 1) * LANES] - m_new).astype(BF16)
                                 for c in range(ct // LANES)], axis=1)
            o_acc[j] = alpha * o_acc[j] + _dot(p, vc_ref[j, tile, :])
            imp_new = _dot(p, ov_ref[tile, :])
            imp_acc[j] = jnp.concatenate(
                [alpha * imp_acc[j, :, c * LANES:(c + 1) * LANES] + imp_new[:, c * LANES:(c + 1) * LANES]
                 for c in range(imp_new.shape[1] // LANES)], axis=1)
            m_scr[j] = m_new

    for t in range(nc // ct):
        pl.when(t < t_full)(functools.partial(cmp_tile, t, False))
        pl.when((t >= t_full) & (t <= t_max))(functools.partial(cmp_tile, t, True))

    n_prefix = SEL_PREFIXES if n_sel % (SEL_PREFIXES * SUBLANES) == 0 else 1
    prefix = ((qi + 1) * (TQ // SEL_BLOCK) - 1) * n_prefix // n_sel
    for v in range(n_prefix):
        @pl.when(prefix == v)
        def _():
            for j in range(gps):
                _nsa_local_tail(j, n_sel * (v + 1) // n_prefix, v == 0, n_sel, n_top, qi, qgs[j], qss[j],
                                o_acc, imp_acc, wmask_ref, kw_refs, vw_refs, gate_ref, pre_ref, qaug_ref)


def _nsa_local_tail(j, n_rows, first_tile, n_sel, n_top, qi, qg, qs, o_acc, imp_acc, wmask_ref,
                    kw_refs, vw_refs, gate_ref, pre_ref, qaug_ref):
    r_n, hd = NSA_GROUP, HEAD_DIM
    rows = r_n * TQ
    o_aug = o_acc[j]
    l_col = o_aug[:, hd:hd + 1]
    inv_l = jnp.broadcast_to(jnp.where(l_col > 0, 1.0 / l_col, 0.0), (rows, LANES))
    o_cmp = o_aug[:, 0:hd] * inv_l[:, 0:hd]

    n_pad = -(-n_rows // LANES) * LANES
    imp_q = None
    for r in range(r_n):
        rs = slice(r * TQ, (r + 1) * TQ)
        imp_r = jnp.concatenate([imp_acc[j, rs, c * LANES:(c + 1) * LANES] * inv_l[rs]
                                 for c in range(n_pad // LANES)], axis=1)
        imp_q = imp_r if imp_q is None else imp_q + imp_r
    imp = imp_q.T[0:n_rows]
    blk = lax.broadcasted_iota(jnp.int32, (n_rows, TQ), 0)
    blk_q = (qi * TQ + lax.broadcasted_iota(jnp.int32, (n_rows, TQ), 1)) // SEL_BLOCK
    forced = (blk == 0) | (blk == blk_q) | (blk == blk_q - 1)
    score = jnp.where((blk <= blk_q) & jnp.logical_not(forced), imp, -jnp.inf)
    sel = jnp.where(forced, 1.0, 0.0)
    n_forced_min = 3
    blk_q_row = blk_q[0:1, :]
    sel_tiles = []
    for c in range(TQ // LANES):
        ls = slice(c * LANES, (c + 1) * LANES)
        sel_c, score_c = _top_k_axis0(score[:, ls], n_top - n_forced_min, sel[:, ls])
        if first_tile:
            for extra in range(1, n_forced_min):
                sel_c, score_c = _top_k_axis0(score_c, 1, sel_c,
                                              enable=blk_q_row[:, ls] < n_forced_min - extra)
        sel_tiles.append(sel_c)
    sel = jnp.concatenate(sel_tiles, axis=1)
    n_pad = -(-n_rows // LANES) * LANES
    if n_pad > n_rows:
        sel = jnp.concatenate([sel, jnp.zeros((n_pad - n_rows, TQ), F32)], axis=0)
    bias = jnp.where(sel > 0, 0.0, MASKED).T.astype(BF16)
    n_var = qaug_ref.shape[0]
    if n_pad < n_var * BIAS_BLOCKS:
        bias = jnp.concatenate(
            [bias, jnp.full((TQ, n_var * BIAS_BLOCKS - n_pad), MASKED, BF16)], axis=1)
    zeros = jnp.zeros((TQ, 2 * LANES - BIAS_BLOCKS - hd), BF16)
    for v in range(n_var):
        for r in range(r_n):
            rs = slice(r * TQ, (r + 1) * TQ)
            qaug_ref[v, j, 0, rs, 0:BIAS_BLOCKS] = bias[:, v * BIAS_BLOCKS:(v + 1) * BIAS_BLOCKS]
            qaug_ref[v, j, 0, rs, BIAS_BLOCKS:BIAS_BLOCKS + hd] = qg[:, r * hd:(r + 1) * hd]
            qaug_ref[v, j, 0, rs, BIAS_BLOCKS + hd:2 * LANES] = zeros

    kcat = jnp.concatenate([kw[j] for kw in kw_refs], axis=0)
    vcat = jnp.concatenate([vw[j] for vw in vw_refs], axis=0)
    s = _dot_t(qs, kcat)
    wmask = wmask_ref[0]
    s = jnp.concatenate([s[r * TQ:(r + 1) * TQ] + wmask for r in range(r_n)], axis=0)
    m = jnp.broadcast_to(jnp.max(s, axis=1, keepdims=True), (rows, LANES))
    p = jnp.concatenate([jnp.exp2(s[:, c * LANES:(c + 1) * LANES] - m).astype(BF16)
                         for c in range(s.shape[1] // LANES)], axis=1)
    o_aug = _dot(p, vcat)
    o_win = o_aug[:, 0:hd] * (1.0 / o_aug[:, hd:hd + 1])

    gate = gate_ref[j]
    for r in range(r_n):
        rs = slice(r * TQ, (r + 1) * TQ)
        c = (j * r_n + r) * hd
        pre_ref[:, c:c + hd] = (gate[:, 3 * r:3 * r + 1] * o_cmp[rs]
                                + gate[:, 3 * r + 2:3 * r + 3] * o_win[rs])


def _overlap_matrix(nc, n_sel):
    i = np.arange(nc)[:, None]
    j = np.arange(n_sel)[None, :]
    lo = np.maximum(i * CMP_STRIDE, j * SEL_BLOCK)
    hi = np.minimum(i * CMP_STRIDE + CMP_BLOCK - 1, j * SEL_BLOCK + SEL_BLOCK - 1)
    ov = np.zeros((nc, -(-n_sel // LANES) * LANES), np.float32)
    ov[:, :n_sel] = np.clip(hi - lo + 1, 0, None) / CMP_STRIDE
    return jnp.asarray(ov, BF16)


def _window_masks(n_wb):
    q = np.arange(TQ)[:, None]
    c = np.arange(n_wb * TQ)[None, :]
    dist = q + (n_wb - 1) * TQ - c
    masks = []
    for v in range(n_wb):
        kpos = (v - (n_wb - 1)) * TQ + c
        masks.append(np.where((kpos >= 0) & (dist >= 0) & (dist < WINDOW), 0.0, MASKED))
    return jnp.asarray(np.stack(masks), F32)


def _nsa_local(q, kc, vc, kwin, vwin, gates, n_sel, n_var):
    s = q.shape[0]
    g_n, r_n, hd = NSA_KV_HEADS, NSA_GROUP, HEAD_DIM
    n_q = s // TQ
    n_wb = WINDOW // TQ + 1
    n_top = min(SEL_TOPK, n_sel)
    nc = kc.shape[1]
    rows = r_n * TQ
    gps = NSA_LOCAL_GROUPS
    assert n_top >= 3 and nc % min(2 * LANES, nc) == 0 and g_n % gps == 0
    overlap = _overlap_matrix(nc, n_sel)

    def win_spec(j, width):
        return pl.BlockSpec((gps, TQ, width), lambda g, i: (g, jnp.maximum(i - (n_wb - 1) + j, 0), 0))

    return pl.pallas_call(
        functools.partial(_nsa_local_kernel, n_sel, n_top),
        grid=(g_n // gps, n_q),
        in_specs=[pl.BlockSpec((TQ, gps * r_n * hd), lambda g, i: (i, g)),
                  pl.BlockSpec((gps, nc, hd), lambda g, i: (g, 0, 0)),
                  pl.BlockSpec((gps, nc, LANES), lambda g, i: (g, 0, 0)),
                  pl.BlockSpec(overlap.shape, lambda g, i: (0, 0)),
                  pl.BlockSpec((1, TQ, n_wb * TQ), lambda g, i: (jnp.minimum(i, n_wb - 1), 0, 0))]
        + [win_spec(j, hd) for j in range(n_wb)] + [win_spec(j, LANES) for j in range(n_wb)]
        + [pl.BlockSpec((gps, TQ, LANES), lambda g, i: (g, i, 0))],
        out_specs=[pl.BlockSpec((TQ, gps * r_n * hd), lambda g, i: (i, g)),
                   pl.BlockSpec((n_var, gps, 1, rows, 2 * LANES), lambda g, i: (0, g, i, 0, 0))],
        out_shape=[jax.ShapeDtypeStruct((s, N_HEADS * hd), F32),
                   jax.ShapeDtypeStruct((n_var, g_n, n_q, rows, 2 * LANES), BF16)],
        scratch_shapes=[pltpu.VMEM((gps, rows, LANES), F32), pltpu.VMEM((gps, rows, LANES), F32),
                        pltpu.VMEM((gps, rows, overlap.shape[1]), F32)],
        name="nsa_local",
        compiler_params=_cparams("parallel", "parallel"),
    )(q, kc, vc, overlap, _window_masks(n_wb),
      *([kwin] * n_wb), *([vwin] * n_wb), gates)


def _causal_pairs(n_q, tq, tk):
    qs, ks = [], []
    for qi in range(n_q):
        for ki in range(((qi + 1) * tq - 1) // tk + 1):
            qs.append(qi)
            ks.append(ki)
    return jnp.asarray(qs, jnp.int32), jnp.asarray(ks, jnp.int32)


def _flash_scores(q, k, causal_mask):
    s = _dot_t(q, k)
    return s if causal_mask is None else jnp.where(causal_mask, s, MASKED)


def _flash_accumulate(s, v, m_ref, acc_ref, idx, row0):
    m_prev = m_ref[idx, row0:, :]
    m_new = jnp.maximum(m_prev, jnp.max(s, axis=1, keepdims=True))
    p = jnp.concatenate([jnp.exp2(s[:, c * LANES:(c + 1) * LANES] - m_new).astype(BF16)
                         for c in range(s.shape[1] // LANES)], axis=1)
    acc_ref[idx, row0:, :] = jnp.exp2(m_prev - m_new) * acc_ref[idx, row0:, :] + _dot(p, v)
    m_ref[idx, row0:, :] = m_new


def _flash_steps(qi, ki, tq, tk, rows_per_q, n_batch, operands, m_ref, acc_ref, finalize):
    first_diag = (qi * tq) // tk
    last_k = ((qi + 1) * tq - 1) // tk
    assert rows_per_q == 1 or tq <= tk

    @pl.when(ki == 0)
    def _():
        m_ref[...] = jnp.full(m_ref.shape, M_INIT, F32)
        acc_ref[...] = jnp.zeros(acc_ref.shape, F32)

    def run(masked, row0, n_keys):
        mask = None
        if masked:
            shape = (rows_per_q * tq - row0, n_keys)
            q_off = lax.broadcasted_iota(jnp.int32, shape, 0) % tq + row0
            k_off = lax.broadcasted_iota(jnp.int32, shape, 1)
            mask = q_off - k_off >= ki * tk - qi * tq
        def scores(b):
            q, k, _ = operands(b, row0, n_keys)
            return _flash_scores(q, k, mask)

        ahead = [scores(b) for b in range(min(QK_AHEAD, n_batch))]
        for b in range(n_batch):
            s = ahead.pop(0)
            if b + QK_AHEAD < n_batch:
                ahead.append(scores(b + QK_AHEAD))
            _flash_accumulate(s, operands(b, row0, n_keys)[2], m_ref, acc_ref, b, row0)

    @pl.when(ki < first_diag)
    def _():
        run(False, 0, tk)

    if tq >= tk:
        variants = [(ki - first_diag == j, j * tk, tk) for j in range(tq // tk)]
    else:
        variants = [(qi % (tk // tq) == j, 0, (j + 1) * tq) for j in range(tk // tq)]
    for cond, row0, n_keys in variants:
        @pl.when((ki >= first_diag) & cond)
        def _():
            run(True, row0, n_keys)

    @pl.when(ki == last_k)
    def _():
        finalize()


def _nsa_sel_kernel(qtab, ktab, qaug_ref, kaug_ref, vaug_ref, pre_ref, gate_ref, o_ref,
                    m_ref, acc_ref):
    g_n, r_n, hd = NSA_KV_HEADS, NSA_GROUP, HEAD_DIM
    t = pl.program_id(0)
    qi, ki = qtab[t], ktab[t]

    def operands(g, row0, n_keys):
        return (qaug_ref[0, g, 0, row0:, :], kaug_ref[0:n_keys, g * 2 * LANES:(g + 1) * 2 * LANES],
                vaug_ref[0:n_keys, g * LANES:(g + 1) * LANES])

    def finalize():
        for g in range(g_n):
            acc = acc_ref[g]
            gate = gate_ref[g]
            o = acc[:, 0:hd] * (1.0 / acc[:, hd:hd + 1])
            for r in range(r_n):
                c = (g * r_n + r) * hd
                o_ref[:, c:c + hd] = (pre_ref[:, c:c + hd] + gate[:, 3 * r + 1:3 * r + 2]
                                      * o[r * TQ:(r + 1) * TQ]).astype(o_ref.dtype)

    _flash_steps(qi, ki, TQ, TK, r_n, g_n, operands, m_ref, acc_ref, finalize)


def _nsa_sel(qaug, kaug, vaug, pre, gates):
    n_var, g_n, n_q, rows, _ = qaug.shape
    s = kaug.shape[0]
    qtab, ktab = _causal_pairs(n_q, TQ, TK)
    tiles_per_var = BIAS_BLOCKS * SEL_BLOCK // TK
    grid_spec = pltpu.PrefetchScalarGridSpec(
        num_scalar_prefetch=2,
        grid=(qtab.shape[0],),
        in_specs=[
            pl.BlockSpec((1, g_n, 1, rows, 2 * LANES),
                         lambda t, qt, kt: (kt[t] // tiles_per_var, 0, qt[t], 0, 0)),
            pl.BlockSpec((TK, g_n * 2 * LANES), lambda t, qt, kt: (kt[t], 0)),
            pl.BlockSpec((TK, g_n * LANES), lambda t, qt, kt: (kt[t], 0)),
            pl.BlockSpec((TQ, N_HEADS * HEAD_DIM), lambda t, qt, kt: (qt[t], 0)),
            pl.BlockSpec((g_n, TQ, LANES), lambda t, qt, kt: (0, qt[t], 0)),
        ],
        out_specs=pl.BlockSpec((TQ, N_HEADS * HEAD_DIM), lambda t, qt, kt: (qt[t], 0)),
        scratch_shapes=[pltpu.VMEM((g_n, rows, LANES), F32), pltpu.VMEM((g_n, rows, LANES), F32)],
    )
    return pl.pallas_call(
        _nsa_sel_kernel,
        grid_spec=grid_spec,
        out_shape=jax.ShapeDtypeStruct((s, N_HEADS * HEAD_DIM), BF16),
        name="nsa_sel",
        compiler_params=_cparams("arbitrary"),
    )(qtab, ktab, qaug, kaug, vaug, pre, gates)


def _moba_proj_kernel(k_eff, x_ref, cos_ref, sa_ref, sb_ref, w_ref,
                      qaug_ref, kaug_ref, vaug_ref, kmean_ref):
    tm = x_ref.shape[0]
    hd, d = HEAD_DIM, N_HEADS * HEAD_DIM
    i = pl.program_id(0)

    @pl.when(i == 0)
    def _():
        kmean_ref[...] = jnp.zeros(kmean_ref.shape, F32)

    xb = x_ref[...].astype(BF16)
    cos_t, sin_a, sin_b = cos_ref[...], sa_ref[...], sb_ref[...]
    q = _rope(_dot(xb, w_ref[:, 0:d]), cos_t, sin_a, sin_b)
    k = _rope(_dot(xb, w_ref[:, d:2 * d]), cos_t, sin_a, sin_b)
    v = _dot(xb, w_ref[:, 2 * d:3 * d])
    kmean = kmean_ref[...]
    nb = kmean.shape[0]
    own = jnp.where(lax.broadcasted_iota(jnp.int32, (1, nb), 1) == i, 1.0, 0.0).astype(BF16)
    own_onehot = jnp.broadcast_to(own, (tm, nb))
    ones_col = _ones_col(tm, LANES - hd)
    blk_t = lax.broadcasted_iota(jnp.int32, (nb, tm), 0)
    own_t = jnp.where(blk_t == i, 1.0, 0.0)
    for hp in range(N_HEADS // 2):
        bias_t = []
        for h in (2 * hp, 2 * hp + 1):
            sl = slice(h * hd, (h + 1) * hd)
            gate_t = _dot_t(kmean[:, sl], q[:, sl], precision=lax.Precision.HIGHEST)
            gate_t = jnp.where(blk_t < i, gate_t, -jnp.inf)
            bias_t.append(jnp.where(_top_k_axis0(gate_t, k_eff, own_t)[0] > 0, 0.0, MASKED))
        bias = jnp.concatenate(bias_t, axis=0).T.astype(BF16)
        for j, h in enumerate((2 * hp, 2 * hp + 1)):
            sl = slice(h * hd, (h + 1) * hd)
            qaug_ref[h, :, 0:hd] = (q[:, sl] * SCALE).astype(BF16)
            qaug_ref[h, :, hd:hd + nb] = bias[:, j * nb:(j + 1) * nb]
            kaug_ref[h, :, 0:hd] = k[:, sl].astype(BF16)
            kaug_ref[h, :, hd:hd + nb] = own_onehot
            vaug_ref[h, :, 0:hd] = v[:, sl].astype(BF16)
            vaug_ref[h, :, hd:LANES] = ones_col
    kmean_ref[pl.ds(i, 1), :] = jnp.mean(k, axis=0, keepdims=True)


def _moba_proj(x, tabs, w):
    s, d = x.shape
    n_blk = s // MOBA_BLOCK
    k_eff = min(MOBA_TOPK, max(n_blk - 1, 1))
    nb = LANES - HEAD_DIM
    assert s % MOBA_BLOCK == 0 and n_blk <= nb
    tm = MOBA_BLOCK
    row = lambda i: (i, 0)
    head = jax.ShapeDtypeStruct((N_HEADS, s, LANES), BF16)
    head_spec = pl.BlockSpec((N_HEADS, tm, LANES), lambda i: (0, i, 0))
    return pl.pallas_call(
        functools.partial(_moba_proj_kernel, k_eff),
        grid=(s // tm,),
        in_specs=[pl.BlockSpec((tm, d), row)] + [pl.BlockSpec((tm, LANES), row)] * 3
        + [pl.BlockSpec(w.shape, lambda i: (0, 0))],
        out_specs=[head_spec] * 3,
        out_shape=[head] * 3,
        scratch_shapes=[pltpu.VMEM((nb, N_HEADS * HEAD_DIM), F32)],
        name="moba_proj",
        compiler_params=_cparams("arbitrary"),
    )(x, *tabs, w)


def _moba_attn_kernel(qtab, ktab, q_ref, k_ref, v_ref, o_ref, m_ref, acc_ref):
    hd = HEAD_DIM
    t = pl.program_id(0)
    qi, ki = qtab[t], ktab[t]

    def operands(h, row0, n_keys):
        return q_ref[h, row0:, :], k_ref[h, 0:n_keys, :], v_ref[h, 0:n_keys, :]

    def finalize():
        for h in range(N_HEADS):
            acc = acc_ref[h]
            o_ref[:, h * hd:(h + 1) * hd] = (acc[:, 0:hd] * (1.0 / acc[:, hd:hd + 1])).astype(o_ref.dtype)

    _flash_steps(qi, ki, q_ref.shape[1], k_ref.shape[1], 1, N_HEADS, operands, m_ref, acc_ref, finalize)


def _moba_attn(qaug, kaug, vaug):
    h_n, s, _ = qaug.shape
    tq, tk = min(MOBA_TQ, s), min(MOBA_TK, s)
    assert s % tq == 0 and s % tk == 0
    qtab, ktab = _causal_pairs(s // tq, tq, tk)
    grid_spec = pltpu.PrefetchScalarGridSpec(
        num_scalar_prefetch=2,
        grid=(qtab.shape[0],),
        in_specs=[pl.BlockSpec((h_n, tq, LANES), lambda t, qt, kt: (0, qt[t], 0)),
                  pl.BlockSpec((h_n, tk, LANES), lambda t, qt, kt: (0, kt[t], 0)),
                  pl.BlockSpec((h_n, tk, LANES), lambda t, qt, kt: (0, kt[t], 0))],
        out_specs=pl.BlockSpec((tq, h_n * HEAD_DIM), lambda t, qt, kt: (qt[t], 0)),
        scratch_shapes=[pltpu.VMEM((h_n, tq, LANES), F32), pltpu.VMEM((h_n, tq, LANES), F32)],
    )
    return pl.pallas_call(
        _moba_attn_kernel,
        grid_spec=grid_spec,
        out_shape=jax.ShapeDtypeStruct((s, h_n * HEAD_DIM), BF16),
        name="moba_attn",
        compiler_params=_cparams("arbitrary"),
    )(qtab, ktab, qaug, kaug, vaug)


def _layer_norm(z, gain, bias):
    mu = jnp.mean(z, axis=-1, keepdims=True)
    zc = z - mu
    var = jnp.mean(zc * zc, axis=-1, keepdims=True)
    return zc * lax.rsqrt(var + LN_EPS) * gain + bias


def _post_mixer_kernel(alpha, n_chunk, a_ref, x_ref, wo_ref, g1_ref, b1_ref, wg_ref, wu_ref, wd_ref,
                       g2_ref, b2_ref, o_ref):
    x1 = _layer_norm(alpha * x_ref[...] + _dot(a_ref[...], wo_ref[...]), g1_ref[...], b1_ref[...])
    xb = x1.astype(BF16)
    cw = wg_ref.shape[1] // n_chunk
    f = None
    for c in range(n_chunk):
        sl = slice(c * cw, (c + 1) * cw)
        u = (jax.nn.silu(_dot(xb, wg_ref[:, sl])) * _dot(xb, wu_ref[:, sl])).astype(BF16)
        part = _dot(u, wd_ref[sl, :])
        f = part if f is None else f + part
    o_ref[...] = _layer_norm(alpha * x1 + f, g2_ref[...], b2_ref[...])


def _post_mixer(a, x, w_o, g1, b1, wg, wu, wd, g2, b2, alpha):
    s, d = x.shape
    f = wg.shape[1]
    n_chunk = 2 if f % (2 * LANES) == 0 else 1
    tm = min(MM_TILE, s)
    row = lambda i: (i, 0)

    def resident(shape):
        return pl.BlockSpec(shape, lambda i: (0, 0), pipeline_mode=pl.Buffered(1))

    vec = lambda v: v.reshape(1, d)
    return pl.pallas_call(
        functools.partial(_post_mixer_kernel, alpha, n_chunk),
        grid=(s // tm,),
        in_specs=[pl.BlockSpec((tm, a.shape[1]), row), pl.BlockSpec((tm, d), row),
                  resident(w_o.shape), resident((1, d)), resident((1, d)),
                  resident(wg.shape), resident(wu.shape), resident(wd.shape),
                  resident((1, d)), resident((1, d))],
        out_specs=pl.BlockSpec((tm, d), row),
        out_shape=jax.ShapeDtypeStruct((s, d), F32),
        name="post_mixer",
        compiler_params=_cparams("parallel"),
    )(a, x, w_o, vec(g1), vec(b1), wg, wu, wd, vec(g2), vec(b2))


def _nsa_mixer(x, tabs, w_in, ck_pos, ck_w1, ck_w2, cv_pos, cv_w1, cv_w2):
    s = x.shape[0]
    g_n, r_n, hd = NSA_KV_HEADS, NSA_GROUP, HEAD_DIM
    q_dim, kv_dim = N_HEADS * hd, NSA_KV_HEADS * hd
    n_sel = s // SEL_BLOCK
    n_var = -(-n_sel // BIAS_BLOCKS)
    assert s % TK == 0 and (BIAS_BLOCKS * SEL_BLOCK) % TK == 0
    wq = w_in[:, :q_dim].astype(BF16)
    wkv = w_in[:, q_dim:q_dim + 6 * kv_dim].astype(BF16)
    wg = w_in[:, q_dim + 6 * kv_dim:].reshape(-1, g_n, 3 * r_n)
    wg = jnp.pad(wg, ((0, 0), (0, 0), (0, LANES - 3 * r_n))).reshape(-1, g_n * LANES).astype(BF16)
    q, cmp_kv, kaug, vaug, kwin, vwin, gates = _nsa_proj(x, tabs, wq, wkv, wg)
    comp = _compress(cmp_kv, jnp.stack([ck_pos, cv_pos]),
                     jnp.stack([ck_w1, cv_w1]).astype(BF16), jnp.stack([ck_w2, cv_w2]).astype(BF16))
    kc = comp[0].astype(BF16)
    ones = jnp.ones(comp.shape[1:3] + (1,), F32)
    zeros = jnp.zeros(comp.shape[1:3] + (LANES - hd - 1,), F32)
    vc = jnp.concatenate([comp[1], ones, zeros], axis=-1).astype(BF16)
    pre, qaug = _nsa_local(q, kc, vc, kwin, vwin, gates, n_sel, n_var)
    return _nsa_sel(qaug, kaug, vaug, pre, gates)


def _moba_mixer(x, tabs, w_in):
    qaug, kaug, vaug = _moba_proj(x, tabs, w_in.astype(BF16))
    return _moba_attn(qaug, kaug, vaug)


def kernel(x, positions, nsa_w_in, nsa_w_o, nsa_ck_pos, nsa_ck_w1, nsa_ck_w2, nsa_cv_pos, nsa_cv_w1, nsa_cv_w2, moba_w_in, moba_w_o, ffn_wg, ffn_wu, ffn_wd, ln1_g, ln1_b, ln2_g, ln2_b):
    b, s, d = x.shape
    depth = ffn_wg.shape[0]
    alpha = (2 * depth) ** 0.25
    outs = []
    for bi in range(b):
        h = x[bi]
        tabs = _rope_tables(positions[bi])
        for i in range(depth):
            j = i // 2
            if i % 2 == 0:
                a = _nsa_mixer(h, tabs, nsa_w_in[j], nsa_ck_pos[j], nsa_ck_w1[j], nsa_ck_w2[j],
                               nsa_cv_pos[j], nsa_cv_w1[j], nsa_cv_w2[j])
                w_o = nsa_w_o[j]
            else:
                a = _moba_mixer(h, tabs, moba_w_in[j])
                w_o = moba_w_o[j]
            h = _post_mixer(a, h, w_o.astype(BF16), ln1_g[i], ln1_b[i], ffn_wg[i].astype(BF16),
                            ffn_wu[i].astype(BF16), ffn_wd[i].astype(BF16), ln2_g[i], ln2_b[i], alpha)
        outs.append(h)
    return jnp.stack(outs)
```

```python
import functools

import numpy as np
import jax
import jax.numpy as jnp
from jax import lax
from jax.experimental import pallas as pl
from jax.experimental.pallas import tpu as pltpu

F32 = jnp.float32
BF16 = jnp.bfloat16

N_HEADS = 16
HEAD_DIM = 64
ROPE_DIM = HEAD_DIM // 4
ROPE_THETA = 500000.0
NSA_KV_HEADS = 4
NSA_GROUP = N_HEADS // NSA_KV_HEADS
CMP_BLOCK = 32
CMP_STRIDE = 16
SEL_BLOCK = 64
SEL_TOPK = 16
WINDOW = 512
MOBA_BLOCK = 256
MOBA_TOPK = 3
LN_EPS = 1e-5
SCALE = HEAD_DIM ** -0.5 * float(np.log2(np.e))

LANES = 128
SUBLANES = 8
VMEM_LIMIT_BYTES = 48 * 1024 * 1024

ROW_TILE = 256
MM_TILE = 512
TQ = 256
TK = 1024
MOBA_TQ = 1024
MOBA_TK = 512
QK_AHEAD = 1
SEL_PREFIXES = 4
NSA_LOCAL_GROUPS = 1
BIAS_BLOCKS = 128
MASKED = -1e30
M_INIT = -5e29

assert WINDOW % TQ == 0 and TK % TQ == 0 and ROW_TILE == MOBA_BLOCK


def _cparams(*sem):
    return pltpu.CompilerParams(dimension_semantics=sem, vmem_limit_bytes=VMEM_LIMIT_BYTES)


def _dot(a, b):
    return jnp.dot(a, b, preferred_element_type=F32)


def _dot_t(a, b, precision=None):
    return lax.dot_general(a, b, (((1,), (1,)), ((), ())), preferred_element_type=F32,
                           precision=precision)


def _rope_tables_kernel(pos_ref, c_ref, cos_ref, sa_ref, sb_ref):
    ang = pos_ref[...].astype(F32) * c_ref[0:1, :]
    c = jnp.cos(ang)
    s = jnp.sin(ang)
    cos_ref[...] = jnp.where(c_ref[1:2, :] > 0, c, 1.0)
    sa_ref[...] = jnp.where(c_ref[2:3, :] > 0, s, 0.0)
    sb_ref[...] = jnp.where(c_ref[3:4, :] > 0, -s, 0.0)


def _rope_tables(positions):
    s = positions.shape[0]
    half = ROPE_DIM // 2
    inv = ROPE_THETA ** (-jnp.arange(0, ROPE_DIM, 2, dtype=F32) / ROPE_DIM)
    d = np.arange(LANES) % HEAD_DIM
    consts = jnp.zeros((SUBLANES, LANES), F32)
    consts = consts.at[0].set(inv[d % half])
    consts = consts.at[1].set(jnp.asarray(d < ROPE_DIM, F32))
    consts = consts.at[2].set(jnp.asarray((d >= half) & (d < ROPE_DIM), F32))
    consts = consts.at[3].set(jnp.asarray(d < half, F32))
    tm = min(MM_TILE, s)
    tab = jax.ShapeDtypeStruct((s, LANES), F32)
    return pl.pallas_call(
        _rope_tables_kernel,
        grid=(s // tm,),
        in_specs=[pl.BlockSpec((tm, 1), lambda i: (i, 0)),
                  pl.BlockSpec((SUBLANES, LANES), lambda i: (0, 0))],
        out_specs=[pl.BlockSpec((tm, LANES), lambda i: (i, 0))] * 3,
        out_shape=[tab, tab, tab],
        name="rope_tables",
        compiler_params=_cparams("parallel"),
    )(positions.reshape(s, 1), consts)


def _rope(x, cos_t, sin_a, sin_b):
    half = ROPE_DIM // 2
    outs = []
    for c in range(x.shape[1] // LANES):
        xs = x[:, c * LANES:(c + 1) * LANES]
        outs.append(xs * cos_t + pltpu.roll(xs, half, 1) * sin_a
                    + pltpu.roll(xs, LANES - half, 1) * sin_b)
    return outs[0] if len(outs) == 1 else jnp.concatenate(outs, axis=1)


def _ones_col(rows, width):
    return jnp.where(lax.broadcasted_iota(jnp.int32, (rows, width), 1) == 0, 1.0, 0.0).astype(BF16)


def _nsa_proj_kernel(x_ref, cos_ref, sa_ref, sb_ref, wq_ref, wkv_ref, wg_ref,
                     q_ref, cmp_ref, kaug_ref, vaug_ref, kwin_ref, vwin_ref, gate_ref):
    tm = x_ref.shape[0]
    g_n, hd, kvd = NSA_KV_HEADS, HEAD_DIM, NSA_KV_HEADS * HEAD_DIM
    xb = x_ref[...].astype(BF16)
    cos_t, sin_a, sin_b = cos_ref[...], sa_ref[...], sb_ref[...]
    q = _rope(_dot(xb, wq_ref[...]), cos_t, sin_a, sin_b) * SCALE
    q_ref[...] = q.astype(BF16)
    kv = _dot(xb, wkv_ref[...])
    k_cmp = _rope(kv[:, 0:kvd], cos_t, sin_a, sin_b)
    v_cmp = kv[:, kvd:2 * kvd]
    k_sel = _rope(kv[:, 2 * kvd:3 * kvd], cos_t, sin_a, sin_b)
    v_sel = kv[:, 3 * kvd:4 * kvd]
    k_win = _rope(kv[:, 4 * kvd:5 * kvd], cos_t, sin_a, sin_b)
    v_win = kv[:, 5 * kvd:6 * kvd]
    gate = jax.nn.sigmoid(_dot(xb, wg_ref[...]))
    tok = pl.program_id(0) * tm + lax.broadcasted_iota(jnp.int32, (tm, BIAS_BLOCKS), 0)
    blk = (tok // SEL_BLOCK) % BIAS_BLOCKS
    onehot = jnp.where(lax.broadcasted_iota(jnp.int32, (tm, BIAS_BLOCKS), 1) == blk,
                       1.0, 0.0).astype(BF16)
    ones_col = _ones_col(tm, LANES - hd)
    for g in range(g_n):
        sl = slice(g * hd, (g + 1) * hd)
        cmp_ref[0, g] = k_cmp[:, sl]
        cmp_ref[1, g] = v_cmp[:, sl]
        kwin_ref[g] = k_win[:, sl].astype(BF16)
        vwin_ref[g, :, 0:hd] = v_win[:, sl].astype(BF16)
        vwin_ref[g, :, hd:LANES] = ones_col
        gate_ref[g] = gate[:, g * LANES:(g + 1) * LANES]
        kb = g * 2 * LANES
        kaug_ref[:, kb:kb + BIAS_BLOCKS] = onehot
        kaug_ref[:, kb + BIAS_BLOCKS:kb + BIAS_BLOCKS + hd] = k_sel[:, sl].astype(BF16)
        kaug_ref[:, kb + BIAS_BLOCKS + hd:kb + 2 * LANES] = jnp.zeros((tm, 2 * LANES - BIAS_BLOCKS - hd), BF16)
        vb = g * LANES
        vaug_ref[:, vb:vb + hd] = v_sel[:, sl].astype(BF16)
        vaug_ref[:, vb + hd:vb + LANES] = ones_col


def _nsa_proj(x, tabs, wq, wkv, wg):
    s, d = x.shape
    g_n, hd = NSA_KV_HEADS, HEAD_DIM
    tm = min(ROW_TILE, s)
    row = lambda i: (i, 0)
    whole = lambda i: (0, 0)
    out_shape = [
        jax.ShapeDtypeStruct((s, N_HEADS * hd), BF16),
        jax.ShapeDtypeStruct((2, g_n, s, hd), F32),
        jax.ShapeDtypeStruct((s, g_n * 2 * LANES), BF16),
        jax.ShapeDtypeStruct((s, g_n * LANES), BF16),
        jax.ShapeDtypeStruct((g_n, s, hd), BF16),
        jax.ShapeDtypeStruct((g_n, s, LANES), BF16),
        jax.ShapeDtypeStruct((g_n, s, LANES), F32),
    ]
    out_specs = [
        pl.BlockSpec((tm, N_HEADS * hd), row),
        pl.BlockSpec((2, g_n, tm, hd), lambda i: (0, 0, i, 0)),
        pl.BlockSpec((tm, g_n * 2 * LANES), row),
        pl.BlockSpec((tm, g_n * LANES), row),
        pl.BlockSpec((g_n, tm, hd), lambda i: (0, i, 0)),
        pl.BlockSpec((g_n, tm, LANES), lambda i: (0, i, 0)),
        pl.BlockSpec((g_n, tm, LANES), lambda i: (0, i, 0)),
    ]
    return pl.pallas_call(
        _nsa_proj_kernel,
        grid=(s // tm,),
        in_specs=[pl.BlockSpec((tm, d), row)] + [pl.BlockSpec((tm, LANES), row)] * 3
        + [pl.BlockSpec(wq.shape, whole), pl.BlockSpec(wkv.shape, whole),
           pl.BlockSpec(wg.shape, whole)],
        out_specs=out_specs,
        out_shape=out_shape,
        name="nsa_proj",
        compiler_params=_cparams("parallel"),
    )(x, *tabs, wq, wkv, wg)


def _compress_kernel(x_ref, pos_ref, w1_ref, w2_ref, out_ref):
    n16 = out_ref.shape[2]
    hd = x_ref.shape[3]
    a = b = None
    for t in range(CMP_STRIDE):
        xt = x_ref[0, 0, pl.ds(t, n16, stride=CMP_STRIDE), :]
        lo = _dot((xt + pos_ref[0, t:t + 1, :]).astype(BF16), w1_ref[0, t * hd:(t + 1) * hd, :])
        t2 = CMP_STRIDE + t
        hi = _dot((xt + pos_ref[0, t2:t2 + 1, :]).astype(BF16), w1_ref[0, t2 * hd:(t2 + 1) * hd, :])
        a = lo if a is None else a + lo
        b = hi if b is None else b + hi
    hid = a + pltpu.roll(b, n16 - 1, 0)
    out_ref[0, 0] = _dot(jax.nn.gelu(hid).astype(BF16), w2_ref[0])


def _compress(cmp_kv, pos, w1, w2):
    _, g_n, s, hd = cmp_kv.shape
    n16 = s // CMP_STRIDE
    assert CMP_BLOCK == 2 * CMP_STRIDE
    return pl.pallas_call(
        _compress_kernel,
        grid=(2, g_n),
        in_specs=[pl.BlockSpec((1, 1, s, hd), lambda a, g: (a, g, 0, 0)),
                  pl.BlockSpec((1,) + pos.shape[1:], lambda a, g: (a, 0, 0)),
                  pl.BlockSpec((1,) + w1.shape[1:], lambda a, g: (a, 0, 0)),
                  pl.BlockSpec((1,) + w2.shape[1:], lambda a, g: (a, 0, 0))],
        out_specs=pl.BlockSpec((1, 1, n16, hd), lambda a, g: (a, g, 0, 0)),
        out_shape=jax.ShapeDtypeStruct((2, g_n, n16, hd), F32),
        name="nsa_compress",
        compiler_params=_cparams("parallel", "parallel"),
    )(cmp_kv, pos, w1, w2)


def _top_k_axis0(score, n_rounds, sel, enable=None):
    n = score.shape[0]
    row = lax.broadcasted_iota(jnp.int32, score.shape, 0).astype(F32)
    for _ in range(n_rounds):
        m = jnp.max(score, axis=0, keepdims=True)
        idx = jnp.min(jnp.where(score == m, row, float(n)), axis=0, keepdims=True)
        ok = m > -jnp.inf if enable is None else (m > -jnp.inf) & enable
        hit = row == jnp.where(ok, idx, -1.0)
        sel = jnp.where(hit, 1.0, sel)
        score = jnp.where(hit, -jnp.inf, score)
    return sel, score


def _nsa_local_kernel(n_sel, n_top, q_ref, kc_ref, vc_ref, ov_ref, wmask_ref, *rest):
    n_wb = WINDOW // TQ + 1
    kw_refs = rest[:n_wb]
    vw_refs = rest[n_wb:2 * n_wb]
    gate_ref, pre_ref, qaug_ref, m_scr, o_acc, imp_acc = rest[2 * n_wb:]
    r_n, hd = NSA_GROUP, HEAD_DIM
    gps = kc_ref.shape[0]
    qi = pl.program_id(1)
    qgs = [q_ref[:, j * r_n * hd:(j + 1) * r_n * hd] for j in range(gps)]
    qss = [jnp.concatenate([qg[:, r * hd:(r + 1) * hd] for r in range(r_n)], axis=0) for qg in qgs]
    rows = r_n * TQ

    nc = kc_ref.shape[1]
    ct = min(2 * LANES, nc)
    t_max = ((qi * TQ + TQ - CMP_BLOCK) // CMP_STRIDE) // ct
    t_full = (jnp.maximum(qi * TQ - (CMP_BLOCK - 1) + CMP_STRIDE, 0) // CMP_STRIDE) // ct

    m_scr[...] = jnp.full(m_scr.shape, M_INIT, F32)
    o_acc[...] = jnp.zeros(o_acc.shape, F32)
    imp_acc[...] = jnp.zeros(imp_acc.shape, F32)

    def cmp_tile(t, masked):
        tile = slice(t * ct, (t + 1) * ct)
        if masked:
            q_off = lax.broadcasted_iota(jnp.int32, (rows, ct), 0) % TQ
            tok = lax.broadcasted_iota(jnp.int32, (rows, ct), 1)
            visible = q_off - CMP_STRIDE * tok >= CMP_STRIDE * ct * t + CMP_BLOCK - 1 - qi * TQ
        for j in range(gps):
            s = _dot_t(qss[j], kc_ref[j, tile, :])
            if masked:
                s = jnp.where(visible, s, MASKED)
            m_prev = m_scr[j]
            m_new = jnp.maximum(m_prev, jnp.max(s, axis=1, keepdims=True))
            alpha = jnp.exp2(m_prev - m_new)
            p = jnp.concatenate([jnp.exp2(s[:, c * LANES:(c + 1) * LANES] - m_new).astype(BF16)
                                 for c in range(ct // LANES)], axis=1)
            o_acc[j] = alpha * o_acc[j] + _dot(p, vc_ref[j, tile, :])
            imp_new = _dot(p, ov_ref[tile, :])
            imp_acc[j] = jnp.concatenate(
                [alpha * imp_acc[j, :, c * LANES:(c + 1) * LANES] + imp_new[:, c * LANES:(c + 1) * LANES]
                 for c in range(imp_new.shape[1] // LANES)], axis=1)
            m_scr[j] = m_new

    for t in range(nc // ct):
        pl.when(t < t_full)(functools.partial(cmp_tile, t, False))
        pl.when((t >= t_full) & (t <= t_max))(functools.partial(cmp_tile, t, True))

    n_prefix = SEL_PREFIXES if n_sel % (SEL_PREFIXES * SUBLANES) == 0 else 1
    prefix = ((qi + 1) * (TQ // SEL_BLOCK) - 1) * n_prefix // n_sel
    for v in range(n_prefix):
        @pl.when(prefix == v)
        def _():
            for j in range(gps):
                _nsa_local_tail(j, n_sel * (v + 1) // n_prefix, v == 0, n_sel, n_top, qi, qgs[j], qss[j],
                                o_acc, imp_acc, wmask_ref, kw_refs, vw_refs, gate_ref, pre_ref, qaug_ref)


def _nsa_local_tail(j, n_rows, first_tile, n_sel, n_top, qi, qg, qs, o_acc, imp_acc, wmask_ref,
                    kw_refs, vw_refs, gate_ref, pre_ref, qaug_ref):
    r_n, hd = NSA_GROUP, HEAD_DIM
    rows = r_n * TQ
    o_aug = o_acc[j]
    l_col = o_aug[:, hd:hd + 1]
    inv_l = jnp.broadcast_to(jnp.where(l_col > 0, 1.0 / l_col, 0.0), (rows, LANES))
    o_cmp = o_aug[:, 0:hd] * inv_l[:, 0:hd]

    n_pad = -(-n_rows // LANES) * LANES
    imp_q = None
    for r in range(r_n):
        rs = slice(r * TQ, (r + 1) * TQ)
        imp_r = jnp.concatenate([imp_acc[j, rs, c * LANES:(c + 1) * LANES] * inv_l[rs]
                                 for c in range(n_pad // LANES)], axis=1)
        imp_q = imp_r if imp_q is None else imp_q + imp_r
    imp = imp_q.T[0:n_rows]
    blk = lax.broadcasted_iota(jnp.int32, (n_rows, TQ), 0)
    blk_q = (qi * TQ + lax.broadcasted_iota(jnp.int32, (n_rows, TQ), 1)) // SEL_BLOCK
    forced = (blk == 0) | (blk == blk_q) | (blk == blk_q - 1)
    score = jnp.where((blk <= blk_q) & jnp.logical_not(forced), imp, -jnp.inf)
    sel = jnp.where(forced, 1.0, 0.0)
    n_forced_min = 3
    blk_q_row = blk_q[0:1, :]
    sel_tiles = []
    for c in range(TQ // LANES):
        ls = slice(c * LANES, (c + 1) * LANES)
        sel_c, score_c = _top_k_axis0(score[:, ls], n_top - n_forced_min, sel[:, ls])
        if first_tile:
            for extra in range(1, n_forced_min):
                sel_c, score_c = _top_k_axis0(score_c, 1, sel_c,
                                              enable=blk_q_row[:, ls] < n_forced_min - extra)
        sel_tiles.append(sel_c)
    sel = jnp.concatenate(sel_tiles, axis=1)
    n_pad = -(-n_rows // LANES) * LANES
    if n_pad > n_rows:
        sel = jnp.concatenate([sel, jnp.zeros((n_pad - n_rows, TQ), F32)], axis=0)
    bias = jnp.where(sel > 0, 0.0, MASKED).T.astype(BF16)
    n_var = qaug_ref.shape[0]
    if n_pad < n_var * BIAS_BLOCKS:
        bias = jnp.concatenate(
            [bias, jnp.full((TQ, n_var * BIAS_BLOCKS - n_pad), MASKED, BF16)], axis=1)
    zeros = jnp.zeros((TQ, 2 * LANES - BIAS_BLOCKS - hd), BF16)
    for v in range(n_var):
        for r in range(r_n):
            rs = slice(r * TQ, (r + 1) * TQ)
            qaug_ref[v, j, 0, rs, 0:BIAS_BLOCKS] = bias[:, v * BIAS_BLOCKS:(v + 1) * BIAS_BLOCKS]
            qaug_ref[v, j, 0, rs, BIAS_BLOCKS:BIAS_BLOCKS + hd] = qg[:, r * hd:(r + 1) * hd]
            qaug_ref[v, j, 0, rs, BIAS_BLOCKS + hd:2 * LANES] = zeros

    kcat = jnp.concatenate([kw[j] for kw in kw_refs], axis=0)
    vcat = jnp.concatenate([vw[j] for vw in vw_refs], axis=0)
    s = _dot_t(qs, kcat)
    wmask = wmask_ref[0]
    s = jnp.concatenate([s[r * TQ:(r + 1) * TQ] + wmask for r in range(r_n)], axis=0)
    m = jnp.broadcast_to(jnp.max(s, axis=1, keepdims=True), (rows, LANES))
    p = jnp.concatenate([jnp.exp2(s[:, c * LANES:(c + 1) * LANES] - m).astype(BF16)
                         for c in range(s.shape[1] // LANES)], axis=1)
    o_aug = _dot(p, vcat)
    o_win = o_aug[:, 0:hd] * (1.0 / o_aug[:, hd:hd + 1])

    gate = gate_ref[j]
    for r in range(r_n):
        rs = slice(r * TQ, (r + 1) * TQ)
        c = (j * r_n + r) * hd
        pre_ref[:, c:c + hd] = (gate[:, 3 * r:3 * r + 1] * o_cmp[rs]
                                + gate[:, 3 * r + 2:3 * r + 3] * o_win[rs])


def _overlap_matrix(nc, n_sel):
    i = np.arange(nc)[:, None]
    j = np.arange(n_sel)[None, :]
    lo = np.maximum(i * CMP_STRIDE, j * SEL_BLOCK)
    hi = np.minimum(i * CMP_STRIDE + CMP_BLOCK - 1, j * SEL_BLOCK + SEL_BLOCK - 1)
    ov = np.zeros((nc, -(-n_sel // LANES) * LANES), np.float32)
    ov[:, :n_sel] = np.clip(hi - lo + 1, 0, None) / CMP_STRIDE
    return jnp.asarray(ov, BF16)


def _window_masks(n_wb):
    q = np.arange(TQ)[:, None]
    c = np.arange(n_wb * TQ)[None, :]
    dist = q + (n_wb - 1) * TQ - c
    masks = []
    for v in range(n_wb):
        kpos = (v - (n_wb - 1)) * TQ + c
        masks.append(np.where((kpos >= 0) & (dist >= 0) & (dist < WINDOW), 0.0, MASKED))
    return jnp.asarray(np.stack(masks), F32)


def _nsa_local(q, kc, vc, kwin, vwin, gates, n_sel, n_var):
    s = q.shape[0]
    g_n, r_n, hd = NSA_KV_HEADS, NSA_GROUP, HEAD_DIM
    n_q = s // TQ
    n_wb = WINDOW // TQ + 1
    n_top = min(SEL_TOPK, n_sel)
    nc = kc.shape[1]
    rows = r_n * TQ
    gps = NSA_LOCAL_GROUPS
    assert n_top >= 3 and nc % min(2 * LANES, nc) == 0 and g_n % gps == 0
    overlap = _overlap_matrix(nc, n_sel)

    def win_spec(j, width):
        return pl.BlockSpec((gps, TQ, width), lambda g, i: (g, jnp.maximum(i - (n_wb - 1) + j, 0), 0))

    return pl.pallas_call(
        functools.partial(_nsa_local_kernel, n_sel, n_top),
        grid=(g_n // gps, n_q),
        in_specs=[pl.BlockSpec((TQ, gps * r_n * hd), lambda g, i: (i, g)),
                  pl.BlockSpec((gps, nc, hd), lambda g, i: (g, 0, 0)),
                  pl.BlockSpec((gps, nc, LANES), lambda g, i: (g, 0, 0)),
                  pl.BlockSpec(overlap.shape, lambda g, i: (0, 0)),
                  pl.BlockSpec((1, TQ, n_wb * TQ), lambda g, i: (jnp.minimum(i, n_wb - 1), 0, 0))]
        + [win_spec(j, hd) for j in range(n_wb)] + [win_spec(j, LANES) for j in range(n_wb)]
        + [pl.BlockSpec((gps, TQ, LANES), lambda g, i: (g, i, 0))],
        out_specs=[pl.BlockSpec((TQ, gps * r_n * hd), lambda g, i: (i, g)),
                   pl.BlockSpec((n_var, gps, 1, rows, 2 * LANES), lambda g, i: (0, g, i, 0, 0))],
        out_shape=[jax.ShapeDtypeStruct((s, N_HEADS * hd), F32),
                   jax.ShapeDtypeStruct((n_var, g_n, n_q, rows, 2 * LANES), BF16)],
        scratch_shapes=[pltpu.VMEM((gps, rows, LANES), F32), pltpu.VMEM((gps, rows, LANES), F32),
                        pltpu.VMEM((gps, rows, overlap.shape[1]), F32)],
        name="nsa_local",
        compiler_params=_cparams("parallel", "parallel"),
    )(q, kc, vc, overlap, _window_masks(n_wb),
      *([kwin] * n_wb), *([vwin] * n_wb), gates)


def _causal_pairs(n_q, tq, tk):
    qs, ks = [], []
    for qi in range(n_q):
        for ki in range(((qi + 1) * tq - 1) // tk + 1):
            qs.append(qi)
            ks.append(ki)
    return jnp.asarray(qs, jnp.int32), jnp.asarray(ks, jnp.int32)


def _flash_scores(q, k, causal_mask):
    s = _dot_t(q, k)
    return s if causal_mask is None else jnp.where(causal_mask, s, MASKED)


def _flash_accumulate(s, v, m_ref, acc_ref, idx, row0):
    m_prev = m_ref[idx, row0:, :]
    m_new = jnp.maximum(m_prev, jnp.max(s, axis=1, keepdims=True))
    p = jnp.concatenate([jnp.exp2(s[:, c * LANES:(c + 1) * LANES] - m_new).astype(BF16)
                         for c in range(s.shape[1] // LANES)], axis=1)
    acc_ref[idx, row0:, :] = jnp.exp2(m_prev - m_new) * acc_ref[idx, row0:, :] + _dot(p, v)
    m_ref[idx, row0:, :] = m_new


def _flash_steps(qi, ki, tq, tk, rows_per_q, n_batch, operands, m_ref, acc_ref, finalize):
    first_diag = (qi * tq) // tk
    last_k = ((qi + 1) * tq - 1) // tk
    assert rows_per_q == 1 or tq <= tk

    @pl.when(ki == 0)
    def _():
        m_ref[...] = jnp.full(m_ref.shape, M_INIT, F32)
        acc_ref[...] = jnp.zeros(acc_ref.shape, F32)

    def run(masked, row0, n_keys):
        mask = None
        if masked:
            shape = (rows_per_q * tq - row0, n_keys)
            q_off = lax.broadcasted_iota(jnp.int32, shape, 0) % tq + row0
            k_off = lax.broadcasted_iota(jnp.int32, shape, 1)
            mask = q_off - k_off >= ki * tk - qi * tq
        def scores(b):
            q, k, _ = operands(b, row0, n_keys)
            return _flash_scores(q, k, mask)

        ahead = [scores(b) for b in range(min(QK_AHEAD, n_batch))]
        for b in range(n_batch):
            s = ahead.pop(0)
            if b + QK_AHEAD < n_batch:
                ahead.append(scores(b + QK_AHEAD))
            _flash_accumulate(s, operands(b, row0, n_keys)[2], m_ref, acc_ref, b, row0)

    @pl.when(ki < first_diag)
    def _():
        run(False, 0, tk)

    if tq >= tk:
        variants = [(ki - first_diag == j, j * tk, tk) for j in range(tq // tk)]
    else:
        variants = [(qi % (tk // tq) == j, 0, (j + 1) * tq) for j in range(tk // tq)]
    for cond, row0, n_keys in variants:
        @pl.when((ki >= first_diag) & cond)
        def _():
            run(True, row0, n_keys)

    @pl.when(ki == last_k)
    def _():
        finalize()


def _nsa_sel_kernel(qtab, ktab, qaug_ref, kaug_ref, vaug_ref, pre_ref, gate_ref, o_ref,
                    m_ref, acc_ref):
    g_n, r_n, hd = NSA_KV_HEADS, NSA_GROUP, HEAD_DIM
    t = pl.program_id(0)
    qi, ki = qtab[t], ktab[t]

    def operands(g, row0, n_keys):
        return (qaug_ref[0, g, 0, row0:, :], kaug_ref[0:n_keys, g * 2 * LANES:(g + 1) * 2 * LANES],
                vaug_ref[0:n_keys, g * LANES:(g + 1) * LANES])

    def finalize():
        for g in range(g_n):
            acc = acc_ref[g]
            gate = gate_ref[g]
            o = acc[:, 0:hd] * (1.0 / acc[:, hd:hd + 1])
            for r in range(r_n):
                c = (g * r_n + r) * hd
                o_ref[:, c:c + hd] = (pre_ref[:, c:c + hd] + gate[:, 3 * r + 1:3 * r + 2]
                                      * o[r * TQ:(r + 1) * TQ]).astype(o_ref.dtype)

    _flash_steps(qi, ki, TQ, TK, r_n, g_n, operands, m_ref, acc_ref, finalize)


def _nsa_sel(qaug, kaug, vaug, pre, gates):
    n_var, g_n, n_q, rows, _ = qaug.shape
    s = kaug.shape[0]
    qtab, ktab = _causal_pairs(n_q, TQ, TK)
    tiles_per_var = BIAS_BLOCKS * SEL_BLOCK // TK
    grid_spec = pltpu.PrefetchScalarGridSpec(
        num_scalar_prefetch=2,
        grid=(qtab.shape[0],),
        in_specs=[
            pl.BlockSpec((1, g_n, 1, rows, 2 * LANES),
                         lambda t, qt, kt: (kt[t] // tiles_per_var, 0, qt[t], 0, 0)),
            pl.BlockSpec((TK, g_n * 2 * LANES), lambda t, qt, kt: (kt[t], 0)),
            pl.BlockSpec((TK, g_n * LANES), lambda t, qt, kt: (kt[t], 0)),
            pl.BlockSpec((TQ, N_HEADS * HEAD_DIM), lambda t, qt, kt: (qt[t], 0)),
            pl.BlockSpec((g_n, TQ, LANES), lambda t, qt, kt: (0, qt[t], 0)),
        ],
        out_specs=pl.BlockSpec((TQ, N_HEADS * HEAD_DIM), lambda t, qt, kt: (qt[t], 0)),
        scratch_shapes=[pltpu.VMEM((g_n, rows, LANES), F32), pltpu.VMEM((g_n, rows, LANES), F32)],
    )
    return pl.pallas_call(
        _nsa_sel_kernel,
        grid_spec=grid_spec,
        out_shape=jax.ShapeDtypeStruct((s, N_HEADS * HEAD_DIM), BF16),
        name="nsa_sel",
        compiler_params=_cparams("arbitrary"),
    )(qtab, ktab, qaug, kaug, vaug, pre, gates)


def _moba_proj_kernel(k_eff, x_ref, cos_ref, sa_ref, sb_ref, w_ref,
                      qaug_ref, kaug_ref, vaug_ref, kmean_ref):
    tm = x_ref.shape[0]
    hd, d = HEAD_DIM, N_HEADS * HEAD_DIM
    i = pl.program_id(0)

    @pl.when(i == 0)
    def _():
        kmean_ref[...] = jnp.zeros(kmean_ref.shape, F32)

    xb = x_ref[...].astype(BF16)
    cos_t, sin_a, sin_b = cos_ref[...], sa_ref[...], sb_ref[...]
    q = _rope(_dot(xb, w_ref[:, 0:d]), cos_t, sin_a, sin_b)
    k = _rope(_dot(xb, w_ref[:, d:2 * d]), cos_t, sin_a, sin_b)
    v = _dot(xb, w_ref[:, 2 * d:3 * d])
    kmean = kmean_ref[...]
    nb = kmean.shape[0]
    own = jnp.where(lax.broadcasted_iota(jnp.int32, (1, nb), 1) == i, 1.0, 0.0).astype(BF16)
    own_onehot = jnp.broadcast_to(own, (tm, nb))
    ones_col = _ones_col(tm, LANES - hd)
    blk_t = lax.broadcasted_iota(jnp.int32, (nb, tm), 0)
    own_t = jnp.where(blk_t == i, 1.0, 0.0)
    q_hi, km_hi = q.astype(BF16), kmean.astype(BF16)
    q_lo = (q - q_hi.astype(F32)).astype(BF16)
    km_lo = (kmean - km_hi.astype(F32)).astype(BF16)
    for hp in range(N_HEADS // 2):
        bias_t = []
        for h in (2 * hp, 2 * hp + 1):
            sl = slice(h * hd, (h + 1) * hd)
            gate_t = (_dot_t(km_hi[:, sl], q_hi[:, sl]) + _dot_t(km_hi[:, sl], q_lo[:, sl])
                      + _dot_t(km_lo[:, sl], q_hi[:, sl]))
            gate_t = jnp.where(blk_t < i, gate_t, -jnp.inf)
            bias_t.append(jnp.where(_top_k_axis0(gate_t, k_eff, own_t)[0] > 0, 0.0, MASKED))
        bias = jnp.concatenate(bias_t, axis=0).T.astype(BF16)
        for j, h in enumerate((2 * hp, 2 * hp + 1)):
            sl = slice(h * hd, (h + 1) * hd)
            qaug_ref[h, :, 0:hd] = (q[:, sl] * SCALE).astype(BF16)
            qaug_ref[h, :, hd:hd + nb] = bias[:, j * nb:(j + 1) * nb]
            kaug_ref[h, :, 0:hd] = k[:, sl].astype(BF16)
            kaug_ref[h, :, hd:hd + nb] = own_onehot
            vaug_ref[h, :, 0:hd] = v[:, sl].astype(BF16)
            vaug_ref[h, :, hd:LANES] = ones_col
    kmean_ref[pl.ds(i, 1), :] = jnp.mean(k, axis=0, keepdims=True)


def _moba_proj(x, tabs, w):
    s, d = x.shape
    n_blk = s // MOBA_BLOCK
    k_eff = min(MOBA_TOPK, max(n_blk - 1, 1))
    nb = LANES - HEAD_DIM
    assert s % MOBA_BLOCK == 0 and n_blk <= nb
    tm = MOBA_BLOCK
    row = lambda i: (i, 0)
    head = jax.ShapeDtypeStruct((N_HEADS, s, LANES), BF16)
    head_spec = pl.BlockSpec((N_HEADS, tm, LANES), lambda i: (0, i, 0))
    return pl.pallas_call(
        functools.partial(_moba_proj_kernel, k_eff),
        grid=(s // tm,),
        in_specs=[pl.BlockSpec((tm, d), row)] + [pl.BlockSpec((tm, LANES), row)] * 3
        + [pl.BlockSpec(w.shape, lambda i: (0, 0))],
        out_specs=[head_spec] * 3,
        out_shape=[head] * 3,
        scratch_shapes=[pltpu.VMEM((nb, N_HEADS * HEAD_DIM), F32)],
        name="moba_proj",
        compiler_params=_cparams("arbitrary"),
    )(x, *tabs, w)


def _moba_attn_kernel(qtab, ktab, q_ref, k_ref, v_ref, o_ref, m_ref, acc_ref):
    hd = HEAD_DIM
    t = pl.program_id(0)
    qi, ki = qtab[t], ktab[t]

    def operands(h, row0, n_keys):
        return q_ref[h, row0:, :], k_ref[h, 0:n_keys, :], v_ref[h, 0:n_keys, :]

    def finalize():
        for h in range(N_HEADS):
            acc = acc_ref[h]
            o_ref[:, h * hd:(h + 1) * hd] = (acc[:, 0:hd] * (1.0 / acc[:, hd:hd + 1])).astype(o_ref.dtype)

    _flash_steps(qi, ki, q_ref.shape[1], k_ref.shape[1], 1, N_HEADS, operands, m_ref, acc_ref, finalize)


def _moba_attn(qaug, kaug, vaug):
    h_n, s, _ = qaug.shape
    tq, tk = min(MOBA_TQ, s), min(MOBA_TK, s)
    assert s % tq == 0 and s % tk == 0
    qtab, ktab = _causal_pairs(s // tq, tq, tk)
    grid_spec = pltpu.PrefetchScalarGridSpec(
        num_scalar_prefetch=2,
        grid=(qtab.shape[0],),
        in_specs=[pl.BlockSpec((h_n, tq, LANES), lambda t, qt, kt: (0, qt[t], 0)),
                  pl.BlockSpec((h_n, tk, LANES), lambda t, qt, kt: (0, kt[t], 0)),
                  pl.BlockSpec((h_n, tk, LANES), lambda t, qt, kt: (0, kt[t], 0))],
        out_specs=pl.BlockSpec((tq, h_n * HEAD_DIM), lambda t, qt, kt: (qt[t], 0)),
        scratch_shapes=[pltpu.VMEM((h_n, tq, LANES), F32), pltpu.VMEM((h_n, tq, LANES), F32)],
    )
    return pl.pallas_call(
        _moba_attn_kernel,
        grid_spec=grid_spec,
        out_shape=jax.ShapeDtypeStruct((s, h_n * HEAD_DIM), BF16),
        name="moba_attn",
        compiler_params=_cparams("arbitrary"),
    )(qtab, ktab, qaug, kaug, vaug)


def _layer_norm(z, gain, bias):
    mu = jnp.mean(z, axis=-1, keepdims=True)
    zc = z - mu
    var = jnp.mean(zc * zc, axis=-1, keepdims=True)
    return zc * lax.rsqrt(var + LN_EPS) * gain + bias


def _post_mixer_kernel(alpha, n_chunk, a_ref, x_ref, wo_ref, g1_ref, b1_ref, wg_ref, wu_ref, wd_ref,
                       g2_ref, b2_ref, o_ref):
    x1 = _layer_norm(alpha * x_ref[...] + _dot(a_ref[...], wo_ref[...]), g1_ref[...], b1_ref[...])
    xb = x1.astype(BF16)
    cw = wg_ref.shape[1] // n_chunk
    f = None
    for c in range(n_chunk):
        sl = slice(c * cw, (c + 1) * cw)
        u = (jax.nn.silu(_dot(xb, wg_ref[:, sl])) * _dot(xb, wu_ref[:, sl])).astype(BF16)
        part = _dot(u, wd_ref[sl, :])
        f = part if f is None else f + part
    o_ref[...] = _layer_norm(alpha * x1 + f, g2_ref[...], b2_ref[...])


def _post_mixer(a, x, w_o, g1, b1, wg, wu, wd, g2, b2, alpha):
    s, d = x.shape
    f = wg.shape[1]
    n_chunk = 2 if f % (2 * LANES) == 0 else 1
    tm = min(MM_TILE, s)
    row = lambda i: (i, 0)

    def resident(shape):
        return pl.BlockSpec(shape, lambda i: (0, 0), pipeline_mode=pl.Buffered(1))

    vec = lambda v: v.reshape(1, d)
    return pl.pallas_call(
        functools.partial(_post_mixer_kernel, alpha, n_chunk),
        grid=(s // tm,),
        in_specs=[pl.BlockSpec((tm, a.shape[1]), row), pl.BlockSpec((tm, d), row),
                  resident(w_o.shape), resident((1, d)), resident((1, d)),
                  resident(wg.shape), resident(wu.shape), resident(wd.shape),
                  resident((1, d)), resident((1, d))],
        out_specs=pl.BlockSpec((tm, d), row),
        out_shape=jax.ShapeDtypeStruct((s, d), F32),
        name="post_mixer",
        compiler_params=_cparams("parallel"),
    )(a, x, w_o, vec(g1), vec(b1), wg, wu, wd, vec(g2), vec(b2))


def _nsa_mixer(x, tabs, w_in, ck_pos, ck_w1, ck_w2, cv_pos, cv_w1, cv_w2):
    s = x.shape[0]
    g_n, r_n, hd = NSA_KV_HEADS, NSA_GROUP, HEAD_DIM
    q_dim, kv_dim = N_HEADS * hd, NSA_KV_HEADS * hd
    n_sel = s // SEL_BLOCK
    n_var = -(-n_sel // BIAS_BLOCKS)
    assert s % TK == 0 and (BIAS_BLOCKS * SEL_BLOCK) % TK == 0
    wq = w_in[:, :q_dim].astype(BF16)
    wkv = w_in[:, q_dim:q_dim + 6 * kv_dim].astype(BF16)
    wg = w_in[:, q_dim + 6 * kv_dim:].reshape(-1, g_n, 3 * r_n)
    wg = jnp.pad(wg, ((0, 0), (0, 0), (0, LANES - 3 * r_n))).reshape(-1, g_n * LANES).astype(BF16)
    q, cmp_kv, kaug, vaug, kwin, vwin, gates = _nsa_proj(x, tabs, wq, wkv, wg)
    comp = _compress(cmp_kv, jnp.stack([ck_pos, cv_pos]),
                     jnp.stack([ck_w1, cv_w1]).astype(BF16), jnp.stack([ck_w2, cv_w2]).astype(BF16))
    kc = comp[0].astype(BF16)
    ones = jnp.ones(comp.shape[1:3] + (1,), F32)
    zeros = jnp.zeros(comp.shape[1:3] + (LANES - hd - 1,), F32)
    vc = jnp.concatenate([comp[1], ones, zeros], axis=-1).astype(BF16)
    pre, qaug = _nsa_local(q, kc, vc, kwin, vwin, gates, n_sel, n_var)
    return _nsa_sel(qaug, kaug, vaug, pre, gates)


def _moba_mixer(x, tabs, w_in):
    qaug, kaug, vaug = _moba_proj(x, tabs, w_in.astype(BF16))
    return _moba_attn(qaug, kaug, vaug)


def kernel(x, positions, nsa_w_in, nsa_w_o, nsa_ck_pos, nsa_ck_w1, nsa_ck_w2, nsa_cv_pos, nsa_cv_w1, nsa_cv_w2, moba_w_in, moba_w_o, ffn_wg, ffn_wu, ffn_wd, ln1_g, ln1_b, ln2_g, ln2_b):
    b, s, d = x.shape
    depth = ffn_wg.shape[0]
    alpha = (2 * depth) ** 0.25
    outs = []
    for bi in range(b):
        h = x[bi]
        tabs = _rope_tables(positions[bi])
        for i in range(depth):
            j = i // 2
            if i % 2 == 0:
                a = _nsa_mixer(h, tabs, nsa_w_in[j], nsa_ck_pos[j], nsa_ck_w1[j], nsa_ck_w2[j],
                               nsa_cv_pos[j], nsa_cv_w1[j], nsa_cv_w2[j])
                w_o = nsa_w_o[j]
            else:
                a = _moba_mixer(h, tabs, moba_w_in[j])
                w_o = moba_w_o[j]
            h = _post_mixer(a, h, w_o.astype(BF16), ln1_g[i], ln1_b[i], ffn_wg[i].astype(BF16),
                            ffn_wu[i].astype(BF16), ffn_wd[i].astype(BF16), ln2_g[i], ln2_b[i], alpha)
        outs.append(h)
    return jnp.stack(outs)
```
